```python
import jax, jax.numpy as jnp
from jax import lax
import numpy as np

D_MODEL = 2048
BATCH = 16
SEQ = 2048
DEPTH = 1

GRID_W = 64
CTX_LEN = 256
NORM_EPS = 1e-6
N_MOD = 6

SSD_HEADS = 16
SSD_HEAD_DIM = 64
D_SSD = SSD_HEADS * SSD_HEAD_DIM
SSD_GROUPS = 2
SSD_STATE = 128
SSD_CONV = 5
SSD_CHUNK = 128
CONV_CH = D_SSD + 2 * SSD_GROUPS * SSD_STATE

ATTN_HEADS = 8
ATTN_KV_HEADS = 2
ATTN_HEAD_DIM = 128
D_ATTN = ATTN_HEADS * ATTN_HEAD_DIM
D_KV = ATTN_KV_HEADS * ATTN_HEAD_DIM
WINDOW = 128
ROPE_BASE = 10000.0

N_BRANCH = 2
IN_W = D_SSD + CONV_CH + 2 * SSD_HEADS + D_ATTN + 2 * D_KV + N_BRANCH * D_MODEL

MOE_GROUPS = 4
EXPERTS_PER_GROUP = 8
N_EXPERTS = MOE_GROUPS * EXPERTS_PER_GROUP
TOP_K = 2
D_EXPERT = 512
MOE_BLOCK = 256

kernel_name = 'hybrid_ssd_swa_hmoe_dit'


def rmsnorm(x, w):
    xf = x.astype(jnp.float32)
    xf = xf * lax.rsqrt(jnp.mean(xf * xf, axis=-1, keepdims=True) + NORM_EPS)
    return (xf * w.astype(jnp.float32)).astype(x.dtype)


def modulate(h, shift, scale):
    return h * (1 + scale) + shift


def split_in(u):
    sizes = (D_SSD, CONV_CH, 2 * SSD_HEADS, D_ATTN, D_KV, D_KV, N_BRANCH * D_MODEL)
    return jnp.split(u, np.cumsum(sizes)[:-1].tolist(), axis=-1)


def dwconv_centred(u, w, b):
    ch = u.shape[-1]
    pad = w.shape[0] // 2
    out = lax.conv_general_dilated(u, w[:, None, :].astype(u.dtype), window_strides=(1,),
                                   padding=[(pad, pad)], dimension_numbers=('NWC', 'WIO', 'NWC'),
                                   feature_group_count=ch)
    return out + b


def rope_1d(x, pos):
    half = x.shape[-1] // 2
    freqs = ROPE_BASE ** (-jnp.arange(half, dtype=jnp.float32) / half)
    ang = pos.astype(jnp.float32)[:, None] * freqs[None, :]
    cos = jnp.cos(ang)[None, :, None, :]
    sin = jnp.sin(ang)[None, :, None, :]
    xf = x.astype(jnp.float32)
    x1, x2 = xf[..., :half], xf[..., half:]
    return jnp.concatenate([x1 * cos - x2 * sin, x2 * cos + x1 * sin], axis=-1).astype(x.dtype)


def rope_2d(x, row, col):
    d = x.shape[-1] // 2
    return jnp.concatenate([rope_1d(x[..., :d], row), rope_1d(x[..., d:], col)], axis=-1)


def ssd_inputs(xbc, dt_raw, conv_w, conv_b, dt_bias):
    bsz, t = xbc.shape[:2]
    xbc = jax.nn.silu(dwconv_centred(xbc, conv_w, conv_b))
    xs, bm, cm = jnp.split(xbc, [D_SSD, D_SSD + SSD_GROUPS * SSD_STATE], axis=-1)
    xs = xs.reshape(bsz, t, SSD_HEADS, SSD_HEAD_DIM)
    bm = bm.reshape(bsz, t, SSD_GROUPS, SSD_STATE)
    cm = cm.reshape(bsz, t, SSD_GROUPS, SSD_STATE)
    dt = jax.nn.softplus(dt_raw.astype(jnp.float32).reshape(bsz, t, 2, SSD_HEADS) + dt_bias.astype(jnp.float32))
    return xs, bm, cm, dt


def ssd_scan(x, dt, a, bm, cm, h0):
    bsz, t, nh, hp = x.shape
    g, n = bm.shape[2], bm.shape[3]
    r = nh // g
    lc = SSD_CHUNK
    nc = t // lc
    f32 = jnp.float32
    xdt = (x.astype(f32) * dt[..., None]).reshape(bsz, nc, lc, g, r, hp)
    cum = jnp.cumsum((dt * a).reshape(bsz, nc, lc, g, r), axis=2)
    bc = bm.astype(f32).reshape(bsz, nc, lc, g, n)
    cc = cm.astype(f32).reshape(bsz, nc, lc, g, n)
    seg = cum[:, :, :, None] - cum[:, :, None, :]
    tril = jnp.tril(jnp.ones((lc, lc), bool))[None, None, :, :, None, None]
    decay = jnp.exp(jnp.where(tril, seg, -jnp.inf))
    cb = jnp.einsum('bclgn,bcsgn->bclsg', cc, bc)
    y_diag = jnp.einsum('bclsg,bclsgr,bcsgrp->bclgrp', cb, decay, xdt)
    decay_end = jnp.exp(cum[:, :, -1:] - cum)
    states = jnp.einsum('bcsgn,bcsgr,bcsgrp->bcgrpn', bc, decay_end, xdt)
    chunk_decay = jnp.exp(cum[:, :, -1])

    def step(h, inp):
        s, d = inp
        return h * d[..., None, None] + s, h

    h_final, h_prev = lax.scan(step, h0.astype(f32).reshape(bsz, g, r, hp, n),
                               (jnp.moveaxis(states, 1, 0), jnp.moveaxis(chunk_decay, 1, 0)))
    h_prev = jnp.moveaxis(h_prev, 0, 1)
    y_off = jnp.einsum('bclgn,bcgrpn,bclgr->bclgrp', cc, h_prev, jnp.exp(cum))
    y = (y_diag + y_off).reshape(bsz, t, nh, hp)
    return y, h_final.reshape(bsz, nh, hp, n)


def ssd_final_state(x, dt, a, bm):
    bsz, t, nh, hp = x.shape
    g = bm.shape[2]
    r = nh // g
    cum = jnp.cumsum(dt * a, axis=1)
    decay_end = jnp.exp(cum[:, -1:] - cum).reshape(bsz, t, g, r)
    xdt = (x.astype(jnp.float32) * dt[..., None]).reshape(bsz, t, g, r, hp)
    st = jnp.einsum('btgn,btgr,btgrp->bgrpn', bm.astype(jnp.float32), decay_end, xdt)
    return st.reshape(bsz, nh, hp, bm.shape[3])


def bidir_ssd(xs, dt, a, bm, cm, d_skip, h0_f, h0_b):
    flip = lambda t: jnp.flip(t, axis=1)
    y_f, h_f = ssd_scan(xs, dt[:, :, 0], a[0], bm, cm, h0_f)
    y_b, h_b = ssd_scan(flip(xs), flip(dt[:, :, 1]), a[1], flip(bm), flip(cm), h0_b)
    y = y_f + flip(y_b) + d_skip.astype(jnp.float32)[:, None] * xs.astype(jnp.float32)
    return y.astype(xs.dtype), h_f, h_b


def gated_rmsnorm(y, z, w):
    gy = y * jax.nn.silu(z)
    shp = gy.shape
    gf = gy.astype(jnp.float32).reshape(*shp[:-1], SSD_GROUPS, shp[-1] // SSD_GROUPS)
    gf = gf * lax.rsqrt(jnp.mean(gf * gf, axis=-1, keepdims=True) + NORM_EPS)
    return (gf.reshape(shp) * w.astype(jnp.float32)).astype(y.dtype)


def window_attention(q, k, v, kc, vc, sink):
    bsz, t = q.shape[:2]
    nb = t // WINDOW
    rep = ATTN_HEADS // ATTN_KV_HEADS
    scale = ATTN_HEAD_DIM ** -0.5
    qb = q.reshape(bsz, nb, WINDOW, ATTN_KV_HEADS, rep, ATTN_HEAD_DIM)

    def band(u):
        up = jnp.pad(u, ((0, 0), (WINDOW, WINDOW), (0, 0), (0, 0)))
        up = up.reshape(bsz, nb + 2, WINDOW, ATTN_KV_HEADS, ATTN_HEAD_DIM)
        return jnp.concatenate([up[:, :-2], up[:, 1:-1], up[:, 2:]], axis=2)

    kb, vb = band(k), band(v)
    nk = 3 * WINDOW
    s_loc = jnp.einsum('bnqgrd,bnkgd->bngrqk', qb, kb).astype(jnp.float32) * scale
    q_off = jnp.arange(WINDOW)
    k_off = jnp.arange(nk) - WINDOW
    k_abs = jnp.arange(nb)[:, None] * WINDOW + k_off[None, :]
    in_band = jnp.abs(q_off[:, None] - k_off[None, :]) <= WINDOW
    mask = in_band[None] & ((k_abs >= 0) & (k_abs < t))[:, None, :]
    s_loc = jnp.where(mask[None, :, None, None], s_loc, -jnp.inf)
    s_ctx = jnp.einsum('bnqgrd,bkgd->bngrqk', qb, kc).astype(jnp.float32) * scale
    s_sink = jnp.broadcast_to(sink.astype(jnp.float32).reshape(1, 1, ATTN_KV_HEADS, rep, 1, 1),
                              s_loc.shape[:-1] + (1,))
    p = jax.nn.softmax(jnp.concatenate([s_loc, s_ctx, s_sink], axis=-1), axis=-1).astype(v.dtype)
    o = (jnp.einsum('bngrqk,bnkgd->bnqgrd', p[..., :nk], vb)
         + jnp.einsum('bngrqk,bkgd->bnqgrd', p[..., nk:nk + kc.shape[1]], vc))
    return o.reshape(bsz, t, D_ATTN)


def context_attention(qc, kc, vc, sink):
    bsz, tc = qc.shape[:2]
    rep = ATTN_HEADS // ATTN_KV_HEADS
    qg = qc.reshape(bsz, tc, ATTN_KV_HEADS, rep, ATTN_HEAD_DIM)
    s = jnp.einsum('bqgrd,bkgd->bgrqk', qg, kc).astype(jnp.float32) * ATTN_HEAD_DIM ** -0.5
    s_sink = jnp.broadcast_to(sink.astype(jnp.float32).reshape(1, ATTN_KV_HEADS, rep, 1, 1), s.shape[:-1] + (1,))
    p = jax.nn.softmax(jnp.concatenate([s, s_sink], axis=-1), axis=-1)[..., :tc].astype(vc.dtype)
    return jnp.einsum('bgrqk,bkgd->bqgrd', p, vc).reshape(bsz, tc, D_ATTN)


def merge_branches(y_ssd, y_attn, gate_logits, w_branch_ssd, w_branch_attn, w_out):
    g = jax.nn.sigmoid(gate_logits).reshape(*gate_logits.shape[:-1], N_BRANCH, D_MODEL)
    merged = g[..., 0, :] * (y_ssd @ w_branch_ssd) + g[..., 1, :] * (y_attn @ w_branch_attn)
    return merged @ w_out


def token_mixer(h, hc, row, col, w_in, conv_w, conv_b, a_log, dt_bias, d_skip, ssd_norm,
                attn_sink, w_branch_ssd, w_branch_attn, w_out, with_ctx_out):
    bsz, t, _ = h.shape
    tc = hc.shape[1]
    z, xbc, dt_raw, q, k, v, gate_logits = split_in(h @ w_in)
    zc, xbcc, dt_rawc, qc, kc, vc, gate_logitsc = split_in(hc @ w_in)
    a = -jnp.exp(a_log.astype(jnp.float32))

    xs, bm, cm, dt = ssd_inputs(xbc, dt_raw, conv_w, conv_b, dt_bias)
    xsc, bmc, cmc, dtc = ssd_inputs(xbcc, dt_rawc, conv_w, conv_b, dt_bias)
    if with_ctx_out:
        zeros = jnp.zeros((bsz, SSD_HEADS, SSD_HEAD_DIM, SSD_STATE), jnp.float32)
        yc_ssd, h_f, h_b = bidir_ssd(xsc, dtc, a, bmc, cmc, d_skip, zeros, zeros)
    else:
        h_f = ssd_final_state(xsc, dtc[:, :, 0], a[0], bmc)
        h_b = ssd_final_state(jnp.flip(xsc, 1), jnp.flip(dtc[:, :, 1], 1), a[1], jnp.flip(bmc, 1))
    y_ssd, _, _ = bidir_ssd(xs, dt, a, bm, cm, d_skip, h_f, h_b)
    y_ssd = gated_rmsnorm(y_ssd.reshape(bsz, t, D_SSD), z, ssd_norm)

    q = rope_2d(q.reshape(bsz, t, ATTN_HEADS, ATTN_HEAD_DIM), row, col)
    k = rope_2d(k.reshape(bsz, t, ATTN_KV_HEADS, ATTN_HEAD_DIM), row, col)
    v = v.reshape(bsz, t, ATTN_KV_HEADS, ATTN_HEAD_DIM)
    kc = kc.reshape(bsz, tc, ATTN_KV_HEADS, ATTN_HEAD_DIM)
    vc = vc.reshape(bsz, tc, ATTN_KV_HEADS, ATTN_HEAD_DIM)
    y_attn = window_attention(q, k, v, kc, vc, attn_sink)

    out = merge_branches(y_ssd, y_attn, gate_logits, w_branch_ssd, w_branch_attn, w_out)
    out_c = None
    if with_ctx_out:
        yc_ssd = gated_rmsnorm(yc_ssd.reshape(bsz, tc, D_SSD), zc, ssd_norm)
        yc_attn = context_attention(qc.reshape(bsz, tc, ATTN_HEADS, ATTN_HEAD_DIM), kc, vc, attn_sink)
        out_c = merge_branches(yc_ssd, yc_attn, gate_logitsc, w_branch_ssd, w_branch_attn, w_out)
    return out, out_c


def moe_dispatch(h, e_idx, e_w, w_gate, w_up, w_down):
    m, d = h.shape
    na = m * TOP_K
    e_flat = e_idx.reshape(na).astype(jnp.int32)
    tok_flat = jnp.repeat(jnp.arange(m, dtype=jnp.int32), TOP_K)
    w_flat = e_w.reshape(na)
    order = jnp.argsort(e_flat)
    e_sorted = e_flat[order]
    counts = jnp.zeros((N_EXPERTS,), jnp.int32).at[e_flat].add(1)
    padded = (counts + MOE_BLOCK - 1) // MOE_BLOCK * MOE_BLOCK
    start = jnp.cumsum(counts) - counts
    pend = jnp.cumsum(padded)
    pstart = pend - padded
    dest = pstart[e_sorted] + (jnp.arange(na, dtype=jnp.int32) - start[e_sorted])
    nblk = -(-(na + N_EXPERTS * (MOE_BLOCK - 1)) // MOE_BLOCK)
    n_slots = nblk * MOE_BLOCK
    slot_tok = jnp.zeros((n_slots,), jnp.int32).at[dest].set(tok_flat[order])
    slot_w = jnp.zeros((n_slots,), jnp.float32).at[dest].set(w_flat[order])
    blk_e = jnp.clip(jnp.searchsorted(pend, jnp.arange(nblk, dtype=jnp.int32) * MOE_BLOCK, side='right'),
                     0, N_EXPERTS - 1)

    def expert_block(args):
        toks, e = args
        xb = h[toks]
        return (jax.nn.silu(xb @ w_gate[e]) * (xb @ w_up[e])) @ w_down[e]

    out = lax.map(expert_block, (slot_tok.reshape(nblk, MOE_BLOCK), blk_e)).reshape(n_slots, d)
    out = out * slot_w[:, None].astype(h.dtype)
    return jnp.zeros_like(h).at[slot_tok].add(out)


def hier_moe(h, w_route_group, w_route_expert, w_gate, w_up, w_down):
    m = h.shape[0]
    g_prob = jax.nn.softmax((h @ w_route_group).astype(jnp.float32), axis=-1)
    g_w, g_idx = lax.top_k(g_prob, 1)
    e_logits = (h @ w_route_expert).astype(jnp.float32).reshape(m, MOE_GROUPS, EXPERTS_PER_GROUP)
    e_in = jnp.take_along_axis(e_logits, g_idx[:, :, None], axis=1)[:, 0]
    top_v, top_i = lax.top_k(e_in, TOP_K)
    e_w = jax.nn.softmax(top_v, axis=-1) * g_w
    e_idx = g_idx * EXPERTS_PER_GROUP + top_i
    return moe_dispatch(h, e_idx, e_w, w_gate, w_up, w_down)


def setup_inputs(seed: int = 0) -> dict:
    key = jax.random.key(seed)
    ks = jax.random.split(key, 32)
    L = DEPTH
    f32 = jnp.float32

    def nrm(k, shape, scale):
        return jax.random.normal(k, shape, f32) * scale

    dt0 = jnp.exp(jax.random.uniform(ks[12], (L, 2, SSD_HEADS), f32, minval=np.log(1e-3), maxval=np.log(1e-1)))
    return {
        'x': nrm(ks[0], (BATCH, SEQ, D_MODEL), 1.0),
        'c': nrm(ks[1], (BATCH, D_MODEL), 1.0),
        'ctx': nrm(ks[2], (BATCH, CTX_LEN, D_MODEL), 1.0),
        'c_ctx': nrm(ks[3], (D_MODEL,), 1.0),
        'w_ada': nrm(ks[4], (L, D_MODEL, N_MOD * D_MODEL), D_MODEL ** -0.5),
        'b_ada': nrm(ks[5], (L, N_MOD * D_MODEL), 0.02),
        'norm_mix': 1.0 + nrm(ks[6], (L, D_MODEL), 0.02),
        'norm_ffn': 1.0 + nrm(ks[7], (L, D_MODEL), 0.02),
        'w_in': nrm(ks[8], (L, D_MODEL, IN_W), D_MODEL ** -0.5),
        'conv_w': nrm(ks[9], (L, SSD_CONV, CONV_CH), SSD_CONV ** -0.5),
        'conv_b': nrm(ks[10], (L, CONV_CH), 0.02),
        'a_log': jnp.log(jax.random.uniform(ks[11], (L, 2, SSD_HEADS), f32, minval=1.0, maxval=16.0)),
        'dt_bias': dt0 + jnp.log(-jnp.expm1(-dt0)),
        'd_skip': 1.0 + nrm(ks[13], (L, SSD_HEADS), 0.1),
        'ssd_norm': 1.0 + nrm(ks[14], (L, D_SSD), 0.02),
        'attn_sink': nrm(ks[15], (L, ATTN_HEADS), 0.5),
        'w_branch_ssd': nrm(ks[16], (L, D_SSD, D_MODEL), D_SSD ** -0.5),
        'w_branch_attn': nrm(ks[17], (L, D_ATTN, D_MODEL), D_ATTN ** -0.5),
        'w_out': nrm(ks[18], (L, D_MODEL, D_MODEL), D_MODEL ** -0.5),
        'w_route_group': nrm(ks[19], (L, D_MODEL, MOE_GROUPS), D_MODEL ** -0.5),
        'w_route_expert': nrm(ks[20], (L, D_MODEL, N_EXPERTS), D_MODEL ** -0.5),
        'w_gate': nrm(ks[21], (L, N_EXPERTS, D_MODEL, D_EXPERT), D_MODEL ** -0.5),
        'w_up': nrm(ks[22], (L, N_EXPERTS, D_MODEL, D_EXPERT), D_MODEL ** -0.5),
        'w_down': nrm(ks[23], (L, N_EXPERTS, D_EXPERT, D_MODEL), D_EXPERT ** -0.5),
        'norm_final': 1.0 + nrm(ks[24], (D_MODEL,), 0.02),
    }


def reference(x, c, ctx, c_ctx, w_ada, b_ada, norm_mix, norm_ffn, w_in, conv_w, conv_b, a_log, dt_bias,
              d_skip, ssd_norm, attn_sink, w_branch_ssd, w_branch_attn, w_out, w_route_group,
              w_route_expert, w_gate, w_up, w_down, norm_final):
    bsz, t, _ = x.shape
    tc = ctx.shape[1]
    rows = t // GRID_W
    row = jnp.repeat(jnp.arange(rows, dtype=jnp.int32), GRID_W)
    col = jnp.tile(jnp.arange(GRID_W, dtype=jnp.int32), rows)
    cx = ctx
    for l in range(DEPTH):
        last = l == DEPTH - 1
        sh1, sc1, g1, sh2, sc2, g2 = jnp.split((jax.nn.silu(c) @ w_ada[l] + b_ada[l])[:, None, :], N_MOD, axis=-1)
        csh1, csc1, cg1, csh2, csc2, cg2 = jnp.split((jax.nn.silu(c_ctx) @ w_ada[l] + b_ada[l])[None, None, :],
                                                     N_MOD, axis=-1)
        h = modulate(rmsnorm(x, norm_mix[l]), sh1, sc1)
        hc = modulate(rmsnorm(cx, norm_mix[l]), csh1, csc1)
        mix, mix_c = token_mixer(h, hc, row, col, w_in[l], conv_w[l], conv_b[l], a_log[l], dt_bias[l],
                                 d_skip[l], ssd_norm[l], attn_sink[l], w_branch_ssd[l], w_branch_attn[l],
                                 w_out[l], not last)
        x = x + g1 * mix
        h2 = modulate(rmsnorm(x, norm_ffn[l]), sh2, sc2)
        moe_w = (w_route_group[l], w_route_expert[l], w_gate[l], w_up[l], w_down[l])
        if last:
            x = x + g2 * hier_moe(h2.reshape(-1, D_MODEL), *moe_w).reshape(bsz, t, D_MODEL)
        else:
            cx = cx + cg1 * mix_c
            h2c = modulate(rmsnorm(cx, norm_ffn[l]), csh2, csc2)
            f = hier_moe(jnp.concatenate([h2c, h2], axis=1).reshape(-1, D_MODEL), *moe_w)
            f = f.reshape(bsz, tc + t, D_MODEL)
            cx = cx + cg2 * f[:, :tc]
            x = x + g2 * f[:, tc:]
    return rmsnorm(x, norm_final)
```

```python
import functools

import numpy as np
import jax
import jax.numpy as jnp
from jax import lax
from jax.experimental import pallas as pl
from jax.experimental.pallas import tpu as pltpu

F32 = jnp.float32
BF16 = jnp.bfloat16
I32 = jnp.int32

D_MODEL = 2048
GRID_W = 64
NORM_EPS = 1e-6
N_MOD = 6
SSD_HEADS = 16
SSD_HEAD_DIM = 64
D_SSD = SSD_HEADS * SSD_HEAD_DIM
SSD_GROUPS = 2
SSD_STATE = 128
SSD_CONV = 5
SSD_CHUNK = 128
CONV_CH = D_SSD + 2 * SSD_GROUPS * SSD_STATE
ATTN_HEADS = 8
ATTN_KV_HEADS = 2
ATTN_HEAD_DIM = 128
D_ATTN = ATTN_HEADS * ATTN_HEAD_DIM
D_KV = ATTN_KV_HEADS * ATTN_HEAD_DIM
WINDOW = 128
ROPE_BASE = 10000.0
N_BRANCH = 2
MOE_GROUPS = 4
EXPERTS_PER_GROUP = 8
N_EXPERTS = MOE_GROUPS * EXPERTS_PER_GROUP
TOP_K = 2
D_EXPERT = 512
MOE_BLOCK = 256

LANES = 128
TOK_TILE = D_MODEL // LANES
VMEM_LIMIT = 56 * 1024 * 1024

U_XS, U_Z, U_Q, U_BC, U_K, U_V, U_G = 0, 1024, 2048, 3072, 3584, 3840, 4096
U_W = 8192
UC_XS, UC_BC, UC_K, UC_V = 0, 1024, 1536, 1792
UC_W = 2048
IN_TN = 512


def _params(sem):
    return pltpu.CompilerParams(dimension_semantics=sem, vmem_limit_bytes=VMEM_LIMIT)


def _silu(v):
    return v * jax.nn.sigmoid(v)


def _split_bf16(v, n):
    parts = []
    r = v
    for _ in range(n):
        p = r.astype(BF16)
        parts.append(p)
        r = r - p.astype(F32)
    return parts


def _dot(a, b):
    return jnp.dot(a, b, preferred_element_type=F32)


def _dot_nt(a, b):
    return lax.dot_general(a, b, (((1,), (1,)), ((), ())), preferred_element_type=F32)


def _ada_kernel(c_ref, w_ref, b_ref, o_ref):
    a = _silu(c_ref[...]).astype(BF16)
    o_ref[...] = _dot(a, w_ref[...].astype(BF16)) + b_ref[...]


def _ada(cc, w, b):
    rows, d = cc.shape
    n = w.shape[1]
    tn = 1024
    return pl.pallas_call(
        _ada_kernel,
        grid=(n // tn,),
        in_specs=[pl.BlockSpec((rows, d), lambda j: (0, 0)),
                  pl.BlockSpec((d, tn), lambda j: (0, j)),
                  pl.BlockSpec((1, tn), lambda j: (0, j))],
        out_specs=pl.BlockSpec((rows, tn), lambda j: (0, j)),
        out_shape=jax.ShapeDtypeStruct((rows, n), F32),
        compiler_params=_params(("arbitrary",)),
        name="ada",
    )(cc, w, b)


def _rope(a, cos, sin_signed, first):
    partner = jnp.where(first, pltpu.roll(a, 96, 1), pltpu.roll(a, 32, 1))
    return a * cos + partner * sin_signed


def _inproj_kernel(*refs, rope_q_blocks, rope_kv_block, sub):
    if rope_q_blocks:
        x_ref, nw_ref, mod_ref, w_ref, wdt_ref, cos_ref, sin_ref, o_ref, dt_ref, h_scr = refs
    else:
        x_ref, nw_ref, mod_ref, w_ref, wdt_ref, o_ref, dt_ref, h_scr = refs
    j = pl.program_id(1)
    tm = x_ref.shape[0]
    tn = w_ref.shape[1]

    @pl.when(j == 0)
    def _():
        nw = nw_ref[...]
        sh = mod_ref[0, 0:1, :]
        sc = mod_ref[0, 1:2, :]

        def body(r, carry):
            rows = pl.ds(pl.multiple_of(r * sub, sub), sub)
            xf = x_ref[rows, :]
            ms = jnp.mean(xf * xf, axis=-1, keepdims=True)
            h = (xf * lax.rsqrt(ms + NORM_EPS) * nw) * (1.0 + sc) + sh
            hb = h.astype(BF16)
            h_scr[rows, :] = hb
            d = _dot(hb, wdt_ref[...])
            dt_ref[0, rows, :] = d[:, :LANES]
            dt_ref[1, rows, :] = d[:, LANES:]
            return carry

        lax.fori_loop(0, tm // sub, body, 0)

    acc = _dot(h_scr[...], w_ref[...])

    def store(n_rope):
        if n_rope:
            cos = cos_ref[...]
            sin = sin_ref[...]
            lane = lax.broadcasted_iota(I32, (tm, LANES), 1)
            first = (lane % 64) < 32
        for hh in range(tn // LANES):
            a = acc[:, hh * LANES:(hh + 1) * LANES]
            if hh < n_rope:
                a = _rope(a, cos, sin, first)
            o_ref[:, hh * LANES:(hh + 1) * LANES] = a.astype(BF16)

    if rope_q_blocks:
        is_q = (j >= rope_q_blocks[0]) & (j <= rope_q_blocks[-1])
        is_kv = j == rope_kv_block

        @pl.when(is_q)
        def _():
            store(tn // LANES)

        @pl.when(is_kv)
        def _():
            store(D_KV // LANES)

        @pl.when(jnp.logical_not(is_q | is_kv))
        def _():
            store(0)
    else:
        store(0)


def _inproj(x2, nw, mods, w, wdt, cos_t, sin_t, *, rows_per_mod, mod_row0, tm, rope):
    m, d = x2.shape
    n = w.shape[1]
    tn = IN_TN
    nbm = rows_per_mod // tm if rows_per_mod else 0
    if rows_per_mod:
        mod_map = lambda i, j: (mod_row0 + i // nbm, 0, 0)
    else:
        mod_map = lambda i, j: (mod_row0, 0, 0)
    in_specs = [pl.BlockSpec((tm, d), lambda i, j: (i, 0)),
                pl.BlockSpec((1, d), lambda i, j: (0, 0)),
                pl.BlockSpec((1, N_MOD, d), mod_map),
                pl.BlockSpec((d, tn), lambda i, j: (0, j)),
                pl.BlockSpec((d, 2 * LANES), lambda i, j: (0, 0))]
    args = [x2, nw, mods, w, wdt]
    if rope:
        in_specs += [pl.BlockSpec((tm, LANES), lambda i, j: (i % nbm, 0)),
                     pl.BlockSpec((tm, LANES), lambda i, j: (i % nbm, 0))]
        args += [cos_t, sin_t]
        rq = tuple(range(U_Q // tn, (U_Q + D_ATTN) // tn))
        rkv = U_K // tn
    else:
        rq, rkv = (), None
    kern = functools.partial(_inproj_kernel, rope_q_blocks=rq, rope_kv_block=rkv, sub=128)
    return pl.pallas_call(
        kern,
        grid=(m // tm, n // tn),
        in_specs=in_specs,
        out_specs=[pl.BlockSpec((tm, tn), lambda i, j: (i, j)),
                   pl.BlockSpec((2, tm, LANES), lambda i, j: (0, i, 0))],
        out_shape=[jax.ShapeDtypeStruct((m, n), BF16),
                   jax.ShapeDtypeStruct((2, m, LANES), F32)],
        scratch_shapes=[pltpu.VMEM((tm, d), BF16)],
        compiler_params=_params(("arbitrary", "arbitrary")),
        name="inproj_rope" if rope else "inproj_ctx",
    )(*args)


CONV_HALO = 16


def _conv_kernel(xs_ref, bc_ref, pxs_ref, pbc_ref, nxs_ref, nbc_ref, w_ref, b_ref,
                 oxs_ref, obc_ref, ext, *, blocks_per_seq):
    i = pl.program_id(0)
    r = xs_ref.shape[0]
    pos = i % blocks_per_seq
    not_first = pos != 0
    not_last = pos != blocks_per_seq - 1
    h = CONV_HALO
    ext[0:h, 0:D_SSD] = jnp.where(not_first, pxs_ref[...].astype(F32), 0.0)
    ext[0:h, D_SSD:] = jnp.where(not_first, pbc_ref[...].astype(F32), 0.0)
    ext[h:h + r, 0:D_SSD] = xs_ref[...].astype(F32)
    ext[h:h + r, D_SSD:] = bc_ref[...].astype(F32)
    ext[h + r:, 0:D_SSD] = jnp.where(not_last, nxs_ref[...].astype(F32), 0.0)
    ext[h + r:, D_SSD:] = jnp.where(not_last, nbc_ref[...].astype(F32), 0.0)
    cw = 256
    pad = SSD_CONV // 2
    for cb in range(0, CONV_CH, cw):
        acc = jnp.broadcast_to(b_ref[:, cb:cb + cw], (r, cw))
        for k in range(SSD_CONV):
            acc = acc + ext[h - pad + k:h - pad + k + r, cb:cb + cw] * w_ref[k:k + 1, cb:cb + cw]
        y = _silu(acc)
        if cb < D_SSD:
            oxs_ref[:, cb:cb + cw] = y
        else:
            obc_ref[:, cb - D_SSD:cb - D_SSD + cw] = y.astype(BF16)


def _conv(u, xs_col, bc_col, conv_w, conv_b, seq_len, r):
    m = u.shape[0]
    h = CONV_HALO
    bps = seq_len // r
    wbc = CONV_CH - D_SSD
    xs_c, bc_c = xs_col // D_SSD, bc_col // wbc
    nh = m // h
    prev = lambda i: jnp.maximum(i * (r // h) - 1, 0)
    nxt = lambda i: jnp.minimum((i + 1) * (r // h), nh - 1)
    return pl.pallas_call(
        functools.partial(_conv_kernel, blocks_per_seq=bps),
        grid=(m // r,),
        in_specs=[pl.BlockSpec((r, D_SSD), lambda i: (i, xs_c)),
                  pl.BlockSpec((r, wbc), lambda i: (i, bc_c)),
                  pl.BlockSpec((h, D_SSD), lambda i: (prev(i), xs_c)),
                  pl.BlockSpec((h, wbc), lambda i: (prev(i), bc_c)),
                  pl.BlockSpec((h, D_SSD), lambda i: (nxt(i), xs_c)),
                  pl.BlockSpec((h, wbc), lambda i: (nxt(i), bc_c)),
                  pl.BlockSpec((SSD_CONV, CONV_CH), lambda i: (0, 0)),
                  pl.BlockSpec((1, CONV_CH), lambda i: (0, 0))],
        out_specs=[pl.BlockSpec((r, D_SSD), lambda i: (i, 0)),
                   pl.BlockSpec((r, wbc), lambda i: (i, 0))],
        out_shape=[jax.ShapeDtypeStruct((m, D_SSD), F32),
                   jax.ShapeDtypeStruct((m, wbc), BF16)],
        scratch_shapes=[pltpu.VMEM((r + 2 * h, CONV_CH), F32)],
        compiler_params=_params(("arbitrary",)),
        name="conv_silu",
    )(u, u, u, u, u, u, conv_w, conv_b)


def _expand_heads(v, e):
    hi, lo = _split_bf16(v, 2)
    return _dot(hi, e) + _dot(lo, e)


def _cumsum_mm(tri, da):
    p0, p1, p2 = _split_bf16(da, 3)
    return _dot(tri, p0) + _dot(tri, p1) + _dot(tri, p2)


def _ctx_state_kernel(xs_ref, b_ref, dt_ref, alog_ref, bias_ref, tri_ref, e_ref, h_ref):
    d = pl.program_id(1)
    tc = xs_ref.shape[0]
    dt = jax.nn.softplus(dt_ref[0] + bias_ref[0])
    a = -jnp.exp(alog_ref[0])
    cum = _cumsum_mm(tri_ref[0], dt * a)
    cum_end = jnp.where(d == 0, cum[tc - 1:tc, :], cum[0:1, :])
    e = e_ref[...]
    dtx = _expand_heads(dt, e)
    dex = _expand_heads(jnp.exp(cum_end - cum), e)
    xw = (xs_ref[...] * dtx * dex).astype(BF16)
    gw = D_SSD // SSD_GROUPS
    for g in range(SSD_GROUPS):
        bgt = b_ref[:, g * SSD_STATE:(g + 1) * SSD_STATE].astype(F32).T.astype(BF16)
        h_ref[0, 0, :, g * gw:(g + 1) * gw] = _dot(bgt, xw[:, g * gw:(g + 1) * gw])


def _ctx_state(xs_c, bc_c, dt2, alog, bias, tri, e, bsz, tc):
    return pl.pallas_call(
        _ctx_state_kernel,
        grid=(bsz, 2),
        in_specs=[pl.BlockSpec((tc, D_SSD), lambda b, d: (b, 0)),
                  pl.BlockSpec((tc, SSD_GROUPS * SSD_STATE), lambda b, d: (b, 0)),
                  pl.BlockSpec((1, tc, LANES), lambda b, d: (d, b, 0)),
                  pl.BlockSpec((1, 1, LANES), lambda b, d: (d, 0, 0)),
                  pl.BlockSpec((1, 1, LANES), lambda b, d: (d, 0, 0)),
                  pl.BlockSpec((1, tc, tc), lambda b, d: (d, 0, 0)),
                  pl.BlockSpec((LANES, D_SSD), lambda b, d: (0, 0))],
        out_specs=pl.BlockSpec((1, 1, SSD_STATE, D_SSD), lambda b, d: (b, d, 0, 0)),
        out_shape=jax.ShapeDtypeStruct((bsz, 2, SSD_STATE, D_SSD), F32),
        compiler_params=_params(("arbitrary", "arbitrary")),
        name="ssd_ctx_state",
    )(xs_c, bc_c, dt2, alog, bias, tri, e)


def _ssd_kernel(xs_ref, bc_ref, dt_ref, alog_ref, bias_ref, tri_ref, e_ref, h0_ref, z_ref,
                dsk_ref, nw_ref, o_ref, st, yf, ycur, *, nc):
    d = pl.program_id(1)
    s = pl.program_id(2)
    lc = SSD_CHUNK
    c = jnp.where(d == 0, s, nc - 1 - s)
    is_f = d == 0

    @pl.when(s == 0)
    def _():
        st[...] = h0_ref[0, 0]

    xs = xs_ref[...]
    dt = jax.nn.softplus(dt_ref[0] + bias_ref[0])
    a = -jnp.exp(alog_ref[0])
    cum = _cumsum_mm(tri_ref[0], dt * a)
    cum_t = cum.T
    cum_end = jnp.where(is_f, cum[lc - 1:lc, :], cum[0:1, :])
    e = e_ref[...]
    ecx = _expand_heads(jnp.exp(cum), e)
    dex = _expand_heads(jnp.exp(cum_end - cum), e)
    dtx = _expand_heads(dt, e)
    xdt = xs * dtx
    xdt_b = xdt.astype(BF16)
    xde_b = (xdt * dex).astype(BF16)
    cd_row = jnp.where(is_f, ecx[lc - 1:lc, :], ecx[0:1, :])
    li = lax.broadcasted_iota(I32, (lc, lc), 0)
    si = lax.broadcasted_iota(I32, (lc, lc), 1)
    causal = (li - si) * jnp.where(is_f, 1, -1) >= 0
    lane = lax.broadcasted_iota(I32, (lc, LANES), 1)
    low = lane < SSD_HEAD_DIM
    gw = D_SSD // SSD_GROUPS
    hpg = SSD_HEADS // SSD_GROUPS
    for g in range(SSD_GROUPS):
        bg = bc_ref[:, g * SSD_STATE:(g + 1) * SSD_STATE]
        cg = bc_ref[:, (SSD_GROUPS + g) * SSD_STATE:(SSD_GROUPS + g + 1) * SSD_STATE]
        cb = _dot_nt(cg, bg)
        yoff = _dot(cg, st[:, g * gw:(g + 1) * gw].astype(BF16)) * ecx[:, g * gw:(g + 1) * gw]
        for pr in range(hpg // 2):
            h0 = g * hpg + 2 * pr
            ms = []
            for h in (h0, h0 + 1):
                seg = cum[:, h:h + 1] - cum_t[h:h + 1, :]
                ms.append((cb * jnp.exp(jnp.where(causal, seg, -jnp.inf))).astype(BF16))
            lhs = jnp.concatenate(ms, axis=1)
            xp = xdt_b[:, h0 * SSD_HEAD_DIM:(h0 + 2) * SSD_HEAD_DIM]
            zero = jnp.zeros_like(xp)
            rhs = jnp.concatenate([jnp.where(low, xp, zero), jnp.where(low, zero, xp)], axis=0)
            ycur[:, h0 * SSD_HEAD_DIM:(h0 + 2) * SSD_HEAD_DIM] = (
                _dot(lhs, rhs) + yoff[:, pr * LANES:(pr + 1) * LANES])
        bgt = bg.astype(F32).T.astype(BF16)
        st[:, g * gw:(g + 1) * gw] = (st[:, g * gw:(g + 1) * gw] * cd_row[:, g * gw:(g + 1) * gw]
                                      + _dot(bgt, xde_b[:, g * gw:(g + 1) * gw]))

    rows = pl.ds(pl.multiple_of(c * lc, lc), lc)

    @pl.when(is_f)
    def _():
        yf[rows, :] = ycur[...]

    @pl.when(jnp.logical_not(is_f))
    def _():
        y = ycur[...] + yf[rows, :] + dsk_ref[...] * xs
        gy = y * _silu(z_ref[...].astype(F32))
        for g in range(SSD_GROUPS):
            blk = gy[:, g * gw:(g + 1) * gw]
            ms = jnp.mean(blk * blk, axis=-1, keepdims=True)
            o_ref[:, g * gw:(g + 1) * gw] = (blk * lax.rsqrt(ms + NORM_EPS)
                                             * nw_ref[:, g * gw:(g + 1) * gw]).astype(BF16)


def _ssd(xs_c, bc_c, dt2, alog, bias, tri, e, h0, u, dsk, nw, bsz, t):
    nc = t // SSD_CHUNK
    lc = SSD_CHUNK
    m = bsz * t
    chunk = lambda b, d, s: b * nc + jnp.where(d == 0, s, nc - 1 - s)
    late = lambda b, d, s: b * nc + jnp.where(d == 0, nc - 1, nc - 1 - s)
    zc = U_Z // D_SSD
    return pl.pallas_call(
        functools.partial(_ssd_kernel, nc=nc),
        grid=(bsz, 2, nc),
        in_specs=[pl.BlockSpec((lc, D_SSD), lambda b, d, s: (chunk(b, d, s), 0)),
                  pl.BlockSpec((lc, 2 * SSD_GROUPS * SSD_STATE), lambda b, d, s: (chunk(b, d, s), 0)),
                  pl.BlockSpec((1, lc, LANES), lambda b, d, s: (d, chunk(b, d, s), 0)),
                  pl.BlockSpec((1, 1, LANES), lambda b, d, s: (d, 0, 0)),
                  pl.BlockSpec((1, 1, LANES), lambda b, d, s: (d, 0, 0)),
                  pl.BlockSpec((1, lc, lc), lambda b, d, s: (d, 0, 0)),
                  pl.BlockSpec((LANES, D_SSD), lambda b, d, s: (0, 0)),
                  pl.BlockSpec((1, 1, SSD_STATE, D_SSD), lambda b, d, s: (b, d, 0, 0)),
                  pl.BlockSpec((lc, D_SSD), lambda b, d, s: (late(b, d, s), zc)),
                  pl.BlockSpec((1, D_SSD), lambda b, d, s: (0, 0)),
                  pl.BlockSpec((1, D_SSD), lambda b, d, s: (0, 0))],
        out_specs=pl.BlockSpec((lc, D_SSD), lambda b, d, s: (late(b, d, s), 0)),
        out_shape=jax.ShapeDtypeStruct((m, D_SSD), BF16),
        scratch_shapes=[pltpu.VMEM((SSD_STATE, D_SSD), F32),
                        pltpu.VMEM((t, D_SSD), F32),
                        pltpu.VMEM((lc, D_SSD), F32)],
        compiler_params=_params(("arbitrary", "arbitrary", "arbitrary")),
        name="ssd_scan",
    )(xs_c, bc_c, dt2, alog, bias, tri, e, h0, u, dsk, nw)


def _attn_kernel(sink_ref, q_ref, k_ref, v_ref, kc_ref, vc_ref, o_ref, kp, vp, bias, *, t):
    g = pl.program_id(1)
    w = WINDOW
    nb = t // w
    rep = ATTN_HEADS // ATTN_KV_HEADS
    tc = kc_ref.shape[0]
    nk = 3 * w + tc
    zeros = jnp.zeros((w, ATTN_HEAD_DIM), BF16)
    kp[0:w, :] = zeros
    kp[w:w + t, :] = k_ref[...]
    kp[w + t:, :] = zeros
    vp[0:w, :] = zeros
    vp[w:w + t, :] = v_ref[...]
    vp[w + t:, :] = zeros
    qi = lax.broadcasted_iota(I32, (rep * w, nk), 0) % w
    kj = lax.broadcasted_iota(I32, (rep * w, nk), 1)
    in_band = (jnp.abs(qi - (kj - w)) <= w) | (kj >= 3 * w)
    bias[...] = jnp.where(in_band, 0.0, -jnp.inf).astype(F32)
    sink_col = jnp.concatenate(
        [jnp.full((w, 1), sink_ref[g * rep + h], F32) for h in range(rep)], axis=0)
    scale = ATTN_HEAD_DIM ** -0.5
    col = lax.broadcasted_iota(I32, (1, nk), 1)

    def body(n, carry):
        rows = pl.ds(pl.multiple_of(n * w, w), w)
        q = jnp.concatenate([q_ref[rows, h * ATTN_HEAD_DIM:(h + 1) * ATTN_HEAD_DIM]
                             for h in range(rep)], axis=0)
        band = pl.ds(pl.multiple_of(n * w, w), 3 * w)
        kall = jnp.concatenate([kp[band, :], kc_ref[...]], axis=0)
        vall = jnp.concatenate([vp[band, :], vc_ref[...]], axis=0)
        lo = jnp.where(n == 0, w, 0)
        hi = jnp.where(n == nb - 1, 2 * w, 3 * w)
        outside = (col < lo) | ((col >= hi) & (col < 3 * w))
        edge = jnp.where(outside, -jnp.inf, 0.0).astype(F32)
        sc = _dot_nt(q, kall) * scale + bias[...] + edge
        mx = jnp.maximum(jnp.max(sc, axis=-1, keepdims=True), sink_col)
        p = jnp.exp(sc - mx)
        den = jnp.sum(p, axis=-1, keepdims=True) + jnp.exp(sink_col - mx)
        o = _dot(p.astype(BF16), vall) / den
        for h in range(rep):
            o_ref[rows, h * ATTN_HEAD_DIM:(h + 1) * ATTN_HEAD_DIM] = o[h * w:(h + 1) * w, :].astype(BF16)
        return carry

    lax.fori_loop(0, nb, body, 0)


def _attn(sink, u, uc, bsz, t, tc):
    rep = ATTN_HEADS // ATTN_KV_HEADS
    qw = rep * ATTN_HEAD_DIM
    hd = ATTN_HEAD_DIM
    return pl.pallas_call(
        functools.partial(_attn_kernel, t=t),
        grid=(bsz, ATTN_KV_HEADS),
        in_specs=[pl.BlockSpec(memory_space=pltpu.SMEM),
                  pl.BlockSpec((t, qw), lambda b, g: (b, U_Q // qw + g)),
                  pl.BlockSpec((t, hd), lambda b, g: (b, U_K // hd + g)),
                  pl.BlockSpec((t, hd), lambda b, g: (b, U_V // hd + g)),
                  pl.BlockSpec((tc, hd), lambda b, g: (b, UC_K // hd + g)),
                  pl.BlockSpec((tc, hd), lambda b, g: (b, UC_V // hd + g))],
        out_specs=pl.BlockSpec((t, qw), lambda b, g: (b, g)),
        out_shape=jax.ShapeDtypeStruct((bsz * t, D_ATTN), BF16),
        scratch_shapes=[pltpu.VMEM((t + 2 * WINDOW, hd), BF16),
                        pltpu.VMEM((t + 2 * WINDOW, hd), BF16),
                        pltpu.VMEM((rep * WINDOW, 3 * WINDOW + tc), F32)],
        compiler_params=_params(("arbitrary", "arbitrary")),
        name="window_attn",
    )(sink, u, u, u, uc, uc)


def _merge_kernel(ys_ref, ya_ref, g0_ref, g1_ref, ws_ref, wa_ref, o_ref):
    ys = ys_ref[...]
    ya = ya_ref[...]
    cw = 512
    for cb in range(0, D_MODEL, cw):
        t0 = _dot(ys, ws_ref[:, cb:cb + cw])
        t1 = _dot(ya, wa_ref[:, cb:cb + cw])
        g0 = jax.nn.sigmoid(g0_ref[:, cb:cb + cw].astype(F32))
        g1 = jax.nn.sigmoid(g1_ref[:, cb:cb + cw].astype(F32))
        o_ref[:, cb:cb + cw] = (g0 * t0 + g1 * t1).astype(BF16)


def _merge(ys, ya, u, ws, wa, tm):
    m = ys.shape[0]
    gc = U_G // D_MODEL
    return pl.pallas_call(
        _merge_kernel,
        grid=(m // tm,),
        in_specs=[pl.BlockSpec((tm, D_SSD), lambda i: (i, 0)),
                  pl.BlockSpec((tm, D_ATTN), lambda i: (i, 0)),
                  pl.BlockSpec((tm, D_MODEL), lambda i: (i, gc)),
                  pl.BlockSpec((tm, D_MODEL), lambda i: (i, gc + 1)),
                  pl.BlockSpec((D_SSD, D_MODEL), lambda i: (0, 0)),
                  pl.BlockSpec((D_ATTN, D_MODEL), lambda i: (0, 0))],
        out_specs=pl.BlockSpec((tm, D_MODEL), lambda i: (i, 0)),
        out_shape=jax.ShapeDtypeStruct((m, D_MODEL), BF16),
        compiler_params=_params(("arbitrary",)),
        name="branch_merge",
    )(ys, ya, u, u, ws, wa)


def _lane_min_index(mask, lane):
    return jnp.min(jnp.where(mask, lane, LANES), axis=-1, keepdims=True)


def _outproj_kernel(mg_ref, wo_ref, x_ref, mod_ref, nw_ref, wr_ref, tri_ref,
                    x1_ref, h2_ref, ridx_ref, rw_ref, cnt_ref, carry):
    i = pl.program_id(0)
    tm = x_ref.shape[0]

    @pl.when(i == 0)
    def _():
        carry[...] = jnp.zeros_like(carry)

    g1 = mod_ref[0, 2:3, :]
    sh2 = mod_ref[0, 3:4, :]
    sc2 = mod_ref[0, 4:5, :]
    x1 = x_ref[...] + g1 * _dot(mg_ref[...], wo_ref[...])
    x1_ref[...] = x1
    ms = jnp.mean(x1 * x1, axis=-1, keepdims=True)
    h2 = (x1 * lax.rsqrt(ms + NORM_EPS) * nw_ref[...]) * (1.0 + sc2) + sh2
    for a in range(TOK_TILE):
        h2_ref[pl.ds(a, tm, stride=TOK_TILE), :] = h2[:, a * LANES:(a + 1) * LANES]

    lg = _dot(h2.astype(BF16), wr_ref[...])
    lane = lax.broadcasted_iota(I32, (tm, LANES), 1)
    gl = jnp.where(lane < MOE_GROUPS, lg[:, :LANES], -jnp.inf)
    gmax = jnp.max(gl, axis=-1, keepdims=True)
    g_w = 1.0 / jnp.sum(jnp.exp(gl - gmax), axis=-1, keepdims=True)
    g_idx = _lane_min_index(gl == gmax, lane)
    el = jnp.where((lane // EXPERTS_PER_GROUP) == g_idx, lg[:, LANES:], -jnp.inf)
    v1 = jnp.max(el, axis=-1, keepdims=True)
    i1 = _lane_min_index(el == v1, lane)
    el2 = jnp.where(lane == i1, -jnp.inf, el)
    v2 = jnp.max(el2, axis=-1, keepdims=True)
    i2 = _lane_min_index(el2 == v2, lane)
    e2 = jnp.exp(v2 - v1)
    w1 = g_w * (1.0 / (1.0 + e2))
    w2 = g_w * (e2 / (1.0 + e2))

    hit1 = lane == i1
    hit2 = lane == i2
    onehot = jnp.where(hit1 | hit2, 1.0, 0.0).astype(F32)
    before = _dot(tri_ref[...], onehot.astype(BF16)) + carry[0:1, :]
    r1 = jnp.sum(jnp.where(hit1, before, 0.0), axis=-1, keepdims=True)
    r2 = jnp.sum(jnp.where(hit2, before, 0.0), axis=-1, keepdims=True)
    total = carry[0:1, :] + jnp.sum(onehot, axis=0, keepdims=True)
    carry[...] = jnp.broadcast_to(total, carry.shape)
    cnt_ref[...] = jnp.broadcast_to(total, cnt_ref.shape)
    ridx_ref[...] = jnp.where(lane == 0, i1, jnp.where(lane == 1, i2, jnp.where(
        lane == 2, r1.astype(I32), jnp.where(lane == 3, r2.astype(I32), 0))))
    rw_ref[...] = jnp.where(lane == 0, w1, jnp.where(lane == 1, w2, 0.0))


def _outproj(mg, wo, x2, mods, nw, wr, tri, tm, t):
    m = x2.shape[0]
    nbm = t // tm
    return pl.pallas_call(
        _outproj_kernel,
        grid=(m // tm,),
        in_specs=[pl.BlockSpec((tm, D_MODEL), lambda i: (i, 0)),
                  pl.BlockSpec((D_MODEL, D_MODEL), lambda i: (0, 0)),
                  pl.BlockSpec((tm, D_MODEL), lambda i: (i, 0)),
                  pl.BlockSpec((1, N_MOD, D_MODEL), lambda i: (i // nbm, 0, 0)),
                  pl.BlockSpec((1, D_MODEL), lambda i: (0, 0)),
                  pl.BlockSpec((D_MODEL, 2 * LANES), lambda i: (0, 0)),
                  pl.BlockSpec((tm, tm), lambda i: (0, 0))],
        out_specs=[pl.BlockSpec((tm, D_MODEL), lambda i: (i, 0)),
                   pl.BlockSpec((tm * TOK_TILE, LANES), lambda i: (i, 0)),
                   pl.BlockSpec((tm, LANES), lambda i: (i, 0)),
                   pl.BlockSpec((tm, LANES), lambda i: (i, 0)),
                   pl.BlockSpec((8, LANES), lambda i: (0, 0))],
        out_shape=[jax.ShapeDtypeStruct((m, D_MODEL), F32),
                   jax.ShapeDtypeStruct((m * TOK_TILE, LANES), F32),
                   jax.ShapeDtypeStruct((m, LANES), I32),
                   jax.ShapeDtypeStruct((m, LANES), F32),
                   jax.ShapeDtypeStruct((8, LANES), F32)],
        scratch_shapes=[pltpu.VMEM((8, LANES), F32)],
        compiler_params=_params(("arbitrary",)),
        name="outproj_router",
    )(mg, wo, x2, mods, nw, wr, tri)


def _gather_rows(idx_ref, base, n, src, dst, sem):
    def body(r, carry):
        tok = idx_ref[base + r]
        pltpu.make_async_copy(src.at[pl.ds(pl.multiple_of(tok * TOK_TILE, TOK_TILE), TOK_TILE), :],
                              dst.at[pl.ds(pl.multiple_of(r * TOK_TILE, TOK_TILE), TOK_TILE), :],
                              sem).start()
        return carry

    lax.fori_loop(0, n, body, 0, unroll=8)


def _expert_kernel(blk_e_ref, tok_ref, h2_hbm, wg_ref, wu_ref, wd_ref, o_ref, xb, sem):
    i = pl.program_id(0)
    nblk = pl.num_programs(0)
    slot = i % 2
    rows = MOE_BLOCK

    @pl.when(i == 0)
    def _():
        _gather_rows(tok_ref, 0, rows, h2_hbm, xb.at[0], sem.at[0])

    @pl.when(i + 1 < nblk)
    def _():
        _gather_rows(tok_ref, (i + 1) * rows, rows, h2_hbm, xb.at[1 - slot], sem.at[1 - slot])

    pltpu.make_async_copy(h2_hbm.at[pl.ds(0, rows * TOK_TILE), :], xb.at[slot], sem.at[slot]).wait()
    x = jnp.concatenate([xb[slot, pl.ds(a, rows, stride=TOK_TILE), :] for a in range(TOK_TILE)],
                        axis=1).astype(BF16)
    gate = _dot(x, wg_ref[0])
    up = _dot(x, wu_ref[0])
    act = (_silu(gate) * up).astype(BF16)
    out = _dot(act, wd_ref[0])
    for a in range(TOK_TILE):
        o_ref[pl.ds(a, rows, stride=TOK_TILE), :] = out[:, a * LANES:(a + 1) * LANES]


def _experts(blk_e, slot_tok, h2tok, wg, wu, wd, nblk):
    rows = MOE_BLOCK
    grid_spec = pltpu.PrefetchScalarGridSpec(
        num_scalar_prefetch=2,
        grid=(nblk,),
        in_specs=[pl.BlockSpec(memory_space=pl.ANY),
                  pl.BlockSpec((1, D_MODEL, D_EXPERT), lambda i, be, st: (be[i], 0, 0)),
                  pl.BlockSpec((1, D_MODEL, D_EXPERT), lambda i, be, st: (be[i], 0, 0)),
                  pl.BlockSpec((1, D_EXPERT, D_MODEL), lambda i, be, st: (be[i], 0, 0))],
        out_specs=pl.BlockSpec((rows * TOK_TILE, LANES), lambda i, be, st: (i, 0)),
        scratch_shapes=[pltpu.VMEM((2, rows * TOK_TILE, LANES), F32),
                        pltpu.SemaphoreType.DMA((2,))],
    )
    return pl.pallas_call(
        _expert_kernel,
        grid_spec=grid_spec,
        out_shape=jax.ShapeDtypeStruct((nblk * rows * TOK_TILE, LANES), F32),
        compiler_params=_params(("arbitrary",)),
        name="expert_mlp",
    )(blk_e, slot_tok, h2tok, wg, wu, wd)


def _combine_kernel(pos_ref, eo_hbm, x1_ref, rw_ref, mod_ref, nf_ref, o_ref, gb, sem):
    i = pl.program_id(0)
    nsteps = pl.num_programs(0)
    tm = x1_ref.shape[0]
    slot = i % 2

    def issue(step, sl):
        for k in range(TOP_K):
            _gather_rows(pos_ref, (k * nsteps + step) * tm, tm, eo_hbm, gb.at[sl, k], sem.at[sl])

    @pl.when(i == 0)
    def _():
        issue(0, 0)

    @pl.when(i + 1 < nsteps)
    def _():
        issue(i + 1, 1 - slot)

    for k in range(TOP_K):
        pltpu.make_async_copy(eo_hbm.at[pl.ds(0, tm * TOK_TILE), :], gb.at[slot, k], sem.at[slot]).wait()
    w1 = rw_ref[:, 0:1]
    w2 = rw_ref[:, 1:2]
    g2 = mod_ref[0, 5:6, :]
    ssq = jnp.zeros((tm, 1), F32)
    for a in range(TOK_TILE):
        cols = slice(a * LANES, (a + 1) * LANES)
        y = (gb[slot, 0, pl.ds(a, tm, stride=TOK_TILE), :] * w1
             + gb[slot, 1, pl.ds(a, tm, stride=TOK_TILE), :] * w2)
        x2 = x1_ref[:, cols] + g2[:, cols] * y
        o_ref[:, cols] = x2
        ssq = ssq + jnp.sum(x2 * x2, axis=-1, keepdims=True)
    o_ref[...] = o_ref[...] * lax.rsqrt(ssq / D_MODEL + NORM_EPS) * nf_ref[...]


def _combine(pos, eo, x1, rw, mods, nf, tm, t):
    m = x1.shape[0]
    nbm = t // tm
    grid_spec = pltpu.PrefetchScalarGridSpec(
        num_scalar_prefetch=1,
        grid=(m // tm,),
        in_specs=[pl.BlockSpec(memory_space=pl.ANY),
                  pl.BlockSpec((tm, D_MODEL), lambda i, p: (i, 0)),
                  pl.BlockSpec((tm, LANES), lambda i, p: (i, 0)),
                  pl.BlockSpec((1, N_MOD, D_MODEL), lambda i, p: (i // nbm, 0, 0)),
                  pl.BlockSpec((1, D_MODEL), lambda i, p: (0, 0))],
        out_specs=pl.BlockSpec((tm, D_MODEL), lambda i, p: (i, 0)),
        scratch_shapes=[pltpu.VMEM((2, TOP_K, tm * TOK_TILE, LANES), F32),
                        pltpu.SemaphoreType.DMA((2,))],
    )
    return pl.pallas_call(
        _combine_kernel,
        grid_spec=grid_spec,
        out_shape=jax.ShapeDtypeStruct((m, D_MODEL), F32),
        compiler_params=_params(("arbitrary",)),
        name="moe_combine_norm",
    )(pos, eo, x1, rw, mods, nf)


def _rope_tables(t):
    half = ATTN_HEAD_DIM // 4
    pos = jnp.arange(t, dtype=I32)
    row = (pos // GRID_W).astype(F32)
    col = (pos % GRID_W).astype(F32)
    freqs = ROPE_BASE ** (-jnp.arange(half, dtype=F32) / half)
    ar = row[:, None] * freqs[None, :]
    ac = col[:, None] * freqs[None, :]
    cos_t = jnp.concatenate([jnp.cos(ar), jnp.cos(ar), jnp.cos(ac), jnp.cos(ac)], axis=1)
    sin_t = jnp.concatenate([-jnp.sin(ar), jnp.sin(ar), -jnp.sin(ac), jnp.sin(ac)], axis=1)
    return cos_t, sin_t


def _tri_pair(n):
    li = np.arange(n)[:, None]
    si = np.arange(n)[None, :]
    return jnp.asarray(np.stack([li >= si, li <= si]).astype(np.float32), BF16)


def _head_expand_matrix():
    k = np.arange(LANES)[:, None]
    j = np.arange(D_SSD)[None, :]
    return jnp.asarray((k == j // SSD_HEAD_DIM).astype(np.float32), BF16)


def _pad_heads(v):
    return jnp.pad(v.astype(F32), ((0, 0), (0, LANES - SSD_HEADS)))[:, None, :]


def kernel(x, c, ctx, c_ctx, w_ada, b_ada, norm_mix, norm_ffn, w_in, conv_w, conv_b, a_log, dt_bias,
           d_skip, ssd_norm, attn_sink, w_branch_ssd, w_branch_attn, w_out, w_route_group,
           w_route_expert, w_gate, w_up, w_down, norm_final):
    bsz, t, d = x.shape
    tc = ctx.shape[1]
    m = bsz * t
    l = 0

    ctx_row = bsz
    n_rows = -(-(bsz + 1) // 8) * 8
    cc = jnp.zeros((n_rows, d), F32).at[:bsz].set(c).at[ctx_row].set(c_ctx)
    mods = _ada(cc, w_ada[l], b_ada[l][None, :]).reshape(n_rows, N_MOD, d)

    wi = w_in[l]
    o_xbc = D_SSD
    o_dt = o_xbc + CONV_CH
    o_q = o_dt + 2 * SSD_HEADS
    o_k = o_q + D_ATTN
    o_v = o_k + D_KV
    o_g = o_v + D_KV
    w_xs = wi[:, o_xbc:o_xbc + D_SSD]
    w_bc = wi[:, o_xbc + D_SSD:o_dt]
    w_k = wi[:, o_k:o_v]
    w_v = wi[:, o_v:o_g]
    w_lat = jnp.concatenate([w_xs, wi[:, :D_SSD], wi[:, o_q:o_k], w_bc, w_k, w_v, wi[:, o_g:]],
                            axis=1).astype(BF16)
    w_ctx = jnp.concatenate([w_xs, w_bc, w_k, w_v], axis=1).astype(BF16)
    w_dt = jnp.zeros((d, 2 * LANES), F32)
    w_dt = w_dt.at[:, :SSD_HEADS].set(wi[:, o_dt:o_dt + SSD_HEADS])
    w_dt = w_dt.at[:, LANES:LANES + SSD_HEADS].set(wi[:, o_dt + SSD_HEADS:o_q]).astype(BF16)

    cos_t, sin_t = _rope_tables(t)
    nmix = norm_mix[l][None, :]
    tm_in = min(1024, t)
    u, dt2 = _inproj(x.reshape(m, d), nmix, mods, w_lat, w_dt, cos_t, sin_t,
                     rows_per_mod=t, mod_row0=0, tm=tm_in, rope=True)
    mc = bsz * tc
    uc, dtc2 = _inproj(ctx.reshape(mc, d), nmix, mods, w_ctx, w_dt, None, None,
                       rows_per_mod=0, mod_row0=ctx_row, tm=min(1024, mc), rope=False)

    cw = conv_w[l]
    cb = conv_b[l][None, :]
    xs_c, bc_c = _conv(u, U_XS, U_BC, cw, cb, t, min(256, t))
    xsc_c, bcc_c = _conv(uc, UC_XS, UC_BC, cw, cb, tc, min(256, tc))
    alog = _pad_heads(a_log[l])
    bias = _pad_heads(dt_bias[l])
    e_mat = _head_expand_matrix()
    h0 = _ctx_state(xsc_c, bcc_c, dtc2, alog, bias, _tri_pair(tc), e_mat, bsz, tc)
    dsk = jnp.repeat(d_skip[l].astype(F32), SSD_HEAD_DIM)[None, :]
    y_ssd = _ssd(xs_c, bc_c, dt2, alog, bias, _tri_pair(SSD_CHUNK), e_mat, h0, u, dsk,
                 ssd_norm[l][None, :], bsz, t)

    y_attn = _attn(attn_sink[l].astype(F32), u, uc, bsz, t, tc)

    merged = _merge(y_ssd, y_attn, u, w_branch_ssd[l].astype(BF16), w_branch_attn[l].astype(BF16),
                    min(512, t))

    tm_o = min(256, t)
    w_r = jnp.zeros((d, 2 * LANES), F32)
    w_r = w_r.at[:, :MOE_GROUPS].set(w_route_group[l])
    w_r = w_r.at[:, LANES:LANES + N_EXPERTS].set(w_route_expert[l]).astype(BF16)
    li = np.arange(tm_o)
    tri_strict = jnp.asarray((li[:, None] > li[None, :]).astype(np.float32), BF16)
    x1, h2tok, ridx, rw, cnt = _outproj(merged, w_out[l].astype(BF16), x.reshape(m, d), mods,
                                        norm_ffn[l][None, :], w_r, tri_strict, tm_o, t)

    na = m * TOP_K
    nblk = -(-(na + N_EXPERTS * (MOE_BLOCK - 1)) // MOE_BLOCK)
    counts = cnt[0, :N_EXPERTS].astype(I32)
    padded = (counts + MOE_BLOCK - 1) // MOE_BLOCK * MOE_BLOCK
    pend = jnp.cumsum(padded)
    pstart = pend - padded
    blk_e = jnp.clip(jnp.searchsorted(pend, jnp.arange(nblk, dtype=I32) * MOE_BLOCK, side='right'),
                     0, N_EXPERTS - 1).astype(I32)
    e_idx = ridx[:, 0:TOP_K]
    pos = pstart[e_idx] + ridx[:, TOP_K:2 * TOP_K]
    tok = jnp.broadcast_to(jnp.arange(m, dtype=I32)[:, None], (m, TOP_K))
    slot_tok = jnp.zeros((nblk * MOE_BLOCK,), I32).at[pos.reshape(-1)].set(tok.reshape(-1))

    eo = _experts(blk_e, slot_tok, h2tok, w_gate[l].astype(BF16), w_up[l].astype(BF16),
                  w_down[l].astype(BF16), nblk)

    tm_c = min(256, t)
    out = _combine(pos.T.reshape(-1), eo, x1, rw, mods, norm_final[None, :], tm_c, t)
    return out.reshape(bsz, t, d)
```

```python
import functools

import numpy as np
import jax
import jax.numpy as jnp
from jax import lax
from jax.experimental import pallas as pl
from jax.experimental.pallas import tpu as pltpu

F32 = jnp.float32
BF16 = jnp.bfloat16
I32 = jnp.int32

D_MODEL = 2048
GRID_W = 64
NORM_EPS = 1e-6
N_MOD = 6
SSD_HEADS = 16
SSD_HEAD_DIM = 64
D_SSD = SSD_HEADS * SSD_HEAD_DIM
SSD_GROUPS = 2
SSD_STATE = 128
SSD_CONV = 5
SSD_CHUNK = 128
CONV_CH = D_SSD + 2 * SSD_GROUPS * SSD_STATE
ATTN_HEADS = 8
ATTN_KV_HEADS = 2
ATTN_HEAD_DIM = 128
D_ATTN = ATTN_HEADS * ATTN_HEAD_DIM
D_KV = ATTN_KV_HEADS * ATTN_HEAD_DIM
WINDOW = 128
ROPE_BASE = 10000.0
N_BRANCH = 2
MOE_GROUPS = 4
EXPERTS_PER_GROUP = 8
N_EXPERTS = MOE_GROUPS * EXPERTS_PER_GROUP
TOP_K = 2
D_EXPERT = 512
MOE_BLOCK = 256

LANES = 128
VMEM_LIMIT = 56 * 1024 * 1024

U_XS, U_Z, U_Q, U_BC, U_K, U_V, U_G = 0, 1024, 2048, 3072, 3584, 3840, 4096
U_W = 8192
UC_XS, UC_BC, UC_K, UC_V = 0, 1024, 1536, 1792
UC_W = 2048
IN_TN = 1024


def _params(sem):
    return pltpu.CompilerParams(dimension_semantics=sem, vmem_limit_bytes=VMEM_LIMIT)


def _silu(v):
    return v * jax.nn.sigmoid(v)


def _split_bf16(v, n):
    parts = []
    r = v
    for _ in range(n):
        p = r.astype(BF16)
        parts.append(p)
        r = r - p.astype(F32)
    return parts


def _dot(a, b):
    return jnp.dot(a, b, preferred_element_type=F32)


def _dot_nt(a, b):
    return lax.dot_general(a, b, (((1,), (1,)), ((), ())), preferred_element_type=F32)


def _ada_kernel(c_ref, w_ref, b_ref, o_ref):
    a = _silu(c_ref[...]).astype(BF16)
    o_ref[...] = _dot(a, w_ref[...].astype(BF16)) + b_ref[...]


def _ada(cc, w, b):
    rows, d = cc.shape
    n = w.shape[1]
    tn = 1024
    return pl.pallas_call(
        _ada_kernel,
        grid=(n // tn,),
        in_specs=[pl.BlockSpec((rows, d), lambda j: (0, 0)),
                  pl.BlockSpec((d, tn), lambda j: (0, j)),
                  pl.BlockSpec((1, tn), lambda j: (0, j))],
        out_specs=pl.BlockSpec((rows, tn), lambda j: (0, j)),
        out_shape=jax.ShapeDtypeStruct((rows, n), F32),
        compiler_params=_params(("arbitrary",)),
        name="ada",
    )(cc, w, b)


def _rope(a, cos, sin_signed, first):
    partner = jnp.where(first, pltpu.roll(a, 96, 1), pltpu.roll(a, 32, 1))
    return a * cos + partner * sin_signed


def _inproj_kernel(*refs, rope_q_blocks, rope_kv_block, sub):
    if rope_q_blocks:
        x_ref, nw_ref, mod_ref, w_ref, wdt_ref, cos_ref, sin_ref, o_ref, dt_ref, h_scr = refs
    else:
        x_ref, nw_ref, mod_ref, w_ref, wdt_ref, o_ref, dt_ref, h_scr = refs
    j = pl.program_id(1)
    tm = x_ref.shape[0]
    tn = w_ref.shape[1]

    @pl.when(j == 0)
    def _():
        nw = nw_ref[...]
        sh = mod_ref[0, 0:1, :]
        sc = mod_ref[0, 1:2, :]

        def body(r, carry):
            rows = pl.ds(pl.multiple_of(r * sub, sub), sub)
            xf = x_ref[rows, :]
            ms = jnp.mean(xf * xf, axis=-1, keepdims=True)
            h = (xf * lax.rsqrt(ms + NORM_EPS) * nw) * (1.0 + sc) + sh
            hb = h.astype(BF16)
            h_scr[rows, :] = hb
            d = _dot(hb, wdt_ref[...])
            dt_ref[0, rows, :] = d[:, :LANES]
            dt_ref[1, rows, :] = d[:, LANES:]
            return carry

        lax.fori_loop(0, tm // sub, body, 0)

    acc = _dot(h_scr[...], w_ref[...])

    def store(lo, hi):
        if hi > lo:
            cos = cos_ref[...]
            sin = sin_ref[...]
            lane = lax.broadcasted_iota(I32, (tm, LANES), 1)
            first = (lane % 64) < 32
        for hh in range(tn // LANES):
            a = acc[:, hh * LANES:(hh + 1) * LANES]
            if lo <= hh < hi:
                a = _rope(a, cos, sin, first)
            o_ref[:, hh * LANES:(hh + 1) * LANES] = a.astype(BF16)

    if rope_q_blocks:
        is_q = (j >= rope_q_blocks[0]) & (j <= rope_q_blocks[-1])
        is_kv = j == rope_kv_block
        k_lo = (U_K % tn) // LANES

        @pl.when(is_q)
        def _():
            store(0, tn // LANES)

        @pl.when(is_kv)
        def _():
            store(k_lo, k_lo + D_KV // LANES)

        @pl.when(jnp.logical_not(is_q | is_kv))
        def _():
            store(0, 0)
    else:
        store(0, 0)


def _inproj(x2, nw, mods, w, wdt, cos_t, sin_t, *, rows_per_mod, mod_row0, tm, rope):
    m, d = x2.shape
    n = w.shape[1]
    tn = IN_TN
    nbm = rows_per_mod // tm if rows_per_mod else 0
    if rows_per_mod:
        mod_map = lambda i, j: (mod_row0 + i // nbm, 0, 0)
    else:
        mod_map = lambda i, j: (mod_row0, 0, 0)
    in_specs = [pl.BlockSpec((tm, d), lambda i, j: (i, 0)),
                pl.BlockSpec((1, d), lambda i, j: (0, 0)),
                pl.BlockSpec((1, N_MOD, d), mod_map),
                pl.BlockSpec((d, tn), lambda i, j: (0, j)),
                pl.BlockSpec((d, 2 * LANES), lambda i, j: (0, 0))]
    args = [x2, nw, mods, w, wdt]
    if rope:
        in_specs += [pl.BlockSpec((tm, LANES), lambda i, j: (i % nbm, 0)),
                     pl.BlockSpec((tm, LANES), lambda i, j: (i % nbm, 0))]
        args += [cos_t, sin_t]
        rq = tuple(range(U_Q // tn, (U_Q + D_ATTN) // tn))
        rkv = U_K // tn
    else:
        rq, rkv = (), None
    kern = functools.partial(_inproj_kernel, rope_q_blocks=rq, rope_kv_block=rkv, sub=128)
    return pl.pallas_call(
        kern,
        grid=(m // tm, n // tn),
        in_specs=in_specs,
        out_specs=[pl.BlockSpec((tm, tn), lambda i, j: (i, j)),
                   pl.BlockSpec((2, tm, LANES), lambda i, j: (0, i, 0))],
        out_shape=[jax.ShapeDtypeStruct((m, n), BF16),
                   jax.ShapeDtypeStruct((2, m, LANES), F32)],
        scratch_shapes=[pltpu.VMEM((tm, d), BF16)],
        compiler_params=_params(("arbitrary", "arbitrary")),
        name="inproj_rope" if rope else "inproj_ctx",
    )(*args)


CONV_HALO = 16


def _conv_kernel(xs_ref, bc_ref, pxs_ref, pbc_ref, nxs_ref, nbc_ref, w_ref, b_ref,
                 oxs_ref, obc_ref, ext, *, blocks_per_seq):
    i = pl.program_id(0)
    r = xs_ref.shape[0]
    pos = i % blocks_per_seq
    not_first = pos != 0
    not_last = pos != blocks_per_seq - 1
    h = CONV_HALO
    ext[0:h, 0:D_SSD] = jnp.where(not_first, pxs_ref[...].astype(F32), 0.0)
    ext[0:h, D_SSD:] = jnp.where(not_first, pbc_ref[...].astype(F32), 0.0)
    ext[h:h + r, 0:D_SSD] = xs_ref[...].astype(F32)
    ext[h:h + r, D_SSD:] = bc_ref[...].astype(F32)
    ext[h + r:, 0:D_SSD] = jnp.where(not_last, nxs_ref[...].astype(F32), 0.0)
    ext[h + r:, D_SSD:] = jnp.where(not_last, nbc_ref[...].astype(F32), 0.0)
    cw = 256
    pad = SSD_CONV // 2
    for cb in range(0, CONV_CH, cw):
        acc = jnp.broadcast_to(b_ref[:, cb:cb + cw], (r, cw))
        for k in range(SSD_CONV):
            acc = acc + ext[h - pad + k:h - pad + k + r, cb:cb + cw] * w_ref[k:k + 1, cb:cb + cw]
        y = _silu(acc)
        if cb < D_SSD:
            oxs_ref[:, cb:cb + cw] = y
        else:
            obc_ref[:, cb - D_SSD:cb - D_SSD + cw] = y.astype(BF16)


def _conv(u, xs_col, bc_col, conv_w, conv_b, seq_len, r):
    m = u.shape[0]
    h = CONV_HALO
    bps = seq_len // r
    wbc = CONV_CH - D_SSD
    xs_c, bc_c = xs_col // D_SSD, bc_col // wbc
    nh = m // h
    prev = lambda i: jnp.maximum(i * (r // h) - 1, 0)
    nxt = lambda i: jnp.minimum((i + 1) * (r // h), nh - 1)
    return pl.pallas_call(
        functools.partial(_conv_kernel, blocks_per_seq=bps),
        grid=(m // r,),
        in_specs=[pl.BlockSpec((r, D_SSD), lambda i: (i, xs_c)),
                  pl.BlockSpec((r, wbc), lambda i: (i, bc_c)),
                  pl.BlockSpec((h, D_SSD), lambda i: (prev(i), xs_c)),
                  pl.BlockSpec((h, wbc), lambda i: (prev(i), bc_c)),
                  pl.BlockSpec((h, D_SSD), lambda i: (nxt(i), xs_c)),
                  pl.BlockSpec((h, wbc), lambda i: (nxt(i), bc_c)),
                  pl.BlockSpec((SSD_CONV, CONV_CH), lambda i: (0, 0)),
                  pl.BlockSpec((1, CONV_CH), lambda i: (0, 0))],
        out_specs=[pl.BlockSpec((r, D_SSD), lambda i: (i, 0)),
                   pl.BlockSpec((r, wbc), lambda i: (i, 0))],
        out_shape=[jax.ShapeDtypeStruct((m, D_SSD), F32),
                   jax.ShapeDtypeStruct((m, wbc), BF16)],
        scratch_shapes=[pltpu.VMEM((r + 2 * h, CONV_CH), F32)],
        compiler_params=_params(("arbitrary",)),
        name="conv_silu",
    )(u, u, u, u, u, u, conv_w, conv_b)


def _expand_heads(v, e):
    hi, lo = _split_bf16(v, 2)
    return _dot(hi, e) + _dot(lo, e)


def _cumsum_mm(tri, da):
    p0, p1, p2 = _split_bf16(da, 3)
    return _dot(tri, p0) + _dot(tri, p1) + _dot(tri, p2)


def _ctx_state_kernel(xs_ref, b_ref, dt_ref, alog_ref, bias_ref, tri_ref, e_ref, h_ref):
    d = pl.program_id(1)
    tc = xs_ref.shape[0]
    dt = jax.nn.softplus(dt_ref[0] + bias_ref[0])
    a = -jnp.exp(alog_ref[0])
    cum = _cumsum_mm(tri_ref[0], dt * a)
    cum_end = jnp.where(d == 0, cum[tc - 1:tc, :], cum[0:1, :])
    e = e_ref[...]
    dtx = _expand_heads(dt, e)
    dex = _expand_heads(jnp.exp(cum_end - cum), e)
    xw = (xs_ref[...] * dtx * dex).astype(BF16)
    gw = D_SSD // SSD_GROUPS
    for g in range(SSD_GROUPS):
        bgt = b_ref[:, g * SSD_STATE:(g + 1) * SSD_STATE].astype(F32).T.astype(BF16)
        h_ref[0, 0, :, g * gw:(g + 1) * gw] = _dot(bgt, xw[:, g * gw:(g + 1) * gw])


def _ctx_state(xs_c, bc_c, dt2, alog, bias, tri, e, bsz, tc):
    return pl.pallas_call(
        _ctx_state_kernel,
        grid=(bsz, 2),
        in_specs=[pl.BlockSpec((tc, D_SSD), lambda b, d: (b, 0)),
                  pl.BlockSpec((tc, SSD_GROUPS * SSD_STATE), lambda b, d: (b, 0)),
                  pl.BlockSpec((1, tc, LANES), lambda b, d: (d, b, 0)),
                  pl.BlockSpec((1, 1, LANES), lambda b, d: (d, 0, 0)),
                  pl.BlockSpec((1, 1, LANES), lambda b, d: (d, 0, 0)),
                  pl.BlockSpec((1, tc, tc), lambda b, d: (d, 0, 0)),
                  pl.BlockSpec((LANES, D_SSD), lambda b, d: (0, 0))],
        out_specs=pl.BlockSpec((1, 1, SSD_STATE, D_SSD), lambda b, d: (b, d, 0, 0)),
        out_shape=jax.ShapeDtypeStruct((bsz, 2, SSD_STATE, D_SSD), F32),
        compiler_params=_params(("arbitrary", "arbitrary")),
        name="ssd_ctx_state",
    )(xs_c, bc_c, dt2, alog, bias, tri, e)


def _ssd_kernel(xs_ref, bc_ref, dt_ref, alog_ref, bias_ref, tri_ref, e_ref, h0_ref, z_ref,
                dsk_ref, nw_ref, o_ref, st, yf, ycur, *, nc):
    d = pl.program_id(1)
    s = pl.program_id(2)
    lc = SSD_CHUNK
    c = jnp.where(d == 0, s, nc - 1 - s)
    is_f = d == 0

    @pl.when(s == 0)
    def _():
        st[...] = h0_ref[0, 0]

    xs = xs_ref[...]
    dt = jax.nn.softplus(dt_ref[0] + bias_ref[0])
    a = -jnp.exp(alog_ref[0])
    cum = _cumsum_mm(tri_ref[0], dt * a)
    cum_t = cum.T
    cum_end = jnp.where(is_f, cum[lc - 1:lc, :], cum[0:1, :])
    e = e_ref[...]
    ecx = _expand_heads(jnp.exp(cum), e)
    dex = _expand_heads(jnp.exp(cum_end - cum), e)
    dtx = _expand_heads(dt, e)
    xdt = xs * dtx
    xdt_b = xdt.astype(BF16)
    xde_b = (xdt * dex).astype(BF16)
    cd_row = jnp.where(is_f, ecx[lc - 1:lc, :], ecx[0:1, :])
    li = lax.broadcasted_iota(I32, (lc, lc), 0)
    si = lax.broadcasted_iota(I32, (lc, lc), 1)
    causal = (li - si) * jnp.where(is_f, 1, -1) >= 0
    lane = lax.broadcasted_iota(I32, (lc, LANES), 1)
    low = lane < SSD_HEAD_DIM
    gw = D_SSD // SSD_GROUPS
    hpg = SSD_HEADS // SSD_GROUPS
    for g in range(SSD_GROUPS):
        bg = bc_ref[:, g * SSD_STATE:(g + 1) * SSD_STATE]
        cg = bc_ref[:, (SSD_GROUPS + g) * SSD_STATE:(SSD_GROUPS + g + 1) * SSD_STATE]
        cb = _dot_nt(cg, bg)
        yoff = _dot(cg, st[:, g * gw:(g + 1) * gw].astype(BF16)) * ecx[:, g * gw:(g + 1) * gw]
        for pr in range(hpg // 2):
            h0 = g * hpg + 2 * pr
            ms = []
            for h in (h0, h0 + 1):
                seg = cum[:, h:h + 1] - cum_t[h:h + 1, :]
                ms.append((cb * jnp.exp(jnp.where(causal, seg, -jnp.inf))).astype(BF16))
            lhs = jnp.concatenate(ms, axis=1)
            xp = xdt_b[:, h0 * SSD_HEAD_DIM:(h0 + 2) * SSD_HEAD_DIM]
            zero = jnp.zeros_like(xp)
            rhs = jnp.concatenate([jnp.where(low, xp, zero), jnp.where(low, zero, xp)], axis=0)
            ycur[:, h0 * SSD_HEAD_DIM:(h0 + 2) * SSD_HEAD_DIM] = (
                _dot(lhs, rhs) + yoff[:, pr * LANES:(pr + 1) * LANES])
        bgt = bg.astype(F32).T.astype(BF16)
        st[:, g * gw:(g + 1) * gw] = (st[:, g * gw:(g + 1) * gw] * cd_row[:, g * gw:(g + 1) * gw]
                                      + _dot(bgt, xde_b[:, g * gw:(g + 1) * gw]))

    rows = pl.ds(pl.multiple_of(c * lc, lc), lc)

    @pl.when(is_f)
    def _():
        yf[rows, :] = ycur[...]

    @pl.when(jnp.logical_not(is_f))
    def _():
        y = ycur[...] + yf[rows, :] + dsk_ref[...] * xs
        gy = y * _silu(z_ref[...].astype(F32))
        for g in range(SSD_GROUPS):
            blk = gy[:, g * gw:(g + 1) * gw]
            ms = jnp.mean(blk * blk, axis=-1, keepdims=True)
            o_ref[:, g * gw:(g + 1) * gw] = (blk * lax.rsqrt(ms + NORM_EPS)
                                             * nw_ref[:, g * gw:(g + 1) * gw]).astype(BF16)


def _ssd(xs_c, bc_c, dt2, alog, bias, tri, e, h0, u, dsk, nw, bsz, t):
    nc = t // SSD_CHUNK
    lc = SSD_CHUNK
    m = bsz * t
    chunk = lambda b, d, s: b * nc + jnp.where(d == 0, s, nc - 1 - s)
    late = lambda b, d, s: b * nc + jnp.where(d == 0, nc - 1, nc - 1 - s)
    zc = U_Z // D_SSD
    return pl.pallas_call(
        functools.partial(_ssd_kernel, nc=nc),
        grid=(bsz, 2, nc),
        in_specs=[pl.BlockSpec((lc, D_SSD), lambda b, d, s: (chunk(b, d, s), 0)),
                  pl.BlockSpec((lc, 2 * SSD_GROUPS * SSD_STATE), lambda b, d, s: (chunk(b, d, s), 0)),
                  pl.BlockSpec((1, lc, LANES), lambda b, d, s: (d, chunk(b, d, s), 0)),
                  pl.BlockSpec((1, 1, LANES), lambda b, d, s: (d, 0, 0)),
                  pl.BlockSpec((1, 1, LANES), lambda b, d, s: (d, 0, 0)),
                  pl.BlockSpec((1, lc, lc), lambda b, d, s: (d, 0, 0)),
                  pl.BlockSpec((LANES, D_SSD), lambda b, d, s: (0, 0)),
                  pl.BlockSpec((1, 1, SSD_STATE, D_SSD), lambda b, d, s: (b, d, 0, 0)),
                  pl.BlockSpec((lc, D_SSD), lambda b, d, s: (late(b, d, s), zc)),
                  pl.BlockSpec((1, D_SSD), lambda b, d, s: (0, 0)),
                  pl.BlockSpec((1, D_SSD), lambda b, d, s: (0, 0))],
        out_specs=pl.BlockSpec((lc, D_SSD), lambda b, d, s: (late(b, d, s), 0)),
        out_shape=jax.ShapeDtypeStruct((m, D_SSD), BF16),
        scratch_shapes=[pltpu.VMEM((SSD_STATE, D_SSD), F32),
                        pltpu.VMEM((t, D_SSD), F32),
                        pltpu.VMEM((lc, D_SSD), F32)],
        compiler_params=_params(("arbitrary", "arbitrary", "arbitrary")),
        name="ssd_scan",
    )(xs_c, bc_c, dt2, alog, bias, tri, e, h0, u, dsk, nw)


def _attn_kernel(sink_ref, q_ref, k_ref, v_ref, kc_ref, vc_ref, o_ref, kp, vp, bias, *, t):
    g = pl.program_id(1)
    w = WINDOW
    nb = t // w
    rep = ATTN_HEADS // ATTN_KV_HEADS
    tc = kc_ref.shape[0]
    nk = 3 * w + tc
    zeros = jnp.zeros((w, ATTN_HEAD_DIM), BF16)
    kp[0:w, :] = zeros
    kp[w:w + t, :] = k_ref[...]
    kp[w + t:, :] = zeros
    vp[0:w, :] = zeros
    vp[w:w + t, :] = v_ref[...]
    vp[w + t:, :] = zeros
    qi = lax.broadcasted_iota(I32, (rep * w, nk), 0) % w
    kj = lax.broadcasted_iota(I32, (rep * w, nk), 1)
    in_band = (jnp.abs(qi - (kj - w)) <= w) | (kj >= 3 * w)
    bias[...] = jnp.where(in_band, 0.0, -jnp.inf).astype(F32)
    sink_col = jnp.concatenate(
        [jnp.full((w, 1), sink_ref[g * rep + h], F32) for h in range(rep)], axis=0)
    scale = ATTN_HEAD_DIM ** -0.5
    col = lax.broadcasted_iota(I32, (1, nk), 1)

    def body(n, carry):
        rows = pl.ds(pl.multiple_of(n * w, w), w)
        q = jnp.concatenate([q_ref[rows, h * ATTN_HEAD_DIM:(h + 1) * ATTN_HEAD_DIM]
                             for h in range(rep)], axis=0)
        band = pl.ds(pl.multiple_of(n * w, w), 3 * w)
        kall = jnp.concatenate([kp[band, :], kc_ref[...]], axis=0)
        vall = jnp.concatenate([vp[band, :], vc_ref[...]], axis=0)
        lo = jnp.where(n == 0, w, 0)
        hi = jnp.where(n == nb - 1, 2 * w, 3 * w)
        outside = (col < lo) | ((col >= hi) & (col < 3 * w))
        edge = jnp.where(outside, -jnp.inf, 0.0).astype(F32)
        sc = _dot_nt(q, kall) * scale + bias[...] + edge
        mx = jnp.maximum(jnp.max(sc, axis=-1, keepdims=True), sink_col)
        p = jnp.exp(sc - mx)
        den = jnp.sum(p, axis=-1, keepdims=True) + jnp.exp(sink_col - mx)
        o = _dot(p.astype(BF16), vall) / den
        for h in range(rep):
            o_ref[rows, h * ATTN_HEAD_DIM:(h + 1) * ATTN_HEAD_DIM] = o[h * w:(h + 1) * w, :].astype(BF16)
        return carry

    lax.fori_loop(0, nb, body, 0)


def _attn(sink, u, uc, bsz, t, tc):
    rep = ATTN_HEADS // ATTN_KV_HEADS
    qw = rep * ATTN_HEAD_DIM
    hd = ATTN_HEAD_DIM
    return pl.pallas_call(
        functools.partial(_attn_kernel, t=t),
        grid=(bsz, ATTN_KV_HEADS),
        in_specs=[pl.BlockSpec(memory_space=pltpu.SMEM),
                  pl.BlockSpec((t, qw), lambda b, g: (b, U_Q // qw + g)),
                  pl.BlockSpec((t, hd), lambda b, g: (b, U_K // hd + g)),
                  pl.BlockSpec((t, hd), lambda b, g: (b, U_V // hd + g)),
                  pl.BlockSpec((tc, hd), lambda b, g: (b, UC_K // hd + g)),
                  pl.BlockSpec((tc, hd), lambda b, g: (b, UC_V // hd + g))],
        out_specs=pl.BlockSpec((t, qw), lambda b, g: (b, g)),
        out_shape=jax.ShapeDtypeStruct((bsz * t, D_ATTN), BF16),
        scratch_shapes=[pltpu.VMEM((t + 2 * WINDOW, hd), BF16),
                        pltpu.VMEM((t + 2 * WINDOW, hd), BF16),
                        pltpu.VMEM((rep * WINDOW, 3 * WINDOW + tc), F32)],
        compiler_params=_params(("arbitrary", "arbitrary")),
        name="window_attn",
    )(sink, u, u, u, uc, uc)


def _merge_kernel(ys_ref, ya_ref, g0_ref, g1_ref, ws_ref, wa_ref, o_ref):
    ys = ys_ref[...]
    ya = ya_ref[...]
    cw = 512
    for cb in range(0, D_MODEL, cw):
        t0 = _dot(ys, ws_ref[:, cb:cb + cw])
        t1 = _dot(ya, wa_ref[:, cb:cb + cw])
        g0 = jax.nn.sigmoid(g0_ref[:, cb:cb + cw].astype(F32))
        g1 = jax.nn.sigmoid(g1_ref[:, cb:cb + cw].astype(F32))
        o_ref[:, cb:cb + cw] = (g0 * t0 + g1 * t1).astype(BF16)


def _merge(ys, ya, u, ws, wa, tm):
    m = ys.shape[0]
    gc = U_G // D_MODEL
    return pl.pallas_call(
        _merge_kernel,
        grid=(m // tm,),
        in_specs=[pl.BlockSpec((tm, D_SSD), lambda i: (i, 0)),
                  pl.BlockSpec((tm, D_ATTN), lambda i: (i, 0)),
                  pl.BlockSpec((tm, D_MODEL), lambda i: (i, gc)),
                  pl.BlockSpec((tm, D_MODEL), lambda i: (i, gc + 1)),
                  pl.BlockSpec((D_SSD, D_MODEL), lambda i: (0, 0)),
                  pl.BlockSpec((D_ATTN, D_MODEL), lambda i: (0, 0))],
        out_specs=pl.BlockSpec((tm, D_MODEL), lambda i: (i, 0)),
        out_shape=jax.ShapeDtypeStruct((m, D_MODEL), BF16),
        compiler_params=_params(("arbitrary",)),
        name="branch_merge",
    )(ys, ya, u, u, ws, wa)


def _lane_min_index(mask, lane):
    return jnp.min(jnp.where(mask, lane, LANES), axis=-1, keepdims=True)


def _outproj_kernel(mg_ref, wo_ref, x_ref, mod_ref, nw_ref, wr_ref, tri_ref,
                    x1_ref, h2_ref, ridx_ref, rw_ref, cnt_ref, carry):
    i = pl.program_id(0)
    tm = x_ref.shape[0]

    @pl.when(i == 0)
    def _():
        carry[...] = jnp.zeros_like(carry)

    g1 = mod_ref[0, 2:3, :]
    sh2 = mod_ref[0, 3:4, :]
    sc2 = mod_ref[0, 4:5, :]
    x1 = x_ref[...] + g1 * _dot(mg_ref[...], wo_ref[...])
    x1_ref[...] = x1
    ms = jnp.mean(x1 * x1, axis=-1, keepdims=True)
    h2 = (x1 * lax.rsqrt(ms + NORM_EPS) * nw_ref[...]) * (1.0 + sc2) + sh2
    h2_ref[...] = h2

    lg = _dot(h2.astype(BF16), wr_ref[...])
    lane = lax.broadcasted_iota(I32, (tm, LANES), 1)
    gl = jnp.where(lane < MOE_GROUPS, lg[:, :LANES], -jnp.inf)
    gmax = jnp.max(gl, axis=-1, keepdims=True)
    g_w = 1.0 / jnp.sum(jnp.exp(gl - gmax), axis=-1, keepdims=True)
    g_idx = _lane_min_index(gl == gmax, lane)
    el = jnp.where((lane // EXPERTS_PER_GROUP) == g_idx, lg[:, LANES:], -jnp.inf)
    v1 = jnp.max(el, axis=-1, keepdims=True)
    i1 = _lane_min_index(el == v1, lane)
    el2 = jnp.where(lane == i1, -jnp.inf, el)
    v2 = jnp.max(el2, axis=-1, keepdims=True)
    i2 = _lane_min_index(el2 == v2, lane)
    e2 = jnp.exp(v2 - v1)
    w1 = g_w * (1.0 / (1.0 + e2))
    w2 = g_w * (e2 / (1.0 + e2))

    hit1 = lane == i1
    hit2 = lane == i2
    onehot = jnp.where(hit1 | hit2, 1.0, 0.0).astype(F32)
    before = _dot(tri_ref[...], onehot.astype(BF16)) + carry[0:1, :]
    r1 = jnp.sum(jnp.where(hit1, before, 0.0), axis=-1, keepdims=True)
    r2 = jnp.sum(jnp.where(hit2, before, 0.0), axis=-1, keepdims=True)
    total = carry[0:1, :] + jnp.sum(onehot, axis=0, keepdims=True)
    carry[...] = jnp.broadcast_to(total, carry.shape)
    cnt_ref[...] = jnp.broadcast_to(total, cnt_ref.shape)
    packed = jnp.where(lane == 0, i1.astype(F32), jnp.where(lane == 1, i2.astype(F32), jnp.where(
        lane == 2, r1, jnp.where(lane == 3, r2, 0.0))))
    ridx_ref[...] = packed.T[0:8, :].astype(I32)
    rw_ref[...] = jnp.where(lane == 0, w1, jnp.where(lane == 1, w2, 0.0))


def _outproj(mg, wo, x2, mods, nw, wr, tri, tm, t):
    m = x2.shape[0]
    nbm = t // tm
    return pl.pallas_call(
        _outproj_kernel,
        grid=(m // tm,),
        in_specs=[pl.BlockSpec((tm, D_MODEL), lambda i: (i, 0)),
                  pl.BlockSpec((D_MODEL, D_MODEL), lambda i: (0, 0)),
                  pl.BlockSpec((tm, D_MODEL), lambda i: (i, 0)),
                  pl.BlockSpec((1, N_MOD, D_MODEL), lambda i: (i // nbm, 0, 0)),
                  pl.BlockSpec((1, D_MODEL), lambda i: (0, 0)),
                  pl.BlockSpec((D_MODEL, 2 * LANES), lambda i: (0, 0)),
                  pl.BlockSpec((tm, tm), lambda i: (0, 0))],
        out_specs=[pl.BlockSpec((tm, D_MODEL), lambda i: (i, 0)),
                   pl.BlockSpec((tm, D_MODEL), lambda i: (i, 0)),
                   pl.BlockSpec((8, tm), lambda i: (0, i)),
                   pl.BlockSpec((tm, LANES), lambda i: (i, 0)),
                   pl.BlockSpec((8, LANES), lambda i: (0, 0))],
        out_shape=[jax.ShapeDtypeStruct((m, D_MODEL), F32),
                   jax.ShapeDtypeStruct((m, D_MODEL), F32),
                   jax.ShapeDtypeStruct((8, m), I32),
                   jax.ShapeDtypeStruct((m, LANES), F32),
                   jax.ShapeDtypeStruct((8, LANES), F32)],
        scratch_shapes=[pltpu.VMEM((8, LANES), F32)],
        compiler_params=_params(("arbitrary",)),
        name="outproj_router",
    )(mg, wo, x2, mods, nw, wr, tri)


def _gather_rows(idx_ref, base, n, src, dst, sem, row0=0):
    for r in range(n):
        tok = idx_ref[base + r]
        pltpu.make_async_copy(src.at[pl.ds(tok, 1), :], dst.at[pl.ds(row0 + r, 1), :],
                              sem).start(priority=r % 2)


EXPERT_SPLIT = 2


def _expert_kernel(blk_e_ref, tok_ref, nused_ref, h2_hbm, wg_ref, wu_ref, wd_ref, o_ref,
                   xb, xbf, act, sem):
    i = pl.program_id(0)
    n_used = nused_ref[0]
    slot = i % 2
    rows = MOE_BLOCK

    def wait_block(sl):
        pltpu.make_async_copy(h2_hbm.at[pl.ds(0, rows), :], xb.at[sl], sem.at[sl]).wait()

    @pl.when(i == 0)
    def _():
        _gather_rows(tok_ref, 0, rows, h2_hbm, xb.at[0], sem.at[0])

    @pl.when(i < n_used)
    def _():
        wait_block(slot)
        xbf[...] = xb[slot].astype(BF16)
        nxt = jnp.minimum(i + 1, n_used - 1)
        part = rows // EXPERT_SPLIT

        def body(c, carry):
            x = xbf[...]
            act[c] = (_silu(_dot(x, wg_ref[0, c])) * _dot(x, wu_ref[0, c])).astype(BF16)
            _gather_rows(tok_ref, nxt * rows + c * part, part, h2_hbm, xb.at[1 - slot],
                         sem.at[1 - slot], row0=c * part)
            return carry

        lax.fori_loop(0, EXPERT_SPLIT, body, 0)
        a = jnp.concatenate([act[c] for c in range(EXPERT_SPLIT)], axis=1)
        o_ref[...] = _dot(a, wd_ref[0])

        @pl.when(i == n_used - 1)
        def _():
            wait_block(1 - slot)

    @pl.when(i >= n_used)
    def _():
        o_ref[...] = jnp.zeros_like(o_ref)


def _experts(blk_e, slot_tok, n_used, h2, wg, wu, wd, nblk):
    rows = MOE_BLOCK
    cw = D_EXPERT // EXPERT_SPLIT
    live = lambda i, nu: jnp.minimum(i, nu[0] - 1)
    grid_spec = pltpu.PrefetchScalarGridSpec(
        num_scalar_prefetch=3,
        grid=(nblk,),
        in_specs=[pl.BlockSpec(memory_space=pl.ANY),
                  pl.BlockSpec((1, EXPERT_SPLIT, D_MODEL, cw),
                               lambda i, be, st, nu: (be[live(i, nu)], 0, 0, 0)),
                  pl.BlockSpec((1, EXPERT_SPLIT, D_MODEL, cw),
                               lambda i, be, st, nu: (be[live(i, nu)], 0, 0, 0)),
                  pl.BlockSpec((1, D_EXPERT, D_MODEL), lambda i, be, st, nu: (be[live(i, nu)], 0, 0))],
        out_specs=pl.BlockSpec((rows, D_MODEL), lambda i, be, st, nu: (i, 0)),
        scratch_shapes=[pltpu.VMEM((2, rows, D_MODEL), F32),
                        pltpu.VMEM((rows, D_MODEL), BF16),
                        pltpu.VMEM((EXPERT_SPLIT, rows, cw), BF16),
                        pltpu.SemaphoreType.DMA((2,))],
    )
    return pl.pallas_call(
        _expert_kernel,
        grid_spec=grid_spec,
        out_shape=jax.ShapeDtypeStruct((nblk * rows, D_MODEL), F32),
        compiler_params=_params(("arbitrary",)),
        name="expert_mlp",
    )(blk_e, slot_tok, n_used, h2, wg, wu, wd)


def _combine_kernel(pos_ref, eo_hbm, x1_ref, rw_ref, mod_ref, nf_ref, o_ref, gb, sem):
    i = pl.program_id(0)
    nsteps = pl.num_programs(0)
    tm = x1_ref.shape[0]
    slot = i % 2

    def issue(step, sl):
        for k in range(TOP_K):
            _gather_rows(pos_ref, (k * nsteps + step) * tm, tm, eo_hbm, gb.at[sl, k], sem.at[sl])

    @pl.when(i == 0)
    def _():
        issue(0, 0)

    @pl.when(i + 1 < nsteps)
    def _():
        issue(i + 1, 1 - slot)

    for k in range(TOP_K):
        pltpu.make_async_copy(eo_hbm.at[pl.ds(0, tm), :], gb.at[slot, k], sem.at[slot]).wait()
    w1 = rw_ref[:, 0:1]
    w2 = rw_ref[:, 1:2]
    g2 = mod_ref[0, 5:6, :]
    x2 = x1_ref[...] + g2 * (gb[slot, 0] * w1 + gb[slot, 1] * w2)
    ms = jnp.mean(x2 * x2, axis=-1, keepdims=True)
    o_ref[...] = x2 * lax.rsqrt(ms + NORM_EPS) * nf_ref[...]


def _combine(pos, eo, x1, rw, mods, nf, tm, t):
    m = x1.shape[0]
    nbm = t // tm
    grid_spec = pltpu.PrefetchScalarGridSpec(
        num_scalar_prefetch=1,
        grid=(m // tm,),
        in_specs=[pl.BlockSpec(memory_space=pl.ANY),
                  pl.BlockSpec((tm, D_MODEL), lambda i, p: (i, 0)),
                  pl.BlockSpec((tm, LANES), lambda i, p: (i, 0)),
                  pl.BlockSpec((1, N_MOD, D_MODEL), lambda i, p: (i // nbm, 0, 0)),
                  pl.BlockSpec((1, D_MODEL), lambda i, p: (0, 0))],
        out_specs=pl.BlockSpec((tm, D_MODEL), lambda i, p: (i, 0)),
        scratch_shapes=[pltpu.VMEM((2, TOP_K, tm, D_MODEL), F32),
                        pltpu.SemaphoreType.DMA((2,))],
    )
    return pl.pallas_call(
        _combine_kernel,
        grid_spec=grid_spec,
        out_shape=jax.ShapeDtypeStruct((m, D_MODEL), F32),
        compiler_params=_params(("arbitrary",)),
        name="moe_combine_norm",
    )(pos, eo, x1, rw, mods, nf)


def _rope_tables(t):
    half = ATTN_HEAD_DIM // 4
    pos = jnp.arange(t, dtype=I32)
    row = (pos // GRID_W).astype(F32)
    col = (pos % GRID_W).astype(F32)
    freqs = ROPE_BASE ** (-jnp.arange(half, dtype=F32) / half)
    ar = row[:, None] * freqs[None, :]
    ac = col[:, None] * freqs[None, :]
    cos_t = jnp.concatenate([jnp.cos(ar), jnp.cos(ar), jnp.cos(ac), jnp.cos(ac)], axis=1)
    sin_t = jnp.concatenate([-jnp.sin(ar), jnp.sin(ar), -jnp.sin(ac), jnp.sin(ac)], axis=1)
    return cos_t, sin_t


def _tri_pair(n):
    li = np.arange(n)[:, None]
    si = np.arange(n)[None, :]
    return jnp.asarray(np.stack([li >= si, li <= si]).astype(np.float32), BF16)


def _head_expand_matrix():
    k = np.arange(LANES)[:, None]
    j = np.arange(D_SSD)[None, :]
    return jnp.asarray((k == j // SSD_HEAD_DIM).astype(np.float32), BF16)


def _pad_heads(v):
    return jnp.pad(v.astype(F32), ((0, 0), (0, LANES - SSD_HEADS)))[:, None, :]


def kernel(x, c, ctx, c_ctx, w_ada, b_ada, norm_mix, norm_ffn, w_in, conv_w, conv_b, a_log, dt_bias,
           d_skip, ssd_norm, attn_sink, w_branch_ssd, w_branch_attn, w_out, w_route_group,
           w_route_expert, w_gate, w_up, w_down, norm_final):
    bsz, t, d = x.shape
    tc = ctx.shape[1]
    m = bsz * t
    l = 0

    ctx_row = bsz
    n_rows = -(-(bsz + 1) // 8) * 8
    cc = jnp.zeros((n_rows, d), F32).at[:bsz].set(c).at[ctx_row].set(c_ctx)
    mods = _ada(cc, w_ada[l], b_ada[l][None, :]).reshape(n_rows, N_MOD, d)

    wi = w_in[l]
    o_xbc = D_SSD
    o_dt = o_xbc + CONV_CH
    o_q = o_dt + 2 * SSD_HEADS
    o_k = o_q + D_ATTN
    o_v = o_k + D_KV
    o_g = o_v + D_KV
    w_xs = wi[:, o_xbc:o_xbc + D_SSD]
    w_bc = wi[:, o_xbc + D_SSD:o_dt]
    w_k = wi[:, o_k:o_v]
    w_v = wi[:, o_v:o_g]
    w_lat = jnp.concatenate([w_xs, wi[:, :D_SSD], wi[:, o_q:o_k], w_bc, w_k, w_v, wi[:, o_g:]],
                            axis=1).astype(BF16)
    w_ctx = jnp.concatenate([w_xs, w_bc, w_k, w_v], axis=1).astype(BF16)
    w_dt = jnp.zeros((d, 2 * LANES), F32)
    w_dt = w_dt.at[:, :SSD_HEADS].set(wi[:, o_dt:o_dt + SSD_HEADS])
    w_dt = w_dt.at[:, LANES:LANES + SSD_HEADS].set(wi[:, o_dt + SSD_HEADS:o_q]).astype(BF16)

    cos_t, sin_t = _rope_tables(t)
    nmix = norm_mix[l][None, :]
    tm_in = min(1024, t)
    u, dt2 = _inproj(x.reshape(m, d), nmix, mods, w_lat, w_dt, cos_t, sin_t,
                     rows_per_mod=t, mod_row0=0, tm=tm_in, rope=True)
    mc = bsz * tc
    uc, dtc2 = _inproj(ctx.reshape(mc, d), nmix, mods, w_ctx, w_dt, None, None,
                       rows_per_mod=0, mod_row0=ctx_row, tm=min(1024, mc), rope=False)

    cw = conv_w[l]
    cb = conv_b[l][None, :]
    xs_c, bc_c = _conv(u, U_XS, U_BC, cw, cb, t, min(256, t))
    xsc_c, bcc_c = _conv(uc, UC_XS, UC_BC, cw, cb, tc, min(256, tc))
    alog = _pad_heads(a_log[l])
    bias = _pad_heads(dt_bias[l])
    e_mat = _head_expand_matrix()
    h0 = _ctx_state(xsc_c, bcc_c, dtc2, alog, bias, _tri_pair(tc), e_mat, bsz, tc)
    dsk = jnp.repeat(d_skip[l].astype(F32), SSD_HEAD_DIM)[None, :]
    y_ssd = _ssd(xs_c, bc_c, dt2, alog, bias, _tri_pair(SSD_CHUNK), e_mat, h0, u, dsk,
                 ssd_norm[l][None, :], bsz, t)

    y_attn = _attn(attn_sink[l].astype(F32), u, uc, bsz, t, tc)

    merged = _merge(y_ssd, y_attn, u, w_branch_ssd[l].astype(BF16), w_branch_attn[l].astype(BF16),
                    min(512, t))

    tm_o = min(256, t)
    w_r = jnp.zeros((d, 2 * LANES), F32)
    w_r = w_r.at[:, :MOE_GROUPS].set(w_route_group[l])
    w_r = w_r.at[:, LANES:LANES + N_EXPERTS].set(w_route_expert[l]).astype(BF16)
    li = np.arange(tm_o)
    tri_strict = jnp.asarray((li[:, None] > li[None, :]).astype(np.float32), BF16)
    x1, h2, ridx, rw, cnt = _outproj(merged, w_out[l].astype(BF16), x.reshape(m, d), mods,
                                        norm_ffn[l][None, :], w_r, tri_strict, tm_o, t)

    na = m * TOP_K
    nblk = -(-(na + N_EXPERTS * (MOE_BLOCK - 1)) // MOE_BLOCK)
    counts = cnt[0, :N_EXPERTS].astype(I32)
    padded = (counts + MOE_BLOCK - 1) // MOE_BLOCK * MOE_BLOCK
    pend = jnp.cumsum(padded)
    pstart = pend - padded
    blk_start = jnp.arange(nblk, dtype=I32) * MOE_BLOCK
    blk_e = jnp.clip(jnp.sum((pend[None, :] <= blk_start[:, None]).astype(I32), axis=1),
                     0, N_EXPERTS - 1)
    n_used = (pend[N_EXPERTS - 1:] // MOE_BLOCK).astype(I32)
    pos = (pstart[ridx[0:TOP_K]] + ridx[TOP_K:2 * TOP_K]).reshape(-1)
    tok = jnp.tile(jnp.arange(m, dtype=I32), TOP_K)
    slot_tok = jnp.zeros((nblk * MOE_BLOCK,), I32).at[pos].set(tok, unique_indices=True)

    def halves(w):
        wb = w.astype(BF16).reshape(N_EXPERTS, D_MODEL, EXPERT_SPLIT, D_EXPERT // EXPERT_SPLIT)
        return wb.transpose(0, 2, 1, 3)

    eo = _experts(blk_e, slot_tok, n_used, h2, halves(w_gate[l]), halves(w_up[l]),
                  w_down[l].astype(BF16), nblk)

    tm_c = min(256, t)
    out = _combine(pos, eo, x1, rw, mods, norm_final[None, :], tm_c, t)
    return out.reshape(bsz, t, d)
```

```python
import functools

import numpy as np
import jax
import jax.numpy as jnp
from jax import lax
from jax.experimental import pallas as pl
from jax.experimental.pallas import tpu as pltpu

F32 = jnp.float32
BF16 = jnp.bfloat16
I32 = jnp.int32

D_MODEL = 2048
GRID_W = 64
NORM_EPS = 1e-6
N_MOD = 6
SSD_HEADS = 16
SSD_HEAD_DIM = 64
D_SSD = SSD_HEADS * SSD_HEAD_DIM
SSD_GROUPS = 2
SSD_STATE = 128
SSD_CONV = 5
SSD_CHUNK = 128
CONV_CH = D_SSD + 2 * SSD_GROUPS * SSD_STATE
ATTN_HEADS = 8
ATTN_KV_HEADS = 2
ATTN_HEAD_DIM = 128
D_ATTN = ATTN_HEADS * ATTN_HEAD_DIM
D_KV = ATTN_KV_HEADS * ATTN_HEAD_DIM
WINDOW = 128
ROPE_BASE = 10000.0
N_BRANCH = 2
MOE_GROUPS = 4
EXPERTS_PER_GROUP = 8
N_EXPERTS = MOE_GROUPS * EXPERTS_PER_GROUP
TOP_K = 2
D_EXPERT = 512
MOE_BLOCK = 256

LANES = 128
VMEM_LIMIT = 56 * 1024 * 1024

U_XS, U_Z, U_Q, U_BC, U_K, U_V, U_G = 0, 1024, 2048, 3072, 3584, 3840, 4096
U_W = 8192
UC_XS, UC_BC, UC_K, UC_V = 0, 1024, 1536, 1792
UC_W = 2048
IN_TN = 1024


def _params(sem):
    return pltpu.CompilerParams(dimension_semantics=sem, vmem_limit_bytes=VMEM_LIMIT)


def _silu(v):
    return v * jax.nn.sigmoid(v)


def _split_bf16(v, n):
    parts = []
    r = v
    for _ in range(n):
        p = r.astype(BF16)
        parts.append(p)
        r = r - p.astype(F32)
    return parts


def _dot(a, b):
    return jnp.dot(a, b, preferred_element_type=F32)


def _dot_nt(a, b):
    return lax.dot_general(a, b, (((1,), (1,)), ((), ())), preferred_element_type=F32)


def _ada_kernel(c_ref, w_ref, b_ref, o_ref):
    a = _silu(c_ref[...]).astype(BF16)
    o_ref[...] = _dot(a, w_ref[...].astype(BF16)) + b_ref[...]


def _ada(cc, w, b):
    rows, d = cc.shape
    n = w.shape[1]
    tn = 1024
    return pl.pallas_call(
        _ada_kernel,
        grid=(n // tn,),
        in_specs=[pl.BlockSpec((rows, d), lambda j: (0, 0)),
                  pl.BlockSpec((d, tn), lambda j: (0, j)),
                  pl.BlockSpec((1, tn), lambda j: (0, j))],
        out_specs=pl.BlockSpec((rows, tn), lambda j: (0, j)),
        out_shape=jax.ShapeDtypeStruct((rows, n), F32),
        compiler_params=_params(("arbitrary",)),
        name="ada",
    )(cc, w, b)


def _rope(a, cos, sin_signed, first):
    partner = jnp.where(first, pltpu.roll(a, 96, 1), pltpu.roll(a, 32, 1))
    return a * cos + partner * sin_signed


def _inproj_kernel(*refs, rope_q_blocks, rope_kv_block, sub):
    if rope_q_blocks:
        x_ref, nw_ref, mod_ref, w_ref, wdt_ref, cos_ref, sin_ref, o_ref, dt_ref, h_scr = refs
    else:
        x_ref, nw_ref, mod_ref, w_ref, wdt_ref, o_ref, dt_ref, h_scr = refs
    j = pl.program_id(1)
    tm = x_ref.shape[0]
    tn = w_ref.shape[1]

    @pl.when(j == 0)
    def _():
        nw = nw_ref[...]
        sh = mod_ref[0, 0:1, :]
        sc = mod_ref[0, 1:2, :]

        def body(r, carry):
            rows = pl.ds(pl.multiple_of(r * sub, sub), sub)
            xf = x_ref[rows, :]
            ms = jnp.mean(xf * xf, axis=-1, keepdims=True)
            h = (xf * lax.rsqrt(ms + NORM_EPS) * nw) * (1.0 + sc) + sh
            hb = h.astype(BF16)
            h_scr[rows, :] = hb
            d = _dot(hb, wdt_ref[...])
            dt_ref[0, rows, :] = d[:, :LANES]
            dt_ref[1, rows, :] = d[:, LANES:]
            return carry

        lax.fori_loop(0, tm // sub, body, 0)

    acc = _dot(h_scr[...], w_ref[...])

    def store(lo, hi):
        if hi > lo:
            cos = cos_ref[...]
            sin = sin_ref[...]
            lane = lax.broadcasted_iota(I32, (tm, LANES), 1)
            first = (lane % 64) < 32
        for hh in range(tn // LANES):
            a = acc[:, hh * LANES:(hh + 1) * LANES]
            if lo <= hh < hi:
                a = _rope(a, cos, sin, first)
            o_ref[:, hh * LANES:(hh + 1) * LANES] = a.astype(BF16)

    if rope_q_blocks:
        is_q = (j >= rope_q_blocks[0]) & (j <= rope_q_blocks[-1])
        is_kv = j == rope_kv_block
        k_lo = (U_K % tn) // LANES

        @pl.when(is_q)
        def _():
            store(0, tn // LANES)

        @pl.when(is_kv)
        def _():
            store(k_lo, k_lo + D_KV // LANES)

        @pl.when(jnp.logical_not(is_q | is_kv))
        def _():
            store(0, 0)
    else:
        store(0, 0)


def _inproj(x2, nw, mods, w, wdt, cos_t, sin_t, *, rows_per_mod, mod_row0, tm, rope):
    m, d = x2.shape
    n = w.shape[1]
    tn = IN_TN
    nbm = rows_per_mod // tm if rows_per_mod else 0
    if rows_per_mod:
        mod_map = lambda i, j: (mod_row0 + i // nbm, 0, 0)
    else:
        mod_map = lambda i, j: (mod_row0, 0, 0)
    in_specs = [pl.BlockSpec((tm, d), lambda i, j: (i, 0)),
                pl.BlockSpec((1, d), lambda i, j: (0, 0)),
                pl.BlockSpec((1, N_MOD, d), mod_map),
                pl.BlockSpec((d, tn), lambda i, j: (0, j)),
                pl.BlockSpec((d, 2 * LANES), lambda i, j: (0, 0))]
    args = [x2, nw, mods, w, wdt]
    if rope:
        in_specs += [pl.BlockSpec((tm, LANES), lambda i, j: (i % nbm, 0)),
                     pl.BlockSpec((tm, LANES), lambda i, j: (i % nbm, 0))]
        args += [cos_t, sin_t]
        rq = tuple(range(U_Q // tn, (U_Q + D_ATTN) // tn))
        rkv = U_K // tn
    else:
        rq, rkv = (), None
    kern = functools.partial(_inproj_kernel, rope_q_blocks=rq, rope_kv_block=rkv, sub=128)
    return pl.pallas_call(
        kern,
        grid=(m // tm, n // tn),
        in_specs=in_specs,
        out_specs=[pl.BlockSpec((tm, tn), lambda i, j: (i, j)),
                   pl.BlockSpec((2, tm, LANES), lambda i, j: (0, i, 0))],
        out_shape=[jax.ShapeDtypeStruct((m, n), BF16),
                   jax.ShapeDtypeStruct((2, m, LANES), F32)],
        scratch_shapes=[pltpu.VMEM((tm, d), BF16)],
        compiler_params=_params(("arbitrary", "arbitrary")),
        name="inproj_rope" if rope else "inproj_ctx",
    )(*args)


CONV_HALO = 16


def _conv_kernel(xs_ref, bc_ref, pxs_ref, pbc_ref, nxs_ref, nbc_ref, w_ref, b_ref,
                 oxs_ref, obc_ref, ext, *, blocks_per_seq):
    i = pl.program_id(0)
    r = xs_ref.shape[0]
    pos = i % blocks_per_seq
    not_first = pos != 0
    not_last = pos != blocks_per_seq - 1
    h = CONV_HALO
    ext[0:h, 0:D_SSD] = jnp.where(not_first, pxs_ref[...].astype(F32), 0.0)
    ext[0:h, D_SSD:] = jnp.where(not_first, pbc_ref[...].astype(F32), 0.0)
    ext[h:h + r, 0:D_SSD] = xs_ref[...].astype(F32)
    ext[h:h + r, D_SSD:] = bc_ref[...].astype(F32)
    ext[h + r:, 0:D_SSD] = jnp.where(not_last, nxs_ref[...].astype(F32), 0.0)
    ext[h + r:, D_SSD:] = jnp.where(not_last, nbc_ref[...].astype(F32), 0.0)
    cw = 256
    pad = SSD_CONV // 2
    for cb in range(0, CONV_CH, cw):
        acc = jnp.broadcast_to(b_ref[:, cb:cb + cw], (r, cw))
        for k in range(SSD_CONV):
            acc = acc + ext[h - pad + k:h - pad + k + r, cb:cb + cw] * w_ref[k:k + 1, cb:cb + cw]
        y = _silu(acc)
        if cb < D_SSD:
            oxs_ref[:, cb:cb + cw] = y
        else:
            obc_ref[:, cb - D_SSD:cb - D_SSD + cw] = y.astype(BF16)


def _conv(u, xs_col, bc_col, conv_w, conv_b, seq_len, r):
    m = u.shape[0]
    h = CONV_HALO
    bps = seq_len // r
    wbc = CONV_CH - D_SSD
    xs_c, bc_c = xs_col // D_SSD, bc_col // wbc
    nh = m // h
    prev = lambda i: jnp.maximum(i * (r // h) - 1, 0)
    nxt = lambda i: jnp.minimum((i + 1) * (r // h), nh - 1)
    return pl.pallas_call(
        functools.partial(_conv_kernel, blocks_per_seq=bps),
        grid=(m // r,),
        in_specs=[pl.BlockSpec((r, D_SSD), lambda i: (i, xs_c)),
                  pl.BlockSpec((r, wbc), lambda i: (i, bc_c)),
                  pl.BlockSpec((h, D_SSD), lambda i: (prev(i), xs_c)),
                  pl.BlockSpec((h, wbc), lambda i: (prev(i), bc_c)),
                  pl.BlockSpec((h, D_SSD), lambda i: (nxt(i), xs_c)),
                  pl.BlockSpec((h, wbc), lambda i: (nxt(i), bc_c)),
                  pl.BlockSpec((SSD_CONV, CONV_CH), lambda i: (0, 0)),
                  pl.BlockSpec((1, CONV_CH), lambda i: (0, 0))],
        out_specs=[pl.BlockSpec((r, D_SSD), lambda i: (i, 0)),
                   pl.BlockSpec((r, wbc), lambda i: (i, 0))],
        out_shape=[jax.ShapeDtypeStruct((m, D_SSD), F32),
                   jax.ShapeDtypeStruct((m, wbc), BF16)],
        scratch_shapes=[pltpu.VMEM((r + 2 * h, CONV_CH), F32)],
        compiler_params=_params(("arbitrary",)),
        name="conv_silu",
    )(u, u, u, u, u, u, conv_w, conv_b)


def _expand_heads(v, e):
    hi, lo = _split_bf16(v, 2)
    return _dot(hi, e) + _dot(lo, e)


def _cumsum_mm(tri, da):
    p0, p1, p2 = _split_bf16(da, 3)
    return _dot(tri, p0) + _dot(tri, p1) + _dot(tri, p2)


def _ctx_state_kernel(xs_ref, b_ref, dt_ref, alog_ref, bias_ref, tri_ref, e_ref, h_ref):
    d = pl.program_id(1)
    tc = xs_ref.shape[0]
    dt = jax.nn.softplus(dt_ref[0] + bias_ref[0])
    a = -jnp.exp(alog_ref[0])
    cum = _cumsum_mm(tri_ref[0], dt * a)
    cum_end = jnp.where(d == 0, cum[tc - 1:tc, :], cum[0:1, :])
    e = e_ref[...]
    dtx = _expand_heads(dt, e)
    dex = _expand_heads(jnp.exp(cum_end - cum), e)
    xw = (xs_ref[...] * dtx * dex).astype(BF16)
    gw = D_SSD // SSD_GROUPS
    for g in range(SSD_GROUPS):
        bgt = b_ref[:, g * SSD_STATE:(g + 1) * SSD_STATE].astype(F32).T.astype(BF16)
        h_ref[0, 0, :, g * gw:(g + 1) * gw] = _dot(bgt, xw[:, g * gw:(g + 1) * gw])


def _ctx_state(xs_c, bc_c, dt2, alog, bias, tri, e, bsz, tc):
    return pl.pallas_call(
        _ctx_state_kernel,
        grid=(bsz, 2),
        in_specs=[pl.BlockSpec((tc, D_SSD), lambda b, d: (b, 0)),
                  pl.BlockSpec((tc, SSD_GROUPS * SSD_STATE), lambda b, d: (b, 0)),
                  pl.BlockSpec((1, tc, LANES), lambda b, d: (d, b, 0)),
                  pl.BlockSpec((1, 1, LANES), lambda b, d: (d, 0, 0)),
                  pl.BlockSpec((1, 1, LANES), lambda b, d: (d, 0, 0)),
                  pl.BlockSpec((1, tc, tc), lambda b, d: (d, 0, 0)),
                  pl.BlockSpec((LANES, D_SSD), lambda b, d: (0, 0))],
        out_specs=pl.BlockSpec((1, 1, SSD_STATE, D_SSD), lambda b, d: (b, d, 0, 0)),
        out_shape=jax.ShapeDtypeStruct((bsz, 2, SSD_STATE, D_SSD), F32),
        compiler_params=_params(("arbitrary", "arbitrary")),
        name="ssd_ctx_state",
    )(xs_c, bc_c, dt2, alog, bias, tri, e)


def _ssd_kernel(xs_ref, bc_ref, dt_ref, alog_ref, bias_ref, tri_ref, e_ref, h0_ref, z_ref,
                dsk_ref, nw_ref, o_ref, st, yf, ycur, *, nc):
    d = pl.program_id(1)
    s = pl.program_id(2)
    lc = SSD_CHUNK
    c = jnp.where(d == 0, s, nc - 1 - s)
    is_f = d == 0

    @pl.when(s == 0)
    def _():
        st[...] = h0_ref[0, 0]

    xs = xs_ref[...]
    dt = jax.nn.softplus(dt_ref[0] + bias_ref[0])
    a = -jnp.exp(alog_ref[0])
    cum = _cumsum_mm(tri_ref[0], dt * a)
    cum_t = cum.T
    cum_end = jnp.where(is_f, cum[lc - 1:lc, :], cum[0:1, :])
    e = e_ref[...]
    ecx = _expand_heads(jnp.exp(cum), e)
    dex = _expand_heads(jnp.exp(cum_end - cum), e)
    dtx = _expand_heads(dt, e)
    xdt = xs * dtx
    xdt_b = xdt.astype(BF16)
    xde_b = (xdt * dex).astype(BF16)
    cd_row = jnp.where(is_f, ecx[lc - 1:lc, :], ecx[0:1, :])
    li = lax.broadcasted_iota(I32, (lc, lc), 0)
    si = lax.broadcasted_iota(I32, (lc, lc), 1)
    causal = (li - si) * jnp.where(is_f, 1, -1) >= 0
    lane = lax.broadcasted_iota(I32, (lc, LANES), 1)
    low = lane < SSD_HEAD_DIM
    gw = D_SSD // SSD_GROUPS
    hpg = SSD_HEADS // SSD_GROUPS
    for g in range(SSD_GROUPS):
        bg = bc_ref[:, g * SSD_STATE:(g + 1) * SSD_STATE]
        cg = bc_ref[:, (SSD_GROUPS + g) * SSD_STATE:(SSD_GROUPS + g + 1) * SSD_STATE]
        cb = _dot_nt(cg, bg)
        yoff = _dot(cg, st[:, g * gw:(g + 1) * gw].astype(BF16)) * ecx[:, g * gw:(g + 1) * gw]
        for pr in range(hpg // 2):
            h0 = g * hpg + 2 * pr
            ms = []
            for h in (h0, h0 + 1):
                seg = cum[:, h:h + 1] - cum_t[h:h + 1, :]
                ms.append((cb * jnp.exp(jnp.where(causal, seg, -jnp.inf))).astype(BF16))
            lhs = jnp.concatenate(ms, axis=1)
            xp = xdt_b[:, h0 * SSD_HEAD_DIM:(h0 + 2) * SSD_HEAD_DIM]
            zero = jnp.zeros_like(xp)
            rhs = jnp.concatenate([jnp.where(low, xp, zero), jnp.where(low, zero, xp)], axis=0)
            ycur[:, h0 * SSD_HEAD_DIM:(h0 + 2) * SSD_HEAD_DIM] = (
                _dot(lhs, rhs) + yoff[:, pr * LANES:(pr + 1) * LANES])
        bgt = bg.astype(F32).T.astype(BF16)
        st[:, g * gw:(g + 1) * gw] = (st[:, g * gw:(g + 1) * gw] * cd_row[:, g * gw:(g + 1) * gw]
                                      + _dot(bgt, xde_b[:, g * gw:(g + 1) * gw]))

    rows = pl.ds(pl.multiple_of(c * lc, lc), lc)

    @pl.when(is_f)
    def _():
        yf[rows, :] = ycur[...]

    @pl.when(jnp.logical_not(is_f))
    def _():
        y = ycur[...] + yf[rows, :] + dsk_ref[...] * xs
        gy = y * _silu(z_ref[...].astype(F32))
        for g in range(SSD_GROUPS):
            blk = gy[:, g * gw:(g + 1) * gw]
            ms = jnp.mean(blk * blk, axis=-1, keepdims=True)
            o_ref[:, g * gw:(g + 1) * gw] = (blk * lax.rsqrt(ms + NORM_EPS)
                                             * nw_ref[:, g * gw:(g + 1) * gw]).astype(BF16)


def _ssd(xs_c, bc_c, dt2, alog, bias, tri, e, h0, u, dsk, nw, bsz, t):
    nc = t // SSD_CHUNK
    lc = SSD_CHUNK
    m = bsz * t
    chunk = lambda b, d, s: b * nc + jnp.where(d == 0, s, nc - 1 - s)
    late = lambda b, d, s: b * nc + jnp.where(d == 0, nc - 1, nc - 1 - s)
    zc = U_Z // D_SSD
    return pl.pallas_call(
        functools.partial(_ssd_kernel, nc=nc),
        grid=(bsz, 2, nc),
        in_specs=[pl.BlockSpec((lc, D_SSD), lambda b, d, s: (chunk(b, d, s), 0)),
                  pl.BlockSpec((lc, 2 * SSD_GROUPS * SSD_STATE), lambda b, d, s: (chunk(b, d, s), 0)),
                  pl.BlockSpec((1, lc, LANES), lambda b, d, s: (d, chunk(b, d, s), 0)),
                  pl.BlockSpec((1, 1, LANES), lambda b, d, s: (d, 0, 0)),
                  pl.BlockSpec((1, 1, LANES), lambda b, d, s: (d, 0, 0)),
                  pl.BlockSpec((1, lc, lc), lambda b, d, s: (d, 0, 0)),
                  pl.BlockSpec((LANES, D_SSD), lambda b, d, s: (0, 0)),
                  pl.BlockSpec((1, 1, SSD_STATE, D_SSD), lambda b, d, s: (b, d, 0, 0)),
                  pl.BlockSpec((lc, D_SSD), lambda b, d, s: (late(b, d, s), zc)),
                  pl.BlockSpec((1, D_SSD), lambda b, d, s: (0, 0)),
                  pl.BlockSpec((1, D_SSD), lambda b, d, s: (0, 0))],
        out_specs=pl.BlockSpec((lc, D_SSD), lambda b, d, s: (late(b, d, s), 0)),
        out_shape=jax.ShapeDtypeStruct((m, D_SSD), BF16),
        scratch_shapes=[pltpu.VMEM((SSD_STATE, D_SSD), F32),
                        pltpu.VMEM((t, D_SSD), F32),
                        pltpu.VMEM((lc, D_SSD), F32)],
        compiler_params=_params(("arbitrary", "arbitrary", "arbitrary")),
        name="ssd_scan",
    )(xs_c, bc_c, dt2, alog, bias, tri, e, h0, u, dsk, nw)


def _attn_kernel(sink_ref, q_ref, k_ref, v_ref, kc_ref, vc_ref, o_ref, kp, vp, bias, *, t):
    g = pl.program_id(1)
    w = WINDOW
    nb = t // w
    rep = ATTN_HEADS // ATTN_KV_HEADS
    tc = kc_ref.shape[0]
    nk = 3 * w + tc
    zeros = jnp.zeros((w, ATTN_HEAD_DIM), BF16)
    kp[0:w, :] = zeros
    kp[w:w + t, :] = k_ref[...]
    kp[w + t:, :] = zeros
    vp[0:w, :] = zeros
    vp[w:w + t, :] = v_ref[...]
    vp[w + t:, :] = zeros
    qi = lax.broadcasted_iota(I32, (rep * w, w), 0) % w
    kj = lax.broadcasted_iota(I32, (rep * w, w), 1)
    bias[0] = jnp.where(kj >= qi, 0.0, -jnp.inf).astype(F32)
    bias[1] = jnp.where(kj <= qi, 0.0, -jnp.inf).astype(F32)
    sink_col = jnp.concatenate(
        [jnp.full((w, 1), sink_ref[g * rep + h], F32) for h in range(rep)], axis=0)
    scale = ATTN_HEAD_DIM ** -0.5
    log2e = float(np.log2(np.e))

    def body(n, carry):
        rows = pl.ds(pl.multiple_of(n * w, w), w)
        q = jnp.concatenate([q_ref[rows, h * ATTN_HEAD_DIM:(h + 1) * ATTN_HEAD_DIM]
                             for h in range(rep)], axis=0)
        band = pl.ds(pl.multiple_of(n * w, w), 3 * w)
        kall = jnp.concatenate([kp[band, :], kc_ref[...]], axis=0)
        vall = jnp.concatenate([vp[band, :], vc_ref[...]], axis=0)
        raw = _dot_nt(q, kall)
        prev = raw[:, 0:w] + (bias[0] + jnp.where(n == 0, -jnp.inf, 0.0))
        nxt = raw[:, 2 * w:3 * w] + (bias[1] + jnp.where(n == nb - 1, -jnp.inf, 0.0))
        sc = jnp.concatenate([prev, raw[:, w:2 * w], nxt, raw[:, 3 * w:]], axis=1)
        mx = jnp.maximum(jnp.max(sc, axis=-1, keepdims=True) * scale, sink_col)
        p = jnp.exp2(sc * (scale * log2e) - mx * log2e)
        den = jnp.sum(p, axis=-1, keepdims=True) + jnp.exp2((sink_col - mx) * log2e)
        o = _dot(p.astype(BF16), vall) / den
        for h in range(rep):
            o_ref[rows, h * ATTN_HEAD_DIM:(h + 1) * ATTN_HEAD_DIM] = o[h * w:(h + 1) * w, :].astype(BF16)
        return carry

    lax.fori_loop(0, nb, body, 0, unroll=2)


def _attn(sink, u, uc, bsz, t, tc):
    rep = ATTN_HEADS // ATTN_KV_HEADS
    qw = rep * ATTN_HEAD_DIM
    hd = ATTN_HEAD_DIM
    return pl.pallas_call(
        functools.partial(_attn_kernel, t=t),
        grid=(bsz, ATTN_KV_HEADS),
        in_specs=[pl.BlockSpec(memory_space=pltpu.SMEM),
                  pl.BlockSpec((t, qw), lambda b, g: (b, U_Q // qw + g)),
                  pl.BlockSpec((t, hd), lambda b, g: (b, U_K // hd + g)),
                  pl.BlockSpec((t, hd), lambda b, g: (b, U_V // hd + g)),
                  pl.BlockSpec((tc, hd), lambda b, g: (b, UC_K // hd + g)),
                  pl.BlockSpec((tc, hd), lambda b, g: (b, UC_V // hd + g))],
        out_specs=pl.BlockSpec((t, qw), lambda b, g: (b, g)),
        out_shape=jax.ShapeDtypeStruct((bsz * t, D_ATTN), BF16),
        scratch_shapes=[pltpu.VMEM((t + 2 * WINDOW, hd), BF16),
                        pltpu.VMEM((t + 2 * WINDOW, hd), BF16),
                        pltpu.VMEM((2, rep * WINDOW, WINDOW), F32)],
        compiler_params=_params(("arbitrary", "arbitrary")),
        name="window_attn",
    )(sink, u, u, u, uc, uc)


def _merge_kernel(ys_ref, ya_ref, g0_ref, g1_ref, ws_ref, wa_ref, o_ref):
    ys = ys_ref[...]
    ya = ya_ref[...]
    cw = 512
    for cb in range(0, D_MODEL, cw):
        t0 = _dot(ys, ws_ref[:, cb:cb + cw])
        t1 = _dot(ya, wa_ref[:, cb:cb + cw])
        g0 = jax.nn.sigmoid(g0_ref[:, cb:cb + cw].astype(F32))
        g1 = jax.nn.sigmoid(g1_ref[:, cb:cb + cw].astype(F32))
        o_ref[:, cb:cb + cw] = (g0 * t0 + g1 * t1).astype(BF16)


def _merge(ys, ya, u, ws, wa, tm):
    m = ys.shape[0]
    gc = U_G // D_MODEL
    return pl.pallas_call(
        _merge_kernel,
        grid=(m // tm,),
        in_specs=[pl.BlockSpec((tm, D_SSD), lambda i: (i, 0)),
                  pl.BlockSpec((tm, D_ATTN), lambda i: (i, 0)),
                  pl.BlockSpec((tm, D_MODEL), lambda i: (i, gc)),
                  pl.BlockSpec((tm, D_MODEL), lambda i: (i, gc + 1)),
                  pl.BlockSpec((D_SSD, D_MODEL), lambda i: (0, 0)),
                  pl.BlockSpec((D_ATTN, D_MODEL), lambda i: (0, 0))],
        out_specs=pl.BlockSpec((tm, D_MODEL), lambda i: (i, 0)),
        out_shape=jax.ShapeDtypeStruct((m, D_MODEL), BF16),
        compiler_params=_params(("arbitrary",)),
        name="branch_merge",
    )(ys, ya, u, u, ws, wa)


def _lane_min_index(mask, lane):
    return jnp.min(jnp.where(mask, lane, LANES), axis=-1, keepdims=True)


def _outproj_kernel(mg_ref, wo_ref, x_ref, mod_ref, nw_ref, wr_ref, tri_ref,
                    x1_ref, h2_ref, ridx_ref, rw_ref, cnt_ref, carry):
    i = pl.program_id(0)
    tm = x_ref.shape[0]

    @pl.when(i == 0)
    def _():
        carry[...] = jnp.zeros_like(carry)

    g1 = mod_ref[0, 2:3, :]
    sh2 = mod_ref[0, 3:4, :]
    sc2 = mod_ref[0, 4:5, :]
    x1 = x_ref[...] + g1 * _dot(mg_ref[...], wo_ref[...])
    x1_ref[...] = x1
    ms = jnp.mean(x1 * x1, axis=-1, keepdims=True)
    h2 = (x1 * lax.rsqrt(ms + NORM_EPS) * nw_ref[...]) * (1.0 + sc2) + sh2
    h2_ref[...] = h2

    lg = _dot(h2.astype(BF16), wr_ref[...])
    lane = lax.broadcasted_iota(I32, (tm, LANES), 1)
    gl = jnp.where(lane < MOE_GROUPS, lg[:, :LANES], -jnp.inf)
    gmax = jnp.max(gl, axis=-1, keepdims=True)
    g_w = 1.0 / jnp.sum(jnp.exp(gl - gmax), axis=-1, keepdims=True)
    g_idx = _lane_min_index(gl == gmax, lane)
    el = jnp.where((lane // EXPERTS_PER_GROUP) == g_idx, lg[:, LANES:], -jnp.inf)
    v1 = jnp.max(el, axis=-1, keepdims=True)
    i1 = _lane_min_index(el == v1, lane)
    el2 = jnp.where(lane == i1, -jnp.inf, el)
    v2 = jnp.max(el2, axis=-1, keepdims=True)
    i2 = _lane_min_index(el2 == v2, lane)
    e2 = jnp.exp(v2 - v1)
    w1 = g_w * (1.0 / (1.0 + e2))
    w2 = g_w * (e2 / (1.0 + e2))

    hit1 = lane == i1
    hit2 = lane == i2
    onehot = jnp.where(hit1 | hit2, 1.0, 0.0).astype(F32)
    before = _dot(tri_ref[...], onehot.astype(BF16)) + carry[0:1, :]
    r1 = jnp.sum(jnp.where(hit1, before, 0.0), axis=-1, keepdims=True)
    r2 = jnp.sum(jnp.where(hit2, before, 0.0), axis=-1, keepdims=True)
    total = carry[0:1, :] + jnp.sum(onehot, axis=0, keepdims=True)
    carry[...] = jnp.broadcast_to(total, carry.shape)
    cnt_ref[...] = jnp.broadcast_to(total, cnt_ref.shape)
    packed = jnp.where(lane == 0, i1.astype(F32), jnp.where(lane == 1, i2.astype(F32), jnp.where(
        lane == 2, r1, jnp.where(lane == 3, r2, 0.0))))
    ridx_ref[...] = packed.T[0:8, :].astype(I32)
    rw_ref[...] = jnp.where(lane == 0, w1, jnp.where(lane == 1, w2, 0.0))


def _outproj(mg, wo, x2, mods, nw, wr, tri, tm, t):
    m = x2.shape[0]
    nbm = t // tm
    return pl.pallas_call(
        _outproj_kernel,
        grid=(m // tm,),
        in_specs=[pl.BlockSpec((tm, D_MODEL), lambda i: (i, 0)),
                  pl.BlockSpec((D_MODEL, D_MODEL), lambda i: (0, 0)),
                  pl.BlockSpec((tm, D_MODEL), lambda i: (i, 0)),
                  pl.BlockSpec((1, N_MOD, D_MODEL), lambda i: (i // nbm, 0, 0)),
                  pl.BlockSpec((1, D_MODEL), lambda i: (0, 0)),
                  pl.BlockSpec((D_MODEL, 2 * LANES), lambda i: (0, 0)),
                  pl.BlockSpec((tm, tm), lambda i: (0, 0))],
        out_specs=[pl.BlockSpec((tm, D_MODEL), lambda i: (i, 0)),
                   pl.BlockSpec((tm, D_MODEL), lambda i: (i, 0)),
                   pl.BlockSpec((8, tm), lambda i: (0, i)),
                   pl.BlockSpec((tm, LANES), lambda i: (i, 0)),
                   pl.BlockSpec((8, LANES), lambda i: (0, 0))],
        out_shape=[jax.ShapeDtypeStruct((m, D_MODEL), F32),
                   jax.ShapeDtypeStruct((m, D_MODEL), F32),
                   jax.ShapeDtypeStruct((8, m), I32),
                   jax.ShapeDtypeStruct((m, LANES), F32),
                   jax.ShapeDtypeStruct((8, LANES), F32)],
        scratch_shapes=[pltpu.VMEM((8, LANES), F32)],
        compiler_params=_params(("arbitrary",)),
        name="outproj_router",
    )(mg, wo, x2, mods, nw, wr, tri)


def _gather_rows(idx_ref, base, n, src, dst, sem, row0=0):
    for r in range(n):
        tok = idx_ref[base + r]
        pltpu.make_async_copy(src.at[pl.ds(tok, 1), :], dst.at[pl.ds(row0 + r, 1), :],
                              sem).start(priority=r % 2)


def _expert_kernel(blk_e_ref, pos_ref, meta_ref, h2_hbm, wg_ref, wu_ref, wd_ref, o_ref,
                   xb, tok_ref, sem):
    i = pl.program_id(0)
    n_used = meta_ref[2 * N_EXPERTS]
    slot = i % 2
    rows = MOE_BLOCK
    m = pos_ref.shape[0] // TOP_K

    def wait_block(sl):
        pltpu.make_async_copy(h2_hbm.at[pl.ds(0, rows), :], xb.at[sl], sem.at[sl]).wait()

    @pl.when(i == 0)
    def _():
        def pad_expert(e, carry):
            def pad_slot(s, c):
                tok_ref[s] = 0
                return c
            return lax.fori_loop(meta_ref[e], meta_ref[N_EXPERTS + e], pad_slot, carry)

        lax.fori_loop(0, N_EXPERTS, pad_expert, 0)

        def place(t, carry):
            for k in range(TOP_K):
                tok_ref[pos_ref[k * m + t]] = t
            return carry

        lax.fori_loop(0, m, place, 0, unroll=8)
        _gather_rows(tok_ref, 0, rows, h2_hbm, xb.at[0], sem.at[0])

    @pl.when(i < n_used)
    def _():
        wait_block(slot)
        x = xb[slot].astype(BF16)
        nxt = jnp.minimum(i + 1, n_used - 1)
        _gather_rows(tok_ref, nxt * rows, rows, h2_hbm, xb.at[1 - slot], sem.at[1 - slot])
        act = (_silu(_dot(x, wg_ref[0])) * _dot(x, wu_ref[0])).astype(BF16)
        o_ref[...] = _dot(act, wd_ref[0])

        @pl.when(i == n_used - 1)
        def _():
            wait_block(1 - slot)

    @pl.when(i >= n_used)
    def _():
        o_ref[...] = jnp.zeros_like(o_ref)


def _experts(blk_e, pos, meta, h2, wg, wu, wd, nblk):
    rows = MOE_BLOCK
    live = lambda i, mt: jnp.minimum(i, mt[2 * N_EXPERTS] - 1)
    grid_spec = pltpu.PrefetchScalarGridSpec(
        num_scalar_prefetch=3,
        grid=(nblk,),
        in_specs=[pl.BlockSpec(memory_space=pl.ANY),
                  pl.BlockSpec((1, D_MODEL, D_EXPERT), lambda i, be, ps, mt: (be[live(i, mt)], 0, 0)),
                  pl.BlockSpec((1, D_MODEL, D_EXPERT), lambda i, be, ps, mt: (be[live(i, mt)], 0, 0)),
                  pl.BlockSpec((1, D_EXPERT, D_MODEL), lambda i, be, ps, mt: (be[live(i, mt)], 0, 0))],
        out_specs=pl.BlockSpec((rows, D_MODEL), lambda i, be, ps, mt: (i, 0)),
        scratch_shapes=[pltpu.VMEM((2, rows, D_MODEL), F32),
                        pltpu.SMEM((nblk * rows,), I32),
                        pltpu.SemaphoreType.DMA((2,))],
    )
    return pl.pallas_call(
        _expert_kernel,
        grid_spec=grid_spec,
        out_shape=jax.ShapeDtypeStruct((nblk * rows, D_MODEL), F32),
        compiler_params=_params(("arbitrary",)),
        name="expert_mlp",
    )(blk_e, pos, meta, h2, wg, wu, wd)


def _combine_kernel(pos_ref, eo_hbm, x1_ref, rw_ref, mod_ref, nf_ref, o_ref, gb, sem):
    i = pl.program_id(0)
    nsteps = pl.num_programs(0)
    tm = x1_ref.shape[0]
    slot = i % 2

    def issue(step, sl):
        for k in range(TOP_K):
            _gather_rows(pos_ref, (k * nsteps + step) * tm, tm, eo_hbm, gb.at[sl, k], sem.at[sl])

    @pl.when(i == 0)
    def _():
        issue(0, 0)

    @pl.when(i + 1 < nsteps)
    def _():
        issue(i + 1, 1 - slot)

    for k in range(TOP_K):
        pltpu.make_async_copy(eo_hbm.at[pl.ds(0, tm), :], gb.at[slot, k], sem.at[slot]).wait()
    w1 = rw_ref[:, 0:1]
    w2 = rw_ref[:, 1:2]
    g2 = mod_ref[0, 5:6, :]
    x2 = x1_ref[...] + g2 * (gb[slot, 0] * w1 + gb[slot, 1] * w2)
    ms = jnp.mean(x2 * x2, axis=-1, keepdims=True)
    o_ref[...] = x2 * lax.rsqrt(ms + NORM_EPS) * nf_ref[...]


def _combine(pos, eo, x1, rw, mods, nf, tm, t):
    m = x1.shape[0]
    nbm = t // tm
    grid_spec = pltpu.PrefetchScalarGridSpec(
        num_scalar_prefetch=1,
        grid=(m // tm,),
        in_specs=[pl.BlockSpec(memory_space=pl.ANY),
                  pl.BlockSpec((tm, D_MODEL), lambda i, p: (i, 0)),
                  pl.BlockSpec((tm, LANES), lambda i, p: (i, 0)),
                  pl.BlockSpec((1, N_MOD, D_MODEL), lambda i, p: (i // nbm, 0, 0)),
                  pl.BlockSpec((1, D_MODEL), lambda i, p: (0, 0))],
        out_specs=pl.BlockSpec((tm, D_MODEL), lambda i, p: (i, 0)),
        scratch_shapes=[pltpu.VMEM((2, TOP_K, tm, D_MODEL), F32),
                        pltpu.SemaphoreType.DMA((2,))],
    )
    return pl.pallas_call(
        _combine_kernel,
        grid_spec=grid_spec,
        out_shape=jax.ShapeDtypeStruct((m, D_MODEL), F32),
        compiler_params=_params(("arbitrary",)),
        name="moe_combine_norm",
    )(pos, eo, x1, rw, mods, nf)


def _rope_tables(t):
    half = ATTN_HEAD_DIM // 4
    pos = jnp.arange(t, dtype=I32)
    row = (pos // GRID_W).astype(F32)
    col = (pos % GRID_W).astype(F32)
    freqs = ROPE_BASE ** (-jnp.arange(half, dtype=F32) / half)
    ar = row[:, None] * freqs[None, :]
    ac = col[:, None] * freqs[None, :]
    cos_t = jnp.concatenate([jnp.cos(ar), jnp.cos(ar), jnp.cos(ac), jnp.cos(ac)], axis=1)
    sin_t = jnp.concatenate([-jnp.sin(ar), jnp.sin(ar), -jnp.sin(ac), jnp.sin(ac)], axis=1)
    return cos_t, sin_t


def _tri_pair(n):
    li = np.arange(n)[:, None]
    si = np.arange(n)[None, :]
    return jnp.asarray(np.stack([li >= si, li <= si]).astype(np.float32), BF16)


def _head_expand_matrix():
    k = np.arange(LANES)[:, None]
    j = np.arange(D_SSD)[None, :]
    return jnp.asarray((k == j // SSD_HEAD_DIM).astype(np.float32), BF16)


def _pad_heads(v):
    return jnp.pad(v.astype(F32), ((0, 0), (0, LANES - SSD_HEADS)))[:, None, :]


def kernel(x, c, ctx, c_ctx, w_ada, b_ada, norm_mix, norm_ffn, w_in, conv_w, conv_b, a_log, dt_bias,
           d_skip, ssd_norm, attn_sink, w_branch_ssd, w_branch_attn, w_out, w_route_group,
           w_route_expert, w_gate, w_up, w_down, norm_final):
    bsz, t, d = x.shape
    tc = ctx.shape[1]
    m = bsz * t
    l = 0

    ctx_row = bsz
    n_rows = -(-(bsz + 1) // 8) * 8
    cc = jnp.zeros((n_rows, d), F32).at[:bsz].set(c).at[ctx_row].set(c_ctx)
    mods = _ada(cc, w_ada[l], b_ada[l][None, :]).reshape(n_rows, N_MOD, d)

    wi = w_in[l]
    o_xbc = D_SSD
    o_dt = o_xbc + CONV_CH
    o_q = o_dt + 2 * SSD_HEADS
    o_k = o_q + D_ATTN
    o_v = o_k + D_KV
    o_g = o_v + D_KV
    w_xs = wi[:, o_xbc:o_xbc + D_SSD]
    w_bc = wi[:, o_xbc + D_SSD:o_dt]
    w_k = wi[:, o_k:o_v]
    w_v = wi[:, o_v:o_g]
    w_lat = jnp.concatenate([w_xs, wi[:, :D_SSD], wi[:, o_q:o_k], w_bc, w_k, w_v, wi[:, o_g:]],
                            axis=1).astype(BF16)
    w_ctx = jnp.concatenate([w_xs, w_bc, w_k, w_v], axis=1).astype(BF16)
    w_dt = jnp.zeros((d, 2 * LANES), F32)
    w_dt = w_dt.at[:, :SSD_HEADS].set(wi[:, o_dt:o_dt + SSD_HEADS])
    w_dt = w_dt.at[:, LANES:LANES + SSD_HEADS].set(wi[:, o_dt + SSD_HEADS:o_q]).astype(BF16)

    cos_t, sin_t = _rope_tables(t)
    nmix = norm_mix[l][None, :]
    tm_in = min(1024, t)
    u, dt2 = _inproj(x.reshape(m, d), nmix, mods, w_lat, w_dt, cos_t, sin_t,
                     rows_per_mod=t, mod_row0=0, tm=tm_in, rope=True)
    mc = bsz * tc
    uc, dtc2 = _inproj(ctx.reshape(mc, d), nmix, mods, w_ctx, w_dt, None, None,
                       rows_per_mod=0, mod_row0=ctx_row, tm=min(1024, mc), rope=False)

    cw = conv_w[l]
    cb = conv_b[l][None, :]
    xs_c, bc_c = _conv(u, U_XS, U_BC, cw, cb, t, min(256, t))
    xsc_c, bcc_c = _conv(uc, UC_XS, UC_BC, cw, cb, tc, min(256, tc))
    alog = _pad_heads(a_log[l])
    bias = _pad_heads(dt_bias[l])
    e_mat = _head_expand_matrix()
    h0 = _ctx_state(xsc_c, bcc_c, dtc2, alog, bias, _tri_pair(tc), e_mat, bsz, tc)
    dsk = jnp.repeat(d_skip[l].astype(F32), SSD_HEAD_DIM)[None, :]
    y_ssd = _ssd(xs_c, bc_c, dt2, alog, bias, _tri_pair(SSD_CHUNK), e_mat, h0, u, dsk,
                 ssd_norm[l][None, :], bsz, t)

    y_attn = _attn(attn_sink[l].astype(F32), u, uc, bsz, t, tc)

    merged = _merge(y_ssd, y_attn, u, w_branch_ssd[l].astype(BF16), w_branch_attn[l].astype(BF16),
                    min(512, t))

    tm_o = min(256, t)
    w_r = jnp.zeros((d, 2 * LANES), F32)
    w_r = w_r.at[:, :MOE_GROUPS].set(w_route_group[l])
    w_r = w_r.at[:, LANES:LANES + N_EXPERTS].set(w_route_expert[l]).astype(BF16)
    li = np.arange(tm_o)
    tri_strict = jnp.asarray((li[:, None] > li[None, :]).astype(np.float32), BF16)
    x1, h2, ridx, rw, cnt = _outproj(merged, w_out[l].astype(BF16), x.reshape(m, d), mods,
                                        norm_ffn[l][None, :], w_r, tri_strict, tm_o, t)

    na = m * TOP_K
    nblk = -(-(na + N_EXPERTS * (MOE_BLOCK - 1)) // MOE_BLOCK)
    counts = cnt[0, :N_EXPERTS].astype(I32)
    padded = (counts + MOE_BLOCK - 1) // MOE_BLOCK * MOE_BLOCK
    pend = jnp.cumsum(padded)
    pstart = pend - padded
    blk_start = jnp.arange(nblk, dtype=I32) * MOE_BLOCK
    blk_e = jnp.clip(jnp.sum((pend[None, :] <= blk_start[:, None]).astype(I32), axis=1),
                     0, N_EXPERTS - 1)
    n_used = (pend[N_EXPERTS - 1:] // MOE_BLOCK).astype(I32)
    meta = jnp.concatenate([pstart + counts, pend, n_used]).astype(I32)
    e_sel = ridx[0:TOP_K]
    seg0 = jnp.zeros_like(e_sel)
    for e in range(N_EXPERTS):
        seg0 = jnp.where(e_sel == e, pstart[e], seg0)
    pos = (seg0 + ridx[TOP_K:2 * TOP_K]).reshape(-1)

    eo = _experts(blk_e, pos, meta, h2, w_gate[l].astype(BF16), w_up[l].astype(BF16),
                  w_down[l].astype(BF16), nblk)

    tm_c = min(256, t)
    out = _combine(pos, eo, x1, rw, mods, norm_final[None, :], tm_c, t)
    return out.reshape(bsz, t, d)
```

```python
import functools

import numpy as np
import jax
import jax.numpy as jnp
from jax import lax
from jax.experimental import pallas as pl
from jax.experimental.pallas import tpu as pltpu

F32 = jnp.float32
BF16 = jnp.bfloat16
I32 = jnp.int32

D_MODEL = 2048
GRID_W = 64
NORM_EPS = 1e-6
N_MOD = 6
SSD_HEADS = 16
SSD_HEAD_DIM = 64
D_SSD = SSD_HEADS * SSD_HEAD_DIM
SSD_GROUPS = 2
SSD_STATE = 128
SSD_CONV = 5
SSD_CHUNK = 128
CONV_CH = D_SSD + 2 * SSD_GROUPS * SSD_STATE
ATTN_HEADS = 8
ATTN_KV_HEADS = 2
ATTN_HEAD_DIM = 128
D_ATTN = ATTN_HEADS * ATTN_HEAD_DIM
D_KV = ATTN_KV_HEADS * ATTN_HEAD_DIM
WINDOW = 128
ROPE_BASE = 10000.0
N_BRANCH = 2
MOE_GROUPS = 4
EXPERTS_PER_GROUP = 8
N_EXPERTS = MOE_GROUPS * EXPERTS_PER_GROUP
TOP_K = 2
D_EXPERT = 512
MOE_BLOCK = 256

LANES = 128
VMEM_LIMIT = 56 * 1024 * 1024

U_XS, U_Z, U_Q, U_BC, U_K, U_V, U_G = 0, 1024, 2048, 3072, 3584, 3840, 4096
U_W = 8192
UC_XS, UC_BC, UC_K, UC_V = 0, 1024, 1536, 1792
UC_W = 2048
IN_TN = 1024


def _params(sem):
    return pltpu.CompilerParams(dimension_semantics=sem, vmem_limit_bytes=VMEM_LIMIT)


def _silu(v):
    return v * jax.nn.sigmoid(v)


def _split_bf16(v, n):
    parts = []
    r = v
    for _ in range(n):
        p = r.astype(BF16)
        parts.append(p)
        r = r - p.astype(F32)
    return parts


def _dot(a, b):
    return jnp.dot(a, b, preferred_element_type=F32)


def _dot_nt(a, b):
    return lax.dot_general(a, b, (((1,), (1,)), ((), ())), preferred_element_type=F32)


def _ada_kernel(c_ref, w_ref, b_ref, o_ref):
    a = _silu(c_ref[...]).astype(BF16)
    o_ref[...] = _dot(a, w_ref[...].astype(BF16)) + b_ref[...]


def _ada(cc, w, b):
    rows, d = cc.shape
    n = w.shape[1]
    tn = 1024
    return pl.pallas_call(
        _ada_kernel,
        grid=(n // tn,),
        in_specs=[pl.BlockSpec((rows, d), lambda j: (0, 0)),
                  pl.BlockSpec((d, tn), lambda j: (0, j)),
                  pl.BlockSpec((1, tn), lambda j: (0, j))],
        out_specs=pl.BlockSpec((rows, tn), lambda j: (0, j)),
        out_shape=jax.ShapeDtypeStruct((rows, n), F32),
        compiler_params=_params(("arbitrary",)),
        name="ada",
    )(cc, w, b)


def _rope(a, cos, sin_signed, first):
    partner = jnp.where(first, pltpu.roll(a, 96, 1), pltpu.roll(a, 32, 1))
    return a * cos + partner * sin_signed


def _inproj_kernel(*refs, rope_q_blocks, rope_kv_block, sub):
    if rope_q_blocks:
        x_ref, nw_ref, mod_ref, w_ref, wdt_ref, cos_ref, sin_ref, o_ref, dt_ref, h_scr = refs
    else:
        x_ref, nw_ref, mod_ref, w_ref, wdt_ref, o_ref, dt_ref, h_scr = refs
    j = pl.program_id(1)
    tm = x_ref.shape[0]
    tn = w_ref.shape[1]

    @pl.when(j == 0)
    def _():
        nw = nw_ref[...]
        sh = mod_ref[0, 0:1, :]
        sc = mod_ref[0, 1:2, :]

        def body(r, carry):
            rows = pl.ds(pl.multiple_of(r * sub, sub), sub)
            xf = x_ref[rows, :]
            ms = jnp.mean(xf * xf, axis=-1, keepdims=True)
            h = (xf * lax.rsqrt(ms + NORM_EPS) * nw) * (1.0 + sc) + sh
            hb = h.astype(BF16)
            h_scr[rows, :] = hb
            d = _dot(hb, wdt_ref[...])
            dt_ref[0, rows, :] = d[:, :LANES]
            dt_ref[1, rows, :] = d[:, LANES:]
            return carry

        lax.fori_loop(0, tm // sub, body, 0)

    n_split = 2 if tm >= 512 else 1
    part = tm // n_split

    def store(lo, hi):
        for sp in range(n_split):
            rows = slice(sp * part, (sp + 1) * part)
            acc = _dot(h_scr[rows, :], w_ref[...])
            if hi > lo:
                cos = cos_ref[rows, :]
                sin = sin_ref[rows, :]
                lane = lax.broadcasted_iota(I32, (part, LANES), 1)
                first = (lane % 64) < 32
            for hh in range(tn // LANES):
                a = acc[:, hh * LANES:(hh + 1) * LANES]
                if lo <= hh < hi:
                    a = _rope(a, cos, sin, first)
                o_ref[rows, hh * LANES:(hh + 1) * LANES] = a.astype(BF16)

    if rope_q_blocks:
        is_q = (j >= rope_q_blocks[0]) & (j <= rope_q_blocks[-1])
        is_kv = j == rope_kv_block
        k_lo = (U_K % tn) // LANES

        @pl.when(is_q)
        def _():
            store(0, tn // LANES)

        @pl.when(is_kv)
        def _():
            store(k_lo, k_lo + D_KV // LANES)

        @pl.when(jnp.logical_not(is_q | is_kv))
        def _():
            store(0, 0)
    else:
        store(0, 0)


def _inproj(x2, nw, mods, w, wdt, cos_t, sin_t, *, rows_per_mod, mod_row0, tm, rope):
    m, d = x2.shape
    n = w.shape[1]
    tn = IN_TN
    nbm = rows_per_mod // tm if rows_per_mod else 0
    if rows_per_mod:
        mod_map = lambda i, j: (mod_row0 + i // nbm, 0, 0)
    else:
        mod_map = lambda i, j: (mod_row0, 0, 0)
    in_specs = [pl.BlockSpec((tm, d), lambda i, j: (i, 0)),
                pl.BlockSpec((1, d), lambda i, j: (0, 0)),
                pl.BlockSpec((1, N_MOD, d), mod_map),
                pl.BlockSpec((d, tn), lambda i, j: (0, j)),
                pl.BlockSpec((d, 2 * LANES), lambda i, j: (0, 0))]
    args = [x2, nw, mods, w, wdt]
    if rope:
        in_specs += [pl.BlockSpec((tm, LANES), lambda i, j: (i % nbm, 0)),
                     pl.BlockSpec((tm, LANES), lambda i, j: (i % nbm, 0))]
        args += [cos_t, sin_t]
        rq = tuple(range(U_Q // tn, (U_Q + D_ATTN) // tn))
        rkv = U_K // tn
    else:
        rq, rkv = (), None
    kern = functools.partial(_inproj_kernel, rope_q_blocks=rq, rope_kv_block=rkv, sub=128)
    return pl.pallas_call(
        kern,
        grid=(m // tm, n // tn),
        in_specs=in_specs,
        out_specs=[pl.BlockSpec((tm, tn), lambda i, j: (i, j)),
                   pl.BlockSpec((2, tm, LANES), lambda i, j: (0, i, 0))],
        out_shape=[jax.ShapeDtypeStruct((m, n), BF16),
                   jax.ShapeDtypeStruct((2, m, LANES), F32)],
        scratch_shapes=[pltpu.VMEM((tm, d), BF16)],
        compiler_params=_params(("arbitrary", "arbitrary")),
        name="inproj_rope" if rope else "inproj_ctx",
    )(*args)


CONV_HALO = 16


def _conv_kernel(xs_ref, bc_ref, pxs_ref, pbc_ref, nxs_ref, nbc_ref, w_ref, b_ref,
                 oxs_ref, obc_ref, ext, *, blocks_per_seq):
    i = pl.program_id(0)
    r = xs_ref.shape[0]
    pos = i % blocks_per_seq
    not_first = pos != 0
    not_last = pos != blocks_per_seq - 1
    h = CONV_HALO
    ext[0:h, 0:D_SSD] = jnp.where(not_first, pxs_ref[...].astype(F32), 0.0)
    ext[0:h, D_SSD:] = jnp.where(not_first, pbc_ref[...].astype(F32), 0.0)
    ext[h:h + r, 0:D_SSD] = xs_ref[...].astype(F32)
    ext[h:h + r, D_SSD:] = bc_ref[...].astype(F32)
    ext[h + r:, 0:D_SSD] = jnp.where(not_last, nxs_ref[...].astype(F32), 0.0)
    ext[h + r:, D_SSD:] = jnp.where(not_last, nbc_ref[...].astype(F32), 0.0)
    cw = 256
    pad = SSD_CONV // 2
    for cb in range(0, CONV_CH, cw):
        acc = jnp.broadcast_to(b_ref[:, cb:cb + cw], (r, cw))
        for k in range(SSD_CONV):
            acc = acc + ext[h - pad + k:h - pad + k + r, cb:cb + cw] * w_ref[k:k + 1, cb:cb + cw]
        y = _silu(acc)
        if cb < D_SSD:
            oxs_ref[:, cb:cb + cw] = y
        else:
            obc_ref[:, cb - D_SSD:cb - D_SSD + cw] = y.astype(BF16)


def _conv(u, xs_col, bc_col, conv_w, conv_b, seq_len, r):
    m = u.shape[0]
    h = CONV_HALO
    bps = seq_len // r
    wbc = CONV_CH - D_SSD
    xs_c, bc_c = xs_col // D_SSD, bc_col // wbc
    nh = m // h
    prev = lambda i: jnp.maximum(i * (r // h) - 1, 0)
    nxt = lambda i: jnp.minimum((i + 1) * (r // h), nh - 1)
    return pl.pallas_call(
        functools.partial(_conv_kernel, blocks_per_seq=bps),
        grid=(m // r,),
        in_specs=[pl.BlockSpec((r, D_SSD), lambda i: (i, xs_c)),
                  pl.BlockSpec((r, wbc), lambda i: (i, bc_c)),
                  pl.BlockSpec((h, D_SSD), lambda i: (prev(i), xs_c)),
                  pl.BlockSpec((h, wbc), lambda i: (prev(i), bc_c)),
                  pl.BlockSpec((h, D_SSD), lambda i: (nxt(i), xs_c)),
                  pl.BlockSpec((h, wbc), lambda i: (nxt(i), bc_c)),
                  pl.BlockSpec((SSD_CONV, CONV_CH), lambda i: (0, 0)),
                  pl.BlockSpec((1, CONV_CH), lambda i: (0, 0))],
        out_specs=[pl.BlockSpec((r, D_SSD), lambda i: (i, 0)),
                   pl.BlockSpec((r, wbc), lambda i: (i, 0))],
        out_shape=[jax.ShapeDtypeStruct((m, D_SSD), F32),
                   jax.ShapeDtypeStruct((m, wbc), BF16)],
        scratch_shapes=[pltpu.VMEM((r + 2 * h, CONV_CH), F32)],
        compiler_params=_params(("arbitrary",)),
        name="conv_silu",
    )(u, u, u, u, u, u, conv_w, conv_b)


def _expand_heads(v, e2):
    return _dot(jnp.concatenate(_split_bf16(v, 2), axis=1), e2)


def _cumsum_mm(tri3, da):
    return _dot(tri3, jnp.concatenate(_split_bf16(da, 3), axis=0))


def _ctx_state_kernel(xs_ref, b_ref, dt_ref, alog_ref, bias_ref, tri_ref, e_ref, h_ref):
    d = pl.program_id(1)
    tc = xs_ref.shape[0]
    dt = jax.nn.softplus(dt_ref[0] + bias_ref[0])
    a = -jnp.exp(alog_ref[0])
    cum = _cumsum_mm(tri_ref[0], dt * a)
    cum_end = jnp.where(d == 0, cum[tc - 1:tc, :], cum[0:1, :])
    wx = _expand_heads(dt * jnp.exp(cum_end - cum), e_ref[...])
    xw = (xs_ref[...] * wx).astype(BF16)
    gw = D_SSD // SSD_GROUPS
    for g in range(SSD_GROUPS):
        bgt = b_ref[:, g * SSD_STATE:(g + 1) * SSD_STATE].astype(F32).T.astype(BF16)
        h_ref[0, 0, :, g * gw:(g + 1) * gw] = _dot(bgt, xw[:, g * gw:(g + 1) * gw])


def _ctx_state(xs_c, bc_c, dt2, alog, bias, tri3, e2, bsz, tc):
    return pl.pallas_call(
        _ctx_state_kernel,
        grid=(bsz, 2),
        in_specs=[pl.BlockSpec((tc, D_SSD), lambda b, d: (b, 0)),
                  pl.BlockSpec((tc, SSD_GROUPS * SSD_STATE), lambda b, d: (b, 0)),
                  pl.BlockSpec((1, tc, LANES), lambda b, d: (d, b, 0)),
                  pl.BlockSpec((1, 1, LANES), lambda b, d: (d, 0, 0)),
                  pl.BlockSpec((1, 1, LANES), lambda b, d: (d, 0, 0)),
                  pl.BlockSpec((1, tc, 3 * tc), lambda b, d: (d, 0, 0)),
                  pl.BlockSpec((2 * LANES, D_SSD), lambda b, d: (0, 0))],
        out_specs=pl.BlockSpec((1, 1, SSD_STATE, D_SSD), lambda b, d: (b, d, 0, 0)),
        out_shape=jax.ShapeDtypeStruct((bsz, 2, SSD_STATE, D_SSD), F32),
        compiler_params=_params(("arbitrary", "arbitrary")),
        name="ssd_ctx_state",
    )(xs_c, bc_c, dt2, alog, bias, tri3, e2)


def _ssd_chunk(fwd, xs, bc_ref, dt_raw, alog, bias, tri3, e2, st, y_out):
    lc = SSD_CHUNK
    dt = jax.nn.softplus(dt_raw + bias)
    cum = _cumsum_mm(tri3, dt * (-jnp.exp(alog)))
    cum_t = cum.T
    dt_t = dt.T
    end = lc - 1 if fwd else 0
    ecx = _expand_heads(jnp.exp(cum), e2)
    wx = _expand_heads(dt * jnp.exp(cum[end:end + 1, :] - cum), e2)
    xs_b = xs.astype(BF16)
    xw_b = (xs * wx).astype(BF16)
    cd_row = ecx[end:end + 1, :]
    li = lax.broadcasted_iota(I32, (lc, lc), 0)
    si = lax.broadcasted_iota(I32, (lc, lc), 1)
    causal = (li >= si) if fwd else (li <= si)
    low = lax.broadcasted_iota(I32, (lc, LANES), 1) < SSD_HEAD_DIM
    gw = D_SSD // SSD_GROUPS
    hpg = SSD_HEADS // SSD_GROUPS
    for g in range(SSD_GROUPS):
        bg = bc_ref[:, g * SSD_STATE:(g + 1) * SSD_STATE]
        cg = bc_ref[:, (SSD_GROUPS + g) * SSD_STATE:(SSD_GROUPS + g + 1) * SSD_STATE]
        cb = _dot_nt(cg, bg)
        yoff = _dot(cg, st[:, g * gw:(g + 1) * gw].astype(BF16)) * ecx[:, g * gw:(g + 1) * gw]
        for pr in range(hpg // 2):
            h0 = g * hpg + 2 * pr
            ms = []
            for h in (h0, h0 + 1):
                seg = cum[:, h:h + 1] - cum_t[h:h + 1, :]
                ms.append((cb * jnp.exp(jnp.where(causal, seg, -jnp.inf)) * dt_t[h:h + 1, :]).astype(BF16))
            xp = xs_b[:, h0 * SSD_HEAD_DIM:(h0 + 2) * SSD_HEAD_DIM]
            zero = jnp.zeros_like(xp)
            rhs = jnp.concatenate([jnp.where(low, xp, zero), jnp.where(low, zero, xp)], axis=0)
            y_out[:, h0 * SSD_HEAD_DIM:(h0 + 2) * SSD_HEAD_DIM] = (
                _dot(jnp.concatenate(ms, axis=1), rhs) + yoff[:, pr * LANES:(pr + 1) * LANES])
        bgt = bg.astype(F32).T.astype(BF16)
        st[:, g * gw:(g + 1) * gw] = (st[:, g * gw:(g + 1) * gw] * cd_row[:, g * gw:(g + 1) * gw]
                                      + _dot(bgt, xw_b[:, g * gw:(g + 1) * gw]))


def _ssd_kernel(xsf_ref, xsb_ref, bcf_ref, bcb_ref, dtf_ref, dtb_ref, alog_ref, bias_ref, tri_ref,
                e_ref, h0_ref, z_ref, dsk_ref, nw_ref, o_ref, st, ybuf, ycur, *, nc):
    s = pl.program_id(1)
    lc = SSD_CHUNK

    @pl.when(s == 0)
    def _():
        st[...] = h0_ref[0]

    e2 = e_ref[...]
    _ssd_chunk(True, xsf_ref[...], bcf_ref, dtf_ref[0], alog_ref[0], bias_ref[0], tri_ref[0], e2,
               st.at[0], ycur.at[0])
    _ssd_chunk(False, xsb_ref[...], bcb_ref, dtb_ref[0], alog_ref[1], bias_ref[1], tri_ref[1], e2,
               st.at[1], ycur.at[1])
    rows = (pl.ds(pl.multiple_of(s * lc, lc), lc), pl.ds(pl.multiple_of((nc - 1 - s) * lc, lc), lc))

    @pl.when(s < nc // 2)
    def _():
        for d in range(2):
            ybuf[rows[d], :] = ycur[d]

    @pl.when(s >= nc // 2)
    def _():
        gw = D_SSD // SSD_GROUPS
        for d, xs_ref in enumerate((xsf_ref, xsb_ref)):
            y = ycur[d] + ybuf[rows[d], :] + dsk_ref[...] * xs_ref[...]
            gy = y * _silu(z_ref[rows[d], :].astype(F32))
            for g in range(SSD_GROUPS):
                blk = gy[:, g * gw:(g + 1) * gw]
                ms = jnp.mean(blk * blk, axis=-1, keepdims=True)
                o_ref[rows[d], g * gw:(g + 1) * gw] = (blk * lax.rsqrt(ms + NORM_EPS)
                                                       * nw_ref[:, g * gw:(g + 1) * gw]).astype(BF16)


def _ssd(xs_c, bc_c, dt2, alog, bias, tri3, e2, h0, u, dsk, nw, bsz, t):
    nc = t // SSD_CHUNK
    lc = SSD_CHUNK
    m = bsz * t
    fw = lambda b, s: b * nc + s
    bw = lambda b, s: b * nc + nc - 1 - s
    zc = U_Z // D_SSD
    bcw = 2 * SSD_GROUPS * SSD_STATE
    return pl.pallas_call(
        functools.partial(_ssd_kernel, nc=nc),
        grid=(bsz, nc),
        in_specs=[pl.BlockSpec((lc, D_SSD), lambda b, s: (fw(b, s), 0)),
                  pl.BlockSpec((lc, D_SSD), lambda b, s: (bw(b, s), 0)),
                  pl.BlockSpec((lc, bcw), lambda b, s: (fw(b, s), 0)),
                  pl.BlockSpec((lc, bcw), lambda b, s: (bw(b, s), 0)),
                  pl.BlockSpec((1, lc, LANES), lambda b, s: (0, fw(b, s), 0)),
                  pl.BlockSpec((1, lc, LANES), lambda b, s: (1, bw(b, s), 0)),
                  pl.BlockSpec((2, 1, LANES), lambda b, s: (0, 0, 0)),
                  pl.BlockSpec((2, 1, LANES), lambda b, s: (0, 0, 0)),
                  pl.BlockSpec((2, lc, 3 * lc), lambda b, s: (0, 0, 0)),
                  pl.BlockSpec((2 * LANES, D_SSD), lambda b, s: (0, 0)),
                  pl.BlockSpec((1, 2, SSD_STATE, D_SSD), lambda b, s: (b, 0, 0, 0)),
                  pl.BlockSpec((t, D_SSD), lambda b, s: (b, zc)),
                  pl.BlockSpec((1, D_SSD), lambda b, s: (0, 0)),
                  pl.BlockSpec((1, D_SSD), lambda b, s: (0, 0))],
        out_specs=pl.BlockSpec((t, D_SSD), lambda b, s: (b, 0)),
        out_shape=jax.ShapeDtypeStruct((m, D_SSD), BF16),
        scratch_shapes=[pltpu.VMEM((2, SSD_STATE, D_SSD), F32),
                        pltpu.VMEM((t, D_SSD), F32),
                        pltpu.VMEM((2, lc, D_SSD), F32)],
        compiler_params=_params(("arbitrary", "arbitrary")),
        name="ssd_scan",
    )(xs_c, xs_c, bc_c, bc_c, dt2, dt2, alog, bias, tri3, e2, h0, u, dsk, nw)


def _attn_kernel(sink_ref, q_ref, k_ref, v_ref, kc_ref, vc_ref, o_ref, kp, vp, bias, *, t):
    g = pl.program_id(1)
    w = WINDOW
    nb = t // w
    rep = ATTN_HEADS // ATTN_KV_HEADS
    tc = kc_ref.shape[0]
    nk = 3 * w + tc
    zeros = jnp.zeros((w, ATTN_HEAD_DIM), BF16)
    kp[0:w, :] = zeros
    kp[w:w + t, :] = k_ref[...]
    kp[w + t:, :] = zeros
    vp[0:w, :] = zeros
    vp[w:w + t, :] = v_ref[...]
    vp[w + t:, :] = zeros
    qi = lax.broadcasted_iota(I32, (rep * w, w), 0) % w
    kj = lax.broadcasted_iota(I32, (rep * w, w), 1)
    bias[0] = jnp.where(kj >= qi, 0.0, -jnp.inf).astype(F32)
    bias[1] = jnp.where(kj <= qi, 0.0, -jnp.inf).astype(F32)
    sink_col = jnp.concatenate(
        [jnp.full((w, 1), sink_ref[g * rep + h], F32) for h in range(rep)], axis=0)
    scale = ATTN_HEAD_DIM ** -0.5
    log2e = float(np.log2(np.e))

    def body(n, carry):
        rows = pl.ds(pl.multiple_of(n * w, w), w)
        q = jnp.concatenate([q_ref[rows, h * ATTN_HEAD_DIM:(h + 1) * ATTN_HEAD_DIM]
                             for h in range(rep)], axis=0)
        band = pl.ds(pl.multiple_of(n * w, w), 3 * w)
        kall = jnp.concatenate([kp[band, :], kc_ref[...]], axis=0)
        vall = jnp.concatenate([vp[band, :], vc_ref[...]], axis=0)
        raw = _dot_nt(q, kall)
        prev = raw[:, 0:w] + (bias[0] + jnp.where(n == 0, -jnp.inf, 0.0))
        nxt = raw[:, 2 * w:3 * w] + (bias[1] + jnp.where(n == nb - 1, -jnp.inf, 0.0))
        sc = jnp.concatenate([prev, raw[:, w:2 * w], nxt, raw[:, 3 * w:]], axis=1)
        mx = jnp.maximum(jnp.max(sc, axis=-1, keepdims=True) * scale, sink_col)
        p = jnp.exp2(sc * (scale * log2e) - mx * log2e)
        den = jnp.sum(p, axis=-1, keepdims=True) + jnp.exp2((sink_col - mx) * log2e)
        o = _dot(p.astype(BF16), vall) / den
        for h in range(rep):
            o_ref[rows, h * ATTN_HEAD_DIM:(h + 1) * ATTN_HEAD_DIM] = o[h * w:(h + 1) * w, :].astype(BF16)
        return carry

    lax.fori_loop(0, nb, body, 0, unroll=2)


def _attn(sink, u, uc, bsz, t, tc):
    rep = ATTN_HEADS // ATTN_KV_HEADS
    qw = rep * ATTN_HEAD_DIM
    hd = ATTN_HEAD_DIM
    return pl.pallas_call(
        functools.partial(_attn_kernel, t=t),
        grid=(bsz, ATTN_KV_HEADS),
        in_specs=[pl.BlockSpec(memory_space=pltpu.SMEM),
                  pl.BlockSpec((t, qw), lambda b, g: (b, U_Q // qw + g)),
                  pl.BlockSpec((t, hd), lambda b, g: (b, U_K // hd + g)),
                  pl.BlockSpec((t, hd), lambda b, g: (b, U_V // hd + g)),
                  pl.BlockSpec((tc, hd), lambda b, g: (b, UC_K // hd + g)),
                  pl.BlockSpec((tc, hd), lambda b, g: (b, UC_V // hd + g))],
        out_specs=pl.BlockSpec((t, qw), lambda b, g: (b, g)),
        out_shape=jax.ShapeDtypeStruct((bsz * t, D_ATTN), BF16),
        scratch_shapes=[pltpu.VMEM((t + 2 * WINDOW, hd), BF16),
                        pltpu.VMEM((t + 2 * WINDOW, hd), BF16),
                        pltpu.VMEM((2, rep * WINDOW, WINDOW), F32)],
        compiler_params=_params(("arbitrary", "arbitrary")),
        name="window_attn",
    )(sink, u, u, u, uc, uc)


def _merge_kernel(ys_ref, ya_ref, g0_ref, g1_ref, ws_ref, wa_ref, o_ref):
    ys = ys_ref[...]
    ya = ya_ref[...]
    cw = 512
    for cb in range(0, D_MODEL, cw):
        t0 = _dot(ys, ws_ref[:, cb:cb + cw])
        t1 = _dot(ya, wa_ref[:, cb:cb + cw])
        g0 = jax.nn.sigmoid(g0_ref[:, cb:cb + cw].astype(F32))
        g1 = jax.nn.sigmoid(g1_ref[:, cb:cb + cw].astype(F32))
        o_ref[:, cb:cb + cw] = (g0 * t0 + g1 * t1).astype(BF16)


def _merge(ys, ya, u, ws, wa, tm):
    m = ys.shape[0]
    gc = U_G // D_MODEL
    return pl.pallas_call(
        _merge_kernel,
        grid=(m // tm,),
        in_specs=[pl.BlockSpec((tm, D_SSD), lambda i: (i, 0)),
                  pl.BlockSpec((tm, D_ATTN), lambda i: (i, 0)),
                  pl.BlockSpec((tm, D_MODEL), lambda i: (i, gc)),
                  pl.BlockSpec((tm, D_MODEL), lambda i: (i, gc + 1)),
                  pl.BlockSpec((D_SSD, D_MODEL), lambda i: (0, 0)),
                  pl.BlockSpec((D_ATTN, D_MODEL), lambda i: (0, 0))],
        out_specs=pl.BlockSpec((tm, D_MODEL), lambda i: (i, 0)),
        out_shape=jax.ShapeDtypeStruct((m, D_MODEL), BF16),
        compiler_params=_params(("arbitrary",)),
        name="branch_merge",
    )(ys, ya, u, u, ws, wa)


def _lane_min_index(mask, lane):
    return jnp.min(jnp.where(mask, lane, LANES), axis=-1, keepdims=True)


def _outproj_kernel(mg_ref, wo_ref, x_ref, mod_ref, nw_ref, wr_ref, tri_ref,
                    x1_ref, h2_ref, ridx_ref, rw_ref, cnt_ref, carry):
    i = pl.program_id(0)
    pt = tri_ref.shape[0]

    @pl.when(i == 0)
    def _():
        carry[...] = jnp.zeros_like(carry)

    for p in range(x_ref.shape[0] // pt):
        _outproj_part(slice(p * pt, (p + 1) * pt), slice(p * pt, (p + 1) * pt), mg_ref, wo_ref,
                      x_ref, mod_ref, nw_ref, wr_ref, tri_ref, x1_ref, h2_ref, ridx_ref, rw_ref,
                      cnt_ref, carry)


def _outproj_part(rows, cols, mg_ref, wo_ref, x_ref, mod_ref, nw_ref, wr_ref, tri_ref,
                  x1_ref, h2_ref, ridx_ref, rw_ref, cnt_ref, carry):
    tm = tri_ref.shape[0]
    g1 = mod_ref[0, 2:3, :]
    sh2 = mod_ref[0, 3:4, :]
    sc2 = mod_ref[0, 4:5, :]
    x1 = x_ref[rows, :] + g1 * _dot(mg_ref[rows, :], wo_ref[...])
    x1_ref[rows, :] = x1
    ms = jnp.mean(x1 * x1, axis=-1, keepdims=True)
    h2 = (x1 * lax.rsqrt(ms + NORM_EPS) * nw_ref[...]) * (1.0 + sc2) + sh2
    h2_ref[rows, :] = h2

    lg = _dot(h2.astype(BF16), wr_ref[...])
    lane = lax.broadcasted_iota(I32, (tm, LANES), 1)
    gl = jnp.where(lane < MOE_GROUPS, lg[:, :LANES], -jnp.inf)
    gmax = jnp.max(gl, axis=-1, keepdims=True)
    g_w = 1.0 / jnp.sum(jnp.exp(gl - gmax), axis=-1, keepdims=True)
    g_idx = _lane_min_index(gl == gmax, lane)
    el = jnp.where((lane // EXPERTS_PER_GROUP) == g_idx, lg[:, LANES:], -jnp.inf)
    v1 = jnp.max(el, axis=-1, keepdims=True)
    i1 = _lane_min_index(el == v1, lane)
    el2 = jnp.where(lane == i1, -jnp.inf, el)
    v2 = jnp.max(el2, axis=-1, keepdims=True)
    i2 = _lane_min_index(el2 == v2, lane)
    e2 = jnp.exp(v2 - v1)
    w1 = g_w * (1.0 / (1.0 + e2))
    w2 = g_w * (e2 / (1.0 + e2))

    hit1 = lane == i1
    hit2 = lane == i2
    onehot = jnp.where(hit1 | hit2, 1.0, 0.0).astype(F32)
    before = _dot(tri_ref[...], onehot.astype(BF16)) + carry[0:1, :]
    r1 = jnp.sum(jnp.where(hit1, before, 0.0), axis=-1, keepdims=True)
    r2 = jnp.sum(jnp.where(hit2, before, 0.0), axis=-1, keepdims=True)
    total = carry[0:1, :] + jnp.sum(onehot, axis=0, keepdims=True)
    carry[...] = jnp.broadcast_to(total, carry.shape)
    cnt_ref[...] = jnp.broadcast_to(total, cnt_ref.shape)
    packed = jnp.where(lane == 0, i1.astype(F32), jnp.where(lane == 1, i2.astype(F32), jnp.where(
        lane == 2, r1, jnp.where(lane == 3, r2, 0.0))))
    ridx_ref[:, cols] = packed.T[0:8, :].astype(I32)
    rw_ref[rows, :] = jnp.where(lane == 0, w1, jnp.where(lane == 1, w2, 0.0))


def _outproj(mg, wo, x2, mods, nw, wr, tri, tm, t):
    m = x2.shape[0]
    nbm = t // tm
    pt = tri.shape[0]
    return pl.pallas_call(
        _outproj_kernel,
        grid=(m // tm,),
        in_specs=[pl.BlockSpec((tm, D_MODEL), lambda i: (i, 0)),
                  pl.BlockSpec((D_MODEL, D_MODEL), lambda i: (0, 0), pipeline_mode=pl.Buffered(1)),
                  pl.BlockSpec((tm, D_MODEL), lambda i: (i, 0)),
                  pl.BlockSpec((1, N_MOD, D_MODEL), lambda i: (i // nbm, 0, 0)),
                  pl.BlockSpec((1, D_MODEL), lambda i: (0, 0)),
                  pl.BlockSpec((D_MODEL, 2 * LANES), lambda i: (0, 0)),
                  pl.BlockSpec((pt, pt), lambda i: (0, 0))],
        out_specs=[pl.BlockSpec((tm, D_MODEL), lambda i: (i, 0)),
                   pl.BlockSpec((tm, D_MODEL), lambda i: (i, 0)),
                   pl.BlockSpec((8, tm), lambda i: (0, i)),
                   pl.BlockSpec((tm, LANES), lambda i: (i, 0)),
                   pl.BlockSpec((8, LANES), lambda i: (0, 0))],
        out_shape=[jax.ShapeDtypeStruct((m, D_MODEL), F32),
                   jax.ShapeDtypeStruct((m, D_MODEL), F32),
                   jax.ShapeDtypeStruct((8, m), I32),
                   jax.ShapeDtypeStruct((m, LANES), F32),
                   jax.ShapeDtypeStruct((8, LANES), F32)],
        scratch_shapes=[pltpu.VMEM((8, LANES), F32)],
        compiler_params=_params(("arbitrary",)),
        name="outproj_router",
    )(mg, wo, x2, mods, nw, wr, tri)


def _gather_rows(idx_ref, base, n, src, dst, sem, row0=0):
    for r in range(n):
        tok = idx_ref[base + r]
        pltpu.make_async_copy(src.at[pl.ds(tok, 1), :], dst.at[pl.ds(row0 + r, 1), :],
                              sem).start(priority=r % 2)


def _gather_rows_loop(idx_ref, base, n, src, dst, sem):
    group = 8

    def body(j, carry):
        _gather_rows(idx_ref, base + j * group, group, src, dst, sem, row0=j * group)
        return carry

    lax.fori_loop(0, n // group, body, 0)


def _expert_kernel(blk_e_ref, pos_ref, meta_ref, h2_hbm, wg_ref, wu_ref, wd_ref, o_ref,
                   xb, tok_ref, sem):
    i = pl.program_id(0)
    n_used = meta_ref[2 * N_EXPERTS]
    slot = i % 2
    rows = MOE_BLOCK
    m = pos_ref.shape[0] // TOP_K

    def wait_block(sl):
        pltpu.make_async_copy(h2_hbm.at[pl.ds(0, rows), :], xb.at[sl], sem.at[sl]).wait()

    @pl.when(i == 0)
    def _():
        def pad_expert(e, carry):
            def pad_slot(s, c):
                tok_ref[s] = 0
                return c
            return lax.fori_loop(meta_ref[e], meta_ref[N_EXPERTS + e], pad_slot, carry)

        lax.fori_loop(0, N_EXPERTS, pad_expert, 0)

        def place(t, carry):
            for k in range(TOP_K):
                tok_ref[pos_ref[k * m + t]] = t
            return carry

        lax.fori_loop(0, m, place, 0, unroll=8)
        _gather_rows_loop(tok_ref, 0, rows, h2_hbm, xb.at[0], sem.at[0])

    @pl.when(i < n_used)
    def _():
        wait_block(slot)
        nxt = jnp.minimum(i + 1, n_used - 1)
        nk = 8
        per = rows // nk
        kw = D_MODEL // nk
        xs = []
        for kt in range(nk):
            xs.append(xb[slot, :, kt * kw:(kt + 1) * kw].astype(BF16))
            _gather_rows(tok_ref, nxt * rows + kt * per, per, h2_hbm, xb.at[1 - slot],
                         sem.at[1 - slot], row0=kt * per)
        x = jnp.concatenate(xs, axis=1)
        act = (_silu(_dot(x, wg_ref[0])) * _dot(x, wu_ref[0])).astype(BF16)
        o_ref[...] = _dot(act, wd_ref[0])

        @pl.when(i == n_used - 1)
        def _():
            wait_block(1 - slot)

    @pl.when(i >= n_used)
    def _():
        o_ref[...] = jnp.zeros_like(o_ref)


def _experts(blk_e, pos, meta, h2, wg, wu, wd, nblk):
    rows = MOE_BLOCK
    live = lambda i, mt: jnp.minimum(i, mt[2 * N_EXPERTS] - 1)
    grid_spec = pltpu.PrefetchScalarGridSpec(
        num_scalar_prefetch=3,
        grid=(nblk,),
        in_specs=[pl.BlockSpec(memory_space=pl.ANY),
                  pl.BlockSpec((1, D_MODEL, D_EXPERT), lambda i, be, ps, mt: (be[live(i, mt)], 0, 0)),
                  pl.BlockSpec((1, D_MODEL, D_EXPERT), lambda i, be, ps, mt: (be[live(i, mt)], 0, 0)),
                  pl.BlockSpec((1, D_EXPERT, D_MODEL), lambda i, be, ps, mt: (be[live(i, mt)], 0, 0))],
        out_specs=pl.BlockSpec((rows, D_MODEL), lambda i, be, ps, mt: (i, 0)),
        scratch_shapes=[pltpu.VMEM((2, rows, D_MODEL), F32),
                        pltpu.SMEM((nblk * rows,), I32),
                        pltpu.SemaphoreType.DMA((2,))],
    )
    return pl.pallas_call(
        _expert_kernel,
        grid_spec=grid_spec,
        out_shape=jax.ShapeDtypeStruct((nblk * rows, D_MODEL), F32),
        compiler_params=_params(("arbitrary",)),
        name="expert_mlp",
    )(blk_e, pos, meta, h2, wg, wu, wd)


def _combine_kernel(pos_ref, eo_hbm, x1_ref, rw_ref, mod_ref, nf_ref, o_ref, gb, sem):
    i = pl.program_id(0)
    nsteps = pl.num_programs(0)
    tm = x1_ref.shape[0]
    slot = i % 2

    def issue(step, sl):
        for k in range(TOP_K):
            _gather_rows(pos_ref, (k * nsteps + step) * tm, tm, eo_hbm, gb.at[sl, k], sem.at[sl])

    @pl.when(i == 0)
    def _():
        issue(0, 0)

    @pl.when(i + 1 < nsteps)
    def _():
        issue(i + 1, 1 - slot)

    for k in range(TOP_K):
        pltpu.make_async_copy(eo_hbm.at[pl.ds(0, tm), :], gb.at[slot, k], sem.at[slot]).wait()
    w1 = rw_ref[:, 0:1]
    w2 = rw_ref[:, 1:2]
    g2 = mod_ref[0, 5:6, :]
    x2 = x1_ref[...] + g2 * (gb[slot, 0] * w1 + gb[slot, 1] * w2)
    ms = jnp.mean(x2 * x2, axis=-1, keepdims=True)
    o_ref[...] = x2 * lax.rsqrt(ms + NORM_EPS) * nf_ref[...]


def _combine(pos, eo, x1, rw, mods, nf, tm, t):
    m = x1.shape[0]
    nbm = t // tm
    grid_spec = pltpu.PrefetchScalarGridSpec(
        num_scalar_prefetch=1,
        grid=(m // tm,),
        in_specs=[pl.BlockSpec(memory_space=pl.ANY),
                  pl.BlockSpec((tm, D_MODEL), lambda i, p: (i, 0)),
                  pl.BlockSpec((tm, LANES), lambda i, p: (i, 0)),
                  pl.BlockSpec((1, N_MOD, D_MODEL), lambda i, p: (i // nbm, 0, 0)),
                  pl.BlockSpec((1, D_MODEL), lambda i, p: (0, 0))],
        out_specs=pl.BlockSpec((tm, D_MODEL), lambda i, p: (i, 0)),
        scratch_shapes=[pltpu.VMEM((2, TOP_K, tm, D_MODEL), F32),
                        pltpu.SemaphoreType.DMA((2,))],
    )
    return pl.pallas_call(
        _combine_kernel,
        grid_spec=grid_spec,
        out_shape=jax.ShapeDtypeStruct((m, D_MODEL), F32),
        compiler_params=_params(("arbitrary",)),
        name="moe_combine_norm",
    )(pos, eo, x1, rw, mods, nf)


def _rope_tables(t):
    half = ATTN_HEAD_DIM // 4
    pos = jnp.arange(t, dtype=I32)
    row = (pos // GRID_W).astype(F32)
    col = (pos % GRID_W).astype(F32)
    freqs = ROPE_BASE ** (-jnp.arange(half, dtype=F32) / half)
    ar = row[:, None] * freqs[None, :]
    ac = col[:, None] * freqs[None, :]
    cos_t = jnp.concatenate([jnp.cos(ar), jnp.cos(ar), jnp.cos(ac), jnp.cos(ac)], axis=1)
    sin_t = jnp.concatenate([-jnp.sin(ar), jnp.sin(ar), -jnp.sin(ac), jnp.sin(ac)], axis=1)
    return cos_t, sin_t


def _tri_pair3(n):
    li = np.arange(n)[:, None]
    si = np.arange(n)[None, :]
    pair = np.stack([li >= si, li <= si]).astype(np.float32)
    return jnp.asarray(np.concatenate([pair] * 3, axis=2), BF16)


def _head_expand_matrix2():
    k = np.arange(LANES)[:, None]
    j = np.arange(D_SSD)[None, :]
    e = (k == j // SSD_HEAD_DIM).astype(np.float32)
    return jnp.asarray(np.concatenate([e, e], axis=0), BF16)


def _pad_heads(v):
    return jnp.pad(v.astype(F32), ((0, 0), (0, LANES - SSD_HEADS)))[:, None, :]


def kernel(x, c, ctx, c_ctx, w_ada, b_ada, norm_mix, norm_ffn, w_in, conv_w, conv_b, a_log, dt_bias,
           d_skip, ssd_norm, attn_sink, w_branch_ssd, w_branch_attn, w_out, w_route_group,
           w_route_expert, w_gate, w_up, w_down, norm_final):
    bsz, t, d = x.shape
    tc = ctx.shape[1]
    m = bsz * t
    l = 0

    ctx_row = bsz
    n_rows = -(-(bsz + 1) // 8) * 8
    cc = jnp.zeros((n_rows, d), F32).at[:bsz].set(c).at[ctx_row].set(c_ctx)
    mods = _ada(cc, w_ada[l], b_ada[l][None, :]).reshape(n_rows, N_MOD, d)

    wi = w_in[l]
    o_xbc = D_SSD
    o_dt = o_xbc + CONV_CH
    o_q = o_dt + 2 * SSD_HEADS
    o_k = o_q + D_ATTN
    o_v = o_k + D_KV
    o_g = o_v + D_KV
    w_xs = wi[:, o_xbc:o_xbc + D_SSD]
    w_bc = wi[:, o_xbc + D_SSD:o_dt]
    w_k = wi[:, o_k:o_v]
    w_v = wi[:, o_v:o_g]
    w_lat = jnp.concatenate([w_xs, wi[:, :D_SSD], wi[:, o_q:o_k], w_bc, w_k, w_v, wi[:, o_g:]],
                            axis=1).astype(BF16)
    w_ctx = jnp.concatenate([w_xs, w_bc, w_k, w_v], axis=1).astype(BF16)
    w_dt = jnp.zeros((d, 2 * LANES), F32)
    w_dt = w_dt.at[:, :SSD_HEADS].set(wi[:, o_dt:o_dt + SSD_HEADS])
    w_dt = w_dt.at[:, LANES:LANES + SSD_HEADS].set(wi[:, o_dt + SSD_HEADS:o_q]).astype(BF16)

    cos_t, sin_t = _rope_tables(t)
    nmix = norm_mix[l][None, :]
    tm_in = min(1024, t)
    u, dt2 = _inproj(x.reshape(m, d), nmix, mods, w_lat, w_dt, cos_t, sin_t,
                     rows_per_mod=t, mod_row0=0, tm=tm_in, rope=True)
    mc = bsz * tc
    uc, dtc2 = _inproj(ctx.reshape(mc, d), nmix, mods, w_ctx, w_dt, None, None,
                       rows_per_mod=0, mod_row0=ctx_row, tm=min(1024, mc), rope=False)

    cw = conv_w[l]
    cb = conv_b[l][None, :]
    xs_c, bc_c = _conv(u, U_XS, U_BC, cw, cb, t, min(256, t))
    xsc_c, bcc_c = _conv(uc, UC_XS, UC_BC, cw, cb, tc, min(256, tc))
    alog = _pad_heads(a_log[l])
    bias = _pad_heads(dt_bias[l])
    e_mat = _head_expand_matrix2()
    h0 = _ctx_state(xsc_c, bcc_c, dtc2, alog, bias, _tri_pair3(tc), e_mat, bsz, tc)
    dsk = jnp.repeat(d_skip[l].astype(F32), SSD_HEAD_DIM)[None, :]
    y_ssd = _ssd(xs_c, bc_c, dt2, alog, bias, _tri_pair3(SSD_CHUNK), e_mat, h0, u, dsk,
                 ssd_norm[l][None, :], bsz, t)

    y_attn = _attn(attn_sink[l].astype(F32), u, uc, bsz, t, tc)

    merged = _merge(y_ssd, y_attn, u, w_branch_ssd[l].astype(BF16), w_branch_attn[l].astype(BF16),
                    min(512, t))

    tm_o = min(512, t)
    w_r = jnp.zeros((d, 2 * LANES), F32)
    w_r = w_r.at[:, :MOE_GROUPS].set(w_route_group[l])
    w_r = w_r.at[:, LANES:LANES + N_EXPERTS].set(w_route_expert[l]).astype(BF16)
    li = np.arange(min(256, t))
    tri_strict = jnp.asarray((li[:, None] > li[None, :]).astype(np.float32), BF16)
    x1, h2, ridx, rw, cnt = _outproj(merged, w_out[l].astype(BF16), x.reshape(m, d), mods,
                                        norm_ffn[l][None, :], w_r, tri_strict, tm_o, t)

    na = m * TOP_K
    nblk = -(-(na + N_EXPERTS * (MOE_BLOCK - 1)) // MOE_BLOCK)
    counts = cnt[0, :N_EXPERTS].astype(I32)
    padded = (counts + MOE_BLOCK - 1) // MOE_BLOCK * MOE_BLOCK
    pend = jnp.cumsum(padded)
    pstart = pend - padded
    blk_start = jnp.arange(nblk, dtype=I32) * MOE_BLOCK
    blk_e = jnp.clip(jnp.sum((pend[None, :] <= blk_start[:, None]).astype(I32), axis=1),
                     0, N_EXPERTS - 1)
    n_used = (pend[N_EXPERTS - 1:] // MOE_BLOCK).astype(I32)
    meta = jnp.concatenate([pstart + counts, pend, n_used]).astype(I32)
    e_sel = ridx[0:TOP_K]
    seg0 = jnp.zeros_like(e_sel)
    for e in range(N_EXPERTS):
        seg0 = jnp.where(e_sel == e, pstart[e], seg0)
    pos = (seg0 + ridx[TOP_K:2 * TOP_K]).reshape(-1)

    eo = _experts(blk_e, pos, meta, h2, w_gate[l].astype(BF16), w_up[l].astype(BF16),
                  w_down[l].astype(BF16), nblk)

    tm_c = min(256, t)
    out = _combine(pos, eo, x1, rw, mods, norm_final[None, :], tm_c, t)
    return out.reshape(bsz, t, d)
```

```python
import functools

import numpy as np
import jax
import jax.numpy as jnp
from jax import lax
from jax.experimental import pallas as pl
from jax.experimental.pallas import tpu as pltpu

F32 = jnp.float32
BF16 = jnp.bfloat16
I32 = jnp.int32

D_MODEL = 2048
GRID_W = 64
NORM_EPS = 1e-6
N_MOD = 6
SSD_HEADS = 16
SSD_HEAD_DIM = 64
D_SSD = SSD_HEADS * SSD_HEAD_DIM
SSD_GROUPS = 2
SSD_STATE = 128
SSD_CONV = 5
SSD_CHUNK = 128
CONV_CH = D_SSD + 2 * SSD_GROUPS * SSD_STATE
ATTN_HEADS = 8
ATTN_KV_HEADS = 2
ATTN_HEAD_DIM = 128
D_ATTN = ATTN_HEADS * ATTN_HEAD_DIM
D_KV = ATTN_KV_HEADS * ATTN_HEAD_DIM
WINDOW = 128
ROPE_BASE = 10000.0
N_BRANCH = 2
MOE_GROUPS = 4
EXPERTS_PER_GROUP = 8
N_EXPERTS = MOE_GROUPS * EXPERTS_PER_GROUP
TOP_K = 2
D_EXPERT = 512
MOE_BLOCK = 256

LANES = 128
TOK_PACK = D_MODEL // (2 * LANES)
TOK_PITCH = TOK_PACK + 1
VMEM_LIMIT = 56 * 1024 * 1024

U_XS, U_Z, U_Q, U_BC, U_K, U_V, U_G = 0, 1024, 2048, 3072, 3584, 3840, 4096
U_W = 8192
UC_XS, UC_BC, UC_K, UC_V = 0, 1024, 1536, 1792
UC_W = 2048
IN_TN = 1024


def _params(sem):
    return pltpu.CompilerParams(dimension_semantics=sem, vmem_limit_bytes=VMEM_LIMIT)


def _silu(v):
    return v * jax.nn.sigmoid(v)


def _split_bf16(v, n):
    parts = []
    r = v
    for _ in range(n):
        p = r.astype(BF16)
        parts.append(p)
        r = r - p.astype(F32)
    return parts


def _dot(a, b):
    return jnp.dot(a, b, preferred_element_type=F32)


def _dot_nt(a, b):
    return lax.dot_general(a, b, (((1,), (1,)), ((), ())), preferred_element_type=F32)


def _ada_kernel(c_ref, w_ref, b_ref, o_ref):
    a = _silu(c_ref[...]).astype(BF16)
    o_ref[...] = _dot(a, w_ref[...].astype(BF16)) + b_ref[...]


def _ada(cc, w, b):
    rows, d = cc.shape
    n = w.shape[1]
    tn = 1024
    return pl.pallas_call(
        _ada_kernel,
        grid=(n // tn,),
        in_specs=[pl.BlockSpec((rows, d), lambda j: (0, 0)),
                  pl.BlockSpec((d, tn), lambda j: (0, j)),
                  pl.BlockSpec((1, tn), lambda j: (0, j))],
        out_specs=pl.BlockSpec((rows, tn), lambda j: (0, j)),
        out_shape=jax.ShapeDtypeStruct((rows, n), F32),
        compiler_params=_params(("arbitrary",)),
        name="ada",
    )(cc, w, b)


def _rope(a, cos, sin_signed, first):
    partner = jnp.where(first, pltpu.roll(a, 96, 1), pltpu.roll(a, 32, 1))
    return a * cos + partner * sin_signed


def _inproj_kernel(*refs, rope_q_blocks, rope_kv_block, sub):
    if rope_q_blocks:
        x_ref, nw_ref, mod_ref, w_ref, wdt_ref, cos_ref, sin_ref, o_ref, dt_ref, h_scr = refs
    else:
        x_ref, nw_ref, mod_ref, w_ref, wdt_ref, o_ref, dt_ref, h_scr = refs
    j = pl.program_id(1)
    tm = x_ref.shape[0]
    tn = w_ref.shape[1]

    @pl.when(j == 0)
    def _():
        nw = nw_ref[...]
        sh = mod_ref[0, 0:1, :]
        sc = mod_ref[0, 1:2, :]

        def body(r, carry):
            rows = pl.ds(pl.multiple_of(r * sub, sub), sub)
            xf = x_ref[rows, :]
            ms = jnp.mean(xf * xf, axis=-1, keepdims=True)
            h = (xf * lax.rsqrt(ms + NORM_EPS) * nw) * (1.0 + sc) + sh
            hb = h.astype(BF16)
            h_scr[rows, :] = hb
            d = _dot(hb, wdt_ref[...])
            dt_ref[0, rows, :] = d[:, :LANES]
            dt_ref[1, rows, :] = d[:, LANES:]
            return carry

        lax.fori_loop(0, tm // sub, body, 0)

    n_split = 2 if tm >= 512 else 1
    part = tm // n_split

    def store(lo, hi):
        for sp in range(n_split):
            rows = slice(sp * part, (sp + 1) * part)
            acc = _dot(h_scr[rows, :], w_ref[...])
            if hi > lo:
                cos = cos_ref[rows, :]
                sin = sin_ref[rows, :]
                lane = lax.broadcasted_iota(I32, (part, LANES), 1)
                first = (lane % 64) < 32
            for hh in range(tn // LANES):
                a = acc[:, hh * LANES:(hh + 1) * LANES]
                if lo <= hh < hi:
                    a = _rope(a, cos, sin, first)
                o_ref[rows, hh * LANES:(hh + 1) * LANES] = a.astype(BF16)

    if rope_q_blocks:
        is_q = (j >= rope_q_blocks[0]) & (j <= rope_q_blocks[-1])
        is_kv = j == rope_kv_block
        k_lo = (U_K % tn) // LANES

        @pl.when(is_q)
        def _():
            store(0, tn // LANES)

        @pl.when(is_kv)
        def _():
            store(k_lo, k_lo + D_KV // LANES)

        @pl.when(jnp.logical_not(is_q | is_kv))
        def _():
            store(0, 0)
    else:
        store(0, 0)


def _inproj(x2, nw, mods, w, wdt, cos_t, sin_t, *, rows_per_mod, mod_row0, tm, rope):
    m, d = x2.shape
    n = w.shape[1]
    tn = IN_TN
    nbm = rows_per_mod // tm if rows_per_mod else 0
    if rows_per_mod:
        mod_map = lambda i, j: (mod_row0 + i // nbm, 0, 0)
    else:
        mod_map = lambda i, j: (mod_row0, 0, 0)
    in_specs = [pl.BlockSpec((tm, d), lambda i, j: (i, 0)),
                pl.BlockSpec((1, d), lambda i, j: (0, 0)),
                pl.BlockSpec((1, N_MOD, d), mod_map),
                pl.BlockSpec((d, tn), lambda i, j: (0, j)),
                pl.BlockSpec((d, 2 * LANES), lambda i, j: (0, 0))]
    args = [x2, nw, mods, w, wdt]
    if rope:
        in_specs += [pl.BlockSpec((tm, LANES), lambda i, j: (i % nbm, 0)),
                     pl.BlockSpec((tm, LANES), lambda i, j: (i % nbm, 0))]
        args += [cos_t, sin_t]
        rq = tuple(range(U_Q // tn, (U_Q + D_ATTN) // tn))
        rkv = U_K // tn
    else:
        rq, rkv = (), None
    kern = functools.partial(_inproj_kernel, rope_q_blocks=rq, rope_kv_block=rkv, sub=128)
    return pl.pallas_call(
        kern,
        grid=(m // tm, n // tn),
        in_specs=in_specs,
        out_specs=[pl.BlockSpec((tm, tn), lambda i, j: (i, j)),
                   pl.BlockSpec((2, tm, LANES), lambda i, j: (0, i, 0))],
        out_shape=[jax.ShapeDtypeStruct((m, n), BF16),
                   jax.ShapeDtypeStruct((2, m, LANES), F32)],
        scratch_shapes=[pltpu.VMEM((tm, d), BF16)],
        compiler_params=_params(("arbitrary", "arbitrary")),
        name="inproj_rope" if rope else "inproj_ctx",
    )(*args)


CONV_HALO = 16


def _conv_kernel(xs_ref, bc_ref, pxs_ref, pbc_ref, nxs_ref, nbc_ref, w_ref, b_ref,
                 oxs_ref, obc_ref, ext, *, blocks_per_seq):
    i = pl.program_id(0)
    r = xs_ref.shape[0]
    pos = i % blocks_per_seq
    not_first = pos != 0
    not_last = pos != blocks_per_seq - 1
    h = CONV_HALO
    ext[0:h, 0:D_SSD] = jnp.where(not_first, pxs_ref[...].astype(F32), 0.0)
    ext[0:h, D_SSD:] = jnp.where(not_first, pbc_ref[...].astype(F32), 0.0)
    ext[h:h + r, 0:D_SSD] = xs_ref[...].astype(F32)
    ext[h:h + r, D_SSD:] = bc_ref[...].astype(F32)
    ext[h + r:, 0:D_SSD] = jnp.where(not_last, nxs_ref[...].astype(F32), 0.0)
    ext[h + r:, D_SSD:] = jnp.where(not_last, nbc_ref[...].astype(F32), 0.0)
    cw = 256
    pad = SSD_CONV // 2
    for cb in range(0, CONV_CH, cw):
        acc = jnp.broadcast_to(b_ref[:, cb:cb + cw], (r, cw))
        for k in range(SSD_CONV):
            acc = acc + ext[h - pad + k:h - pad + k + r, cb:cb + cw] * w_ref[k:k + 1, cb:cb + cw]
        y = _silu(acc)
        if cb < D_SSD:
            oxs_ref[:, cb:cb + cw] = y
        else:
            obc_ref[:, cb - D_SSD:cb - D_SSD + cw] = y.astype(BF16)


def _conv(u, xs_col, bc_col, conv_w, conv_b, seq_len, r):
    m = u.shape[0]
    h = CONV_HALO
    bps = seq_len // r
    wbc = CONV_CH - D_SSD
    xs_c, bc_c = xs_col // D_SSD, bc_col // wbc
    nh = m // h
    prev = lambda i: jnp.maximum(i * (r // h) - 1, 0)
    nxt = lambda i: jnp.minimum((i + 1) * (r // h), nh - 1)
    return pl.pallas_call(
        functools.partial(_conv_kernel, blocks_per_seq=bps),
        grid=(m // r,),
        in_specs=[pl.BlockSpec((r, D_SSD), lambda i: (i, xs_c)),
                  pl.BlockSpec((r, wbc), lambda i: (i, bc_c)),
                  pl.BlockSpec((h, D_SSD), lambda i: (prev(i), xs_c)),
                  pl.BlockSpec((h, wbc), lambda i: (prev(i), bc_c)),
                  pl.BlockSpec((h, D_SSD), lambda i: (nxt(i), xs_c)),
                  pl.BlockSpec((h, wbc), lambda i: (nxt(i), bc_c)),
                  pl.BlockSpec((SSD_CONV, CONV_CH), lambda i: (0, 0)),
                  pl.BlockSpec((1, CONV_CH), lambda i: (0, 0))],
        out_specs=[pl.BlockSpec((r, D_SSD), lambda i: (i, 0)),
                   pl.BlockSpec((r, wbc), lambda i: (i, 0))],
        out_shape=[jax.ShapeDtypeStruct((m, D_SSD), F32),
                   jax.ShapeDtypeStruct((m, wbc), BF16)],
        scratch_shapes=[pltpu.VMEM((r + 2 * h, CONV_CH), F32)],
        compiler_params=_params(("arbitrary",)),
        name="conv_silu",
    )(u, u, u, u, u, u, conv_w, conv_b)


def _expand_heads(v, e2):
    return _dot(jnp.concatenate(_split_bf16(v, 2), axis=1), e2)


def _cumsum_mm(tri3, da):
    return _dot(tri3, jnp.concatenate(_split_bf16(da, 3), axis=0))


def _ctx_state_kernel(xs_ref, b_ref, dt_ref, alog_ref, bias_ref, tri_ref, e_ref, h_ref):
    d = pl.program_id(1)
    tc = xs_ref.shape[0]
    dt = jax.nn.softplus(dt_ref[0] + bias_ref[0])
    a = -jnp.exp(alog_ref[0])
    cum = _cumsum_mm(tri_ref[0], dt * a)
    cum_end = jnp.where(d == 0, cum[tc - 1:tc, :], cum[0:1, :])
    wx = _expand_heads(dt * jnp.exp(cum_end - cum), e_ref[...])
    xw = (xs_ref[...] * wx).astype(BF16)
    gw = D_SSD // SSD_GROUPS
    for g in range(SSD_GROUPS):
        bgt = b_ref[:, g * SSD_STATE:(g + 1) * SSD_STATE].astype(F32).T.astype(BF16)
        h_ref[0, 0, :, g * gw:(g + 1) * gw] = _dot(bgt, xw[:, g * gw:(g + 1) * gw])


def _ctx_state(xs_c, bc_c, dt2, alog, bias, tri3, e2, bsz, tc):
    return pl.pallas_call(
        _ctx_state_kernel,
        grid=(bsz, 2),
        in_specs=[pl.BlockSpec((tc, D_SSD), lambda b, d: (b, 0)),
                  pl.BlockSpec((tc, SSD_GROUPS * SSD_STATE), lambda b, d: (b, 0)),
                  pl.BlockSpec((1, tc, LANES), lambda b, d: (d, b, 0)),
                  pl.BlockSpec((1, 1, LANES), lambda b, d: (d, 0, 0)),
                  pl.BlockSpec((1, 1, LANES), lambda b, d: (d, 0, 0)),
                  pl.BlockSpec((1, tc, 3 * tc), lambda b, d: (d, 0, 0)),
                  pl.BlockSpec((2 * LANES, D_SSD), lambda b, d: (0, 0))],
        out_specs=pl.BlockSpec((1, 1, SSD_STATE, D_SSD), lambda b, d: (b, d, 0, 0)),
        out_shape=jax.ShapeDtypeStruct((bsz, 2, SSD_STATE, D_SSD), F32),
        compiler_params=_params(("arbitrary", "arbitrary")),
        name="ssd_ctx_state",
    )(xs_c, bc_c, dt2, alog, bias, tri3, e2)


def _ssd_chunk(fwd, xs, bc_ref, dt_raw, alog, bias, tri3, e2, st, y_out):
    lc = SSD_CHUNK
    dt = jax.nn.softplus(dt_raw + bias)
    cum = _cumsum_mm(tri3, dt * (-jnp.exp(alog)))
    cum_t = cum.T
    dt_t = dt.T
    end = lc - 1 if fwd else 0
    ecx = _expand_heads(jnp.exp(cum), e2)
    wx = _expand_heads(dt * jnp.exp(cum[end:end + 1, :] - cum), e2)
    xs_b = xs.astype(BF16)
    xw_b = (xs * wx).astype(BF16)
    cd_row = ecx[end:end + 1, :]
    li = lax.broadcasted_iota(I32, (lc, lc), 0)
    si = lax.broadcasted_iota(I32, (lc, lc), 1)
    causal = (li >= si) if fwd else (li <= si)
    low = lax.broadcasted_iota(I32, (lc, LANES), 1) < SSD_HEAD_DIM
    gw = D_SSD // SSD_GROUPS
    hpg = SSD_HEADS // SSD_GROUPS
    for g in range(SSD_GROUPS):
        bg = bc_ref[:, g * SSD_STATE:(g + 1) * SSD_STATE]
        cg = bc_ref[:, (SSD_GROUPS + g) * SSD_STATE:(SSD_GROUPS + g + 1) * SSD_STATE]
        cb = _dot_nt(cg, bg)
        yoff = _dot(cg, st[:, g * gw:(g + 1) * gw].astype(BF16)) * ecx[:, g * gw:(g + 1) * gw]
        for pr in range(hpg // 2):
            h0 = g * hpg + 2 * pr
            ms = []
            for h in (h0, h0 + 1):
                seg = cum[:, h:h + 1] - cum_t[h:h + 1, :]
                ms.append((cb * jnp.exp(jnp.where(causal, seg, -jnp.inf)) * dt_t[h:h + 1, :]).astype(BF16))
            xp = xs_b[:, h0 * SSD_HEAD_DIM:(h0 + 2) * SSD_HEAD_DIM]
            zero = jnp.zeros_like(xp)
            rhs = jnp.concatenate([jnp.where(low, xp, zero), jnp.where(low, zero, xp)], axis=0)
            y_out[:, h0 * SSD_HEAD_DIM:(h0 + 2) * SSD_HEAD_DIM] = (
                _dot(jnp.concatenate(ms, axis=1), rhs) + yoff[:, pr * LANES:(pr + 1) * LANES])
        bgt = bg.astype(F32).T.astype(BF16)
        st[:, g * gw:(g + 1) * gw] = (st[:, g * gw:(g + 1) * gw] * cd_row[:, g * gw:(g + 1) * gw]
                                      + _dot(bgt, xw_b[:, g * gw:(g + 1) * gw]))


def _ssd_kernel(xsf_ref, xsb_ref, bcf_ref, bcb_ref, dtf_ref, dtb_ref, alog_ref, bias_ref, tri_ref,
                e_ref, h0_ref, z_ref, dsk_ref, nw_ref, o_ref, st, ybuf, ycur, *, nc):
    s = pl.program_id(1)
    lc = SSD_CHUNK

    @pl.when(s == 0)
    def _():
        st[...] = h0_ref[0]

    e2 = e_ref[...]
    _ssd_chunk(True, xsf_ref[...], bcf_ref, dtf_ref[0], alog_ref[0], bias_ref[0], tri_ref[0], e2,
               st.at[0], ycur.at[0])
    _ssd_chunk(False, xsb_ref[...], bcb_ref, dtb_ref[0], alog_ref[1], bias_ref[1], tri_ref[1], e2,
               st.at[1], ycur.at[1])
    rows = (pl.ds(pl.multiple_of(s * lc, lc), lc), pl.ds(pl.multiple_of((nc - 1 - s) * lc, lc), lc))

    @pl.when(s < nc // 2)
    def _():
        for d in range(2):
            ybuf[rows[d], :] = ycur[d]

    @pl.when(s >= nc // 2)
    def _():
        gw = D_SSD // SSD_GROUPS
        for d, xs_ref in enumerate((xsf_ref, xsb_ref)):
            y = ycur[d] + ybuf[rows[d], :] + dsk_ref[...] * xs_ref[...]
            gy = y * _silu(z_ref[rows[d], :].astype(F32))
            for g in range(SSD_GROUPS):
                blk = gy[:, g * gw:(g + 1) * gw]
                ms = jnp.mean(blk * blk, axis=-1, keepdims=True)
                o_ref[rows[d], g * gw:(g + 1) * gw] = (blk * lax.rsqrt(ms + NORM_EPS)
                                                       * nw_ref[:, g * gw:(g + 1) * gw]).astype(BF16)


def _ssd(xs_c, bc_c, dt2, alog, bias, tri3, e2, h0, u, dsk, nw, bsz, t):
    nc = t // SSD_CHUNK
    lc = SSD_CHUNK
    m = bsz * t
    fw = lambda b, s: b * nc + s
    bw = lambda b, s: b * nc + nc - 1 - s
    zc = U_Z // D_SSD
    bcw = 2 * SSD_GROUPS * SSD_STATE
    return pl.pallas_call(
        functools.partial(_ssd_kernel, nc=nc),
        grid=(bsz, nc),
        in_specs=[pl.BlockSpec((lc, D_SSD), lambda b, s: (fw(b, s), 0)),
                  pl.BlockSpec((lc, D_SSD), lambda b, s: (bw(b, s), 0)),
                  pl.BlockSpec((lc, bcw), lambda b, s: (fw(b, s), 0)),
                  pl.BlockSpec((lc, bcw), lambda b, s: (bw(b, s), 0)),
                  pl.BlockSpec((1, lc, LANES), lambda b, s: (0, fw(b, s), 0)),
                  pl.BlockSpec((1, lc, LANES), lambda b, s: (1, bw(b, s), 0)),
                  pl.BlockSpec((2, 1, LANES), lambda b, s: (0, 0, 0)),
                  pl.BlockSpec((2, 1, LANES), lambda b, s: (0, 0, 0)),
                  pl.BlockSpec((2, lc, 3 * lc), lambda b, s: (0, 0, 0)),
                  pl.BlockSpec((2 * LANES, D_SSD), lambda b, s: (0, 0)),
                  pl.BlockSpec((1, 2, SSD_STATE, D_SSD), lambda b, s: (b, 0, 0, 0)),
                  pl.BlockSpec((t, D_SSD), lambda b, s: (b, zc)),
                  pl.BlockSpec((1, D_SSD), lambda b, s: (0, 0)),
                  pl.BlockSpec((1, D_SSD), lambda b, s: (0, 0))],
        out_specs=pl.BlockSpec((t, D_SSD), lambda b, s: (b, 0)),
        out_shape=jax.ShapeDtypeStruct((m, D_SSD), BF16),
        scratch_shapes=[pltpu.VMEM((2, SSD_STATE, D_SSD), F32),
                        pltpu.VMEM((t, D_SSD), F32),
                        pltpu.VMEM((2, lc, D_SSD), F32)],
        compiler_params=_params(("arbitrary", "arbitrary")),
        name="ssd_scan",
    )(xs_c, xs_c, bc_c, bc_c, dt2, dt2, alog, bias, tri3, e2, h0, u, dsk, nw)


def _attn_kernel(sink_ref, q_ref, k_ref, v_ref, kc_ref, vc_ref, o_ref, kp, vp, bias, *, t):
    g = pl.program_id(1)
    w = WINDOW
    nb = t // w
    rep = ATTN_HEADS // ATTN_KV_HEADS
    tc = kc_ref.shape[0]
    nk = 3 * w + tc
    zeros = jnp.zeros((w, ATTN_HEAD_DIM), BF16)
    kp[0:w, :] = zeros
    kp[w:w + t, :] = k_ref[...]
    kp[w + t:, :] = zeros
    vp[0:w, :] = zeros
    vp[w:w + t, :] = v_ref[...]
    vp[w + t:, :] = zeros
    qi = lax.broadcasted_iota(I32, (rep * w, w), 0) % w
    kj = lax.broadcasted_iota(I32, (rep * w, w), 1)
    bias[0] = jnp.where(kj >= qi, 0.0, -jnp.inf).astype(F32)
    bias[1] = jnp.where(kj <= qi, 0.0, -jnp.inf).astype(F32)
    sink_col = jnp.concatenate(
        [jnp.full((w, 1), sink_ref[g * rep + h], F32) for h in range(rep)], axis=0)
    scale = ATTN_HEAD_DIM ** -0.5
    log2e = float(np.log2(np.e))

    def body(n, carry):
        rows = pl.ds(pl.multiple_of(n * w, w), w)
        q = jnp.concatenate([q_ref[rows, h * ATTN_HEAD_DIM:(h + 1) * ATTN_HEAD_DIM]
                             for h in range(rep)], axis=0)
        band = pl.ds(pl.multiple_of(n * w, w), 3 * w)
        kall = jnp.concatenate([kp[band, :], kc_ref[...]], axis=0)
        vall = jnp.concatenate([vp[band, :], vc_ref[...]], axis=0)
        raw = _dot_nt(q, kall)
        prev = raw[:, 0:w] + (bias[0] + jnp.where(n == 0, -jnp.inf, 0.0))
        nxt = raw[:, 2 * w:3 * w] + (bias[1] + jnp.where(n == nb - 1, -jnp.inf, 0.0))
        sc = jnp.concatenate([prev, raw[:, w:2 * w], nxt, raw[:, 3 * w:]], axis=1)
        mx = jnp.maximum(jnp.max(sc, axis=-1, keepdims=True) * scale, sink_col)
        p = jnp.exp2(sc * (scale * log2e) - mx * log2e)
        den = jnp.sum(p, axis=-1, keepdims=True) + jnp.exp2((sink_col - mx) * log2e)
        o = _dot(p.astype(BF16), vall) / den
        for h in range(rep):
            o_ref[rows, h * ATTN_HEAD_DIM:(h + 1) * ATTN_HEAD_DIM] = o[h * w:(h + 1) * w, :].astype(BF16)
        return carry

    lax.fori_loop(0, nb, body, 0, unroll=2)


def _attn(sink, u, uc, bsz, t, tc):
    rep = ATTN_HEADS // ATTN_KV_HEADS
    qw = rep * ATTN_HEAD_DIM
    hd = ATTN_HEAD_DIM
    return pl.pallas_call(
        functools.partial(_attn_kernel, t=t),
        grid=(bsz, ATTN_KV_HEADS),
        in_specs=[pl.BlockSpec(memory_space=pltpu.SMEM),
                  pl.BlockSpec((t, qw), lambda b, g: (b, U_Q // qw + g)),
                  pl.BlockSpec((t, hd), lambda b, g: (b, U_K // hd + g)),
                  pl.BlockSpec((t, hd), lambda b, g: (b, U_V // hd + g)),
                  pl.BlockSpec((tc, hd), lambda b, g: (b, UC_K // hd + g)),
                  pl.BlockSpec((tc, hd), lambda b, g: (b, UC_V // hd + g))],
        out_specs=pl.BlockSpec((t, qw), lambda b, g: (b, g)),
        out_shape=jax.ShapeDtypeStruct((bsz * t, D_ATTN), BF16),
        scratch_shapes=[pltpu.VMEM((t + 2 * WINDOW, hd), BF16),
                        pltpu.VMEM((t + 2 * WINDOW, hd), BF16),
                        pltpu.VMEM((2, rep * WINDOW, WINDOW), F32)],
        compiler_params=_params(("arbitrary", "arbitrary")),
        name="window_attn",
    )(sink, u, u, u, uc, uc)


def _merge_kernel(ys_ref, ya_ref, g0_ref, g1_ref, ws_ref, wa_ref, o_ref):
    ys = ys_ref[...]
    ya = ya_ref[...]
    cw = 512
    for cb in range(0, D_MODEL, cw):
        t0 = _dot(ys, ws_ref[:, cb:cb + cw])
        t1 = _dot(ya, wa_ref[:, cb:cb + cw])
        g0 = jax.nn.sigmoid(g0_ref[:, cb:cb + cw].astype(F32))
        g1 = jax.nn.sigmoid(g1_ref[:, cb:cb + cw].astype(F32))
        o_ref[:, cb:cb + cw] = (g0 * t0 + g1 * t1).astype(BF16)


def _merge(ys, ya, u, ws, wa, tm):
    m = ys.shape[0]
    gc = U_G // D_MODEL
    return pl.pallas_call(
        _merge_kernel,
        grid=(m // tm,),
        in_specs=[pl.BlockSpec((tm, D_SSD), lambda i: (i, 0)),
                  pl.BlockSpec((tm, D_ATTN), lambda i: (i, 0)),
                  pl.BlockSpec((tm, D_MODEL), lambda i: (i, gc)),
                  pl.BlockSpec((tm, D_MODEL), lambda i: (i, gc + 1)),
                  pl.BlockSpec((D_SSD, D_MODEL), lambda i: (0, 0)),
                  pl.BlockSpec((D_ATTN, D_MODEL), lambda i: (0, 0))],
        out_specs=pl.BlockSpec((tm, D_MODEL), lambda i: (i, 0)),
        out_shape=jax.ShapeDtypeStruct((m, D_MODEL), BF16),
        compiler_params=_params(("arbitrary",)),
        name="branch_merge",
    )(ys, ya, u, u, ws, wa)


def _lane_min_index(mask, lane):
    return jnp.min(jnp.where(mask, lane, LANES), axis=-1, keepdims=True)


def _outproj_kernel(mg_ref, wo_ref, x_ref, mod_ref, nw_ref, wr_ref, tri_ref,
                    x1_ref, h2_ref, ridx_ref, rw_ref, cnt_ref, carry):
    i = pl.program_id(0)
    pt = tri_ref.shape[0]

    @pl.when(i == 0)
    def _():
        carry[...] = jnp.zeros_like(carry)

    for p in range(x_ref.shape[0] // pt):
        _outproj_part(slice(p * pt, (p + 1) * pt), slice(p * pt, (p + 1) * pt), mg_ref, wo_ref,
                      x_ref, mod_ref, nw_ref, wr_ref, tri_ref, x1_ref, h2_ref, ridx_ref, rw_ref,
                      cnt_ref, carry)


def _outproj_part(rows, cols, mg_ref, wo_ref, x_ref, mod_ref, nw_ref, wr_ref, tri_ref,
                  x1_ref, h2_ref, ridx_ref, rw_ref, cnt_ref, carry):
    tm = tri_ref.shape[0]
    g1 = mod_ref[0, 2:3, :]
    sh2 = mod_ref[0, 3:4, :]
    sc2 = mod_ref[0, 4:5, :]
    x1 = x_ref[rows, :] + g1 * _dot(mg_ref[rows, :], wo_ref[...])
    x1_ref[rows, :] = x1
    ms = jnp.mean(x1 * x1, axis=-1, keepdims=True)
    h2 = (x1 * lax.rsqrt(ms + NORM_EPS) * nw_ref[...]) * (1.0 + sc2) + sh2
    hb = h2.astype(BF16)
    bits = pltpu.bitcast(hb.astype(F32), jnp.uint32)
    for a in range(TOK_PACK):
        lo = bits[:, (2 * a) * LANES:(2 * a + 1) * LANES] >> 16
        hi = bits[:, (2 * a + 1) * LANES:(2 * a + 2) * LANES] & jnp.uint32(0xFFFF0000)
        h2_ref[pl.ds(rows.start * TOK_PACK + a, tm, stride=TOK_PACK), :] = hi | lo

    lg = _dot(hb, wr_ref[...])
    lane = lax.broadcasted_iota(I32, (tm, LANES), 1)
    gl = jnp.where(lane < MOE_GROUPS, lg[:, :LANES], -jnp.inf)
    gmax = jnp.max(gl, axis=-1, keepdims=True)
    g_w = 1.0 / jnp.sum(jnp.exp(gl - gmax), axis=-1, keepdims=True)
    g_idx = _lane_min_index(gl == gmax, lane)
    el = jnp.where((lane // EXPERTS_PER_GROUP) == g_idx, lg[:, LANES:], -jnp.inf)
    v1 = jnp.max(el, axis=-1, keepdims=True)
    i1 = _lane_min_index(el == v1, lane)
    el2 = jnp.where(lane == i1, -jnp.inf, el)
    v2 = jnp.max(el2, axis=-1, keepdims=True)
    i2 = _lane_min_index(el2 == v2, lane)
    e2 = jnp.exp(v2 - v1)
    w1 = g_w * (1.0 / (1.0 + e2))
    w2 = g_w * (e2 / (1.0 + e2))

    hit1 = lane == i1
    hit2 = lane == i2
    onehot = jnp.where(hit1 | hit2, 1.0, 0.0).astype(F32)
    before = _dot(tri_ref[...], onehot.astype(BF16)) + carry[0:1, :]
    r1 = jnp.sum(jnp.where(hit1, before, 0.0), axis=-1, keepdims=True)
    r2 = jnp.sum(jnp.where(hit2, before, 0.0), axis=-1, keepdims=True)
    total = carry[0:1, :] + jnp.sum(onehot, axis=0, keepdims=True)
    carry[...] = jnp.broadcast_to(total, carry.shape)
    cnt_ref[...] = jnp.broadcast_to(total, cnt_ref.shape)
    packed = jnp.where(lane == 0, i1.astype(F32), jnp.where(lane == 1, i2.astype(F32), jnp.where(
        lane == 2, r1, jnp.where(lane == 3, r2, 0.0))))
    ridx_ref[:, cols] = packed.T[0:8, :].astype(I32)
    rw_ref[rows, :] = jnp.where(lane == 0, w1, jnp.where(lane == 1, w2, 0.0))


def _outproj(mg, wo, x2, mods, nw, wr, tri, tm, t):
    m = x2.shape[0]
    nbm = t // tm
    pt = tri.shape[0]
    return pl.pallas_call(
        _outproj_kernel,
        grid=(m // tm,),
        in_specs=[pl.BlockSpec((tm, D_MODEL), lambda i: (i, 0)),
                  pl.BlockSpec((D_MODEL, D_MODEL), lambda i: (0, 0), pipeline_mode=pl.Buffered(1)),
                  pl.BlockSpec((tm, D_MODEL), lambda i: (i, 0)),
                  pl.BlockSpec((1, N_MOD, D_MODEL), lambda i: (i // nbm, 0, 0)),
                  pl.BlockSpec((1, D_MODEL), lambda i: (0, 0)),
                  pl.BlockSpec((D_MODEL, 2 * LANES), lambda i: (0, 0)),
                  pl.BlockSpec((pt, pt), lambda i: (0, 0))],
        out_specs=[pl.BlockSpec((tm, D_MODEL), lambda i: (i, 0)),
                   pl.BlockSpec((tm * TOK_PACK, LANES), lambda i: (i, 0)),
                   pl.BlockSpec((8, tm), lambda i: (0, i)),
                   pl.BlockSpec((tm, LANES), lambda i: (i, 0)),
                   pl.BlockSpec((8, LANES), lambda i: (0, 0))],
        out_shape=[jax.ShapeDtypeStruct((m, D_MODEL), F32),
                   jax.ShapeDtypeStruct((m * TOK_PACK, LANES), jnp.uint32),
                   jax.ShapeDtypeStruct((8, m), I32),
                   jax.ShapeDtypeStruct((m, LANES), F32),
                   jax.ShapeDtypeStruct((8, LANES), F32)],
        scratch_shapes=[pltpu.VMEM((8, LANES), F32)],
        compiler_params=_params(("arbitrary",)),
        name="outproj_router",
    )(mg, wo, x2, mods, nw, wr, tri)


def _gather_rows(idx_ref, base, n, src, dst, sem, row0=0):
    for r in range(n):
        tok = idx_ref[base + r]
        pltpu.make_async_copy(src.at[pl.ds(tok, 1), :], dst.at[pl.ds(row0 + r, 1), :],
                              sem).start(priority=r % 2)


def _gather_slabs(idx_ref, base, n, src, dst, sem, row0=0):
    for r in range(n):
        tok = idx_ref[base + r]
        pltpu.make_async_copy(src.at[pl.ds(pl.multiple_of(tok * TOK_PACK, TOK_PACK), TOK_PACK), :],
                              dst.at[pl.ds((row0 + r) * TOK_PITCH, TOK_PACK), :],
                              sem).start(priority=r % 2)


def _expert_kernel(blk_e_ref, pos_ref, meta_ref, h2_hbm, wg_ref, wu_ref, wd_ref, o_ref,
                   xb, tok_ref, sem):
    i = pl.program_id(0)
    n_used = meta_ref[2 * N_EXPERTS]
    slot = i % 2
    rows = MOE_BLOCK
    m = pos_ref.shape[0] // TOP_K

    def wait_block(sl):
        pltpu.make_async_copy(h2_hbm.at[pl.ds(0, rows * TOK_PACK), :],
                              xb.at[sl, pl.ds(0, rows * TOK_PACK), :], sem.at[sl]).wait()

    @pl.when(i == 0)
    def _():
        def pad_expert(e, carry):
            def pad_slot(s, c):
                tok_ref[s] = 0
                return c
            return lax.fori_loop(meta_ref[e], meta_ref[N_EXPERTS + e], pad_slot, carry)

        lax.fori_loop(0, N_EXPERTS, pad_expert, 0)

        def place(t, carry):
            for k in range(TOP_K):
                tok_ref[pos_ref[k * m + t]] = t
            return carry

        lax.fori_loop(0, m, place, 0, unroll=8)
        _gather_slabs(tok_ref, 0, rows, h2_hbm, xb.at[0], sem.at[0])

    @pl.when(i < n_used)
    def _():
        wait_block(slot)
        nxt = jnp.minimum(i + 1, n_used - 1)
        per = rows // TOK_PACK
        xs = []
        for a in range(TOK_PACK):
            pair = xb[slot, pl.ds(a, rows, stride=TOK_PITCH), :]
            xs.append(pltpu.bitcast(pair << 16, F32).astype(BF16))
            xs.append(pltpu.bitcast(pair & jnp.uint32(0xFFFF0000), F32).astype(BF16))
            _gather_slabs(tok_ref, nxt * rows + a * per, per, h2_hbm, xb.at[1 - slot],
                          sem.at[1 - slot], row0=a * per)
        x = jnp.concatenate(xs, axis=1)
        act = (_silu(_dot(x, wg_ref[0])) * _dot(x, wu_ref[0])).astype(BF16)
        o_ref[...] = _dot(act, wd_ref[0])

        @pl.when(i == n_used - 1)
        def _():
            wait_block(1 - slot)

    @pl.when(i >= n_used)
    def _():
        o_ref[...] = jnp.zeros_like(o_ref)


def _experts(blk_e, pos, meta, h2, wg, wu, wd, nblk):
    rows = MOE_BLOCK
    live = lambda i, mt: jnp.minimum(i, mt[2 * N_EXPERTS] - 1)
    grid_spec = pltpu.PrefetchScalarGridSpec(
        num_scalar_prefetch=3,
        grid=(nblk,),
        in_specs=[pl.BlockSpec(memory_space=pl.ANY),
                  pl.BlockSpec((1, D_MODEL, D_EXPERT), lambda i, be, ps, mt: (be[live(i, mt)], 0, 0)),
                  pl.BlockSpec((1, D_MODEL, D_EXPERT), lambda i, be, ps, mt: (be[live(i, mt)], 0, 0)),
                  pl.BlockSpec((1, D_EXPERT, D_MODEL), lambda i, be, ps, mt: (be[live(i, mt)], 0, 0))],
        out_specs=pl.BlockSpec((rows, D_MODEL), lambda i, be, ps, mt: (i, 0)),
        scratch_shapes=[pltpu.VMEM((2, rows * TOK_PITCH, LANES), jnp.uint32),
                        pltpu.SMEM((nblk * rows,), I32),
                        pltpu.SemaphoreType.DMA((2,))],
    )
    return pl.pallas_call(
        _expert_kernel,
        grid_spec=grid_spec,
        out_shape=jax.ShapeDtypeStruct((nblk * rows, D_MODEL), F32),
        compiler_params=_params(("arbitrary",)),
        name="expert_mlp",
    )(blk_e, pos, meta, h2, wg, wu, wd)


def _combine_kernel(pos_ref, eo_hbm, x1_ref, rw_ref, mod_ref, nf_ref, o_ref, gb, sem):
    i = pl.program_id(0)
    nsteps = pl.num_programs(0)
    tm = x1_ref.shape[0]
    slot = i % 2

    def issue(step, sl):
        for k in range(TOP_K):
            _gather_rows(pos_ref, (k * nsteps + step) * tm, tm, eo_hbm, gb.at[sl, k], sem.at[sl])

    @pl.when(i == 0)
    def _():
        issue(0, 0)

    @pl.when(i + 1 < nsteps)
    def _():
        issue(i + 1, 1 - slot)

    for k in range(TOP_K):
        pltpu.make_async_copy(eo_hbm.at[pl.ds(0, tm), :], gb.at[slot, k], sem.at[slot]).wait()
    w1 = rw_ref[:, 0:1]
    w2 = rw_ref[:, 1:2]
    g2 = mod_ref[0, 5:6, :]
    x2 = x1_ref[...] + g2 * (gb[slot, 0] * w1 + gb[slot, 1] * w2)
    ms = jnp.mean(x2 * x2, axis=-1, keepdims=True)
    o_ref[...] = x2 * lax.rsqrt(ms + NORM_EPS) * nf_ref[...]


def _combine(pos, eo, x1, rw, mods, nf, tm, t):
    m = x1.shape[0]
    nbm = t // tm
    grid_spec = pltpu.PrefetchScalarGridSpec(
        num_scalar_prefetch=1,
        grid=(m // tm,),
        in_specs=[pl.BlockSpec(memory_space=pl.ANY),
                  pl.BlockSpec((tm, D_MODEL), lambda i, p: (i, 0)),
                  pl.BlockSpec((tm, LANES), lambda i, p: (i, 0)),
                  pl.BlockSpec((1, N_MOD, D_MODEL), lambda i, p: (i // nbm, 0, 0)),
                  pl.BlockSpec((1, D_MODEL), lambda i, p: (0, 0))],
        out_specs=pl.BlockSpec((tm, D_MODEL), lambda i, p: (i, 0)),
        scratch_shapes=[pltpu.VMEM((2, TOP_K, tm, D_MODEL), F32),
                        pltpu.SemaphoreType.DMA((2,))],
    )
    return pl.pallas_call(
        _combine_kernel,
        grid_spec=grid_spec,
        out_shape=jax.ShapeDtypeStruct((m, D_MODEL), F32),
        compiler_params=_params(("arbitrary",)),
        name="moe_combine_norm",
    )(pos, eo, x1, rw, mods, nf)


def _rope_tables(t):
    half = ATTN_HEAD_DIM // 4
    pos = jnp.arange(t, dtype=I32)
    row = (pos // GRID_W).astype(F32)
    col = (pos % GRID_W).astype(F32)
    freqs = ROPE_BASE ** (-jnp.arange(half, dtype=F32) / half)
    ar = row[:, None] * freqs[None, :]
    ac = col[:, None] * freqs[None, :]
    cos_t = jnp.concatenate([jnp.cos(ar), jnp.cos(ar), jnp.cos(ac), jnp.cos(ac)], axis=1)
    sin_t = jnp.concatenate([-jnp.sin(ar), jnp.sin(ar), -jnp.sin(ac), jnp.sin(ac)], axis=1)
    return cos_t, sin_t


def _tri_pair3(n):
    li = np.arange(n)[:, None]
    si = np.arange(n)[None, :]
    pair = np.stack([li >= si, li <= si]).astype(np.float32)
    return jnp.asarray(np.concatenate([pair] * 3, axis=2), BF16)


def _head_expand_matrix2():
    k = np.arange(LANES)[:, None]
    j = np.arange(D_SSD)[None, :]
    e = (k == j // SSD_HEAD_DIM).astype(np.float32)
    return jnp.asarray(np.concatenate([e, e], axis=0), BF16)


def _pad_heads(v):
    return jnp.pad(v.astype(F32), ((0, 0), (0, LANES - SSD_HEADS)))[:, None, :]


def kernel(x, c, ctx, c_ctx, w_ada, b_ada, norm_mix, norm_ffn, w_in, conv_w, conv_b, a_log, dt_bias,
           d_skip, ssd_norm, attn_sink, w_branch_ssd, w_branch_attn, w_out, w_route_group,
           w_route_expert, w_gate, w_up, w_down, norm_final):
    bsz, t, d = x.shape
    tc = ctx.shape[1]
    m = bsz * t
    l = 0

    ctx_row = bsz
    n_rows = -(-(bsz + 1) // 8) * 8
    cc = jnp.zeros((n_rows, d), F32).at[:bsz].set(c).at[ctx_row].set(c_ctx)
    mods = _ada(cc, w_ada[l], b_ada[l][None, :]).reshape(n_rows, N_MOD, d)

    wi = w_in[l]
    o_xbc = D_SSD
    o_dt = o_xbc + CONV_CH
    o_q = o_dt + 2 * SSD_HEADS
    o_k = o_q + D_ATTN
    o_v = o_k + D_KV
    o_g = o_v + D_KV
    w_xs = wi[:, o_xbc:o_xbc + D_SSD]
    w_bc = wi[:, o_xbc + D_SSD:o_dt]
    w_k = wi[:, o_k:o_v]
    w_v = wi[:, o_v:o_g]
    w_lat = jnp.concatenate([w_xs, wi[:, :D_SSD], wi[:, o_q:o_k], w_bc, w_k, w_v, wi[:, o_g:]],
                            axis=1).astype(BF16)
    w_ctx = jnp.concatenate([w_xs, w_bc, w_k, w_v], axis=1).astype(BF16)
    w_dt = jnp.zeros((d, 2 * LANES), F32)
    w_dt = w_dt.at[:, :SSD_HEADS].set(wi[:, o_dt:o_dt + SSD_HEADS])
    w_dt = w_dt.at[:, LANES:LANES + SSD_HEADS].set(wi[:, o_dt + SSD_HEADS:o_q]).astype(BF16)

    cos_t, sin_t = _rope_tables(t)
    nmix = norm_mix[l][None, :]
    tm_in = min(1024, t)
    u, dt2 = _inproj(x.reshape(m, d), nmix, mods, w_lat, w_dt, cos_t, sin_t,
                     rows_per_mod=t, mod_row0=0, tm=tm_in, rope=True)
    mc = bsz * tc
    uc, dtc2 = _inproj(ctx.reshape(mc, d), nmix, mods, w_ctx, w_dt, None, None,
                       rows_per_mod=0, mod_row0=ctx_row, tm=min(1024, mc), rope=False)

    cw = conv_w[l]
    cb = conv_b[l][None, :]
    xs_c, bc_c = _conv(u, U_XS, U_BC, cw, cb, t, min(256, t))
    xsc_c, bcc_c = _conv(uc, UC_XS, UC_BC, cw, cb, tc, min(256, tc))
    alog = _pad_heads(a_log[l])
    bias = _pad_heads(dt_bias[l])
    e_mat = _head_expand_matrix2()
    h0 = _ctx_state(xsc_c, bcc_c, dtc2, alog, bias, _tri_pair3(tc), e_mat, bsz, tc)
    dsk = jnp.repeat(d_skip[l].astype(F32), SSD_HEAD_DIM)[None, :]
    y_ssd = _ssd(xs_c, bc_c, dt2, alog, bias, _tri_pair3(SSD_CHUNK), e_mat, h0, u, dsk,
                 ssd_norm[l][None, :], bsz, t)

    y_attn = _attn(attn_sink[l].astype(F32), u, uc, bsz, t, tc)

    merged = _merge(y_ssd, y_attn, u, w_branch_ssd[l].astype(BF16), w_branch_attn[l].astype(BF16),
                    min(512, t))

    tm_o = min(512, t)
    w_r = jnp.zeros((d, 2 * LANES), F32)
    w_r = w_r.at[:, :MOE_GROUPS].set(w_route_group[l])
    w_r = w_r.at[:, LANES:LANES + N_EXPERTS].set(w_route_expert[l]).astype(BF16)
    li = np.arange(min(256, t))
    tri_strict = jnp.asarray((li[:, None] > li[None, :]).astype(np.float32), BF16)
    x1, h2, ridx, rw, cnt = _outproj(merged, w_out[l].astype(BF16), x.reshape(m, d), mods,
                                        norm_ffn[l][None, :], w_r, tri_strict, tm_o, t)

    na = m * TOP_K
    nblk = -(-(na + N_EXPERTS * (MOE_BLOCK - 1)) // MOE_BLOCK)
    counts = cnt[0, :N_EXPERTS].astype(I32)
    padded = (counts + MOE_BLOCK - 1) // MOE_BLOCK * MOE_BLOCK
    pend = jnp.cumsum(padded)
    pstart = pend - padded
    blk_start = jnp.arange(nblk, dtype=I32) * MOE_BLOCK
    blk_e = jnp.clip(jnp.sum((pend[None, :] <= blk_start[:, None]).astype(I32), axis=1),
                     0, N_EXPERTS - 1)
    n_used = (pend[N_EXPERTS - 1:] // MOE_BLOCK).astype(I32)
    meta = jnp.concatenate([pstart + counts, pend, n_used]).astype(I32)
    e_sel = ridx[0:TOP_K]
    seg0 = jnp.zeros_like(e_sel)
    for e in range(N_EXPERTS):
        seg0 = jnp.where(e_sel == e, pstart[e], seg0)
    pos = (seg0 + ridx[TOP_K:2 * TOP_K]).reshape(-1)

    eo = _experts(blk_e, pos, meta, h2, w_gate[l].astype(BF16), w_up[l].astype(BF16),
                  w_down[l].astype(BF16), nblk)

    tm_c = min(256, t)
    out = _combine(pos, eo, x1, rw, mods, norm_final[None, :], tm_c, t)
    return out.reshape(bsz, t, d)
```

```python
import functools

import numpy as np
import jax
import jax.numpy as jnp
from jax import lax
from jax.experimental import pallas as pl
from jax.experimental.pallas import tpu as pltpu

F32 = jnp.float32
BF16 = jnp.bfloat16
I32 = jnp.int32

D_MODEL = 2048
GRID_W = 64
NORM_EPS = 1e-6
N_MOD = 6
SSD_HEADS = 16
SSD_HEAD_DIM = 64
D_SSD = SSD_HEADS * SSD_HEAD_DIM
SSD_GROUPS = 2
SSD_STATE = 128
SSD_CONV = 5
SSD_CHUNK = 128
CONV_CH = D_SSD + 2 * SSD_GROUPS * SSD_STATE
ATTN_HEADS = 8
ATTN_KV_HEADS = 2
ATTN_HEAD_DIM = 128
D_ATTN = ATTN_HEADS * ATTN_HEAD_DIM
D_KV = ATTN_KV_HEADS * ATTN_HEAD_DIM
WINDOW = 128
ROPE_BASE = 10000.0
N_BRANCH = 2
MOE_GROUPS = 4
EXPERTS_PER_GROUP = 8
N_EXPERTS = MOE_GROUPS * EXPERTS_PER_GROUP
TOP_K = 2
D_EXPERT = 512
MOE_BLOCK = 256

LANES = 128
TOK_PACK = D_MODEL // (2 * LANES)
TOK_PITCH = TOK_PACK + 1
VMEM_LIMIT = 56 * 1024 * 1024

U_XS, U_Z, U_Q, U_BC, U_K, U_V, U_G = 0, 1024, 2048, 3072, 3584, 3840, 4096
U_W = 8192
UC_XS, UC_BC, UC_K, UC_V = 0, 1024, 1536, 1792
UC_W = 2048
IN_TN = 1024


def _params(sem):
    return pltpu.CompilerParams(dimension_semantics=sem, vmem_limit_bytes=VMEM_LIMIT)


def _silu(v):
    return v * jax.nn.sigmoid(v)


def _split_bf16(v, n):
    parts = []
    r = v
    for _ in range(n):
        p = r.astype(BF16)
        parts.append(p)
        r = r - p.astype(F32)
    return parts


def _dot(a, b):
    return jnp.dot(a, b, preferred_element_type=F32)


def _dot_nt(a, b):
    return lax.dot_general(a, b, (((1,), (1,)), ((), ())), preferred_element_type=F32)


def _ada_kernel(c_ref, w_ref, b_ref, o_ref):
    a = _silu(c_ref[...]).astype(BF16)
    o_ref[...] = _dot(a, w_ref[...].astype(BF16)) + b_ref[...]


def _ada(cc, w, b):
    rows, d = cc.shape
    n = w.shape[1]
    tn = 1024
    return pl.pallas_call(
        _ada_kernel,
        grid=(n // tn,),
        in_specs=[pl.BlockSpec((rows, d), lambda j: (0, 0)),
                  pl.BlockSpec((d, tn), lambda j: (0, j)),
                  pl.BlockSpec((1, tn), lambda j: (0, j))],
        out_specs=pl.BlockSpec((rows, tn), lambda j: (0, j)),
        out_shape=jax.ShapeDtypeStruct((rows, n), F32),
        compiler_params=_params(("arbitrary",)),
        name="ada",
    )(cc, w, b)


def _rope(a, cos, sin_signed, first):
    partner = jnp.where(first, pltpu.roll(a, 96, 1), pltpu.roll(a, 32, 1))
    return a * cos + partner * sin_signed


def _inproj_kernel(*refs, rope_q_blocks, rope_kv_block, sub):
    if rope_q_blocks:
        x_ref, nw_ref, mod_ref, w_ref, wdt_ref, cos_ref, sin_ref, o_ref, dt_ref, h_scr = refs
    else:
        x_ref, nw_ref, mod_ref, w_ref, wdt_ref, o_ref, dt_ref, h_scr = refs
    j = pl.program_id(1)
    tm = x_ref.shape[0]
    tn = w_ref.shape[1]

    @pl.when(j == 0)
    def _():
        nw = nw_ref[...]
        sh = mod_ref[0, 0:1, :]
        sc = mod_ref[0, 1:2, :]

        def body(r, carry):
            rows = pl.ds(pl.multiple_of(r * sub, sub), sub)
            xf = x_ref[rows, :]
            ms = jnp.mean(xf * xf, axis=-1, keepdims=True)
            h = (xf * lax.rsqrt(ms + NORM_EPS) * nw) * (1.0 + sc) + sh
            hb = h.astype(BF16)
            h_scr[rows, :] = hb
            d = _dot(hb, wdt_ref[...])
            dt_ref[0, rows, :] = d[:, :LANES]
            dt_ref[1, rows, :] = d[:, LANES:]
            return carry

        lax.fori_loop(0, tm // sub, body, 0)

    n_split = 2 if tm >= 512 else 1
    part = tm // n_split

    def store(lo, hi):
        for sp in range(n_split):
            rows = slice(sp * part, (sp + 1) * part)
            acc = _dot(h_scr[rows, :], w_ref[...])
            if hi > lo:
                cos = cos_ref[rows, :]
                sin = sin_ref[rows, :]
                lane = lax.broadcasted_iota(I32, (part, LANES), 1)
                first = (lane % 64) < 32
            for hh in range(tn // LANES):
                a = acc[:, hh * LANES:(hh + 1) * LANES]
                if lo <= hh < hi:
                    a = _rope(a, cos, sin, first)
                o_ref[rows, hh * LANES:(hh + 1) * LANES] = a.astype(BF16)

    if rope_q_blocks:
        is_q = (j >= rope_q_blocks[0]) & (j <= rope_q_blocks[-1])
        is_kv = j == rope_kv_block
        k_lo = (U_K % tn) // LANES

        @pl.when(is_q)
        def _():
            store(0, tn // LANES)

        @pl.when(is_kv)
        def _():
            store(k_lo, k_lo + D_KV // LANES)

        @pl.when(jnp.logical_not(is_q | is_kv))
        def _():
            store(0, 0)
    else:
        store(0, 0)


def _inproj(x2, nw, mods, w, wdt, cos_t, sin_t, *, rows_per_mod, mod_row0, tm, rope):
    m, d = x2.shape
    n = w.shape[1]
    tn = IN_TN
    nbm = rows_per_mod // tm if rows_per_mod else 0
    if rows_per_mod:
        mod_map = lambda i, j: (mod_row0 + i // nbm, 0, 0)
    else:
        mod_map = lambda i, j: (mod_row0, 0, 0)
    in_specs = [pl.BlockSpec((tm, d), lambda i, j: (i, 0)),
                pl.BlockSpec((1, d), lambda i, j: (0, 0)),
                pl.BlockSpec((1, N_MOD, d), mod_map),
                pl.BlockSpec((d, tn), lambda i, j: (0, j)),
                pl.BlockSpec((d, 2 * LANES), lambda i, j: (0, 0))]
    args = [x2, nw, mods, w, wdt]
    if rope:
        in_specs += [pl.BlockSpec((tm, LANES), lambda i, j: (i % nbm, 0)),
                     pl.BlockSpec((tm, LANES), lambda i, j: (i % nbm, 0))]
        args += [cos_t, sin_t]
        rq = tuple(range(U_Q // tn, (U_Q + D_ATTN) // tn))
        rkv = U_K // tn
    else:
        rq, rkv = (), None
    kern = functools.partial(_inproj_kernel, rope_q_blocks=rq, rope_kv_block=rkv, sub=128)
    return pl.pallas_call(
        kern,
        grid=(m // tm, n // tn),
        in_specs=in_specs,
        out_specs=[pl.BlockSpec((tm, tn), lambda i, j: (i, j)),
                   pl.BlockSpec((2, tm, LANES), lambda i, j: (0, i, 0))],
        out_shape=[jax.ShapeDtypeStruct((m, n), BF16),
                   jax.ShapeDtypeStruct((2, m, LANES), F32)],
        scratch_shapes=[pltpu.VMEM((tm, d), BF16)],
        compiler_params=_params(("arbitrary", "arbitrary")),
        name="inproj_rope" if rope else "inproj_ctx",
    )(*args)


CONV_HALO = 16


def _conv_kernel(xs_ref, bc_ref, pxs_ref, pbc_ref, nxs_ref, nbc_ref, w_ref, b_ref,
                 oxs_ref, obc_ref, ext, *, blocks_per_seq):
    i = pl.program_id(0)
    r = xs_ref.shape[0]
    pos = i % blocks_per_seq
    not_first = pos != 0
    not_last = pos != blocks_per_seq - 1
    h = CONV_HALO
    ext[0:h, 0:D_SSD] = jnp.where(not_first, pxs_ref[...].astype(F32), 0.0)
    ext[0:h, D_SSD:] = jnp.where(not_first, pbc_ref[...].astype(F32), 0.0)
    ext[h:h + r, 0:D_SSD] = xs_ref[...].astype(F32)
    ext[h:h + r, D_SSD:] = bc_ref[...].astype(F32)
    ext[h + r:, 0:D_SSD] = jnp.where(not_last, nxs_ref[...].astype(F32), 0.0)
    ext[h + r:, D_SSD:] = jnp.where(not_last, nbc_ref[...].astype(F32), 0.0)
    cw = 256
    pad = SSD_CONV // 2
    for cb in range(0, CONV_CH, cw):
        acc = jnp.broadcast_to(b_ref[:, cb:cb + cw], (r, cw))
        for k in range(SSD_CONV):
            acc = acc + ext[h - pad + k:h - pad + k + r, cb:cb + cw] * w_ref[k:k + 1, cb:cb + cw]
        y = _silu(acc)
        if cb < D_SSD:
            oxs_ref[:, cb:cb + cw] = y
        else:
            obc_ref[:, cb - D_SSD:cb - D_SSD + cw] = y.astype(BF16)


def _conv(u, xs_col, bc_col, conv_w, conv_b, seq_len, r):
    m = u.shape[0]
    h = CONV_HALO
    bps = seq_len // r
    wbc = CONV_CH - D_SSD
    xs_c, bc_c = xs_col // D_SSD, bc_col // wbc
    nh = m // h
    prev = lambda i: jnp.maximum(i * (r // h) - 1, 0)
    nxt = lambda i: jnp.minimum((i + 1) * (r // h), nh - 1)
    return pl.pallas_call(
        functools.partial(_conv_kernel, blocks_per_seq=bps),
        grid=(m // r,),
        in_specs=[pl.BlockSpec((r, D_SSD), lambda i: (i, xs_c)),
                  pl.BlockSpec((r, wbc), lambda i: (i, bc_c)),
                  pl.BlockSpec((h, D_SSD), lambda i: (prev(i), xs_c)),
                  pl.BlockSpec((h, wbc), lambda i: (prev(i), bc_c)),
                  pl.BlockSpec((h, D_SSD), lambda i: (nxt(i), xs_c)),
                  pl.BlockSpec((h, wbc), lambda i: (nxt(i), bc_c)),
                  pl.BlockSpec((SSD_CONV, CONV_CH), lambda i: (0, 0)),
                  pl.BlockSpec((1, CONV_CH), lambda i: (0, 0))],
        out_specs=[pl.BlockSpec((r, D_SSD), lambda i: (i, 0)),
                   pl.BlockSpec((r, wbc), lambda i: (i, 0))],
        out_shape=[jax.ShapeDtypeStruct((m, D_SSD), F32),
                   jax.ShapeDtypeStruct((m, wbc), BF16)],
        scratch_shapes=[pltpu.VMEM((r + 2 * h, CONV_CH), F32)],
        compiler_params=_params(("arbitrary",)),
        name="conv_silu",
    )(u, u, u, u, u, u, conv_w, conv_b)


def _expand_heads(v, e2):
    return _dot(jnp.concatenate(_split_bf16(v, 2), axis=1), e2)


def _cumsum_mm(tri3, da):
    return _dot(tri3, jnp.concatenate(_split_bf16(da, 3), axis=0))


def _ctx_state_kernel(xs_ref, b_ref, dt_ref, alog_ref, bias_ref, tri_ref, e_ref, h_ref):
    d = pl.program_id(1)
    tc = xs_ref.shape[0]
    dt = jax.nn.softplus(dt_ref[0] + bias_ref[0])
    a = -jnp.exp(alog_ref[0])
    cum = _cumsum_mm(tri_ref[0], dt * a)
    cum_end = jnp.where(d == 0, cum[tc - 1:tc, :], cum[0:1, :])
    wx = _expand_heads(dt * jnp.exp(cum_end - cum), e_ref[...])
    xw = (xs_ref[...] * wx).astype(BF16)
    gw = D_SSD // SSD_GROUPS
    for g in range(SSD_GROUPS):
        bgt = b_ref[:, g * SSD_STATE:(g + 1) * SSD_STATE].astype(F32).T.astype(BF16)
        h_ref[0, 0, :, g * gw:(g + 1) * gw] = _dot(bgt, xw[:, g * gw:(g + 1) * gw])


def _ctx_state(xs_c, bc_c, dt2, alog, bias, tri3, e2, bsz, tc):
    return pl.pallas_call(
        _ctx_state_kernel,
        grid=(bsz, 2),
        in_specs=[pl.BlockSpec((tc, D_SSD), lambda b, d: (b, 0)),
                  pl.BlockSpec((tc, SSD_GROUPS * SSD_STATE), lambda b, d: (b, 0)),
                  pl.BlockSpec((1, tc, LANES), lambda b, d: (d, b, 0)),
                  pl.BlockSpec((1, 1, LANES), lambda b, d: (d, 0, 0)),
                  pl.BlockSpec((1, 1, LANES), lambda b, d: (d, 0, 0)),
                  pl.BlockSpec((1, tc, 3 * tc), lambda b, d: (d, 0, 0)),
                  pl.BlockSpec((2 * LANES, D_SSD), lambda b, d: (0, 0))],
        out_specs=pl.BlockSpec((1, 1, SSD_STATE, D_SSD), lambda b, d: (b, d, 0, 0)),
        out_shape=jax.ShapeDtypeStruct((bsz, 2, SSD_STATE, D_SSD), F32),
        compiler_params=_params(("arbitrary", "arbitrary")),
        name="ssd_ctx_state",
    )(xs_c, bc_c, dt2, alog, bias, tri3, e2)


def _ssd_chunk(fwd, xs, bc_ref, dt_raw, alog, bias, tri3, e2, st, y_out):
    lc = SSD_CHUNK
    dt = jax.nn.softplus(dt_raw + bias)
    cum = _cumsum_mm(tri3, dt * (-jnp.exp(alog)))
    cum_t = cum.T
    dt_t = dt.T
    end = lc - 1 if fwd else 0
    ecx = _expand_heads(jnp.exp(cum), e2)
    wx = _expand_heads(dt * jnp.exp(cum[end:end + 1, :] - cum), e2)
    xs_b = xs.astype(BF16)
    xw_b = (xs * wx).astype(BF16)
    cd_row = ecx[end:end + 1, :]
    li = lax.broadcasted_iota(I32, (lc, lc), 0)
    si = lax.broadcasted_iota(I32, (lc, lc), 1)
    causal = (li >= si) if fwd else (li <= si)
    low = lax.broadcasted_iota(I32, (lc, LANES), 1) < SSD_HEAD_DIM
    gw = D_SSD // SSD_GROUPS
    hpg = SSD_HEADS // SSD_GROUPS
    for g in range(SSD_GROUPS):
        bg = bc_ref[:, g * SSD_STATE:(g + 1) * SSD_STATE]
        cg = bc_ref[:, (SSD_GROUPS + g) * SSD_STATE:(SSD_GROUPS + g + 1) * SSD_STATE]
        cb = _dot_nt(cg, bg)
        yoff = _dot(cg, st[:, g * gw:(g + 1) * gw].astype(BF16)) * ecx[:, g * gw:(g + 1) * gw]
        for pr in range(hpg // 2):
            h0 = g * hpg + 2 * pr
            ms = []
            for h in (h0, h0 + 1):
                seg = cum[:, h:h + 1] - cum_t[h:h + 1, :]
                ms.append((cb * jnp.exp(jnp.where(causal, seg, -jnp.inf)) * dt_t[h:h + 1, :]).astype(BF16))
            xp = xs_b[:, h0 * SSD_HEAD_DIM:(h0 + 2) * SSD_HEAD_DIM]
            zero = jnp.zeros_like(xp)
            rhs = jnp.concatenate([jnp.where(low, xp, zero), jnp.where(low, zero, xp)], axis=0)
            y_out[:, h0 * SSD_HEAD_DIM:(h0 + 2) * SSD_HEAD_DIM] = (
                _dot(jnp.concatenate(ms, axis=1), rhs) + yoff[:, pr * LANES:(pr + 1) * LANES])
        bgt = bg.astype(F32).T.astype(BF16)
        st[:, g * gw:(g + 1) * gw] = (st[:, g * gw:(g + 1) * gw] * cd_row[:, g * gw:(g + 1) * gw]
                                      + _dot(bgt, xw_b[:, g * gw:(g + 1) * gw]))


def _ssd_kernel(xsf_ref, xsb_ref, bcf_ref, bcb_ref, dtf_ref, dtb_ref, alog_ref, bias_ref, tri_ref,
                e_ref, h0_ref, z_ref, dsk_ref, nw_ref, o_ref, st, ybuf, ycur, *, nc):
    s = pl.program_id(1)
    lc = SSD_CHUNK

    @pl.when(s == 0)
    def _():
        st[...] = h0_ref[0]

    e2 = e_ref[...]
    _ssd_chunk(True, xsf_ref[...], bcf_ref, dtf_ref[0], alog_ref[0], bias_ref[0], tri_ref[0], e2,
               st.at[0], ycur.at[0])
    _ssd_chunk(False, xsb_ref[...], bcb_ref, dtb_ref[0], alog_ref[1], bias_ref[1], tri_ref[1], e2,
               st.at[1], ycur.at[1])
    rows = (pl.ds(pl.multiple_of(s * lc, lc), lc), pl.ds(pl.multiple_of((nc - 1 - s) * lc, lc), lc))

    @pl.when(s < nc // 2)
    def _():
        for d in range(2):
            ybuf[rows[d], :] = ycur[d]

    @pl.when(s >= nc // 2)
    def _():
        gw = D_SSD // SSD_GROUPS
        for d, xs_ref in enumerate((xsf_ref, xsb_ref)):
            y = ycur[d] + ybuf[rows[d], :] + dsk_ref[...] * xs_ref[...]
            gy = y * _silu(z_ref[rows[d], :].astype(F32))
            for g in range(SSD_GROUPS):
                blk = gy[:, g * gw:(g + 1) * gw]
                ms = jnp.mean(blk * blk, axis=-1, keepdims=True)
                o_ref[rows[d], g * gw:(g + 1) * gw] = (blk * lax.rsqrt(ms + NORM_EPS)
                                                       * nw_ref[:, g * gw:(g + 1) * gw]).astype(BF16)


def _ssd(xs_c, bc_c, dt2, alog, bias, tri3, e2, h0, u, dsk, nw, bsz, t):
    nc = t // SSD_CHUNK
    lc = SSD_CHUNK
    m = bsz * t
    fw = lambda b, s: b * nc + s
    bw = lambda b, s: b * nc + nc - 1 - s
    zc = U_Z // D_SSD
    bcw = 2 * SSD_GROUPS * SSD_STATE
    return pl.pallas_call(
        functools.partial(_ssd_kernel, nc=nc),
        grid=(bsz, nc),
        in_specs=[pl.BlockSpec((lc, D_SSD), lambda b, s: (fw(b, s), 0)),
                  pl.BlockSpec((lc, D_SSD), lambda b, s: (bw(b, s), 0)),
                  pl.BlockSpec((lc, bcw), lambda b, s: (fw(b, s), 0)),
                  pl.BlockSpec((lc, bcw), lambda b, s: (bw(b, s), 0)),
                  pl.BlockSpec((1, lc, LANES), lambda b, s: (0, fw(b, s), 0)),
                  pl.BlockSpec((1, lc, LANES), lambda b, s: (1, bw(b, s), 0)),
                  pl.BlockSpec((2, 1, LANES), lambda b, s: (0, 0, 0)),
                  pl.BlockSpec((2, 1, LANES), lambda b, s: (0, 0, 0)),
                  pl.BlockSpec((2, lc, 3 * lc), lambda b, s: (0, 0, 0)),
                  pl.BlockSpec((2 * LANES, D_SSD), lambda b, s: (0, 0)),
                  pl.BlockSpec((1, 2, SSD_STATE, D_SSD), lambda b, s: (b, 0, 0, 0)),
                  pl.BlockSpec((t, D_SSD), lambda b, s: (b, zc)),
                  pl.BlockSpec((1, D_SSD), lambda b, s: (0, 0)),
                  pl.BlockSpec((1, D_SSD), lambda b, s: (0, 0))],
        out_specs=pl.BlockSpec((t, D_SSD), lambda b, s: (b, 0)),
        out_shape=jax.ShapeDtypeStruct((m, D_SSD), BF16),
        scratch_shapes=[pltpu.VMEM((2, SSD_STATE, D_SSD), F32),
                        pltpu.VMEM((t, D_SSD), F32),
                        pltpu.VMEM((2, lc, D_SSD), F32)],
        compiler_params=_params(("arbitrary", "arbitrary")),
        name="ssd_scan",
    )(xs_c, xs_c, bc_c, bc_c, dt2, dt2, alog, bias, tri3, e2, h0, u, dsk, nw)


def _attn_kernel(sink_ref, q_ref, k_ref, v_ref, kc_ref, vc_ref, o_ref, kp, vp, bias, *, t):
    g = pl.program_id(1)
    w = WINDOW
    nb = t // w
    rep = ATTN_HEADS // ATTN_KV_HEADS
    tc = kc_ref.shape[0]
    nk = 3 * w + tc
    zeros = jnp.zeros((w, ATTN_HEAD_DIM), BF16)
    kp[0:w, :] = zeros
    kp[w:w + t, :] = k_ref[...]
    kp[w + t:, :] = zeros
    vp[0:w, :] = zeros
    vp[w:w + t, :] = v_ref[...]
    vp[w + t:, :] = zeros
    qi = lax.broadcasted_iota(I32, (rep * w, w), 0) % w
    kj = lax.broadcasted_iota(I32, (rep * w, w), 1)
    bias[0] = jnp.where(kj >= qi, 0.0, -jnp.inf).astype(F32)
    bias[1] = jnp.where(kj <= qi, 0.0, -jnp.inf).astype(F32)
    sink_col = jnp.concatenate(
        [jnp.full((w, 1), sink_ref[g * rep + h], F32) for h in range(rep)], axis=0)
    scale = ATTN_HEAD_DIM ** -0.5
    log2e = float(np.log2(np.e))

    def body(n, carry):
        rows = pl.ds(pl.multiple_of(n * w, w), w)
        q = jnp.concatenate([q_ref[rows, h * ATTN_HEAD_DIM:(h + 1) * ATTN_HEAD_DIM]
                             for h in range(rep)], axis=0)
        band = pl.ds(pl.multiple_of(n * w, w), 3 * w)
        kall = jnp.concatenate([kp[band, :], kc_ref[...]], axis=0)
        vall = jnp.concatenate([vp[band, :], vc_ref[...]], axis=0)
        raw = _dot_nt(q, kall)
        prev = raw[:, 0:w] + (bias[0] + jnp.where(n == 0, -jnp.inf, 0.0))
        nxt = raw[:, 2 * w:3 * w] + (bias[1] + jnp.where(n == nb - 1, -jnp.inf, 0.0))
        sc = jnp.concatenate([prev, raw[:, w:2 * w], nxt, raw[:, 3 * w:]], axis=1)
        mx = jnp.maximum(jnp.max(sc, axis=-1, keepdims=True) * scale, sink_col)
        p = jnp.exp2(sc * (scale * log2e) - mx * log2e)
        den = jnp.sum(p, axis=-1, keepdims=True) + jnp.exp2((sink_col - mx) * log2e)
        o = _dot(p.astype(BF16), vall) / den
        for h in range(rep):
            o_ref[rows, h * ATTN_HEAD_DIM:(h + 1) * ATTN_HEAD_DIM] = o[h * w:(h + 1) * w, :].astype(BF16)
        return carry

    lax.fori_loop(0, nb, body, 0, unroll=2)


def _attn(sink, u, uc, bsz, t, tc):
    rep = ATTN_HEADS // ATTN_KV_HEADS
    qw = rep * ATTN_HEAD_DIM
    hd = ATTN_HEAD_DIM
    return pl.pallas_call(
        functools.partial(_attn_kernel, t=t),
        grid=(bsz, ATTN_KV_HEADS),
        in_specs=[pl.BlockSpec(memory_space=pltpu.SMEM),
                  pl.BlockSpec((t, qw), lambda b, g: (b, U_Q // qw + g)),
                  pl.BlockSpec((t, hd), lambda b, g: (b, U_K // hd + g)),
                  pl.BlockSpec((t, hd), lambda b, g: (b, U_V // hd + g)),
                  pl.BlockSpec((tc, hd), lambda b, g: (b, UC_K // hd + g)),
                  pl.BlockSpec((tc, hd), lambda b, g: (b, UC_V // hd + g))],
        out_specs=pl.BlockSpec((t, qw), lambda b, g: (b, g)),
        out_shape=jax.ShapeDtypeStruct((bsz * t, D_ATTN), BF16),
        scratch_shapes=[pltpu.VMEM((t + 2 * WINDOW, hd), BF16),
                        pltpu.VMEM((t + 2 * WINDOW, hd), BF16),
                        pltpu.VMEM((2, rep * WINDOW, WINDOW), F32)],
        compiler_params=_params(("arbitrary", "arbitrary")),
        name="window_attn",
    )(sink, u, u, u, uc, uc)


def _merge_kernel(ys_ref, ya_ref, g0_ref, g1_ref, ws_ref, wa_ref, o_ref):
    ys = ys_ref[...]
    ya = ya_ref[...]
    cw = 512
    for cb in range(0, D_MODEL, cw):
        t0 = _dot(ys, ws_ref[:, cb:cb + cw])
        t1 = _dot(ya, wa_ref[:, cb:cb + cw])
        g0 = jax.nn.sigmoid(g0_ref[:, cb:cb + cw].astype(F32))
        g1 = jax.nn.sigmoid(g1_ref[:, cb:cb + cw].astype(F32))
        o_ref[:, cb:cb + cw] = (g0 * t0 + g1 * t1).astype(BF16)


def _merge(ys, ya, u, ws, wa, tm):
    m = ys.shape[0]
    gc = U_G // D_MODEL
    return pl.pallas_call(
        _merge_kernel,
        grid=(m // tm,),
        in_specs=[pl.BlockSpec((tm, D_SSD), lambda i: (i, 0)),
                  pl.BlockSpec((tm, D_ATTN), lambda i: (i, 0)),
                  pl.BlockSpec((tm, D_MODEL), lambda i: (i, gc)),
                  pl.BlockSpec((tm, D_MODEL), lambda i: (i, gc + 1)),
                  pl.BlockSpec((D_SSD, D_MODEL), lambda i: (0, 0)),
                  pl.BlockSpec((D_ATTN, D_MODEL), lambda i: (0, 0))],
        out_specs=pl.BlockSpec((tm, D_MODEL), lambda i: (i, 0)),
        out_shape=jax.ShapeDtypeStruct((m, D_MODEL), BF16),
        compiler_params=_params(("arbitrary",)),
        name="branch_merge",
    )(ys, ya, u, u, ws, wa)


def _lane_min_index(mask, lane):
    return jnp.min(jnp.where(mask, lane, LANES), axis=-1, keepdims=True)


def _outproj_kernel(mg_ref, wo_ref, x_ref, mod_ref, nw_ref, wr_ref, tri_ref,
                    x1_ref, h2_ref, ridx_ref, rw_ref, cnt_ref, carry):
    i = pl.program_id(0)
    pt = tri_ref.shape[0]

    @pl.when(i == 0)
    def _():
        carry[...] = jnp.zeros_like(carry)

    for p in range(x_ref.shape[0] // pt):
        _outproj_part(slice(p * pt, (p + 1) * pt), slice(p * pt, (p + 1) * pt), mg_ref, wo_ref,
                      x_ref, mod_ref, nw_ref, wr_ref, tri_ref, x1_ref, h2_ref, ridx_ref, rw_ref,
                      cnt_ref, carry)


def _outproj_part(rows, cols, mg_ref, wo_ref, x_ref, mod_ref, nw_ref, wr_ref, tri_ref,
                  x1_ref, h2_ref, ridx_ref, rw_ref, cnt_ref, carry):
    tm = tri_ref.shape[0]
    g1 = mod_ref[0, 2:3, :]
    sh2 = mod_ref[0, 3:4, :]
    sc2 = mod_ref[0, 4:5, :]
    x1 = x_ref[rows, :] + g1 * _dot(mg_ref[rows, :], wo_ref[...])
    x1_ref[rows, :] = x1
    ms = jnp.mean(x1 * x1, axis=-1, keepdims=True)
    h2 = (x1 * lax.rsqrt(ms + NORM_EPS) * nw_ref[...]) * (1.0 + sc2) + sh2
    hb = h2.astype(BF16)
    bits = pltpu.bitcast(hb.astype(F32), jnp.uint32)
    for a in range(TOK_PACK):
        lo = bits[:, (2 * a) * LANES:(2 * a + 1) * LANES] >> 16
        hi = bits[:, (2 * a + 1) * LANES:(2 * a + 2) * LANES] & jnp.uint32(0xFFFF0000)
        h2_ref[pl.ds(rows.start * TOK_PACK + a, tm, stride=TOK_PACK), :] = hi | lo

    lg = _dot(hb, wr_ref[...])
    lane = lax.broadcasted_iota(I32, (tm, LANES), 1)
    gl = jnp.where(lane < MOE_GROUPS, lg[:, :LANES], -jnp.inf)
    gmax = jnp.max(gl, axis=-1, keepdims=True)
    g_w = 1.0 / jnp.sum(jnp.exp(gl - gmax), axis=-1, keepdims=True)
    g_idx = _lane_min_index(gl == gmax, lane)
    el = jnp.where((lane // EXPERTS_PER_GROUP) == g_idx, lg[:, LANES:], -jnp.inf)
    v1 = jnp.max(el, axis=-1, keepdims=True)
    i1 = _lane_min_index(el == v1, lane)
    el2 = jnp.where(lane == i1, -jnp.inf, el)
    v2 = jnp.max(el2, axis=-1, keepdims=True)
    i2 = _lane_min_index(el2 == v2, lane)
    e2 = jnp.exp(v2 - v1)
    w1 = g_w * (1.0 / (1.0 + e2))
    w2 = g_w * (e2 / (1.0 + e2))

    hit1 = lane == i1
    hit2 = lane == i2
    onehot = jnp.where(hit1 | hit2, 1.0, 0.0).astype(F32)
    before = _dot(tri_ref[...], onehot.astype(BF16)) + carry[0:1, :]
    r1 = jnp.sum(jnp.where(hit1, before, 0.0), axis=-1, keepdims=True)
    r2 = jnp.sum(jnp.where(hit2, before, 0.0), axis=-1, keepdims=True)
    total = carry[0:1, :] + jnp.sum(onehot, axis=0, keepdims=True)
    carry[...] = jnp.broadcast_to(total, carry.shape)
    cnt_ref[...] = jnp.broadcast_to(total, cnt_ref.shape)
    packed = jnp.where(lane == 0, i1.astype(F32), jnp.where(lane == 1, i2.astype(F32), jnp.where(
        lane == 2, r1, jnp.where(lane == 3, r2, 0.0))))
    ridx_ref[:, cols] = packed.T[0:8, :].astype(I32)
    rw_ref[rows, :] = jnp.where(lane == 0, w1, jnp.where(lane == 1, w2, 0.0))


def _outproj(mg, wo, x2, mods, nw, wr, tri, tm, t):
    m = x2.shape[0]
    nbm = t // tm
    pt = tri.shape[0]
    return pl.pallas_call(
        _outproj_kernel,
        grid=(m // tm,),
        in_specs=[pl.BlockSpec((tm, D_MODEL), lambda i: (i, 0)),
                  pl.BlockSpec((D_MODEL, D_MODEL), lambda i: (0, 0), pipeline_mode=pl.Buffered(1)),
                  pl.BlockSpec((tm, D_MODEL), lambda i: (i, 0)),
                  pl.BlockSpec((1, N_MOD, D_MODEL), lambda i: (i // nbm, 0, 0)),
                  pl.BlockSpec((1, D_MODEL), lambda i: (0, 0)),
                  pl.BlockSpec((D_MODEL, 2 * LANES), lambda i: (0, 0)),
                  pl.BlockSpec((pt, pt), lambda i: (0, 0))],
        out_specs=[pl.BlockSpec((tm, D_MODEL), lambda i: (i, 0)),
                   pl.BlockSpec((tm * TOK_PACK, LANES), lambda i: (i, 0)),
                   pl.BlockSpec((8, tm), lambda i: (0, i)),
                   pl.BlockSpec((tm, LANES), lambda i: (i, 0)),
                   pl.BlockSpec((8, LANES), lambda i: (0, 0))],
        out_shape=[jax.ShapeDtypeStruct((m, D_MODEL), F32),
                   jax.ShapeDtypeStruct((m * TOK_PACK, LANES), jnp.uint32),
                   jax.ShapeDtypeStruct((8, m), I32),
                   jax.ShapeDtypeStruct((m, LANES), F32),
                   jax.ShapeDtypeStruct((8, LANES), F32)],
        scratch_shapes=[pltpu.VMEM((8, LANES), F32)],
        compiler_params=_params(("arbitrary",)),
        name="outproj_router",
    )(mg, wo, x2, mods, nw, wr, tri)


def _gather_rows(idx_ref, base, n, src, dst, sem, row0=0):
    for r in range(n):
        tok = idx_ref[base + r]
        pltpu.make_async_copy(src.at[pl.ds(tok, 1), :], dst.at[pl.ds(row0 + r, 1), :],
                              sem).start(priority=r % 2)


def _gather_slabs(idx_ref, base, n, src, dst, sem, row0=0):
    for r in range(n):
        tok = idx_ref[base + r]
        pltpu.make_async_copy(src.at[pl.ds(pl.multiple_of(tok * TOK_PACK, TOK_PACK), TOK_PACK), :],
                              dst.at[pl.ds((row0 + r) * TOK_PITCH, TOK_PACK), :],
                              sem).start(priority=r % 2)


def _expert_kernel(blk_e_ref, pos_ref, meta_ref, h2_hbm, wg_ref, wu_ref, wd_ref, o_ref,
                   xb, tok_ref, wgb, wub, wdb, sem):
    i = pl.program_id(0)
    n_used = meta_ref[2 * N_EXPERTS]
    slot = i % 2
    rows = MOE_BLOCK
    m = pos_ref.shape[0] // TOP_K

    def wait_block(sl):
        pltpu.make_async_copy(h2_hbm.at[pl.ds(0, rows * TOK_PACK), :],
                              xb.at[sl, pl.ds(0, rows * TOK_PACK), :], sem.at[sl]).wait()

    @pl.when(i == 0)
    def _():
        def pad_expert(e, carry):
            def pad_slot(s, c):
                tok_ref[s] = 0
                return c
            return lax.fori_loop(meta_ref[e], meta_ref[N_EXPERTS + e], pad_slot, carry)

        lax.fori_loop(0, N_EXPERTS, pad_expert, 0)

        def place(t, carry):
            for k in range(TOP_K):
                tok_ref[pos_ref[k * m + t]] = t
            return carry

        lax.fori_loop(0, m, place, 0, unroll=8)
        _gather_slabs(tok_ref, 0, rows, h2_hbm, xb.at[0], sem.at[0])

    @pl.when(i < n_used)
    def _():
        wait_block(slot)
        nxt = jnp.minimum(i + 1, n_used - 1)
        n_pass = 2
        per = rows // (n_pass * TOK_PACK)

        def read_block(ps):
            xs = []
            for a in range(TOK_PACK):
                pair = xb[slot, pl.ds(a, rows, stride=TOK_PITCH), :]
                xs.append(pltpu.bitcast(pair << 16, F32).astype(BF16))
                xs.append(pltpu.bitcast(pair & jnp.uint32(0xFFFF0000), F32).astype(BF16))
                g = ps * TOK_PACK + a
                _gather_slabs(tok_ref, nxt * rows + g * per, per, h2_hbm, xb.at[1 - slot],
                              sem.at[1 - slot], row0=g * per)
            return jnp.concatenate(xs, axis=1)

        prev_e = blk_e_ref[jnp.maximum(i - 1, 0)]

        @pl.when((i == 0) | (blk_e_ref[i] != prev_e))
        def _():
            wgb[...] = wg_ref[0].astype(BF16)
            wub[...] = wu_ref[0].astype(BF16)
            wdb[...] = wd_ref[0].astype(BF16)

        gate = _dot(read_block(0), wgb[...])
        act = (_silu(gate) * _dot(read_block(1), wub[...])).astype(BF16)
        o_ref[...] = _dot(act, wdb[...])

        @pl.when(i == n_used - 1)
        def _():
            wait_block(1 - slot)

    @pl.when(i >= n_used)
    def _():
        o_ref[...] = jnp.zeros_like(o_ref)


def _experts(blk_e, pos, meta, h2, wg, wu, wd, nblk):
    rows = MOE_BLOCK
    live = lambda i, mt: jnp.minimum(i, mt[2 * N_EXPERTS] - 1)
    grid_spec = pltpu.PrefetchScalarGridSpec(
        num_scalar_prefetch=3,
        grid=(nblk,),
        in_specs=[pl.BlockSpec(memory_space=pl.ANY),
                  pl.BlockSpec((1, D_MODEL, D_EXPERT), lambda i, be, ps, mt: (be[live(i, mt)], 0, 0)),
                  pl.BlockSpec((1, D_MODEL, D_EXPERT), lambda i, be, ps, mt: (be[live(i, mt)], 0, 0)),
                  pl.BlockSpec((1, D_EXPERT, D_MODEL), lambda i, be, ps, mt: (be[live(i, mt)], 0, 0))],
        out_specs=pl.BlockSpec((rows, D_MODEL), lambda i, be, ps, mt: (i, 0)),
        scratch_shapes=[pltpu.VMEM((2, rows * TOK_PITCH, LANES), jnp.uint32),
                        pltpu.SMEM((nblk * rows,), I32),
                        pltpu.VMEM((D_MODEL, D_EXPERT), BF16),
                        pltpu.VMEM((D_MODEL, D_EXPERT), BF16),
                        pltpu.VMEM((D_EXPERT, D_MODEL), BF16),
                        pltpu.SemaphoreType.DMA((2,))],
    )
    return pl.pallas_call(
        _expert_kernel,
        grid_spec=grid_spec,
        out_shape=jax.ShapeDtypeStruct((nblk * rows, D_MODEL), F32),
        compiler_params=_params(("arbitrary",)),
        name="expert_mlp",
    )(blk_e, pos, meta, h2, wg, wu, wd)


def _combine_kernel(pos_ref, eo_hbm, x1_ref, rw_ref, mod_ref, nf_ref, o_ref, gb, sem):
    i = pl.program_id(0)
    nsteps = pl.num_programs(0)
    tm = x1_ref.shape[0]
    slot = i % 2

    def issue(step, sl):
        for k in range(TOP_K):
            _gather_rows(pos_ref, (k * nsteps + step) * tm, tm, eo_hbm, gb.at[sl, k], sem.at[sl])

    @pl.when(i == 0)
    def _():
        issue(0, 0)

    @pl.when(i + 1 < nsteps)
    def _():
        issue(i + 1, 1 - slot)

    for k in range(TOP_K):
        pltpu.make_async_copy(eo_hbm.at[pl.ds(0, tm), :], gb.at[slot, k], sem.at[slot]).wait()
    w1 = rw_ref[:, 0:1]
    w2 = rw_ref[:, 1:2]
    g2 = mod_ref[0, 5:6, :]
    x2 = x1_ref[...] + g2 * (gb[slot, 0] * w1 + gb[slot, 1] * w2)
    ms = jnp.mean(x2 * x2, axis=-1, keepdims=True)
    o_ref[...] = x2 * lax.rsqrt(ms + NORM_EPS) * nf_ref[...]


def _combine(pos, eo, x1, rw, mods, nf, tm, t):
    m = x1.shape[0]
    nbm = t // tm
    grid_spec = pltpu.PrefetchScalarGridSpec(
        num_scalar_prefetch=1,
        grid=(m // tm,),
        in_specs=[pl.BlockSpec(memory_space=pl.ANY),
                  pl.BlockSpec((tm, D_MODEL), lambda i, p: (i, 0)),
                  pl.BlockSpec((tm, LANES), lambda i, p: (i, 0)),
                  pl.BlockSpec((1, N_MOD, D_MODEL), lambda i, p: (i // nbm, 0, 0)),
                  pl.BlockSpec((1, D_MODEL), lambda i, p: (0, 0))],
        out_specs=pl.BlockSpec((tm, D_MODEL), lambda i, p: (i, 0)),
        scratch_shapes=[pltpu.VMEM((2, TOP_K, tm, D_MODEL), F32),
                        pltpu.SemaphoreType.DMA((2,))],
    )
    return pl.pallas_call(
        _combine_kernel,
        grid_spec=grid_spec,
        out_shape=jax.ShapeDtypeStruct((m, D_MODEL), F32),
        compiler_params=_params(("arbitrary",)),
        name="moe_combine_norm",
    )(pos, eo, x1, rw, mods, nf)


def _rope_tables(t):
    half = ATTN_HEAD_DIM // 4
    pos = jnp.arange(t, dtype=I32)
    row = (pos // GRID_W).astype(F32)
    col = (pos % GRID_W).astype(F32)
    freqs = ROPE_BASE ** (-jnp.arange(half, dtype=F32) / half)
    ar = row[:, None] * freqs[None, :]
    ac = col[:, None] * freqs[None, :]
    cos_t = jnp.concatenate([jnp.cos(ar), jnp.cos(ar), jnp.cos(ac), jnp.cos(ac)], axis=1)
    sin_t = jnp.concatenate([-jnp.sin(ar), jnp.sin(ar), -jnp.sin(ac), jnp.sin(ac)], axis=1)
    return cos_t, sin_t


def _tri_pair3(n):
    li = np.arange(n)[:, None]
    si = np.arange(n)[None, :]
    pair = np.stack([li >= si, li <= si]).astype(np.float32)
    return jnp.asarray(np.concatenate([pair] * 3, axis=2), BF16)


def _head_expand_matrix2():
    k = np.arange(LANES)[:, None]
    j = np.arange(D_SSD)[None, :]
    e = (k == j // SSD_HEAD_DIM).astype(np.float32)
    return jnp.asarray(np.concatenate([e, e], axis=0), BF16)


def _pad_heads(v):
    return jnp.pad(v.astype(F32), ((0, 0), (0, LANES - SSD_HEADS)))[:, None, :]


def kernel(x, c, ctx, c_ctx, w_ada, b_ada, norm_mix, norm_ffn, w_in, conv_w, conv_b, a_log, dt_bias,
           d_skip, ssd_norm, attn_sink, w_branch_ssd, w_branch_attn, w_out, w_route_group,
           w_route_expert, w_gate, w_up, w_down, norm_final):
    bsz, t, d = x.shape
    tc = ctx.shape[1]
    m = bsz * t
    l = 0

    ctx_row = bsz
    n_rows = -(-(bsz + 1) // 8) * 8
    cc = jnp.zeros((n_rows, d), F32).at[:bsz].set(c).at[ctx_row].set(c_ctx)
    mods = _ada(cc, w_ada[l], b_ada[l][None, :]).reshape(n_rows, N_MOD, d)

    wi = w_in[l]
    o_xbc = D_SSD
    o_dt = o_xbc + CONV_CH
    o_q = o_dt + 2 * SSD_HEADS
    o_k = o_q + D_ATTN
    o_v = o_k + D_KV
    o_g = o_v + D_KV
    w_xs = wi[:, o_xbc:o_xbc + D_SSD]
    w_bc = wi[:, o_xbc + D_SSD:o_dt]
    w_k = wi[:, o_k:o_v]
    w_v = wi[:, o_v:o_g]
    w_lat = jnp.concatenate([w_xs, wi[:, :D_SSD], wi[:, o_q:o_k], w_bc, w_k, w_v, wi[:, o_g:]],
                            axis=1).astype(BF16)
    w_ctx = jnp.concatenate([w_xs, w_bc, w_k, w_v], axis=1).astype(BF16)
    w_dt = jnp.zeros((d, 2 * LANES), F32)
    w_dt = w_dt.at[:, :SSD_HEADS].set(wi[:, o_dt:o_dt + SSD_HEADS])
    w_dt = w_dt.at[:, LANES:LANES + SSD_HEADS].set(wi[:, o_dt + SSD_HEADS:o_q]).astype(BF16)

    cos_t, sin_t = _rope_tables(t)
    nmix = norm_mix[l][None, :]
    tm_in = min(1024, t)
    u, dt2 = _inproj(x.reshape(m, d), nmix, mods, w_lat, w_dt, cos_t, sin_t,
                     rows_per_mod=t, mod_row0=0, tm=tm_in, rope=True)
    mc = bsz * tc
    uc, dtc2 = _inproj(ctx.reshape(mc, d), nmix, mods, w_ctx, w_dt, None, None,
                       rows_per_mod=0, mod_row0=ctx_row, tm=min(1024, mc), rope=False)

    cw = conv_w[l]
    cb = conv_b[l][None, :]
    xs_c, bc_c = _conv(u, U_XS, U_BC, cw, cb, t, min(256, t))
    xsc_c, bcc_c = _conv(uc, UC_XS, UC_BC, cw, cb, tc, min(256, tc))
    alog = _pad_heads(a_log[l])
    bias = _pad_heads(dt_bias[l])
    e_mat = _head_expand_matrix2()
    h0 = _ctx_state(xsc_c, bcc_c, dtc2, alog, bias, _tri_pair3(tc), e_mat, bsz, tc)
    dsk = jnp.repeat(d_skip[l].astype(F32), SSD_HEAD_DIM)[None, :]
    y_ssd = _ssd(xs_c, bc_c, dt2, alog, bias, _tri_pair3(SSD_CHUNK), e_mat, h0, u, dsk,
                 ssd_norm[l][None, :], bsz, t)

    y_attn = _attn(attn_sink[l].astype(F32), u, uc, bsz, t, tc)

    merged = _merge(y_ssd, y_attn, u, w_branch_ssd[l].astype(BF16), w_branch_attn[l].astype(BF16),
                    min(512, t))

    tm_o = min(512, t)
    w_r = jnp.zeros((d, 2 * LANES), F32)
    w_r = w_r.at[:, :MOE_GROUPS].set(w_route_group[l])
    w_r = w_r.at[:, LANES:LANES + N_EXPERTS].set(w_route_expert[l]).astype(BF16)
    li = np.arange(min(256, t))
    tri_strict = jnp.asarray((li[:, None] > li[None, :]).astype(np.float32), BF16)
    x1, h2, ridx, rw, cnt = _outproj(merged, w_out[l].astype(BF16), x.reshape(m, d), mods,
                                        norm_ffn[l][None, :], w_r, tri_strict, tm_o, t)

    na = m * TOP_K
    nblk = -(-(na + N_EXPERTS * (MOE_BLOCK - 1)) // MOE_BLOCK)
    counts = cnt[0, :N_EXPERTS].astype(I32)
    padded = (counts + MOE_BLOCK - 1) // MOE_BLOCK * MOE_BLOCK
    pend = jnp.cumsum(padded)
    pstart = pend - padded
    blk_start = jnp.arange(nblk, dtype=I32) * MOE_BLOCK
    blk_e = jnp.clip(jnp.sum((pend[None, :] <= blk_start[:, None]).astype(I32), axis=1),
                     0, N_EXPERTS - 1)
    n_used = (pend[N_EXPERTS - 1:] // MOE_BLOCK).astype(I32)
    meta = jnp.concatenate([pstart + counts, pend, n_used]).astype(I32)
    e_sel = ridx[0:TOP_K]
    seg0 = jnp.zeros_like(e_sel)
    for e in range(N_EXPERTS):
        seg0 = jnp.where(e_sel == e, pstart[e], seg0)
    pos = (seg0 + ridx[TOP_K:2 * TOP_K]).reshape(-1)

    eo = _experts(blk_e, pos, meta, h2, w_gate[l], w_up[l], w_down[l], nblk)

    tm_c = min(256, t)
    out = _combine(pos, eo, x1, rw, mods, norm_final[None, :], tm_c, t)
    return out.reshape(bsz, t, d)
```

```python
import functools

import numpy as np
import jax
import jax.numpy as jnp
from jax import lax
from jax.experimental import pallas as pl
from jax.experimental.pallas import tpu as pltpu

F32 = jnp.float32
BF16 = jnp.bfloat16
I32 = jnp.int32

D_MODEL = 2048
GRID_W = 64
NORM_EPS = 1e-6
N_MOD = 6
SSD_HEADS = 16
SSD_HEAD_DIM = 64
D_SSD = SSD_HEADS * SSD_HEAD_DIM
SSD_GROUPS = 2
SSD_STATE = 128
SSD_CONV = 5
SSD_CHUNK = 128
CONV_CH = D_SSD + 2 * SSD_GROUPS * SSD_STATE
ATTN_HEADS = 8
ATTN_KV_HEADS = 2
ATTN_HEAD_DIM = 128
D_ATTN = ATTN_HEADS * ATTN_HEAD_DIM
D_KV = ATTN_KV_HEADS * ATTN_HEAD_DIM
WINDOW = 128
ROPE_BASE = 10000.0
N_BRANCH = 2
MOE_GROUPS = 4
EXPERTS_PER_GROUP = 8
N_EXPERTS = MOE_GROUPS * EXPERTS_PER_GROUP
TOP_K = 2
D_EXPERT = 512
MOE_BLOCK = 256

LANES = 128
TOK_PACK = D_MODEL // (2 * LANES)
TOK_PITCH = TOK_PACK + 1
VMEM_LIMIT = 56 * 1024 * 1024

U_XS, U_Z, U_Q, U_BC, U_K, U_V, U_G = 0, 1024, 2048, 3072, 3584, 3840, 4096
U_W = 8192
UC_XS, UC_BC, UC_K, UC_V = 0, 1024, 1536, 1792
UC_W = 2048
IN_TN = 1024


def _params(sem):
    return pltpu.CompilerParams(dimension_semantics=sem, vmem_limit_bytes=VMEM_LIMIT)


def _silu(v):
    return v * jax.nn.sigmoid(v)


def _split_bf16(v, n):
    parts = []
    r = v
    for _ in range(n):
        p = r.astype(BF16)
        parts.append(p)
        r = r - p.astype(F32)
    return parts


def _dot(a, b):
    return jnp.dot(a, b, preferred_element_type=F32)


def _dot_nt(a, b):
    return lax.dot_general(a, b, (((1,), (1,)), ((), ())), preferred_element_type=F32)


def _ada_kernel(c_ref, w_ref, b_ref, o_ref):
    a = _silu(c_ref[...]).astype(BF16)
    o_ref[...] = _dot(a, w_ref[...].astype(BF16)) + b_ref[...]


def _ada(cc, w, b):
    rows, d = cc.shape
    n = w.shape[1]
    tn = 1024
    return pl.pallas_call(
        _ada_kernel,
        grid=(n // tn,),
        in_specs=[pl.BlockSpec((rows, d), lambda j: (0, 0)),
                  pl.BlockSpec((d, tn), lambda j: (0, j)),
                  pl.BlockSpec((1, tn), lambda j: (0, j))],
        out_specs=pl.BlockSpec((rows, tn), lambda j: (0, j)),
        out_shape=jax.ShapeDtypeStruct((rows, n), F32),
        compiler_params=_params(("arbitrary",)),
        name="ada",
    )(cc, w, b)


def _rope(a, cos, sin_signed, first):
    partner = jnp.where(first, pltpu.roll(a, 96, 1), pltpu.roll(a, 32, 1))
    return a * cos + partner * sin_signed


def _inproj_kernel(*refs, rope_q_blocks, rope_kv_block, sub):
    if rope_q_blocks:
        x_ref, nw_ref, mod_ref, w_ref, wdt_ref, cos_ref, sin_ref, o_ref, dt_ref, h_scr = refs
    else:
        x_ref, nw_ref, mod_ref, w_ref, wdt_ref, o_ref, dt_ref, h_scr = refs
    j = pl.program_id(1)
    tm = x_ref.shape[0]
    tn = w_ref.shape[1]

    @pl.when(j == 0)
    def _():
        nw = nw_ref[...]
        sh = mod_ref[0, 0:1, :]
        sc = mod_ref[0, 1:2, :]

        def body(r, carry):
            rows = pl.ds(pl.multiple_of(r * sub, sub), sub)
            xf = x_ref[rows, :]
            ms = jnp.mean(xf * xf, axis=-1, keepdims=True)
            h = (xf * lax.rsqrt(ms + NORM_EPS) * nw) * (1.0 + sc) + sh
            hb = h.astype(BF16)
            h_scr[rows, :] = hb
            d = _dot(hb, wdt_ref[...])
            dt_ref[0, rows, :] = d[:, :LANES]
            dt_ref[1, rows, :] = d[:, LANES:]
            return carry

        lax.fori_loop(0, tm // sub, body, 0)

    n_split = 2 if tm >= 512 else 1
    part = tm // n_split

    def store(lo, hi):
        for sp in range(n_split):
            rows = slice(sp * part, (sp + 1) * part)
            acc = _dot(h_scr[rows, :], w_ref[...])
            if hi > lo:
                cos = cos_ref[rows, :]
                sin = sin_ref[rows, :]
                lane = lax.broadcasted_iota(I32, (part, LANES), 1)
                first = (lane % 64) < 32
            for hh in range(tn // LANES):
                a = acc[:, hh * LANES:(hh + 1) * LANES]
                if lo <= hh < hi:
                    a = _rope(a, cos, sin, first)
                o_ref[rows, hh * LANES:(hh + 1) * LANES] = a.astype(BF16)

    if rope_q_blocks:
        is_q = (j >= rope_q_blocks[0]) & (j <= rope_q_blocks[-1])
        is_kv = j == rope_kv_block
        k_lo = (U_K % tn) // LANES

        @pl.when(is_q)
        def _():
            store(0, tn // LANES)

        @pl.when(is_kv)
        def _():
            store(k_lo, k_lo + D_KV // LANES)

        @pl.when(jnp.logical_not(is_q | is_kv))
        def _():
            store(0, 0)
    else:
        store(0, 0)


def _inproj(x2, nw, mods, w, wdt, cos_t, sin_t, *, rows_per_mod, mod_row0, tm, rope):
    m, d = x2.shape
    n = w.shape[1]
    tn = IN_TN
    nbm = rows_per_mod // tm if rows_per_mod else 0
    if rows_per_mod:
        mod_map = lambda i, j: (mod_row0 + i // nbm, 0, 0)
    else:
        mod_map = lambda i, j: (mod_row0, 0, 0)
    in_specs = [pl.BlockSpec((tm, d), lambda i, j: (i, 0)),
                pl.BlockSpec((1, d), lambda i, j: (0, 0)),
                pl.BlockSpec((1, N_MOD, d), mod_map),
                pl.BlockSpec((d, tn), lambda i, j: (0, j)),
                pl.BlockSpec((d, 2 * LANES), lambda i, j: (0, 0))]
    args = [x2, nw, mods, w, wdt]
    if rope:
        in_specs += [pl.BlockSpec((tm, LANES), lambda i, j: (i % nbm, 0)),
                     pl.BlockSpec((tm, LANES), lambda i, j: (i % nbm, 0))]
        args += [cos_t, sin_t]
        rq = tuple(range(U_Q // tn, (U_Q + D_ATTN) // tn))
        rkv = U_K // tn
    else:
        rq, rkv = (), None
    kern = functools.partial(_inproj_kernel, rope_q_blocks=rq, rope_kv_block=rkv, sub=128)
    return pl.pallas_call(
        kern,
        grid=(m // tm, n // tn),
        in_specs=in_specs,
        out_specs=[pl.BlockSpec((tm, tn), lambda i, j: (i, j)),
                   pl.BlockSpec((2, tm, LANES), lambda i, j: (0, i, 0))],
        out_shape=[jax.ShapeDtypeStruct((m, n), BF16),
                   jax.ShapeDtypeStruct((2, m, LANES), F32)],
        scratch_shapes=[pltpu.VMEM((tm, d), BF16)],
        compiler_params=_params(("arbitrary", "arbitrary")),
        name="inproj_rope" if rope else "inproj_ctx",
    )(*args)


CONV_HALO = 16


def _conv_kernel(xs_ref, bc_ref, pxs_ref, pbc_ref, nxs_ref, nbc_ref, w_ref, b_ref,
                 oxs_ref, obc_ref, ext, *, blocks_per_seq):
    i = pl.program_id(0)
    r = xs_ref.shape[0]
    pos = i % blocks_per_seq
    not_first = pos != 0
    not_last = pos != blocks_per_seq - 1
    h = CONV_HALO
    ext[0:h, 0:D_SSD] = jnp.where(not_first, pxs_ref[...].astype(F32), 0.0)
    ext[0:h, D_SSD:] = jnp.where(not_first, pbc_ref[...].astype(F32), 0.0)
    ext[h:h + r, 0:D_SSD] = xs_ref[...].astype(F32)
    ext[h:h + r, D_SSD:] = bc_ref[...].astype(F32)
    ext[h + r:, 0:D_SSD] = jnp.where(not_last, nxs_ref[...].astype(F32), 0.0)
    ext[h + r:, D_SSD:] = jnp.where(not_last, nbc_ref[...].astype(F32), 0.0)
    cw = 256
    pad = SSD_CONV // 2
    for cb in range(0, CONV_CH, cw):
        acc = jnp.broadcast_to(b_ref[:, cb:cb + cw], (r, cw))
        for k in range(SSD_CONV):
            acc = acc + ext[h - pad + k:h - pad + k + r, cb:cb + cw] * w_ref[k:k + 1, cb:cb + cw]
        y = _silu(acc)
        if cb < D_SSD:
            oxs_ref[:, cb:cb + cw] = y
        else:
            obc_ref[:, cb - D_SSD:cb - D_SSD + cw] = y.astype(BF16)


def _conv(u, xs_col, bc_col, conv_w, conv_b, seq_len, r):
    m = u.shape[0]
    h = CONV_HALO
    bps = seq_len // r
    wbc = CONV_CH - D_SSD
    xs_c, bc_c = xs_col // D_SSD, bc_col // wbc
    nh = m // h
    prev = lambda i: jnp.maximum(i * (r // h) - 1, 0)
    nxt = lambda i: jnp.minimum((i + 1) * (r // h), nh - 1)
    return pl.pallas_call(
        functools.partial(_conv_kernel, blocks_per_seq=bps),
        grid=(m // r,),
        in_specs=[pl.BlockSpec((r, D_SSD), lambda i: (i, xs_c)),
                  pl.BlockSpec((r, wbc), lambda i: (i, bc_c)),
                  pl.BlockSpec((h, D_SSD), lambda i: (prev(i), xs_c)),
                  pl.BlockSpec((h, wbc), lambda i: (prev(i), bc_c)),
                  pl.BlockSpec((h, D_SSD), lambda i: (nxt(i), xs_c)),
                  pl.BlockSpec((h, wbc), lambda i: (nxt(i), bc_c)),
                  pl.BlockSpec((SSD_CONV, CONV_CH), lambda i: (0, 0)),
                  pl.BlockSpec((1, CONV_CH), lambda i: (0, 0))],
        out_specs=[pl.BlockSpec((r, D_SSD), lambda i: (i, 0)),
                   pl.BlockSpec((r, wbc), lambda i: (i, 0))],
        out_shape=[jax.ShapeDtypeStruct((m, D_SSD), F32),
                   jax.ShapeDtypeStruct((m, wbc), BF16)],
        scratch_shapes=[pltpu.VMEM((r + 2 * h, CONV_CH), F32)],
        compiler_params=_params(("arbitrary",)),
        name="conv_silu",
    )(u, u, u, u, u, u, conv_w, conv_b)


def _expand_heads(v, e2):
    return _dot(jnp.concatenate(_split_bf16(v, 2), axis=1), e2)


def _cumsum_mm(tri3, da):
    return _dot(tri3, jnp.concatenate(_split_bf16(da, 3), axis=0))


def _ctx_state_kernel(xs_ref, b_ref, dt_ref, alog_ref, bias_ref, tri_ref, e_ref, h_ref):
    d = pl.program_id(1)
    tc = xs_ref.shape[0]
    dt = jax.nn.softplus(dt_ref[0] + bias_ref[0])
    a = -jnp.exp(alog_ref[0])
    cum = _cumsum_mm(tri_ref[0], dt * a)
    cum_end = jnp.where(d == 0, cum[tc - 1:tc, :], cum[0:1, :])
    wx = _expand_heads(dt * jnp.exp(cum_end - cum), e_ref[...])
    xw = (xs_ref[...] * wx).astype(BF16)
    gw = D_SSD // SSD_GROUPS
    for g in range(SSD_GROUPS):
        bgt = b_ref[:, g * SSD_STATE:(g + 1) * SSD_STATE].astype(F32).T.astype(BF16)
        h_ref[0, 0, :, g * gw:(g + 1) * gw] = _dot(bgt, xw[:, g * gw:(g + 1) * gw])


def _ctx_state(xs_c, bc_c, dt2, alog, bias, tri3, e2, bsz, tc):
    return pl.pallas_call(
        _ctx_state_kernel,
        grid=(bsz, 2),
        in_specs=[pl.BlockSpec((tc, D_SSD), lambda b, d: (b, 0)),
                  pl.BlockSpec((tc, SSD_GROUPS * SSD_STATE), lambda b, d: (b, 0)),
                  pl.BlockSpec((1, tc, LANES), lambda b, d: (d, b, 0)),
                  pl.BlockSpec((1, 1, LANES), lambda b, d: (d, 0, 0)),
                  pl.BlockSpec((1, 1, LANES), lambda b, d: (d, 0, 0)),
                  pl.BlockSpec((1, tc, 3 * tc), lambda b, d: (d, 0, 0)),
                  pl.BlockSpec((2 * LANES, D_SSD), lambda b, d: (0, 0))],
        out_specs=pl.BlockSpec((1, 1, SSD_STATE, D_SSD), lambda b, d: (b, d, 0, 0)),
        out_shape=jax.ShapeDtypeStruct((bsz, 2, SSD_STATE, D_SSD), F32),
        compiler_params=_params(("arbitrary", "arbitrary")),
        name="ssd_ctx_state",
    )(xs_c, bc_c, dt2, alog, bias, tri3, e2)


def _ssd_chunk(fwd, xs, bc_ref, dt_raw, alog, bias, tri3, e2, st, y_out):
    lc = SSD_CHUNK
    dt = jax.nn.softplus(dt_raw + bias)
    cum = _cumsum_mm(tri3, dt * (-jnp.exp(alog)))
    cum_t = cum.T
    dt_t = dt.T
    end = lc - 1 if fwd else 0
    ecx = _expand_heads(jnp.exp(cum), e2)
    wx = _expand_heads(dt * jnp.exp(cum[end:end + 1, :] - cum), e2)
    xs_b = xs.astype(BF16)
    xw_b = (xs * wx).astype(BF16)
    cd_row = ecx[end:end + 1, :]
    li = lax.broadcasted_iota(I32, (lc, lc), 0)
    si = lax.broadcasted_iota(I32, (lc, lc), 1)
    causal = (li >= si) if fwd else (li <= si)
    low = lax.broadcasted_iota(I32, (lc, LANES), 1) < SSD_HEAD_DIM
    gw = D_SSD // SSD_GROUPS
    hpg = SSD_HEADS // SSD_GROUPS
    for g in range(SSD_GROUPS):
        bg = bc_ref[:, g * SSD_STATE:(g + 1) * SSD_STATE]
        cg = bc_ref[:, (SSD_GROUPS + g) * SSD_STATE:(SSD_GROUPS + g + 1) * SSD_STATE]
        cb = _dot_nt(cg, bg)
        yoff = _dot(cg, st[:, g * gw:(g + 1) * gw].astype(BF16)) * ecx[:, g * gw:(g + 1) * gw]
        for pr in range(hpg // 2):
            h0 = g * hpg + 2 * pr
            ms = []
            for h in (h0, h0 + 1):
                seg = cum[:, h:h + 1] - cum_t[h:h + 1, :]
                ms.append((cb * jnp.exp(jnp.where(causal, seg, -jnp.inf)) * dt_t[h:h + 1, :]).astype(BF16))
            xp = xs_b[:, h0 * SSD_HEAD_DIM:(h0 + 2) * SSD_HEAD_DIM]
            zero = jnp.zeros_like(xp)
            rhs = jnp.concatenate([jnp.where(low, xp, zero), jnp.where(low, zero, xp)], axis=0)
            y_out[:, h0 * SSD_HEAD_DIM:(h0 + 2) * SSD_HEAD_DIM] = (
                _dot(jnp.concatenate(ms, axis=1), rhs) + yoff[:, pr * LANES:(pr + 1) * LANES])
        bgt = bg.astype(F32).T.astype(BF16)
        st[:, g * gw:(g + 1) * gw] = (st[:, g * gw:(g + 1) * gw] * cd_row[:, g * gw:(g + 1) * gw]
                                      + _dot(bgt, xw_b[:, g * gw:(g + 1) * gw]))


def _ssd_kernel(xsf_ref, xsb_ref, bcf_ref, bcb_ref, dtf_ref, dtb_ref, alog_ref, bias_ref, tri_ref,
                e_ref, h0_ref, z_ref, dsk_ref, nw_ref, o_ref, st, ybuf, ycur, *, nc):
    s = pl.program_id(1)
    lc = SSD_CHUNK

    @pl.when(s == 0)
    def _():
        st[...] = h0_ref[0]

    e2 = e_ref[...]
    _ssd_chunk(True, xsf_ref[...], bcf_ref, dtf_ref[0], alog_ref[0], bias_ref[0], tri_ref[0], e2,
               st.at[0], ycur.at[0])
    _ssd_chunk(False, xsb_ref[...], bcb_ref, dtb_ref[0], alog_ref[1], bias_ref[1], tri_ref[1], e2,
               st.at[1], ycur.at[1])
    rows = (pl.ds(pl.multiple_of(s * lc, lc), lc), pl.ds(pl.multiple_of((nc - 1 - s) * lc, lc), lc))

    @pl.when(s < nc // 2)
    def _():
        for d in range(2):
            ybuf[rows[d], :] = ycur[d]

    @pl.when(s >= nc // 2)
    def _():
        gw = D_SSD // SSD_GROUPS
        for d, xs_ref in enumerate((xsf_ref, xsb_ref)):
            y = ycur[d] + ybuf[rows[d], :] + dsk_ref[...] * xs_ref[...]
            gy = y * _silu(z_ref[rows[d], :].astype(F32))
            for g in range(SSD_GROUPS):
                blk = gy[:, g * gw:(g + 1) * gw]
                ms = jnp.mean(blk * blk, axis=-1, keepdims=True)
                o_ref[rows[d], g * gw:(g + 1) * gw] = (blk * lax.rsqrt(ms + NORM_EPS)
                                                       * nw_ref[:, g * gw:(g + 1) * gw]).astype(BF16)


def _ssd(xs_c, bc_c, dt2, alog, bias, tri3, e2, h0, u, dsk, nw, bsz, t):
    nc = t // SSD_CHUNK
    lc = SSD_CHUNK
    m = bsz * t
    fw = lambda b, s: b * nc + s
    bw = lambda b, s: b * nc + nc - 1 - s
    zc = U_Z // D_SSD
    bcw = 2 * SSD_GROUPS * SSD_STATE
    return pl.pallas_call(
        functools.partial(_ssd_kernel, nc=nc),
        grid=(bsz, nc),
        in_specs=[pl.BlockSpec((lc, D_SSD), lambda b, s: (fw(b, s), 0)),
                  pl.BlockSpec((lc, D_SSD), lambda b, s: (bw(b, s), 0)),
                  pl.BlockSpec((lc, bcw), lambda b, s: (fw(b, s), 0)),
                  pl.BlockSpec((lc, bcw), lambda b, s: (bw(b, s), 0)),
                  pl.BlockSpec((1, lc, LANES), lambda b, s: (0, fw(b, s), 0)),
                  pl.BlockSpec((1, lc, LANES), lambda b, s: (1, bw(b, s), 0)),
                  pl.BlockSpec((2, 1, LANES), lambda b, s: (0, 0, 0)),
                  pl.BlockSpec((2, 1, LANES), lambda b, s: (0, 0, 0)),
                  pl.BlockSpec((2, lc, 3 * lc), lambda b, s: (0, 0, 0)),
                  pl.BlockSpec((2 * LANES, D_SSD), lambda b, s: (0, 0)),
                  pl.BlockSpec((1, 2, SSD_STATE, D_SSD), lambda b, s: (b, 0, 0, 0)),
                  pl.BlockSpec((t, D_SSD), lambda b, s: (b, zc)),
                  pl.BlockSpec((1, D_SSD), lambda b, s: (0, 0)),
                  pl.BlockSpec((1, D_SSD), lambda b, s: (0, 0))],
        out_specs=pl.BlockSpec((t, D_SSD), lambda b, s: (b, 0)),
        out_shape=jax.ShapeDtypeStruct((m, D_SSD), BF16),
        scratch_shapes=[pltpu.VMEM((2, SSD_STATE, D_SSD), F32),
                        pltpu.VMEM((t, D_SSD), F32),
                        pltpu.VMEM((2, lc, D_SSD), F32)],
        compiler_params=_params(("arbitrary", "arbitrary")),
        name="ssd_scan",
    )(xs_c, xs_c, bc_c, bc_c, dt2, dt2, alog, bias, tri3, e2, h0, u, dsk, nw)


def _attn_kernel(sink_ref, q_ref, k_ref, v_ref, kc_ref, vc_ref, o_ref, kp, vp, bias, *, t):
    g = pl.program_id(1)
    w = WINDOW
    nb = t // w
    rep = ATTN_HEADS // ATTN_KV_HEADS
    tc = kc_ref.shape[0]
    nk = 3 * w + tc
    zeros = jnp.zeros((w, ATTN_HEAD_DIM), BF16)
    kp[0:w, :] = zeros
    kp[w:w + t, :] = k_ref[...]
    kp[w + t:, :] = zeros
    vp[0:w, :] = zeros
    vp[w:w + t, :] = v_ref[...]
    vp[w + t:, :] = zeros
    qi = lax.broadcasted_iota(I32, (rep * w, w), 0) % w
    kj = lax.broadcasted_iota(I32, (rep * w, w), 1)
    bias[0] = jnp.where(kj >= qi, 0.0, -jnp.inf).astype(F32)
    bias[1] = jnp.where(kj <= qi, 0.0, -jnp.inf).astype(F32)
    sink_col = jnp.concatenate(
        [jnp.full((w, 1), sink_ref[g * rep + h], F32) for h in range(rep)], axis=0)
    scale = ATTN_HEAD_DIM ** -0.5
    log2e = float(np.log2(np.e))

    def body(n, carry):
        rows = pl.ds(pl.multiple_of(n * w, w), w)
        q = jnp.concatenate([q_ref[rows, h * ATTN_HEAD_DIM:(h + 1) * ATTN_HEAD_DIM]
                             for h in range(rep)], axis=0)
        band = pl.ds(pl.multiple_of(n * w, w), 3 * w)
        kall = jnp.concatenate([kp[band, :], kc_ref[...]], axis=0)
        vall = jnp.concatenate([vp[band, :], vc_ref[...]], axis=0)
        raw = _dot_nt(q, kall)
        prev = raw[:, 0:w] + (bias[0] + jnp.where(n == 0, -jnp.inf, 0.0))
        nxt = raw[:, 2 * w:3 * w] + (bias[1] + jnp.where(n == nb - 1, -jnp.inf, 0.0))
        sc = jnp.concatenate([prev, raw[:, w:2 * w], nxt, raw[:, 3 * w:]], axis=1)
        mx = jnp.maximum(jnp.max(sc, axis=-1, keepdims=True) * scale, sink_col)
        p = jnp.exp2(sc * (scale * log2e) - mx * log2e)
        den = jnp.sum(p, axis=-1, keepdims=True) + jnp.exp2((sink_col - mx) * log2e)
        o = _dot(p.astype(BF16), vall) / den
        for h in range(rep):
            o_ref[rows, h * ATTN_HEAD_DIM:(h + 1) * ATTN_HEAD_DIM] = o[h * w:(h + 1) * w, :].astype(BF16)
        return carry

    lax.fori_loop(0, nb, body, 0, unroll=2)


def _attn(sink, u, uc, bsz, t, tc):
    rep = ATTN_HEADS // ATTN_KV_HEADS
    qw = rep * ATTN_HEAD_DIM
    hd = ATTN_HEAD_DIM
    return pl.pallas_call(
        functools.partial(_attn_kernel, t=t),
        grid=(bsz, ATTN_KV_HEADS),
        in_specs=[pl.BlockSpec(memory_space=pltpu.SMEM),
                  pl.BlockSpec((t, qw), lambda b, g: (b, U_Q // qw + g)),
                  pl.BlockSpec((t, hd), lambda b, g: (b, U_K // hd + g)),
                  pl.BlockSpec((t, hd), lambda b, g: (b, U_V // hd + g)),
                  pl.BlockSpec((tc, hd), lambda b, g: (b, UC_K // hd + g)),
                  pl.BlockSpec((tc, hd), lambda b, g: (b, UC_V // hd + g))],
        out_specs=pl.BlockSpec((t, qw), lambda b, g: (b, g)),
        out_shape=jax.ShapeDtypeStruct((bsz * t, D_ATTN), BF16),
        scratch_shapes=[pltpu.VMEM((t + 2 * WINDOW, hd), BF16),
                        pltpu.VMEM((t + 2 * WINDOW, hd), BF16),
                        pltpu.VMEM((2, rep * WINDOW, WINDOW), F32)],
        compiler_params=_params(("arbitrary", "arbitrary")),
        name="window_attn",
    )(sink, u, u, u, uc, uc)


def _merge_kernel(ys_ref, ya_ref, g0_ref, g1_ref, ws_ref, wa_ref, o_ref):
    ys = ys_ref[...]
    ya = ya_ref[...]
    cw = 512
    for cb in range(0, D_MODEL, cw):
        t0 = _dot(ys, ws_ref[:, cb:cb + cw])
        t1 = _dot(ya, wa_ref[:, cb:cb + cw])
        g0 = jax.nn.sigmoid(g0_ref[:, cb:cb + cw].astype(F32))
        g1 = jax.nn.sigmoid(g1_ref[:, cb:cb + cw].astype(F32))
        o_ref[:, cb:cb + cw] = (g0 * t0 + g1 * t1).astype(BF16)


def _merge(ys, ya, u, ws, wa, tm):
    m = ys.shape[0]
    gc = U_G // D_MODEL
    return pl.pallas_call(
        _merge_kernel,
        grid=(m // tm,),
        in_specs=[pl.BlockSpec((tm, D_SSD), lambda i: (i, 0)),
                  pl.BlockSpec((tm, D_ATTN), lambda i: (i, 0)),
                  pl.BlockSpec((tm, D_MODEL), lambda i: (i, gc)),
                  pl.BlockSpec((tm, D_MODEL), lambda i: (i, gc + 1)),
                  pl.BlockSpec((D_SSD, D_MODEL), lambda i: (0, 0)),
                  pl.BlockSpec((D_ATTN, D_MODEL), lambda i: (0, 0))],
        out_specs=pl.BlockSpec((tm, D_MODEL), lambda i: (i, 0)),
        out_shape=jax.ShapeDtypeStruct((m, D_MODEL), BF16),
        compiler_params=_params(("arbitrary",)),
        name="branch_merge",
    )(ys, ya, u, u, ws, wa)


def _lane_min_index(mask, lane):
    return jnp.min(jnp.where(mask, lane, LANES), axis=-1, keepdims=True)


def _outproj_kernel(mg_ref, wo_ref, x_ref, mod_ref, nw_ref, wr_ref, tri_ref,
                    x1_ref, h2_ref, ridx_ref, rw_ref, cnt_ref, carry, mixbuf):
    i = pl.program_id(0)
    tm = x_ref.shape[0]

    @pl.when(i == 0)
    def _():
        mixbuf[1] = jnp.zeros(mixbuf.shape[1:], F32)

    @pl.when(i <= 1)
    def _():
        carry[...] = jnp.zeros_like(carry)

    g1 = mod_ref[0, 2:3, :]
    sh2 = mod_ref[0, 3:4, :]
    sc2 = mod_ref[0, 4:5, :]
    cw = 2 * LANES
    dw = D_MODEL // 4

    def step(cur):
        def project(j):
            mixbuf[cur, :, j * dw:(j + 1) * dw] = _dot(mg_ref[...], wo_ref[:, j * dw:(j + 1) * dw])

        ssq = jnp.zeros((tm, 1), F32)
        for c in range(TOK_PACK):
            cols = slice(c * cw, (c + 1) * cw)
            x1c = x_ref[:, cols] + g1[:, cols] * mixbuf[1 - cur, :, cols]
            x1_ref[:, cols] = x1c
            ssq = ssq + jnp.sum(x1c * x1c, axis=-1, keepdims=True)
            if c == 1:
                project(0)
            if c == 5:
                project(1)
        rinv = lax.rsqrt(ssq / D_MODEL + NORM_EPS)
        hbs = []
        for c in range(TOK_PACK):
            cols = slice(c * cw, (c + 1) * cw)
            h2c = (x1_ref[:, cols] * rinv * nw_ref[:, cols]) * (1.0 + sc2[:, cols]) + sh2[:, cols]
            hb = h2c.astype(BF16)
            hbs.append(hb)
            bits = pltpu.bitcast(hb.astype(F32), jnp.uint32)
            h2_ref[pl.ds(c, tm, stride=TOK_PACK), :] = (
                (bits[:, LANES:] & jnp.uint32(0xFFFF0000)) | (bits[:, :LANES] >> 16))
            if c == 3:
                project(2)
        lg = _dot(jnp.concatenate(hbs, axis=1), wr_ref[...])
        project(3)
        _route(lg, tri_ref, ridx_ref, rw_ref, cnt_ref, carry)

    for cur in range(2):
        pl.when(i % 2 == cur)(functools.partial(step, cur))


def _route(lg, tri_ref, ridx_ref, rw_ref, cnt_ref, carry):
    tm = tri_ref.shape[0]
    lane = lax.broadcasted_iota(I32, (tm, LANES), 1)
    gl = jnp.where(lane < MOE_GROUPS, lg[:, :LANES], -jnp.inf)
    gmax = jnp.max(gl, axis=-1, keepdims=True)
    g_w = 1.0 / jnp.sum(jnp.exp(gl - gmax), axis=-1, keepdims=True)
    g_idx = _lane_min_index(gl == gmax, lane)
    el = jnp.where((lane // EXPERTS_PER_GROUP) == g_idx, lg[:, LANES:], -jnp.inf)
    v1 = jnp.max(el, axis=-1, keepdims=True)
    i1 = _lane_min_index(el == v1, lane)
    el2 = jnp.where(lane == i1, -jnp.inf, el)
    v2 = jnp.max(el2, axis=-1, keepdims=True)
    i2 = _lane_min_index(el2 == v2, lane)
    e2 = jnp.exp(v2 - v1)
    w1 = g_w * (1.0 / (1.0 + e2))
    w2 = g_w * (e2 / (1.0 + e2))

    hit1 = lane == i1
    hit2 = lane == i2
    onehot = jnp.where(hit1 | hit2, 1.0, 0.0).astype(F32)
    before = _dot(tri_ref[...], onehot.astype(BF16)) + carry[0:1, :]
    r1 = jnp.sum(jnp.where(hit1, before, 0.0), axis=-1, keepdims=True)
    r2 = jnp.sum(jnp.where(hit2, before, 0.0), axis=-1, keepdims=True)
    total = carry[0:1, :] + jnp.sum(onehot, axis=0, keepdims=True)
    carry[...] = jnp.broadcast_to(total, carry.shape)
    cnt_ref[...] = jnp.broadcast_to(total, cnt_ref.shape)
    packed = jnp.where(lane == 0, i1.astype(F32), jnp.where(lane == 1, i2.astype(F32), jnp.where(
        lane == 2, r1, jnp.where(lane == 3, r2, 0.0))))
    ridx_ref[...] = packed.T[0:8, :].astype(I32)
    rw_ref[...] = jnp.where(lane == 0, w1, jnp.where(lane == 1, w2, 0.0))


def _outproj(mg, wo, x2, mods, nw, wr, tri, tm, t):
    m = x2.shape[0]
    nbm = t // tm
    nsteps = m // tm
    mm = lambda i: jnp.minimum(i, nsteps - 1)
    ep = lambda i: jnp.maximum(i - 1, 0)
    return pl.pallas_call(
        _outproj_kernel,
        grid=(nsteps + 1,),
        in_specs=[pl.BlockSpec((tm, D_MODEL), lambda i: (mm(i), 0)),
                  pl.BlockSpec((D_MODEL, D_MODEL), lambda i: (0, 0), pipeline_mode=pl.Buffered(1)),
                  pl.BlockSpec((tm, D_MODEL), lambda i: (ep(i), 0)),
                  pl.BlockSpec((1, N_MOD, D_MODEL), lambda i: (ep(i) // nbm, 0, 0)),
                  pl.BlockSpec((1, D_MODEL), lambda i: (0, 0)),
                  pl.BlockSpec((D_MODEL, 2 * LANES), lambda i: (0, 0)),
                  pl.BlockSpec((tm, tm), lambda i: (0, 0))],
        out_specs=[pl.BlockSpec((tm, D_MODEL), lambda i: (ep(i), 0)),
                   pl.BlockSpec((tm * TOK_PACK, LANES), lambda i: (ep(i), 0)),
                   pl.BlockSpec((8, tm), lambda i: (0, ep(i))),
                   pl.BlockSpec((tm, LANES), lambda i: (ep(i), 0)),
                   pl.BlockSpec((8, LANES), lambda i: (0, 0))],
        out_shape=[jax.ShapeDtypeStruct((m, D_MODEL), F32),
                   jax.ShapeDtypeStruct((m * TOK_PACK, LANES), jnp.uint32),
                   jax.ShapeDtypeStruct((8, m), I32),
                   jax.ShapeDtypeStruct((m, LANES), F32),
                   jax.ShapeDtypeStruct((8, LANES), F32)],
        scratch_shapes=[pltpu.VMEM((8, LANES), F32),
                        pltpu.VMEM((2, tm, D_MODEL), F32)],
        compiler_params=_params(("arbitrary",)),
        name="outproj_router",
    )(mg, wo, x2, mods, nw, wr, tri)


def _gather_rows(idx_ref, base, n, src, dst, sem, row0=0):
    for r in range(n):
        tok = idx_ref[base + r]
        pltpu.make_async_copy(src.at[pl.ds(tok, 1), :], dst.at[pl.ds(row0 + r, 1), :],
                              sem).start(priority=r % 2)


def _gather_slabs(idx_ref, base, n, src, dst, sem, row0=0):
    for r in range(n):
        tok = idx_ref[base + r]
        pltpu.make_async_copy(src.at[pl.ds(pl.multiple_of(tok * TOK_PACK, TOK_PACK), TOK_PACK), :],
                              dst.at[pl.ds((row0 + r) * TOK_PITCH, TOK_PACK), :],
                              sem).start(priority=r % 2)


def _expert_kernel(blk_e_ref, pos_ref, meta_ref, h2_hbm, wg_ref, wu_ref, wd_ref, o_ref,
                   xb, tok_ref, wgb, wub, wdb, sem):
    i = pl.program_id(0)
    n_used = meta_ref[2 * N_EXPERTS]
    slot = i % 2
    rows = MOE_BLOCK
    m = pos_ref.shape[0] // TOP_K

    def wait_block(sl):
        pltpu.make_async_copy(h2_hbm.at[pl.ds(0, rows * TOK_PACK), :],
                              xb.at[sl, pl.ds(0, rows * TOK_PACK), :], sem.at[sl]).wait()

    @pl.when(i == 0)
    def _():
        def pad_expert(e, carry):
            def pad_slot(s, c):
                tok_ref[s] = 0
                return c
            return lax.fori_loop(meta_ref[e], meta_ref[N_EXPERTS + e], pad_slot, carry)

        lax.fori_loop(0, N_EXPERTS, pad_expert, 0)

        def place(t, carry):
            for k in range(TOP_K):
                tok_ref[pos_ref[k * m + t]] = t
            return carry

        lax.fori_loop(0, m, place, 0, unroll=8)
        _gather_slabs(tok_ref, 0, rows, h2_hbm, xb.at[0], sem.at[0])

    @pl.when(i < n_used)
    def _():
        wait_block(slot)
        nxt = jnp.minimum(i + 1, n_used - 1)
        n_pass = 2
        per = rows // (n_pass * TOK_PACK)

        def read_block(ps):
            xs = []
            for a in range(TOK_PACK):
                pair = xb[slot, pl.ds(a, rows, stride=TOK_PITCH), :]
                xs.append(pltpu.bitcast(pair << 16, F32).astype(BF16))
                xs.append(pltpu.bitcast(pair & jnp.uint32(0xFFFF0000), F32).astype(BF16))
                g = ps * TOK_PACK + a
                _gather_slabs(tok_ref, nxt * rows + g * per, per, h2_hbm, xb.at[1 - slot],
                              sem.at[1 - slot], row0=g * per)
            return jnp.concatenate(xs, axis=1)

        prev_e = blk_e_ref[jnp.maximum(i - 1, 0)]

        @pl.when((i == 0) | (blk_e_ref[i] != prev_e))
        def _():
            wgb[...] = wg_ref[0].astype(BF16)
            wub[...] = wu_ref[0].astype(BF16)
            wdb[...] = wd_ref[0].astype(BF16)

        gate = _dot(read_block(0), wgb[...])
        act = (_silu(gate) * _dot(read_block(1), wub[...])).astype(BF16)
        o_ref[...] = _dot(act, wdb[...])

        @pl.when(i == n_used - 1)
        def _():
            wait_block(1 - slot)

    @pl.when(i >= n_used)
    def _():
        o_ref[...] = jnp.zeros_like(o_ref)


def _experts(blk_e, pos, meta, h2, wg, wu, wd, nblk):
    rows = MOE_BLOCK
    live = lambda i, mt: jnp.minimum(i, mt[2 * N_EXPERTS] - 1)
    grid_spec = pltpu.PrefetchScalarGridSpec(
        num_scalar_prefetch=3,
        grid=(nblk,),
        in_specs=[pl.BlockSpec(memory_space=pl.ANY),
                  pl.BlockSpec((1, D_MODEL, D_EXPERT), lambda i, be, ps, mt: (be[live(i, mt)], 0, 0)),
                  pl.BlockSpec((1, D_MODEL, D_EXPERT), lambda i, be, ps, mt: (be[live(i, mt)], 0, 0)),
                  pl.BlockSpec((1, D_EXPERT, D_MODEL), lambda i, be, ps, mt: (be[live(i, mt)], 0, 0))],
        out_specs=pl.BlockSpec((rows, D_MODEL), lambda i, be, ps, mt: (i, 0)),
        scratch_shapes=[pltpu.VMEM((2, rows * TOK_PITCH, LANES), jnp.uint32),
                        pltpu.SMEM((nblk * rows,), I32),
                        pltpu.VMEM((D_MODEL, D_EXPERT), BF16),
                        pltpu.VMEM((D_MODEL, D_EXPERT), BF16),
                        pltpu.VMEM((D_EXPERT, D_MODEL), BF16),
                        pltpu.SemaphoreType.DMA((2,))],
    )
    return pl.pallas_call(
        _expert_kernel,
        grid_spec=grid_spec,
        out_shape=jax.ShapeDtypeStruct((nblk * rows, D_MODEL), F32),
        compiler_params=_params(("arbitrary",)),
        name="expert_mlp",
    )(blk_e, pos, meta, h2, wg, wu, wd)


def _combine_kernel(pos_ref, eo_hbm, x1_ref, rw_ref, mod_ref, nf_ref, o_ref, gb, sem):
    i = pl.program_id(0)
    nsteps = pl.num_programs(0)
    tm = x1_ref.shape[0]
    slot = i % 2

    def issue(step, sl):
        for k in range(TOP_K):
            _gather_rows(pos_ref, (k * nsteps + step) * tm, tm, eo_hbm, gb.at[sl, k], sem.at[sl])

    @pl.when(i == 0)
    def _():
        issue(0, 0)

    @pl.when(i + 1 < nsteps)
    def _():
        issue(i + 1, 1 - slot)

    for k in range(TOP_K):
        pltpu.make_async_copy(eo_hbm.at[pl.ds(0, tm), :], gb.at[slot, k], sem.at[slot]).wait()
    w1 = rw_ref[:, 0:1]
    w2 = rw_ref[:, 1:2]
    g2 = mod_ref[0, 5:6, :]
    x2 = x1_ref[...] + g2 * (gb[slot, 0] * w1 + gb[slot, 1] * w2)
    ms = jnp.mean(x2 * x2, axis=-1, keepdims=True)
    o_ref[...] = x2 * lax.rsqrt(ms + NORM_EPS) * nf_ref[...]


def _combine(pos, eo, x1, rw, mods, nf, tm, t):
    m = x1.shape[0]
    nbm = t // tm
    grid_spec = pltpu.PrefetchScalarGridSpec(
        num_scalar_prefetch=1,
        grid=(m // tm,),
        in_specs=[pl.BlockSpec(memory_space=pl.ANY),
                  pl.BlockSpec((tm, D_MODEL), lambda i, p: (i, 0)),
                  pl.BlockSpec((tm, LANES), lambda i, p: (i, 0)),
                  pl.BlockSpec((1, N_MOD, D_MODEL), lambda i, p: (i // nbm, 0, 0)),
                  pl.BlockSpec((1, D_MODEL), lambda i, p: (0, 0))],
        out_specs=pl.BlockSpec((tm, D_MODEL), lambda i, p: (i, 0)),
        scratch_shapes=[pltpu.VMEM((2, TOP_K, tm, D_MODEL), F32),
                        pltpu.SemaphoreType.DMA((2,))],
    )
    return pl.pallas_call(
        _combine_kernel,
        grid_spec=grid_spec,
        out_shape=jax.ShapeDtypeStruct((m, D_MODEL), F32),
        compiler_params=_params(("arbitrary",)),
        name="moe_combine_norm",
    )(pos, eo, x1, rw, mods, nf)


def _rope_tables(t):
    half = ATTN_HEAD_DIM // 4
    pos = jnp.arange(t, dtype=I32)
    row = (pos // GRID_W).astype(F32)
    col = (pos % GRID_W).astype(F32)
    freqs = ROPE_BASE ** (-jnp.arange(half, dtype=F32) / half)
    ar = row[:, None] * freqs[None, :]
    ac = col[:, None] * freqs[None, :]
    cos_t = jnp.concatenate([jnp.cos(ar), jnp.cos(ar), jnp.cos(ac), jnp.cos(ac)], axis=1)
    sin_t = jnp.concatenate([-jnp.sin(ar), jnp.sin(ar), -jnp.sin(ac), jnp.sin(ac)], axis=1)
    return cos_t, sin_t


def _tri_pair3(n):
    li = np.arange(n)[:, None]
    si = np.arange(n)[None, :]
    pair = np.stack([li >= si, li <= si]).astype(np.float32)
    return jnp.asarray(np.concatenate([pair] * 3, axis=2), BF16)


def _head_expand_matrix2():
    k = np.arange(LANES)[:, None]
    j = np.arange(D_SSD)[None, :]
    e = (k == j // SSD_HEAD_DIM).astype(np.float32)
    return jnp.asarray(np.concatenate([e, e], axis=0), BF16)


def _pad_heads(v):
    return jnp.pad(v.astype(F32), ((0, 0), (0, LANES - SSD_HEADS)))[:, None, :]


def kernel(x, c, ctx, c_ctx, w_ada, b_ada, norm_mix, norm_ffn, w_in, conv_w, conv_b, a_log, dt_bias,
           d_skip, ssd_norm, attn_sink, w_branch_ssd, w_branch_attn, w_out, w_route_group,
           w_route_expert, w_gate, w_up, w_down, norm_final):
    bsz, t, d = x.shape
    tc = ctx.shape[1]
    m = bsz * t
    l = 0

    ctx_row = bsz
    n_rows = -(-(bsz + 1) // 8) * 8
    cc = jnp.zeros((n_rows, d), F32).at[:bsz].set(c).at[ctx_row].set(c_ctx)
    mods = _ada(cc, w_ada[l], b_ada[l][None, :]).reshape(n_rows, N_MOD, d)

    wi = w_in[l]
    o_xbc = D_SSD
    o_dt = o_xbc + CONV_CH
    o_q = o_dt + 2 * SSD_HEADS
    o_k = o_q + D_ATTN
    o_v = o_k + D_KV
    o_g = o_v + D_KV
    w_xs = wi[:, o_xbc:o_xbc + D_SSD]
    w_bc = wi[:, o_xbc + D_SSD:o_dt]
    w_k = wi[:, o_k:o_v]
    w_v = wi[:, o_v:o_g]
    w_lat = jnp.concatenate([w_xs, wi[:, :D_SSD], wi[:, o_q:o_k], w_bc, w_k, w_v, wi[:, o_g:]],
                            axis=1).astype(BF16)
    w_ctx = jnp.concatenate([w_xs, w_bc, w_k, w_v], axis=1).astype(BF16)
    w_dt = jnp.zeros((d, 2 * LANES), F32)
    w_dt = w_dt.at[:, :SSD_HEADS].set(wi[:, o_dt:o_dt + SSD_HEADS])
    w_dt = w_dt.at[:, LANES:LANES + SSD_HEADS].set(wi[:, o_dt + SSD_HEADS:o_q]).astype(BF16)

    cos_t, sin_t = _rope_tables(t)
    nmix = norm_mix[l][None, :]
    tm_in = min(1024, t)
    u, dt2 = _inproj(x.reshape(m, d), nmix, mods, w_lat, w_dt, cos_t, sin_t,
                     rows_per_mod=t, mod_row0=0, tm=tm_in, rope=True)
    mc = bsz * tc
    uc, dtc2 = _inproj(ctx.reshape(mc, d), nmix, mods, w_ctx, w_dt, None, None,
                       rows_per_mod=0, mod_row0=ctx_row, tm=min(1024, mc), rope=False)

    cw = conv_w[l]
    cb = conv_b[l][None, :]
    xs_c, bc_c = _conv(u, U_XS, U_BC, cw, cb, t, min(256, t))
    xsc_c, bcc_c = _conv(uc, UC_XS, UC_BC, cw, cb, tc, min(256, tc))
    alog = _pad_heads(a_log[l])
    bias = _pad_heads(dt_bias[l])
    e_mat = _head_expand_matrix2()
    h0 = _ctx_state(xsc_c, bcc_c, dtc2, alog, bias, _tri_pair3(tc), e_mat, bsz, tc)
    dsk = jnp.repeat(d_skip[l].astype(F32), SSD_HEAD_DIM)[None, :]
    y_ssd = _ssd(xs_c, bc_c, dt2, alog, bias, _tri_pair3(SSD_CHUNK), e_mat, h0, u, dsk,
                 ssd_norm[l][None, :], bsz, t)

    y_attn = _attn(attn_sink[l].astype(F32), u, uc, bsz, t, tc)

    merged = _merge(y_ssd, y_attn, u, w_branch_ssd[l].astype(BF16), w_branch_attn[l].astype(BF16),
                    min(512, t))

    tm_o = min(512, t)
    w_r = jnp.zeros((d, 2 * LANES), F32)
    w_r = w_r.at[:, :MOE_GROUPS].set(w_route_group[l])
    w_r = w_r.at[:, LANES:LANES + N_EXPERTS].set(w_route_expert[l]).astype(BF16)
    li = np.arange(tm_o)
    tri_strict = jnp.asarray((li[:, None] > li[None, :]).astype(np.float32), BF16)
    x1, h2, ridx, rw, cnt = _outproj(merged, w_out[l].astype(BF16), x.reshape(m, d), mods,
                                        norm_ffn[l][None, :], w_r, tri_strict, tm_o, t)

    na = m * TOP_K
    nblk = -(-(na + N_EXPERTS * (MOE_BLOCK - 1)) // MOE_BLOCK)
    counts = cnt[0, :N_EXPERTS].astype(I32)
    padded = (counts + MOE_BLOCK - 1) // MOE_BLOCK * MOE_BLOCK
    pend = jnp.cumsum(padded)
    pstart = pend - padded
    blk_start = jnp.arange(nblk, dtype=I32) * MOE_BLOCK
    blk_e = jnp.clip(jnp.sum((pend[None, :] <= blk_start[:, None]).astype(I32), axis=1),
                     0, N_EXPERTS - 1)
    n_used = (pend[N_EXPERTS - 1:] // MOE_BLOCK).astype(I32)
    meta = jnp.concatenate([pstart + counts, pend, n_used]).astype(I32)
    e_sel = ridx[0:TOP_K]
    seg0 = jnp.zeros_like(e_sel)
    for e in range(N_EXPERTS):
        seg0 = jnp.where(e_sel == e, pstart[e], seg0)
    pos = (seg0 + ridx[TOP_K:2 * TOP_K]).reshape(-1)

    eo = _experts(blk_e, pos, meta, h2, w_gate[l], w_up[l], w_down[l], nblk)

    tm_c = min(256, t)
    out = _combine(pos, eo, x1, rw, mods, norm_final[None, :], tm_c, t)
    return out.reshape(bsz, t, d)
```

```python
import functools

import numpy as np
import jax
import jax.numpy as jnp
from jax import lax
from jax.experimental import pallas as pl
from jax.experimental.pallas import tpu as pltpu

F32 = jnp.float32
BF16 = jnp.bfloat16
I32 = jnp.int32

D_MODEL = 2048
GRID_W = 64
NORM_EPS = 1e-6
N_MOD = 6
SSD_HEADS = 16
SSD_HEAD_DIM = 64
D_SSD = SSD_HEADS * SSD_HEAD_DIM
SSD_GROUPS = 2
SSD_STATE = 128
SSD_CONV = 5
SSD_CHUNK = 128
CONV_CH = D_SSD + 2 * SSD_GROUPS * SSD_STATE
ATTN_HEADS = 8
ATTN_KV_HEADS = 2
ATTN_HEAD_DIM = 128
D_ATTN = ATTN_HEADS * ATTN_HEAD_DIM
D_KV = ATTN_KV_HEADS * ATTN_HEAD_DIM
WINDOW = 128
ROPE_BASE = 10000.0
N_BRANCH = 2
MOE_GROUPS = 4
EXPERTS_PER_GROUP = 8
N_EXPERTS = MOE_GROUPS * EXPERTS_PER_GROUP
TOP_K = 2
D_EXPERT = 512
MOE_BLOCK = 256

LANES = 128
TOK_PACK = D_MODEL // (2 * LANES)
TOK_PITCH = TOK_PACK + 1
VMEM_LIMIT = 56 * 1024 * 1024

U_XS, U_Z, U_Q, U_BC, U_K, U_V, U_G = 0, 1024, 2048, 3072, 3584, 3840, 4096
U_W = 8192
UC_XS, UC_BC, UC_K, UC_V = 0, 1024, 1536, 1792
UC_W = 2048
IN_TN = 1024


def _params(sem):
    return pltpu.CompilerParams(dimension_semantics=sem, vmem_limit_bytes=VMEM_LIMIT)


def _silu(v):
    return v * jax.nn.sigmoid(v)


def _split_bf16(v, n):
    parts = []
    r = v
    for _ in range(n):
        p = r.astype(BF16)
        parts.append(p)
        r = r - p.astype(F32)
    return parts


def _dot(a, b):
    return jnp.dot(a, b, preferred_element_type=F32)


def _dot_nt(a, b):
    return lax.dot_general(a, b, (((1,), (1,)), ((), ())), preferred_element_type=F32)


def _ada_kernel(c_ref, w_ref, b_ref, o_ref):
    a = _silu(c_ref[...]).astype(BF16)
    o_ref[...] = _dot(a, w_ref[...].astype(BF16)) + b_ref[...]


def _ada(cc, w, b):
    rows, d = cc.shape
    n = w.shape[1]
    tn = 1024
    return pl.pallas_call(
        _ada_kernel,
        grid=(n // tn,),
        in_specs=[pl.BlockSpec((rows, d), lambda j: (0, 0)),
                  pl.BlockSpec((d, tn), lambda j: (0, j)),
                  pl.BlockSpec((1, tn), lambda j: (0, j))],
        out_specs=pl.BlockSpec((rows, tn), lambda j: (0, j)),
        out_shape=jax.ShapeDtypeStruct((rows, n), F32),
        compiler_params=_params(("arbitrary",)),
        name="ada",
    )(cc, w, b)


def _rope(a, cos, sin_signed, first):
    partner = jnp.where(first, pltpu.roll(a, 96, 1), pltpu.roll(a, 32, 1))
    return a * cos + partner * sin_signed


def _inproj_kernel(*refs, rope_q_blocks, rope_kv_block, sub):
    if rope_q_blocks:
        x_ref, nw_ref, mod_ref, w_ref, wdt_ref, cos_ref, sin_ref, o_ref, dt_ref, h_scr = refs
    else:
        x_ref, nw_ref, mod_ref, w_ref, wdt_ref, o_ref, dt_ref, h_scr = refs
    j = pl.program_id(1)
    tm = x_ref.shape[0]
    tn = w_ref.shape[1]

    @pl.when(j == 0)
    def _():
        nw = nw_ref[...]
        sh = mod_ref[0, 0:1, :]
        sc = mod_ref[0, 1:2, :]

        def body(r, carry):
            rows = pl.ds(pl.multiple_of(r * sub, sub), sub)
            xf = x_ref[rows, :]
            ms = jnp.mean(xf * xf, axis=-1, keepdims=True)
            h = (xf * lax.rsqrt(ms + NORM_EPS) * nw) * (1.0 + sc) + sh
            hb = h.astype(BF16)
            h_scr[rows, :] = hb
            d = _dot(hb, wdt_ref[...])
            dt_ref[0, rows, :] = d[:, :LANES]
            dt_ref[1, rows, :] = d[:, LANES:]
            return carry

        lax.fori_loop(0, tm // sub, body, 0)

    n_split = 2 if tm >= 512 else 1
    part = tm // n_split

    def store(lo, hi):
        for sp in range(n_split):
            rows = slice(sp * part, (sp + 1) * part)
            acc = _dot(h_scr[rows, :], w_ref[...])
            if hi > lo:
                cos = cos_ref[rows, :]
                sin = sin_ref[rows, :]
                lane = lax.broadcasted_iota(I32, (part, LANES), 1)
                first = (lane % 64) < 32
            for hh in range(tn // LANES):
                a = acc[:, hh * LANES:(hh + 1) * LANES]
                if lo <= hh < hi:
                    a = _rope(a, cos, sin, first)
                o_ref[rows, hh * LANES:(hh + 1) * LANES] = a.astype(BF16)

    if rope_q_blocks:
        is_q = (j >= rope_q_blocks[0]) & (j <= rope_q_blocks[-1])
        is_kv = j == rope_kv_block
        k_lo = (U_K % tn) // LANES

        @pl.when(is_q)
        def _():
            store(0, tn // LANES)

        @pl.when(is_kv)
        def _():
            store(k_lo, k_lo + D_KV // LANES)

        @pl.when(jnp.logical_not(is_q | is_kv))
        def _():
            store(0, 0)
    else:
        store(0, 0)


def _inproj(x2, nw, mods, w, wdt, cos_t, sin_t, *, rows_per_mod, mod_row0, tm, rope):
    m, d = x2.shape
    n = w.shape[1]
    tn = IN_TN
    nbm = rows_per_mod // tm if rows_per_mod else 0
    if rows_per_mod:
        mod_map = lambda i, j: (mod_row0 + i // nbm, 0, 0)
    else:
        mod_map = lambda i, j: (mod_row0, 0, 0)
    in_specs = [pl.BlockSpec((tm, d), lambda i, j: (i, 0)),
                pl.BlockSpec((1, d), lambda i, j: (0, 0)),
                pl.BlockSpec((1, N_MOD, d), mod_map),
                pl.BlockSpec((d, tn), lambda i, j: (0, j)),
                pl.BlockSpec((d, 2 * LANES), lambda i, j: (0, 0))]
    args = [x2, nw, mods, w, wdt]
    if rope:
        in_specs += [pl.BlockSpec((tm, LANES), lambda i, j: (i % nbm, 0)),
                     pl.BlockSpec((tm, LANES), lambda i, j: (i % nbm, 0))]
        args += [cos_t, sin_t]
        rq = tuple(range(U_Q // tn, (U_Q + D_ATTN) // tn))
        rkv = U_K // tn
    else:
        rq, rkv = (), None
    kern = functools.partial(_inproj_kernel, rope_q_blocks=rq, rope_kv_block=rkv, sub=128)
    return pl.pallas_call(
        kern,
        grid=(m // tm, n // tn),
        in_specs=in_specs,
        out_specs=[pl.BlockSpec((tm, tn), lambda i, j: (i, j)),
                   pl.BlockSpec((2, tm, LANES), lambda i, j: (0, i, 0))],
        out_shape=[jax.ShapeDtypeStruct((m, n), BF16),
                   jax.ShapeDtypeStruct((2, m, LANES), F32)],
        scratch_shapes=[pltpu.VMEM((tm, d), BF16)],
        compiler_params=_params(("arbitrary", "arbitrary")),
        name="inproj_rope" if rope else "inproj_ctx",
    )(*args)


CONV_HALO = 16


def _conv_kernel(xs_ref, bc_ref, pxs_ref, pbc_ref, nxs_ref, nbc_ref, w_ref, b_ref,
                 oxs_ref, obc_ref, ext, *, blocks_per_seq):
    i = pl.program_id(0)
    r = xs_ref.shape[0]
    pos = i % blocks_per_seq
    not_first = pos != 0
    not_last = pos != blocks_per_seq - 1
    h = CONV_HALO
    ext[0:h, 0:D_SSD] = jnp.where(not_first, pxs_ref[...].astype(F32), 0.0)
    ext[0:h, D_SSD:] = jnp.where(not_first, pbc_ref[...].astype(F32), 0.0)
    ext[h:h + r, 0:D_SSD] = xs_ref[...].astype(F32)
    ext[h:h + r, D_SSD:] = bc_ref[...].astype(F32)
    ext[h + r:, 0:D_SSD] = jnp.where(not_last, nxs_ref[...].astype(F32), 0.0)
    ext[h + r:, D_SSD:] = jnp.where(not_last, nbc_ref[...].astype(F32), 0.0)
    cw = 256
    pad = SSD_CONV // 2
    for cb in range(0, CONV_CH, cw):
        acc = jnp.broadcast_to(b_ref[:, cb:cb + cw], (r, cw))
        for k in range(SSD_CONV):
            acc = acc + ext[h - pad + k:h - pad + k + r, cb:cb + cw] * w_ref[k:k + 1, cb:cb + cw]
        y = _silu(acc)
        if cb < D_SSD:
            oxs_ref[:, cb:cb + cw] = y
        else:
            obc_ref[:, cb - D_SSD:cb - D_SSD + cw] = y.astype(BF16)


def _conv(u, xs_col, bc_col, conv_w, conv_b, seq_len, r):
    m = u.shape[0]
    h = CONV_HALO
    bps = seq_len // r
    wbc = CONV_CH - D_SSD
    xs_c, bc_c = xs_col // D_SSD, bc_col // wbc
    nh = m // h
    prev = lambda i: jnp.maximum(i * (r // h) - 1, 0)
    nxt = lambda i: jnp.minimum((i + 1) * (r // h), nh - 1)
    return pl.pallas_call(
        functools.partial(_conv_kernel, blocks_per_seq=bps),
        grid=(m // r,),
        in_specs=[pl.BlockSpec((r, D_SSD), lambda i: (i, xs_c)),
                  pl.BlockSpec((r, wbc), lambda i: (i, bc_c)),
                  pl.BlockSpec((h, D_SSD), lambda i: (prev(i), xs_c)),
                  pl.BlockSpec((h, wbc), lambda i: (prev(i), bc_c)),
                  pl.BlockSpec((h, D_SSD), lambda i: (nxt(i), xs_c)),
                  pl.BlockSpec((h, wbc), lambda i: (nxt(i), bc_c)),
                  pl.BlockSpec((SSD_CONV, CONV_CH), lambda i: (0, 0)),
                  pl.BlockSpec((1, CONV_CH), lambda i: (0, 0))],
        out_specs=[pl.BlockSpec((r, D_SSD), lambda i: (i, 0)),
                   pl.BlockSpec((r, wbc), lambda i: (i, 0))],
        out_shape=[jax.ShapeDtypeStruct((m, D_SSD), F32),
                   jax.ShapeDtypeStruct((m, wbc), BF16)],
        scratch_shapes=[pltpu.VMEM((r + 2 * h, CONV_CH), F32)],
        compiler_params=_params(("arbitrary",)),
        name="conv_silu",
    )(u, u, u, u, u, u, conv_w, conv_b)


def _expand_heads(v, e2):
    return _dot(jnp.concatenate(_split_bf16(v, 2), axis=1), e2)


def _cumsum_mm(tri3, da):
    return _dot(tri3, jnp.concatenate(_split_bf16(da, 3), axis=0))


def _ctx_state_kernel(xs_ref, b_ref, dt_ref, alog_ref, bias_ref, tri_ref, e_ref, h_ref):
    d = pl.program_id(1)
    tc = xs_ref.shape[0]
    dt = jax.nn.softplus(dt_ref[0] + bias_ref[0])
    a = -jnp.exp(alog_ref[0])
    cum = _cumsum_mm(tri_ref[0], dt * a)
    cum_end = jnp.where(d == 0, cum[tc - 1:tc, :], cum[0:1, :])
    wx = _expand_heads(dt * jnp.exp(cum_end - cum), e_ref[...])
    xw = (xs_ref[...] * wx).astype(BF16)
    gw = D_SSD // SSD_GROUPS
    for g in range(SSD_GROUPS):
        bgt = b_ref[:, g * SSD_STATE:(g + 1) * SSD_STATE].astype(F32).T.astype(BF16)
        h_ref[0, 0, :, g * gw:(g + 1) * gw] = _dot(bgt, xw[:, g * gw:(g + 1) * gw])


def _ctx_state(xs_c, bc_c, dt2, alog, bias, tri3, e2, bsz, tc):
    return pl.pallas_call(
        _ctx_state_kernel,
        grid=(bsz, 2),
        in_specs=[pl.BlockSpec((tc, D_SSD), lambda b, d: (b, 0)),
                  pl.BlockSpec((tc, SSD_GROUPS * SSD_STATE), lambda b, d: (b, 0)),
                  pl.BlockSpec((1, tc, LANES), lambda b, d: (d, b, 0)),
                  pl.BlockSpec((1, 1, LANES), lambda b, d: (d, 0, 0)),
                  pl.BlockSpec((1, 1, LANES), lambda b, d: (d, 0, 0)),
                  pl.BlockSpec((1, tc, 3 * tc), lambda b, d: (d, 0, 0)),
                  pl.BlockSpec((2 * LANES, D_SSD), lambda b, d: (0, 0))],
        out_specs=pl.BlockSpec((1, 1, SSD_STATE, D_SSD), lambda b, d: (b, d, 0, 0)),
        out_shape=jax.ShapeDtypeStruct((bsz, 2, SSD_STATE, D_SSD), F32),
        compiler_params=_params(("arbitrary", "arbitrary")),
        name="ssd_ctx_state",
    )(xs_c, bc_c, dt2, alog, bias, tri3, e2)


class _SsdChunk:
    GW = D_SSD // SSD_GROUPS
    HPG = SSD_HEADS // SSD_GROUPS

    def __init__(self, fwd, xs, bc_ref, dt_raw, alog, bias, tri3, e2, st, y_out):
        lc = SSD_CHUNK
        self.bc_ref, self.st, self.y_out = bc_ref, st, y_out
        dt = jax.nn.softplus(dt_raw + bias)
        self.cum = _cumsum_mm(tri3, dt * (-jnp.exp(alog)))
        self.cum_t = self.cum.T
        self.dt_t = dt.T
        end = lc - 1 if fwd else 0
        self.ecx = _expand_heads(jnp.exp(self.cum), e2)
        wx = _expand_heads(dt * jnp.exp(self.cum[end:end + 1, :] - self.cum), e2)
        self.xs_b = xs.astype(BF16)
        self.xw_b = (xs * wx).astype(BF16)
        self.cd_row = self.ecx[end:end + 1, :]
        li = lax.broadcasted_iota(I32, (lc, lc), 0)
        si = lax.broadcasted_iota(I32, (lc, lc), 1)
        self.causal = (li >= si) if fwd else (li <= si)
        self.low = lax.broadcasted_iota(I32, (lc, LANES), 1) < SSD_HEAD_DIM

    def group_head(self, g):
        cols = slice(g * self.GW, (g + 1) * self.GW)
        self.bg = self.bc_ref[:, g * SSD_STATE:(g + 1) * SSD_STATE]
        cg = self.bc_ref[:, (SSD_GROUPS + g) * SSD_STATE:(SSD_GROUPS + g + 1) * SSD_STATE]
        self.cb = _dot_nt(cg, self.bg)
        self.yoff = _dot(cg, self.st[:, cols].astype(BF16)) * self.ecx[:, cols]

    def head_pair(self, g, pr):
        h0 = g * self.HPG + 2 * pr
        ms = []
        for h in (h0, h0 + 1):
            seg = self.cum[:, h:h + 1] - self.cum_t[h:h + 1, :]
            ms.append((self.cb * jnp.exp(jnp.where(self.causal, seg, -jnp.inf))
                       * self.dt_t[h:h + 1, :]).astype(BF16))
        xp = self.xs_b[:, h0 * SSD_HEAD_DIM:(h0 + 2) * SSD_HEAD_DIM]
        zero = jnp.zeros_like(xp)
        rhs = jnp.concatenate([jnp.where(self.low, xp, zero), jnp.where(self.low, zero, xp)], axis=0)
        self.y_out[:, h0 * SSD_HEAD_DIM:(h0 + 2) * SSD_HEAD_DIM] = (
            _dot(jnp.concatenate(ms, axis=1), rhs) + self.yoff[:, pr * LANES:(pr + 1) * LANES])

    def group_tail(self, g):
        cols = slice(g * self.GW, (g + 1) * self.GW)
        bgt = self.bg.astype(F32).T.astype(BF16)
        self.st[:, cols] = self.st[:, cols] * self.cd_row[:, cols] + _dot(bgt, self.xw_b[:, cols])


def _ssd_kernel(xsf_ref, xsb_ref, bcf_ref, bcb_ref, dtf_ref, dtb_ref, alog_ref, bias_ref, tri_ref,
                e_ref, h0_ref, z_ref, dsk_ref, nw_ref, o_ref, st, ybuf, ycur, *, nc):
    s = pl.program_id(1)
    lc = SSD_CHUNK

    @pl.when(s == 0)
    def _():
        st[...] = h0_ref[0]

    e2 = e_ref[...]
    scans = (_SsdChunk(True, xsf_ref[...], bcf_ref, dtf_ref[0], alog_ref[0], bias_ref[0], tri_ref[0],
                       e2, st.at[0], ycur.at[0]),
             _SsdChunk(False, xsb_ref[...], bcb_ref, dtb_ref[0], alog_ref[1], bias_ref[1], tri_ref[1],
                       e2, st.at[1], ycur.at[1]))
    for g in range(SSD_GROUPS):
        for sc in scans:
            sc.group_head(g)
        for pr in range(_SsdChunk.HPG // 2):
            for sc in scans:
                sc.head_pair(g, pr)
        for sc in scans:
            sc.group_tail(g)
    rows = (pl.ds(pl.multiple_of(s * lc, lc), lc), pl.ds(pl.multiple_of((nc - 1 - s) * lc, lc), lc))

    @pl.when(s < nc // 2)
    def _():
        for d in range(2):
            ybuf[rows[d], :] = ycur[d]

    @pl.when(s >= nc // 2)
    def _():
        gw = D_SSD // SSD_GROUPS
        for d, xs_ref in enumerate((xsf_ref, xsb_ref)):
            y = ycur[d] + ybuf[rows[d], :] + dsk_ref[...] * xs_ref[...]
            gy = y * _silu(z_ref[rows[d], :].astype(F32))
            for g in range(SSD_GROUPS):
                blk = gy[:, g * gw:(g + 1) * gw]
                ms = jnp.mean(blk * blk, axis=-1, keepdims=True)
                o_ref[rows[d], g * gw:(g + 1) * gw] = (blk * lax.rsqrt(ms + NORM_EPS)
                                                       * nw_ref[:, g * gw:(g + 1) * gw]).astype(BF16)


def _ssd(xs_c, bc_c, dt2, alog, bias, tri3, e2, h0, u, dsk, nw, bsz, t):
    nc = t // SSD_CHUNK
    lc = SSD_CHUNK
    m = bsz * t
    fw = lambda b, s: b * nc + s
    bw = lambda b, s: b * nc + nc - 1 - s
    zc = U_Z // D_SSD
    bcw = 2 * SSD_GROUPS * SSD_STATE
    return pl.pallas_call(
        functools.partial(_ssd_kernel, nc=nc),
        grid=(bsz, nc),
        in_specs=[pl.BlockSpec((lc, D_SSD), lambda b, s: (fw(b, s), 0)),
                  pl.BlockSpec((lc, D_SSD), lambda b, s: (bw(b, s), 0)),
                  pl.BlockSpec((lc, bcw), lambda b, s: (fw(b, s), 0)),
                  pl.BlockSpec((lc, bcw), lambda b, s: (bw(b, s), 0)),
                  pl.BlockSpec((1, lc, LANES), lambda b, s: (0, fw(b, s), 0)),
                  pl.BlockSpec((1, lc, LANES), lambda b, s: (1, bw(b, s), 0)),
                  pl.BlockSpec((2, 1, LANES), lambda b, s: (0, 0, 0)),
                  pl.BlockSpec((2, 1, LANES), lambda b, s: (0, 0, 0)),
                  pl.BlockSpec((2, lc, 3 * lc), lambda b, s: (0, 0, 0)),
                  pl.BlockSpec((2 * LANES, D_SSD), lambda b, s: (0, 0)),
                  pl.BlockSpec((1, 2, SSD_STATE, D_SSD), lambda b, s: (b, 0, 0, 0)),
                  pl.BlockSpec((t, D_SSD), lambda b, s: (b, zc)),
                  pl.BlockSpec((1, D_SSD), lambda b, s: (0, 0)),
                  pl.BlockSpec((1, D_SSD), lambda b, s: (0, 0))],
        out_specs=pl.BlockSpec((t, D_SSD), lambda b, s: (b, 0)),
        out_shape=jax.ShapeDtypeStruct((m, D_SSD), BF16),
        scratch_shapes=[pltpu.VMEM((2, SSD_STATE, D_SSD), F32),
                        pltpu.VMEM((t, D_SSD), F32),
                        pltpu.VMEM((2, lc, D_SSD), F32)],
        compiler_params=_params(("arbitrary", "arbitrary")),
        name="ssd_scan",
    )(xs_c, xs_c, bc_c, bc_c, dt2, dt2, alog, bias, tri3, e2, h0, u, dsk, nw)


def _attn_kernel(sink_ref, q_ref, k_ref, v_ref, kc_ref, vc_ref, o_ref, kp, vp, bias, sbuf, *, t):
    g = pl.program_id(1)
    w = WINDOW
    nb = t // w
    rep = ATTN_HEADS // ATTN_KV_HEADS
    tc = kc_ref.shape[0]
    nk = 3 * w + tc
    zeros = jnp.zeros((w, ATTN_HEAD_DIM), BF16)
    kp[0:w, :] = zeros
    kp[w:w + t, :] = k_ref[...]
    kp[w + t:, :] = zeros
    vp[0:w, :] = zeros
    vp[w:w + t, :] = v_ref[...]
    vp[w + t:, :] = zeros
    qi = lax.broadcasted_iota(I32, (rep * w, w), 0) % w
    kj = lax.broadcasted_iota(I32, (rep * w, w), 1)
    bias[0] = jnp.where(kj >= qi, 0.0, -jnp.inf).astype(F32)
    bias[1] = jnp.where(kj <= qi, 0.0, -jnp.inf).astype(F32)
    sink_col = jnp.concatenate(
        [jnp.full((w, 1), sink_ref[g * rep + h], F32) for h in range(rep)], axis=0)
    scale = ATTN_HEAD_DIM ** -0.5
    log2e = float(np.log2(np.e))

    def scores(n):
        rows = pl.ds(pl.multiple_of(n * w, w), w)
        q = jnp.concatenate([q_ref[rows, h * ATTN_HEAD_DIM:(h + 1) * ATTN_HEAD_DIM]
                             for h in range(rep)], axis=0)
        kall = jnp.concatenate([kp[pl.ds(pl.multiple_of(n * w, w), 3 * w), :], kc_ref[...]], axis=0)
        raw = _dot_nt(q, kall)
        prev = raw[:, 0:w] + (bias[0] + jnp.where(n == 0, -jnp.inf, 0.0))
        nxt = raw[:, 2 * w:3 * w] + (bias[1] + jnp.where(n == nb - 1, -jnp.inf, 0.0))
        return jnp.concatenate([prev, raw[:, w:2 * w], nxt, raw[:, 3 * w:]], axis=1)

    def finish(n, sc):
        rows = pl.ds(pl.multiple_of(n * w, w), w)
        vall = jnp.concatenate([vp[pl.ds(pl.multiple_of(n * w, w), 3 * w), :], vc_ref[...]], axis=0)
        mx = jnp.maximum(jnp.max(sc, axis=-1, keepdims=True) * scale, sink_col)
        p = jnp.exp2(sc * (scale * log2e) - mx * log2e)
        den = jnp.sum(p, axis=-1, keepdims=True) + jnp.exp2((sink_col - mx) * log2e)
        o = _dot(p.astype(BF16), vall) / den
        for h in range(rep):
            o_ref[rows, h * ATTN_HEAD_DIM:(h + 1) * ATTN_HEAD_DIM] = o[h * w:(h + 1) * w, :].astype(BF16)

    sbuf[0] = scores(0)

    def body(j, carry):
        n0 = 2 * j
        sbuf[1] = scores(n0 + 1)
        finish(n0, sbuf[0])
        sbuf[0] = scores(jnp.minimum(n0 + 2, nb - 1))
        finish(n0 + 1, sbuf[1])
        return carry

    lax.fori_loop(0, nb // 2, body, 0)


def _attn(sink, u, uc, bsz, t, tc):
    rep = ATTN_HEADS // ATTN_KV_HEADS
    qw = rep * ATTN_HEAD_DIM
    hd = ATTN_HEAD_DIM
    return pl.pallas_call(
        functools.partial(_attn_kernel, t=t),
        grid=(bsz, ATTN_KV_HEADS),
        in_specs=[pl.BlockSpec(memory_space=pltpu.SMEM),
                  pl.BlockSpec((t, qw), lambda b, g: (b, U_Q // qw + g)),
                  pl.BlockSpec((t, hd), lambda b, g: (b, U_K // hd + g)),
                  pl.BlockSpec((t, hd), lambda b, g: (b, U_V // hd + g)),
                  pl.BlockSpec((tc, hd), lambda b, g: (b, UC_K // hd + g)),
                  pl.BlockSpec((tc, hd), lambda b, g: (b, UC_V // hd + g))],
        out_specs=pl.BlockSpec((t, qw), lambda b, g: (b, g)),
        out_shape=jax.ShapeDtypeStruct((bsz * t, D_ATTN), BF16),
        scratch_shapes=[pltpu.VMEM((t + 2 * WINDOW, hd), BF16),
                        pltpu.VMEM((t + 2 * WINDOW, hd), BF16),
                        pltpu.VMEM((2, rep * WINDOW, WINDOW), F32),
                        pltpu.VMEM((2, rep * WINDOW, 3 * WINDOW + tc), F32)],
        compiler_params=_params(("arbitrary", "arbitrary")),
        name="window_attn",
    )(sink, u, u, u, uc, uc)


def _merge_kernel(ys_ref, ya_ref, g0_ref, g1_ref, ws_ref, wa_ref, o_ref):
    ys = ys_ref[...]
    ya = ya_ref[...]
    cw = 512
    for cb in range(0, D_MODEL, cw):
        t0 = _dot(ys, ws_ref[:, cb:cb + cw])
        t1 = _dot(ya, wa_ref[:, cb:cb + cw])
        g0 = jax.nn.sigmoid(g0_ref[:, cb:cb + cw].astype(F32))
        g1 = jax.nn.sigmoid(g1_ref[:, cb:cb + cw].astype(F32))
        o_ref[:, cb:cb + cw] = (g0 * t0 + g1 * t1).astype(BF16)


def _merge(ys, ya, u, ws, wa, tm):
    m = ys.shape[0]
    gc = U_G // D_MODEL
    return pl.pallas_call(
        _merge_kernel,
        grid=(m // tm,),
        in_specs=[pl.BlockSpec((tm, D_SSD), lambda i: (i, 0)),
                  pl.BlockSpec((tm, D_ATTN), lambda i: (i, 0)),
                  pl.BlockSpec((tm, D_MODEL), lambda i: (i, gc)),
                  pl.BlockSpec((tm, D_MODEL), lambda i: (i, gc + 1)),
                  pl.BlockSpec((D_SSD, D_MODEL), lambda i: (0, 0)),
                  pl.BlockSpec((D_ATTN, D_MODEL), lambda i: (0, 0))],
        out_specs=pl.BlockSpec((tm, D_MODEL), lambda i: (i, 0)),
        out_shape=jax.ShapeDtypeStruct((m, D_MODEL), BF16),
        compiler_params=_params(("arbitrary",)),
        name="branch_merge",
    )(ys, ya, u, u, ws, wa)


def _lane_min_index(mask, lane):
    return jnp.min(jnp.where(mask, lane, LANES), axis=-1, keepdims=True)


def _outproj_kernel(mg_ref, wo_ref, x_ref, mod_ref, nw_ref, wr_ref, tri_ref,
                    x1_ref, h2_ref, ridx_ref, rw_ref, cnt_ref, carry, mixbuf):
    i = pl.program_id(0)
    tm = x_ref.shape[0]

    @pl.when(i == 0)
    def _():
        mixbuf[1] = jnp.zeros(mixbuf.shape[1:], F32)

    @pl.when(i <= 1)
    def _():
        carry[...] = jnp.zeros_like(carry)

    g1 = mod_ref[0, 2:3, :]
    sh2 = mod_ref[0, 3:4, :]
    sc2 = mod_ref[0, 4:5, :]
    cw = 2 * LANES
    dw = D_MODEL // 4

    def step(cur):
        def project(j):
            mixbuf[cur, :, j * dw:(j + 1) * dw] = _dot(mg_ref[...], wo_ref[:, j * dw:(j + 1) * dw])

        ssq = jnp.zeros((tm, 1), F32)
        for c in range(TOK_PACK):
            cols = slice(c * cw, (c + 1) * cw)
            x1c = x_ref[:, cols] + g1[:, cols] * mixbuf[1 - cur, :, cols]
            x1_ref[:, cols] = x1c
            ssq = ssq + jnp.sum(x1c * x1c, axis=-1, keepdims=True)
            if c == 1:
                project(0)
            if c == 5:
                project(1)
        rinv = lax.rsqrt(ssq / D_MODEL + NORM_EPS)
        hbs = []
        for c in range(TOK_PACK):
            cols = slice(c * cw, (c + 1) * cw)
            h2c = (x1_ref[:, cols] * rinv * nw_ref[:, cols]) * (1.0 + sc2[:, cols]) + sh2[:, cols]
            hb = h2c.astype(BF16)
            hbs.append(hb)
            bits = pltpu.bitcast(hb.astype(F32), jnp.uint32)
            h2_ref[pl.ds(c, tm, stride=TOK_PACK), :] = (
                (bits[:, LANES:] & jnp.uint32(0xFFFF0000)) | (bits[:, :LANES] >> 16))
            if c == 3:
                project(2)
        lg = _dot(jnp.concatenate(hbs, axis=1), wr_ref[...])
        project(3)
        _route(lg, tri_ref, ridx_ref, rw_ref, cnt_ref, carry)

    for cur in range(2):
        pl.when(i % 2 == cur)(functools.partial(step, cur))


def _route(lg, tri_ref, ridx_ref, rw_ref, cnt_ref, carry):
    tm = tri_ref.shape[0]
    lane = lax.broadcasted_iota(I32, (tm, LANES), 1)
    gl = jnp.where(lane < MOE_GROUPS, lg[:, :LANES], -jnp.inf)
    gmax = jnp.max(gl, axis=-1, keepdims=True)
    g_w = 1.0 / jnp.sum(jnp.exp(gl - gmax), axis=-1, keepdims=True)
    g_idx = _lane_min_index(gl == gmax, lane)
    el = jnp.where((lane // EXPERTS_PER_GROUP) == g_idx, lg[:, LANES:], -jnp.inf)
    v1 = jnp.max(el, axis=-1, keepdims=True)
    i1 = _lane_min_index(el == v1, lane)
    el2 = jnp.where(lane == i1, -jnp.inf, el)
    v2 = jnp.max(el2, axis=-1, keepdims=True)
    i2 = _lane_min_index(el2 == v2, lane)
    e2 = jnp.exp(v2 - v1)
    w1 = g_w * (1.0 / (1.0 + e2))
    w2 = g_w * (e2 / (1.0 + e2))

    hit1 = lane == i1
    hit2 = lane == i2
    onehot = jnp.where(hit1 | hit2, 1.0, 0.0).astype(F32)
    before = _dot(tri_ref[...], onehot.astype(BF16)) + carry[0:1, :]
    r1 = jnp.sum(jnp.where(hit1, before, 0.0), axis=-1, keepdims=True)
    r2 = jnp.sum(jnp.where(hit2, before, 0.0), axis=-1, keepdims=True)
    total = carry[0:1, :] + jnp.sum(onehot, axis=0, keepdims=True)
    carry[...] = jnp.broadcast_to(total, carry.shape)
    cnt_ref[...] = jnp.broadcast_to(total, cnt_ref.shape)
    packed = jnp.where(lane == 0, i1.astype(F32), jnp.where(lane == 1, i2.astype(F32), jnp.where(
        lane == 2, r1, jnp.where(lane == 3, r2, 0.0))))
    ridx_ref[...] = packed.T[0:8, :].astype(I32)
    rw_ref[...] = jnp.where(lane == 0, w1, jnp.where(lane == 1, w2, 0.0))


def _outproj(mg, wo, x2, mods, nw, wr, tri, tm, t):
    m = x2.shape[0]
    nbm = t // tm
    nsteps = m // tm
    mm = lambda i: jnp.minimum(i, nsteps - 1)
    ep = lambda i: jnp.maximum(i - 1, 0)
    return pl.pallas_call(
        _outproj_kernel,
        grid=(nsteps + 1,),
        in_specs=[pl.BlockSpec((tm, D_MODEL), lambda i: (mm(i), 0)),
                  pl.BlockSpec((D_MODEL, D_MODEL), lambda i: (0, 0), pipeline_mode=pl.Buffered(1)),
                  pl.BlockSpec((tm, D_MODEL), lambda i: (ep(i), 0)),
                  pl.BlockSpec((1, N_MOD, D_MODEL), lambda i: (ep(i) // nbm, 0, 0)),
                  pl.BlockSpec((1, D_MODEL), lambda i: (0, 0)),
                  pl.BlockSpec((D_MODEL, 2 * LANES), lambda i: (0, 0)),
                  pl.BlockSpec((tm, tm), lambda i: (0, 0))],
        out_specs=[pl.BlockSpec((tm, D_MODEL), lambda i: (ep(i), 0)),
                   pl.BlockSpec((tm * TOK_PACK, LANES), lambda i: (ep(i), 0)),
                   pl.BlockSpec((8, tm), lambda i: (0, ep(i))),
                   pl.BlockSpec((tm, LANES), lambda i: (ep(i), 0)),
                   pl.BlockSpec((8, LANES), lambda i: (0, 0))],
        out_shape=[jax.ShapeDtypeStruct((m, D_MODEL), F32),
                   jax.ShapeDtypeStruct((m * TOK_PACK, LANES), jnp.uint32),
                   jax.ShapeDtypeStruct((8, m), I32),
                   jax.ShapeDtypeStruct((m, LANES), F32),
                   jax.ShapeDtypeStruct((8, LANES), F32)],
        scratch_shapes=[pltpu.VMEM((8, LANES), F32),
                        pltpu.VMEM((2, tm, D_MODEL), F32)],
        compiler_params=_params(("arbitrary",)),
        name="outproj_router",
    )(mg, wo, x2, mods, nw, wr, tri)


def _gather_rows(idx_ref, base, n, src, dst, sem, row0=0):
    for r in range(n):
        tok = idx_ref[base + r]
        pltpu.make_async_copy(src.at[pl.ds(tok, 1), :], dst.at[pl.ds(row0 + r, 1), :],
                              sem).start(priority=r % 2)


def _gather_slabs(idx_ref, base, n, src, dst, sem, row0=0):
    for r in range(n):
        tok = idx_ref[base + r]
        pltpu.make_async_copy(src.at[pl.ds(pl.multiple_of(tok * TOK_PACK, TOK_PACK), TOK_PACK), :],
                              dst.at[pl.ds((row0 + r) * TOK_PITCH, TOK_PACK), :],
                              sem).start(priority=r % 2)


def _expert_kernel(blk_e_ref, pos_ref, meta_ref, h2_hbm, wg_ref, wu_ref, wd_ref, o_ref,
                   xb, tok_ref, wgb, wub, wdb, sem):
    i = pl.program_id(0)
    n_used = meta_ref[2 * N_EXPERTS]
    slot = i % 2
    rows = MOE_BLOCK
    m = pos_ref.shape[0] // TOP_K

    def wait_block(sl):
        pltpu.make_async_copy(h2_hbm.at[pl.ds(0, rows * TOK_PACK), :],
                              xb.at[sl, pl.ds(0, rows * TOK_PACK), :], sem.at[sl]).wait()

    @pl.when(i == 0)
    def _():
        def pad_expert(e, carry):
            def pad_slot(s, c):
                tok_ref[s] = 0
                return c
            return lax.fori_loop(meta_ref[e], meta_ref[N_EXPERTS + e], pad_slot, carry)

        lax.fori_loop(0, N_EXPERTS, pad_expert, 0)

        def place(t, carry):
            for k in range(TOP_K):
                tok_ref[pos_ref[k * m + t]] = t
            return carry

        lax.fori_loop(0, m, place, 0, unroll=8)
        _gather_slabs(tok_ref, 0, rows, h2_hbm, xb.at[0], sem.at[0])

    @pl.when(i < n_used)
    def _():
        wait_block(slot)
        nxt = jnp.minimum(i + 1, n_used - 1)
        n_pass = 2
        per = rows // (n_pass * TOK_PACK)

        def read_block(ps):
            xs = []
            for a in range(TOK_PACK):
                pair = xb[slot, pl.ds(a, rows, stride=TOK_PITCH), :]
                xs.append(pltpu.bitcast(pair << 16, F32).astype(BF16))
                xs.append(pltpu.bitcast(pair & jnp.uint32(0xFFFF0000), F32).astype(BF16))
                g = ps * TOK_PACK + a
                _gather_slabs(tok_ref, nxt * rows + g * per, per, h2_hbm, xb.at[1 - slot],
                              sem.at[1 - slot], row0=g * per)
            return jnp.concatenate(xs, axis=1)

        prev_e = blk_e_ref[jnp.maximum(i - 1, 0)]

        @pl.when((i == 0) | (blk_e_ref[i] != prev_e))
        def _():
            wgb[...] = wg_ref[0].astype(BF16)
            wub[...] = wu_ref[0].astype(BF16)
            wdb[...] = wd_ref[0].astype(BF16)

        gate = _dot(read_block(0), wgb[...])
        act = (_silu(gate) * _dot(read_block(1), wub[...])).astype(BF16)
        o_ref[...] = _dot(act, wdb[...])

        @pl.when(i == n_used - 1)
        def _():
            wait_block(1 - slot)

    @pl.when(i >= n_used)
    def _():
        o_ref[...] = jnp.zeros_like(o_ref)


def _experts(blk_e, pos, meta, h2, wg, wu, wd, nblk):
    rows = MOE_BLOCK
    live = lambda i, mt: jnp.minimum(i, mt[2 * N_EXPERTS] - 1)
    grid_spec = pltpu.PrefetchScalarGridSpec(
        num_scalar_prefetch=3,
        grid=(nblk,),
        in_specs=[pl.BlockSpec(memory_space=pl.ANY),
                  pl.BlockSpec((1, D_MODEL, D_EXPERT), lambda i, be, ps, mt: (be[live(i, mt)], 0, 0)),
                  pl.BlockSpec((1, D_MODEL, D_EXPERT), lambda i, be, ps, mt: (be[live(i, mt)], 0, 0)),
                  pl.BlockSpec((1, D_EXPERT, D_MODEL), lambda i, be, ps, mt: (be[live(i, mt)], 0, 0))],
        out_specs=pl.BlockSpec((rows, D_MODEL), lambda i, be, ps, mt: (i, 0)),
        scratch_shapes=[pltpu.VMEM((2, rows * TOK_PITCH, LANES), jnp.uint32),
                        pltpu.SMEM((nblk * rows,), I32),
                        pltpu.VMEM((D_MODEL, D_EXPERT), BF16),
                        pltpu.VMEM((D_MODEL, D_EXPERT), BF16),
                        pltpu.VMEM((D_EXPERT, D_MODEL), BF16),
                        pltpu.SemaphoreType.DMA((2,))],
    )
    return pl.pallas_call(
        _expert_kernel,
        grid_spec=grid_spec,
        out_shape=jax.ShapeDtypeStruct((nblk * rows, D_MODEL), F32),
        compiler_params=_params(("arbitrary",)),
        name="expert_mlp",
    )(blk_e, pos, meta, h2, wg, wu, wd)


def _combine_kernel(pos_ref, eo_hbm, x1_ref, rw_ref, mod_ref, nf_ref, o_ref, gb, sem):
    i = pl.program_id(0)
    nsteps = pl.num_programs(0)
    tm = x1_ref.shape[0]
    slot = i % 2

    def issue(step, sl):
        for k in range(TOP_K):
            _gather_rows(pos_ref, (k * nsteps + step) * tm, tm, eo_hbm, gb.at[sl, k], sem.at[sl])

    @pl.when(i == 0)
    def _():
        issue(0, 0)

    @pl.when(i + 1 < nsteps)
    def _():
        issue(i + 1, 1 - slot)

    for k in range(TOP_K):
        pltpu.make_async_copy(eo_hbm.at[pl.ds(0, tm), :], gb.at[slot, k], sem.at[slot]).wait()
    w1 = rw_ref[:, 0:1]
    w2 = rw_ref[:, 1:2]
    g2 = mod_ref[0, 5:6, :]
    x2 = x1_ref[...] + g2 * (gb[slot, 0] * w1 + gb[slot, 1] * w2)
    ms = jnp.mean(x2 * x2, axis=-1, keepdims=True)
    o_ref[...] = x2 * lax.rsqrt(ms + NORM_EPS) * nf_ref[...]


def _combine(pos, eo, x1, rw, mods, nf, tm, t):
    m = x1.shape[0]
    nbm = t // tm
    grid_spec = pltpu.PrefetchScalarGridSpec(
        num_scalar_prefetch=1,
        grid=(m // tm,),
        in_specs=[pl.BlockSpec(memory_space=pl.ANY),
                  pl.BlockSpec((tm, D_MODEL), lambda i, p: (i, 0)),
                  pl.BlockSpec((tm, LANES), lambda i, p: (i, 0)),
                  pl.BlockSpec((1, N_MOD, D_MODEL), lambda i, p: (i // nbm, 0, 0)),
                  pl.BlockSpec((1, D_MODEL), lambda i, p: (0, 0))],
        out_specs=pl.BlockSpec((tm, D_MODEL), lambda i, p: (i, 0)),
        scratch_shapes=[pltpu.VMEM((2, TOP_K, tm, D_MODEL), F32),
                        pltpu.SemaphoreType.DMA((2,))],
    )
    return pl.pallas_call(
        _combine_kernel,
        grid_spec=grid_spec,
        out_shape=jax.ShapeDtypeStruct((m, D_MODEL), F32),
        compiler_params=_params(("arbitrary",)),
        name="moe_combine_norm",
    )(pos, eo, x1, rw, mods, nf)


def _rope_tables(t):
    half = ATTN_HEAD_DIM // 4
    pos = jnp.arange(t, dtype=I32)
    row = (pos // GRID_W).astype(F32)
    col = (pos % GRID_W).astype(F32)
    freqs = ROPE_BASE ** (-jnp.arange(half, dtype=F32) / half)
    ar = row[:, None] * freqs[None, :]
    ac = col[:, None] * freqs[None, :]
    cos_t = jnp.concatenate([jnp.cos(ar), jnp.cos(ar), jnp.cos(ac), jnp.cos(ac)], axis=1)
    sin_t = jnp.concatenate([-jnp.sin(ar), jnp.sin(ar), -jnp.sin(ac), jnp.sin(ac)], axis=1)
    return cos_t, sin_t


def _tri_pair3(n):
    li = np.arange(n)[:, None]
    si = np.arange(n)[None, :]
    pair = np.stack([li >= si, li <= si]).astype(np.float32)
    return jnp.asarray(np.concatenate([pair] * 3, axis=2), BF16)


def _head_expand_matrix2():
    k = np.arange(LANES)[:, None]
    j = np.arange(D_SSD)[None, :]
    e = (k == j // SSD_HEAD_DIM).astype(np.float32)
    return jnp.asarray(np.concatenate([e, e], axis=0), BF16)


def _pad_heads(v):
    return jnp.pad(v.astype(F32), ((0, 0), (0, LANES - SSD_HEADS)))[:, None, :]


def kernel(x, c, ctx, c_ctx, w_ada, b_ada, norm_mix, norm_ffn, w_in, conv_w, conv_b, a_log, dt_bias,
           d_skip, ssd_norm, attn_sink, w_branch_ssd, w_branch_attn, w_out, w_route_group,
           w_route_expert, w_gate, w_up, w_down, norm_final):
    bsz, t, d = x.shape
    tc = ctx.shape[1]
    m = bsz * t
    l = 0

    ctx_row = bsz
    n_rows = -(-(bsz + 1) // 8) * 8
    cc = jnp.zeros((n_rows, d), F32).at[:bsz].set(c).at[ctx_row].set(c_ctx)
    mods = _ada(cc, w_ada[l], b_ada[l][None, :]).reshape(n_rows, N_MOD, d)

    wi = w_in[l]
    o_xbc = D_SSD
    o_dt = o_xbc + CONV_CH
    o_q = o_dt + 2 * SSD_HEADS
    o_k = o_q + D_ATTN
    o_v = o_k + D_KV
    o_g = o_v + D_KV
    w_xs = wi[:, o_xbc:o_xbc + D_SSD]
    w_bc = wi[:, o_xbc + D_SSD:o_dt]
    w_k = wi[:, o_k:o_v]
    w_v = wi[:, o_v:o_g]
    w_lat = jnp.concatenate([w_xs, wi[:, :D_SSD], wi[:, o_q:o_k], w_bc, w_k, w_v, wi[:, o_g:]],
                            axis=1).astype(BF16)
    w_ctx = jnp.concatenate([w_xs, w_bc, w_k, w_v], axis=1).astype(BF16)
    w_dt = jnp.zeros((d, 2 * LANES), F32)
    w_dt = w_dt.at[:, :SSD_HEADS].set(wi[:, o_dt:o_dt + SSD_HEADS])
    w_dt = w_dt.at[:, LANES:LANES + SSD_HEADS].set(wi[:, o_dt + SSD_HEADS:o_q]).astype(BF16)

    cos_t, sin_t = _rope_tables(t)
    nmix = norm_mix[l][None, :]
    tm_in = min(1024, t)
    u, dt2 = _inproj(x.reshape(m, d), nmix, mods, w_lat, w_dt, cos_t, sin_t,
                     rows_per_mod=t, mod_row0=0, tm=tm_in, rope=True)
    mc = bsz * tc
    uc, dtc2 = _inproj(ctx.reshape(mc, d), nmix, mods, w_ctx, w_dt, None, None,
                       rows_per_mod=0, mod_row0=ctx_row, tm=min(1024, mc), rope=False)

    cw = conv_w[l]
    cb = conv_b[l][None, :]
    xs_c, bc_c = _conv(u, U_XS, U_BC, cw, cb, t, min(256, t))
    xsc_c, bcc_c = _conv(uc, UC_XS, UC_BC, cw, cb, tc, min(256, tc))
    alog = _pad_heads(a_log[l])
    bias = _pad_heads(dt_bias[l])
    e_mat = _head_expand_matrix2()
    h0 = _ctx_state(xsc_c, bcc_c, dtc2, alog, bias, _tri_pair3(tc), e_mat, bsz, tc)
    dsk = jnp.repeat(d_skip[l].astype(F32), SSD_HEAD_DIM)[None, :]
    y_ssd = _ssd(xs_c, bc_c, dt2, alog, bias, _tri_pair3(SSD_CHUNK), e_mat, h0, u, dsk,
                 ssd_norm[l][None, :], bsz, t)

    y_attn = _attn(attn_sink[l].astype(F32), u, uc, bsz, t, tc)

    merged = _merge(y_ssd, y_attn, u, w_branch_ssd[l].astype(BF16), w_branch_attn[l].astype(BF16),
                    min(512, t))

    tm_o = min(512, t)
    w_r = jnp.zeros((d, 2 * LANES), F32)
    w_r = w_r.at[:, :MOE_GROUPS].set(w_route_group[l])
    w_r = w_r.at[:, LANES:LANES + N_EXPERTS].set(w_route_expert[l]).astype(BF16)
    li = np.arange(tm_o)
    tri_strict = jnp.asarray((li[:, None] > li[None, :]).astype(np.float32), BF16)
    x1, h2, ridx, rw, cnt = _outproj(merged, w_out[l].astype(BF16), x.reshape(m, d), mods,
                                        norm_ffn[l][None, :], w_r, tri_strict, tm_o, t)

    na = m * TOP_K
    nblk = -(-(na + N_EXPERTS * (MOE_BLOCK - 1)) // MOE_BLOCK)
    counts = cnt[0, :N_EXPERTS].astype(I32)
    padded = (counts + MOE_BLOCK - 1) // MOE_BLOCK * MOE_BLOCK
    pend = jnp.cumsum(padded)
    pstart = pend - padded
    blk_start = jnp.arange(nblk, dtype=I32) * MOE_BLOCK
    blk_e = jnp.clip(jnp.sum((pend[None, :] <= blk_start[:, None]).astype(I32), axis=1),
                     0, N_EXPERTS - 1)
    n_used = (pend[N_EXPERTS - 1:] // MOE_BLOCK).astype(I32)
    meta = jnp.concatenate([pstart + counts, pend, n_used]).astype(I32)
    e_sel = ridx[0:TOP_K]
    seg0 = jnp.zeros_like(e_sel)
    for e in range(N_EXPERTS):
        seg0 = jnp.where(e_sel == e, pstart[e], seg0)
    pos = (seg0 + ridx[TOP_K:2 * TOP_K]).reshape(-1)

    eo = _experts(blk_e, pos, meta, h2, w_gate[l], w_up[l], w_down[l], nblk)

    tm_c = min(256, t)
    out = _combine(pos, eo, x1, rw, mods, norm_final[None, :], tm_c, t)
    return out.reshape(bsz, t, d)
```

```python
import functools

import numpy as np
import jax
import jax.numpy as jnp
from jax import lax
from jax.experimental import pallas as pl
from jax.experimental.pallas import tpu as pltpu

F32 = jnp.float32
BF16 = jnp.bfloat16
I32 = jnp.int32

D_MODEL = 2048
GRID_W = 64
NORM_EPS = 1e-6
N_MOD = 6
SSD_HEADS = 16
SSD_HEAD_DIM = 64
D_SSD = SSD_HEADS * SSD_HEAD_DIM
SSD_GROUPS = 2
SSD_STATE = 128
SSD_CONV = 5
SSD_CHUNK = 128
CONV_CH = D_SSD + 2 * SSD_GROUPS * SSD_STATE
ATTN_HEADS = 8
ATTN_KV_HEADS = 2
ATTN_HEAD_DIM = 128
D_ATTN = ATTN_HEADS * ATTN_HEAD_DIM
D_KV = ATTN_KV_HEADS * ATTN_HEAD_DIM
WINDOW = 128
ROPE_BASE = 10000.0
N_BRANCH = 2
MOE_GROUPS = 4
EXPERTS_PER_GROUP = 8
N_EXPERTS = MOE_GROUPS * EXPERTS_PER_GROUP
TOP_K = 2
D_EXPERT = 512
MOE_BLOCK = 256

LANES = 128
TOK_PACK = D_MODEL // (2 * LANES)
TOK_PITCH = TOK_PACK + 1
VMEM_LIMIT = 56 * 1024 * 1024

U_XS, U_Z, U_Q, U_BC, U_K, U_V, U_G = 0, 1024, 2048, 3072, 3584, 3840, 4096
U_W = 8192
UC_XS, UC_BC, UC_K, UC_V = 0, 1024, 1536, 1792
UC_W = 2048
IN_TN = 1024


def _params(sem):
    return pltpu.CompilerParams(dimension_semantics=sem, vmem_limit_bytes=VMEM_LIMIT)


def _silu(v):
    return v * jax.nn.sigmoid(v)


def _split_bf16(v, n):
    parts = []
    r = v
    for _ in range(n):
        p = r.astype(BF16)
        parts.append(p)
        r = r - p.astype(F32)
    return parts


def _dot(a, b):
    return jnp.dot(a, b, preferred_element_type=F32)


def _dot_nt(a, b):
    return lax.dot_general(a, b, (((1,), (1,)), ((), ())), preferred_element_type=F32)


def _ada_kernel(c_ref, w_ref, b_ref, o_ref):
    a = _silu(c_ref[...]).astype(BF16)
    o_ref[...] = _dot(a, w_ref[...].astype(BF16)) + b_ref[...]


def _ada(cc, w, b):
    rows, d = cc.shape
    n = w.shape[1]
    tn = 1024
    return pl.pallas_call(
        _ada_kernel,
        grid=(n // tn,),
        in_specs=[pl.BlockSpec((rows, d), lambda j: (0, 0)),
                  pl.BlockSpec((d, tn), lambda j: (0, j)),
                  pl.BlockSpec((1, tn), lambda j: (0, j))],
        out_specs=pl.BlockSpec((rows, tn), lambda j: (0, j)),
        out_shape=jax.ShapeDtypeStruct((rows, n), F32),
        compiler_params=_params(("arbitrary",)),
        name="ada",
    )(cc, w, b)


def _rope(a, cos, sin_signed, first):
    partner = jnp.where(first, pltpu.roll(a, 96, 1), pltpu.roll(a, 32, 1))
    return a * cos + partner * sin_signed


def _inproj_kernel(*refs, rope_heads, sub):
    if rope_heads:
        x_ref, nw_ref, mod_ref, w_ref, wdt_ref, cos_ref, sin_ref, o_ref, dt_ref, h_scr = refs
    else:
        x_ref, nw_ref, mod_ref, w_ref, wdt_ref, o_ref, dt_ref, h_scr = refs
    j = pl.program_id(1)
    tm = x_ref.shape[0]
    tn = w_ref.shape[1]

    @pl.when(j == 0)
    def _():
        nw = nw_ref[...]
        sh = mod_ref[0, 0:1, :]
        sc = mod_ref[0, 1:2, :]

        def body(r, carry):
            rows = pl.ds(pl.multiple_of(r * sub, sub), sub)
            xf = x_ref[rows, :]
            ms = jnp.mean(xf * xf, axis=-1, keepdims=True)
            h = (xf * lax.rsqrt(ms + NORM_EPS) * nw) * (1.0 + sc) + sh
            hb = h.astype(BF16)
            h_scr[rows, :] = hb
            d = _dot(hb, wdt_ref[...])
            dt_ref[0, rows, :] = d[:, :LANES]
            dt_ref[1, rows, :] = d[:, LANES:]
            return carry

        lax.fori_loop(0, tm // sub, body, 0)

    n_split = 2 if tm >= 512 else 1
    part = tm // n_split

    def store(heads):
        for sp in range(n_split):
            rows = slice(sp * part, (sp + 1) * part)
            acc = _dot(h_scr[rows, :], w_ref[...])
            if heads:
                cos = cos_ref[rows, :]
                sin = sin_ref[rows, :]
                lane = lax.broadcasted_iota(I32, (part, LANES), 1)
                first = (lane % 64) < 32
            for hh in range(tn // LANES):
                a = acc[:, hh * LANES:(hh + 1) * LANES]
                if hh in heads:
                    a = _rope(a, cos, sin, first)
                o_ref[rows, hh * LANES:(hh + 1) * LANES] = a.astype(BF16)

    plain = None
    for blk, heads in rope_heads:
        pl.when(j == blk)(functools.partial(store, heads))
        plain = (j != blk) if plain is None else plain & (j != blk)
    if plain is None:
        store(())
    else:
        pl.when(plain)(functools.partial(store, ()))


def _inproj(x2, nw, mods, w, wdt, cos_t, sin_t, *, rows_per_mod, mod_row0, tm, rope):
    m, d = x2.shape
    n = w.shape[1]
    tn = IN_TN
    nbm = rows_per_mod // tm if rows_per_mod else 0
    if rows_per_mod:
        mod_map = lambda i, j: (mod_row0 + i // nbm, 0, 0)
    else:
        mod_map = lambda i, j: (mod_row0, 0, 0)
    in_specs = [pl.BlockSpec((tm, d), lambda i, j: (i, 0)),
                pl.BlockSpec((1, d), lambda i, j: (0, 0)),
                pl.BlockSpec((1, N_MOD, d), mod_map),
                pl.BlockSpec((d, tn), lambda i, j: (0, j)),
                pl.BlockSpec((d, 2 * LANES), lambda i, j: (0, 0))]
    args = [x2, nw, mods, w, wdt]
    if rope:
        in_specs += [pl.BlockSpec((tm, LANES), lambda i, j: (i % nbm, 0)),
                     pl.BlockSpec((tm, LANES), lambda i, j: (i % nbm, 0))]
        args += [cos_t, sin_t]
        rotary = lambda col: U_Q <= col < U_Q + D_ATTN or U_K <= col < U_K + D_KV
        rope_heads = tuple(
            (blk, tuple(hh for hh in range(tn // LANES) if rotary(blk * tn + hh * LANES)))
            for blk in range(n // tn)
            if any(rotary(blk * tn + hh * LANES) for hh in range(tn // LANES)))
    else:
        rope_heads = ()
    kern = functools.partial(_inproj_kernel, rope_heads=rope_heads, sub=128)
    return pl.pallas_call(
        kern,
        grid=(m // tm, n // tn),
        in_specs=in_specs,
        out_specs=[pl.BlockSpec((tm, tn), lambda i, j: (i, j)),
                   pl.BlockSpec((2, tm, LANES), lambda i, j: (0, i, 0))],
        out_shape=[jax.ShapeDtypeStruct((m, n), BF16),
                   jax.ShapeDtypeStruct((2, m, LANES), F32)],
        scratch_shapes=[pltpu.VMEM((tm, d), BF16)],
        compiler_params=_params(("arbitrary", "arbitrary")),
        name="inproj_rope" if rope else "inproj_ctx",
    )(*args)


CONV_HALO = 16


def _conv_kernel(xs_ref, bc_ref, pxs_ref, pbc_ref, nxs_ref, nbc_ref, w_ref, b_ref,
                 oxs_ref, obc_ref, ext, *, blocks_per_seq):
    i = pl.program_id(0)
    r = xs_ref.shape[0]
    pos = i % blocks_per_seq
    not_first = pos != 0
    not_last = pos != blocks_per_seq - 1
    h = CONV_HALO
    ext[0:h, 0:D_SSD] = jnp.where(not_first, pxs_ref[...].astype(F32), 0.0)
    ext[0:h, D_SSD:] = jnp.where(not_first, pbc_ref[...].astype(F32), 0.0)
    ext[h:h + r, 0:D_SSD] = xs_ref[...].astype(F32)
    ext[h:h + r, D_SSD:] = bc_ref[...].astype(F32)
    ext[h + r:, 0:D_SSD] = jnp.where(not_last, nxs_ref[...].astype(F32), 0.0)
    ext[h + r:, D_SSD:] = jnp.where(not_last, nbc_ref[...].astype(F32), 0.0)
    cw = 256
    pad = SSD_CONV // 2
    for cb in range(0, CONV_CH, cw):
        acc = jnp.broadcast_to(b_ref[:, cb:cb + cw], (r, cw))
        for k in range(SSD_CONV):
            acc = acc + ext[h - pad + k:h - pad + k + r, cb:cb + cw] * w_ref[k:k + 1, cb:cb + cw]
        y = _silu(acc)
        if cb < D_SSD:
            oxs_ref[:, cb:cb + cw] = y
        else:
            obc_ref[:, cb - D_SSD:cb - D_SSD + cw] = y.astype(BF16)


def _conv(u, xs_col, bc_col, conv_w, conv_b, seq_len, r):
    m = u.shape[0]
    h = CONV_HALO
    bps = seq_len // r
    wbc = CONV_CH - D_SSD
    xs_c, bc_c = xs_col // D_SSD, bc_col // wbc
    nh = m // h
    prev = lambda i: jnp.maximum(i * (r // h) - 1, 0)
    nxt = lambda i: jnp.minimum((i + 1) * (r // h), nh - 1)
    return pl.pallas_call(
        functools.partial(_conv_kernel, blocks_per_seq=bps),
        grid=(m // r,),
        in_specs=[pl.BlockSpec((r, D_SSD), lambda i: (i, xs_c)),
                  pl.BlockSpec((r, wbc), lambda i: (i, bc_c)),
                  pl.BlockSpec((h, D_SSD), lambda i: (prev(i), xs_c)),
                  pl.BlockSpec((h, wbc), lambda i: (prev(i), bc_c)),
                  pl.BlockSpec((h, D_SSD), lambda i: (nxt(i), xs_c)),
                  pl.BlockSpec((h, wbc), lambda i: (nxt(i), bc_c)),
                  pl.BlockSpec((SSD_CONV, CONV_CH), lambda i: (0, 0)),
                  pl.BlockSpec((1, CONV_CH), lambda i: (0, 0))],
        out_specs=[pl.BlockSpec((r, D_SSD), lambda i: (i, 0)),
                   pl.BlockSpec((r, wbc), lambda i: (i, 0))],
        out_shape=[jax.ShapeDtypeStruct((m, D_SSD), F32),
                   jax.ShapeDtypeStruct((m, wbc), BF16)],
        scratch_shapes=[pltpu.VMEM((r + 2 * h, CONV_CH), F32)],
        compiler_params=_params(("arbitrary",)),
        name="conv_silu",
    )(u, u, u, u, u, u, conv_w, conv_b)


def _expand_heads(v, e2):
    return _dot(jnp.concatenate(_split_bf16(v, 2), axis=1), e2)


def _cumsum_mm(tri3, da):
    return _dot(tri3, jnp.concatenate(_split_bf16(da, 3), axis=0))


def _ctx_state_kernel(xs_ref, b_ref, dt_ref, alog_ref, bias_ref, tri_ref, e_ref, h_ref):
    d = pl.program_id(1)
    tc = xs_ref.shape[0]
    dt = jax.nn.softplus(dt_ref[0] + bias_ref[0])
    a = -jnp.exp(alog_ref[0])
    cum = _cumsum_mm(tri_ref[0], dt * a)
    cum_end = jnp.where(d == 0, cum[tc - 1:tc, :], cum[0:1, :])
    wx = _expand_heads(dt * jnp.exp(cum_end - cum), e_ref[...])
    xw = (xs_ref[...] * wx).astype(BF16)
    gw = D_SSD // SSD_GROUPS
    for g in range(SSD_GROUPS):
        bgt = b_ref[:, g * SSD_STATE:(g + 1) * SSD_STATE].astype(F32).T.astype(BF16)
        h_ref[0, 0, :, g * gw:(g + 1) * gw] = _dot(bgt, xw[:, g * gw:(g + 1) * gw])


def _ctx_state(xs_c, bc_c, dt2, alog, bias, tri3, e2, bsz, tc):
    return pl.pallas_call(
        _ctx_state_kernel,
        grid=(bsz, 2),
        in_specs=[pl.BlockSpec((tc, D_SSD), lambda b, d: (b, 0)),
                  pl.BlockSpec((tc, SSD_GROUPS * SSD_STATE), lambda b, d: (b, 0)),
                  pl.BlockSpec((1, tc, LANES), lambda b, d: (d, b, 0)),
                  pl.BlockSpec((1, 1, LANES), lambda b, d: (d, 0, 0)),
                  pl.BlockSpec((1, 1, LANES), lambda b, d: (d, 0, 0)),
                  pl.BlockSpec((1, tc, 3 * tc), lambda b, d: (d, 0, 0)),
                  pl.BlockSpec((2 * LANES, D_SSD), lambda b, d: (0, 0))],
        out_specs=pl.BlockSpec((1, 1, SSD_STATE, D_SSD), lambda b, d: (b, d, 0, 0)),
        out_shape=jax.ShapeDtypeStruct((bsz, 2, SSD_STATE, D_SSD), F32),
        compiler_params=_params(("arbitrary", "arbitrary")),
        name="ssd_ctx_state",
    )(xs_c, bc_c, dt2, alog, bias, tri3, e2)


class _SsdChunk:
    GW = D_SSD // SSD_GROUPS
    HPG = SSD_HEADS // SSD_GROUPS

    def __init__(self, fwd, xs, bc_ref, dt_raw, alog, bias, tri3, e2, st, y_out):
        lc = SSD_CHUNK
        self.bc_ref, self.st, self.y_out = bc_ref, st, y_out
        dt = jax.nn.softplus(dt_raw + bias)
        self.cum = _cumsum_mm(tri3, dt * (-jnp.exp(alog)))
        self.cum_t = self.cum.T
        self.dt_t = dt.T
        end = lc - 1 if fwd else 0
        self.ecx = _expand_heads(jnp.exp(self.cum), e2)
        wx = _expand_heads(dt * jnp.exp(self.cum[end:end + 1, :] - self.cum), e2)
        self.xs_b = xs.astype(BF16)
        self.xw_b = (xs * wx).astype(BF16)
        self.cd_row = self.ecx[end:end + 1, :]
        li = lax.broadcasted_iota(I32, (lc, lc), 0)
        si = lax.broadcasted_iota(I32, (lc, lc), 1)
        self.causal = (li >= si) if fwd else (li <= si)
        self.low = lax.broadcasted_iota(I32, (lc, LANES), 1) < SSD_HEAD_DIM

    def group_head(self, g):
        cols = slice(g * self.GW, (g + 1) * self.GW)
        self.bg = self.bc_ref[:, g * SSD_STATE:(g + 1) * SSD_STATE]
        cg = self.bc_ref[:, (SSD_GROUPS + g) * SSD_STATE:(SSD_GROUPS + g + 1) * SSD_STATE]
        self.cb = _dot_nt(cg, self.bg)
        self.yoff = _dot(cg, self.st[:, cols].astype(BF16)) * self.ecx[:, cols]

    def head_pair(self, g, pr):
        h0 = g * self.HPG + 2 * pr
        ms = []
        for h in (h0, h0 + 1):
            seg = self.cum[:, h:h + 1] - self.cum_t[h:h + 1, :]
            ms.append((self.cb * jnp.exp(jnp.where(self.causal, seg, -jnp.inf))
                       * self.dt_t[h:h + 1, :]).astype(BF16))
        xp = self.xs_b[:, h0 * SSD_HEAD_DIM:(h0 + 2) * SSD_HEAD_DIM]
        zero = jnp.zeros_like(xp)
        rhs = jnp.concatenate([jnp.where(self.low, xp, zero), jnp.where(self.low, zero, xp)], axis=0)
        self.y_out[:, h0 * SSD_HEAD_DIM:(h0 + 2) * SSD_HEAD_DIM] = (
            _dot(jnp.concatenate(ms, axis=1), rhs) + self.yoff[:, pr * LANES:(pr + 1) * LANES])

    def group_tail(self, g):
        cols = slice(g * self.GW, (g + 1) * self.GW)
        bgt = self.bg.astype(F32).T.astype(BF16)
        self.st[:, cols] = self.st[:, cols] * self.cd_row[:, cols] + _dot(bgt, self.xw_b[:, cols])


def _ssd_kernel(xsf_ref, xsb_ref, bcf_ref, bcb_ref, dtf_ref, dtb_ref, alog_ref, bias_ref, tri_ref,
                e_ref, h0_ref, z_ref, dsk_ref, nw_ref, o_ref, st, ybuf, ycur, *, nc):
    s = pl.program_id(1)
    lc = SSD_CHUNK

    @pl.when(s == 0)
    def _():
        st[...] = h0_ref[0]

    e2 = e_ref[...]
    scans = (_SsdChunk(True, xsf_ref[...], bcf_ref, dtf_ref[0], alog_ref[0], bias_ref[0], tri_ref[0],
                       e2, st.at[0], ycur.at[0]),
             _SsdChunk(False, xsb_ref[...], bcb_ref, dtb_ref[0], alog_ref[1], bias_ref[1], tri_ref[1],
                       e2, st.at[1], ycur.at[1]))
    for g in range(SSD_GROUPS):
        for sc in scans:
            sc.group_head(g)
        for pr in range(_SsdChunk.HPG // 2):
            for sc in scans:
                sc.head_pair(g, pr)
        for sc in scans:
            sc.group_tail(g)
    rows = (pl.ds(pl.multiple_of(s * lc, lc), lc), pl.ds(pl.multiple_of((nc - 1 - s) * lc, lc), lc))

    @pl.when(s < nc // 2)
    def _():
        for d in range(2):
            ybuf[rows[d], :] = ycur[d]

    @pl.when(s >= nc // 2)
    def _():
        gw = D_SSD // SSD_GROUPS
        for d, xs_ref in enumerate((xsf_ref, xsb_ref)):
            y = ycur[d] + ybuf[rows[d], :] + dsk_ref[...] * xs_ref[...]
            gy = y * _silu(z_ref[rows[d], :].astype(F32))
            for g in range(SSD_GROUPS):
                blk = gy[:, g * gw:(g + 1) * gw]
                ms = jnp.mean(blk * blk, axis=-1, keepdims=True)
                o_ref[rows[d], g * gw:(g + 1) * gw] = (blk * lax.rsqrt(ms + NORM_EPS)
                                                       * nw_ref[:, g * gw:(g + 1) * gw]).astype(BF16)


def _ssd(xs_c, bc_c, dt2, alog, bias, tri3, e2, h0, u, dsk, nw, bsz, t):
    nc = t // SSD_CHUNK
    lc = SSD_CHUNK
    m = bsz * t
    fw = lambda b, s: b * nc + s
    bw = lambda b, s: b * nc + nc - 1 - s
    zc = U_Z // D_SSD
    bcw = 2 * SSD_GROUPS * SSD_STATE
    return pl.pallas_call(
        functools.partial(_ssd_kernel, nc=nc),
        grid=(bsz, nc),
        in_specs=[pl.BlockSpec((lc, D_SSD), lambda b, s: (fw(b, s), 0)),
                  pl.BlockSpec((lc, D_SSD), lambda b, s: (bw(b, s), 0)),
                  pl.BlockSpec((lc, bcw), lambda b, s: (fw(b, s), 0)),
                  pl.BlockSpec((lc, bcw), lambda b, s: (bw(b, s), 0)),
                  pl.BlockSpec((1, lc, LANES), lambda b, s: (0, fw(b, s), 0)),
                  pl.BlockSpec((1, lc, LANES), lambda b, s: (1, bw(b, s), 0)),
                  pl.BlockSpec((2, 1, LANES), lambda b, s: (0, 0, 0)),
                  pl.BlockSpec((2, 1, LANES), lambda b, s: (0, 0, 0)),
                  pl.BlockSpec((2, lc, 3 * lc), lambda b, s: (0, 0, 0)),
                  pl.BlockSpec((2 * LANES, D_SSD), lambda b, s: (0, 0)),
                  pl.BlockSpec((1, 2, SSD_STATE, D_SSD), lambda b, s: (b, 0, 0, 0)),
                  pl.BlockSpec((t, D_SSD), lambda b, s: (b, zc)),
                  pl.BlockSpec((1, D_SSD), lambda b, s: (0, 0)),
                  pl.BlockSpec((1, D_SSD), lambda b, s: (0, 0))],
        out_specs=pl.BlockSpec((t, D_SSD), lambda b, s: (b, 0)),
        out_shape=jax.ShapeDtypeStruct((m, D_SSD), BF16),
        scratch_shapes=[pltpu.VMEM((2, SSD_STATE, D_SSD), F32),
                        pltpu.VMEM((t, D_SSD), F32),
                        pltpu.VMEM((2, lc, D_SSD), F32)],
        compiler_params=_params(("arbitrary", "arbitrary")),
        name="ssd_scan",
    )(xs_c, xs_c, bc_c, bc_c, dt2, dt2, alog, bias, tri3, e2, h0, u, dsk, nw)


def _attn_kernel(sink_ref, q_ref, k_ref, v_ref, kc_ref, vc_ref, o_ref, kp, vp, bias, sbuf, *, t):
    g = pl.program_id(1)
    w = WINDOW
    nb = t // w
    rep = ATTN_HEADS // ATTN_KV_HEADS
    tc = kc_ref.shape[0]
    nk = 3 * w + tc
    zeros = jnp.zeros((w, ATTN_HEAD_DIM), BF16)
    kp[0:w, :] = zeros
    kp[w:w + t, :] = k_ref[...]
    kp[w + t:, :] = zeros
    vp[0:w, :] = zeros
    vp[w:w + t, :] = v_ref[...]
    vp[w + t:, :] = zeros
    qi = lax.broadcasted_iota(I32, (rep * w, w), 0) % w
    kj = lax.broadcasted_iota(I32, (rep * w, w), 1)
    bias[0] = jnp.where(kj >= qi, 0.0, -jnp.inf).astype(F32)
    bias[1] = jnp.where(kj <= qi, 0.0, -jnp.inf).astype(F32)
    sink_col = jnp.concatenate(
        [jnp.full((w, 1), sink_ref[g * rep + h], F32) for h in range(rep)], axis=0)
    scale = ATTN_HEAD_DIM ** -0.5
    log2e = float(np.log2(np.e))

    def scores(n):
        rows = pl.ds(pl.multiple_of(n * w, w), w)
        q = jnp.concatenate([q_ref[rows, h * ATTN_HEAD_DIM:(h + 1) * ATTN_HEAD_DIM]
                             for h in range(rep)], axis=0)
        kall = jnp.concatenate([kp[pl.ds(pl.multiple_of(n * w, w), 3 * w), :], kc_ref[...]], axis=0)
        raw = _dot_nt(q, kall)
        prev = raw[:, 0:w] + (bias[0] + jnp.where(n == 0, -jnp.inf, 0.0))
        nxt = raw[:, 2 * w:3 * w] + (bias[1] + jnp.where(n == nb - 1, -jnp.inf, 0.0))
        return jnp.concatenate([prev, raw[:, w:2 * w], nxt, raw[:, 3 * w:]], axis=1)

    def finish(n, sc):
        rows = pl.ds(pl.multiple_of(n * w, w), w)
        vall = jnp.concatenate([vp[pl.ds(pl.multiple_of(n * w, w), 3 * w), :], vc_ref[...]], axis=0)
        mx = jnp.maximum(jnp.max(sc, axis=-1, keepdims=True) * scale, sink_col)
        p = jnp.exp2(sc * (scale * log2e) - mx * log2e)
        den = jnp.sum(p, axis=-1, keepdims=True) + jnp.exp2((sink_col - mx) * log2e)
        o = _dot(p.astype(BF16), vall) / den
        for h in range(rep):
            o_ref[rows, h * ATTN_HEAD_DIM:(h + 1) * ATTN_HEAD_DIM] = o[h * w:(h + 1) * w, :].astype(BF16)

    sbuf[0] = scores(0)

    def body(j, carry):
        n0 = 2 * j
        sbuf[1] = scores(n0 + 1)
        finish(n0, sbuf[0])
        sbuf[0] = scores(jnp.minimum(n0 + 2, nb - 1))
        finish(n0 + 1, sbuf[1])
        return carry

    lax.fori_loop(0, nb // 2, body, 0)


def _attn(sink, u, uc, bsz, t, tc):
    rep = ATTN_HEADS // ATTN_KV_HEADS
    qw = rep * ATTN_HEAD_DIM
    hd = ATTN_HEAD_DIM
    return pl.pallas_call(
        functools.partial(_attn_kernel, t=t),
        grid=(bsz, ATTN_KV_HEADS),
        in_specs=[pl.BlockSpec(memory_space=pltpu.SMEM),
                  pl.BlockSpec((t, qw), lambda b, g: (b, U_Q // qw + g)),
                  pl.BlockSpec((t, hd), lambda b, g: (b, U_K // hd + g)),
                  pl.BlockSpec((t, hd), lambda b, g: (b, U_V // hd + g)),
                  pl.BlockSpec((tc, hd), lambda b, g: (b, UC_K // hd + g)),
                  pl.BlockSpec((tc, hd), lambda b, g: (b, UC_V // hd + g))],
        out_specs=pl.BlockSpec((t, qw), lambda b, g: (b, g)),
        out_shape=jax.ShapeDtypeStruct((bsz * t, D_ATTN), BF16),
        scratch_shapes=[pltpu.VMEM((t + 2 * WINDOW, hd), BF16),
                        pltpu.VMEM((t + 2 * WINDOW, hd), BF16),
                        pltpu.VMEM((2, rep * WINDOW, WINDOW), F32),
                        pltpu.VMEM((2, rep * WINDOW, 3 * WINDOW + tc), F32)],
        compiler_params=_params(("arbitrary", "arbitrary")),
        name="window_attn",
    )(sink, u, u, u, uc, uc)


def _merge_kernel(ys_ref, ya_ref, g0_ref, g1_ref, ws_ref, wa_ref, o_ref):
    ys = ys_ref[...]
    ya = ya_ref[...]
    cw = 512
    for cb in range(0, D_MODEL, cw):
        t0 = _dot(ys, ws_ref[:, cb:cb + cw])
        t1 = _dot(ya, wa_ref[:, cb:cb + cw])
        g0 = jax.nn.sigmoid(g0_ref[:, cb:cb + cw].astype(F32))
        g1 = jax.nn.sigmoid(g1_ref[:, cb:cb + cw].astype(F32))
        o_ref[:, cb:cb + cw] = (g0 * t0 + g1 * t1).astype(BF16)


def _merge(ys, ya, u, ws, wa, tm):
    m = ys.shape[0]
    gc = U_G // D_MODEL
    return pl.pallas_call(
        _merge_kernel,
        grid=(m // tm,),
        in_specs=[pl.BlockSpec((tm, D_SSD), lambda i: (i, 0)),
                  pl.BlockSpec((tm, D_ATTN), lambda i: (i, 0)),
                  pl.BlockSpec((tm, D_MODEL), lambda i: (i, gc)),
                  pl.BlockSpec((tm, D_MODEL), lambda i: (i, gc + 1)),
                  pl.BlockSpec((D_SSD, D_MODEL), lambda i: (0, 0)),
                  pl.BlockSpec((D_ATTN, D_MODEL), lambda i: (0, 0))],
        out_specs=pl.BlockSpec((tm, D_MODEL), lambda i: (i, 0)),
        out_shape=jax.ShapeDtypeStruct((m, D_MODEL), BF16),
        compiler_params=_params(("arbitrary",)),
        name="branch_merge",
    )(ys, ya, u, u, ws, wa)


def _lane_min_index(mask, lane):
    return jnp.min(jnp.where(mask, lane, LANES), axis=-1, keepdims=True)


def _outproj_kernel(mg_ref, wo_ref, x_ref, mod_ref, nw_ref, wr_ref, tri_ref,
                    x1_ref, h2_ref, ridx_ref, rw_ref, cnt_ref, carry, mixbuf):
    i = pl.program_id(0)
    tm = x_ref.shape[0]

    @pl.when(i == 0)
    def _():
        mixbuf[1] = jnp.zeros(mixbuf.shape[1:], F32)

    @pl.when(i <= 1)
    def _():
        carry[...] = jnp.zeros_like(carry)

    g1 = mod_ref[0, 2:3, :]
    sh2 = mod_ref[0, 3:4, :]
    sc2 = mod_ref[0, 4:5, :]
    cw = 2 * LANES
    dw = D_MODEL // 4

    def step(cur):
        def project(j):
            mixbuf[cur, :, j * dw:(j + 1) * dw] = _dot(mg_ref[...], wo_ref[:, j * dw:(j + 1) * dw])

        ssq = jnp.zeros((tm, 1), F32)
        for c in range(TOK_PACK):
            cols = slice(c * cw, (c + 1) * cw)
            x1c = x_ref[:, cols] + g1[:, cols] * mixbuf[1 - cur, :, cols]
            x1_ref[:, cols] = x1c
            ssq = ssq + jnp.sum(x1c * x1c, axis=-1, keepdims=True)
            if c == 1:
                project(0)
            if c == 5:
                project(1)
        rinv = lax.rsqrt(ssq / D_MODEL + NORM_EPS)
        hbs = []
        for c in range(TOK_PACK):
            cols = slice(c * cw, (c + 1) * cw)
            h2c = (x1_ref[:, cols] * rinv * nw_ref[:, cols]) * (1.0 + sc2[:, cols]) + sh2[:, cols]
            hb = h2c.astype(BF16)
            hbs.append(hb)
            bits = pltpu.bitcast(hb.astype(F32), jnp.uint32)
            h2_ref[pl.ds(c, tm, stride=TOK_PACK), :] = (
                (bits[:, LANES:] & jnp.uint32(0xFFFF0000)) | (bits[:, :LANES] >> 16))
            if c == 3:
                project(2)
        lg = _dot(jnp.concatenate(hbs, axis=1), wr_ref[...])
        project(3)
        _route(lg, tri_ref, ridx_ref, rw_ref, cnt_ref, carry)

    for cur in range(2):
        pl.when(i % 2 == cur)(functools.partial(step, cur))


def _route(lg, tri_ref, ridx_ref, rw_ref, cnt_ref, carry):
    tm = tri_ref.shape[0]
    lane = lax.broadcasted_iota(I32, (tm, LANES), 1)
    gl = jnp.where(lane < MOE_GROUPS, lg[:, :LANES], -jnp.inf)
    gmax = jnp.max(gl, axis=-1, keepdims=True)
    g_w = 1.0 / jnp.sum(jnp.exp(gl - gmax), axis=-1, keepdims=True)
    g_idx = _lane_min_index(gl == gmax, lane)
    el = jnp.where((lane // EXPERTS_PER_GROUP) == g_idx, lg[:, LANES:], -jnp.inf)
    v1 = jnp.max(el, axis=-1, keepdims=True)
    i1 = _lane_min_index(el == v1, lane)
    el2 = jnp.where(lane == i1, -jnp.inf, el)
    v2 = jnp.max(el2, axis=-1, keepdims=True)
    i2 = _lane_min_index(el2 == v2, lane)
    e2 = jnp.exp(v2 - v1)
    w1 = g_w * (1.0 / (1.0 + e2))
    w2 = g_w * (e2 / (1.0 + e2))

    hit1 = lane == i1
    hit2 = lane == i2
    onehot = jnp.where(hit1 | hit2, 1.0, 0.0).astype(F32)
    before = _dot(tri_ref[...], onehot.astype(BF16)) + carry[0:1, :]
    r1 = jnp.sum(jnp.where(hit1, before, 0.0), axis=-1, keepdims=True)
    r2 = jnp.sum(jnp.where(hit2, before, 0.0), axis=-1, keepdims=True)
    total = carry[0:1, :] + jnp.sum(onehot, axis=0, keepdims=True)
    carry[...] = jnp.broadcast_to(total, carry.shape)
    cnt_ref[...] = jnp.broadcast_to(total, cnt_ref.shape)
    packed = jnp.where(lane == 0, i1.astype(F32), jnp.where(lane == 1, i2.astype(F32), jnp.where(
        lane == 2, r1, jnp.where(lane == 3, r2, 0.0))))
    ridx_ref[...] = packed.T[0:8, :].astype(I32)
    rw_ref[...] = jnp.where(lane == 0, w1, jnp.where(lane == 1, w2, 0.0))


def _outproj(mg, wo, x2, mods, nw, wr, tri, tm, t):
    m = x2.shape[0]
    nbm = t // tm
    nsteps = m // tm
    mm = lambda i: jnp.minimum(i, nsteps - 1)
    ep = lambda i: jnp.maximum(i - 1, 0)
    return pl.pallas_call(
        _outproj_kernel,
        grid=(nsteps + 1,),
        in_specs=[pl.BlockSpec((tm, D_MODEL), lambda i: (mm(i), 0)),
                  pl.BlockSpec((D_MODEL, D_MODEL), lambda i: (0, 0), pipeline_mode=pl.Buffered(1)),
                  pl.BlockSpec((tm, D_MODEL), lambda i: (ep(i), 0)),
                  pl.BlockSpec((1, N_MOD, D_MODEL), lambda i: (ep(i) // nbm, 0, 0)),
                  pl.BlockSpec((1, D_MODEL), lambda i: (0, 0)),
                  pl.BlockSpec((D_MODEL, 2 * LANES), lambda i: (0, 0)),
                  pl.BlockSpec((tm, tm), lambda i: (0, 0))],
        out_specs=[pl.BlockSpec((tm, D_MODEL), lambda i: (ep(i), 0)),
                   pl.BlockSpec((tm * TOK_PACK, LANES), lambda i: (ep(i), 0)),
                   pl.BlockSpec((8, tm), lambda i: (0, ep(i))),
                   pl.BlockSpec((tm, LANES), lambda i: (ep(i), 0)),
                   pl.BlockSpec((8, LANES), lambda i: (0, 0))],
        out_shape=[jax.ShapeDtypeStruct((m, D_MODEL), F32),
                   jax.ShapeDtypeStruct((m * TOK_PACK, LANES), jnp.uint32),
                   jax.ShapeDtypeStruct((8, m), I32),
                   jax.ShapeDtypeStruct((m, LANES), F32),
                   jax.ShapeDtypeStruct((8, LANES), F32)],
        scratch_shapes=[pltpu.VMEM((8, LANES), F32),
                        pltpu.VMEM((2, tm, D_MODEL), F32)],
        compiler_params=_params(("arbitrary",)),
        name="outproj_router",
    )(mg, wo, x2, mods, nw, wr, tri)


def _gather_rows(idx_ref, base, n, src, dst, sem, row0=0):
    for r in range(n):
        tok = idx_ref[base + r]
        pltpu.make_async_copy(src.at[pl.ds(tok, 1), :], dst.at[pl.ds(row0 + r, 1), :],
                              sem).start(priority=r % 2)


def _gather_slabs(idx_ref, base, n, src, dst, sem, row0=0):
    for r in range(n):
        tok = idx_ref[base + r]
        pltpu.make_async_copy(src.at[pl.ds(pl.multiple_of(tok * TOK_PACK, TOK_PACK), TOK_PACK), :],
                              dst.at[pl.ds((row0 + r) * TOK_PITCH, TOK_PACK), :],
                              sem).start(priority=r % 2)


def _expert_kernel(blk_e_ref, pos_ref, meta_ref, h2_hbm, wg_ref, wu_ref, wd_ref, o_ref,
                   xb, tok_ref, wgb, wub, wdb, sem):
    i = pl.program_id(0)
    n_used = meta_ref[2 * N_EXPERTS]
    slot = i % 2
    rows = MOE_BLOCK
    m = pos_ref.shape[0] // TOP_K

    def wait_block(sl):
        pltpu.make_async_copy(h2_hbm.at[pl.ds(0, rows * TOK_PACK), :],
                              xb.at[sl, pl.ds(0, rows * TOK_PACK), :], sem.at[sl]).wait()

    @pl.when(i == 0)
    def _():
        def pad_expert(e, carry):
            def pad_slot(s, c):
                tok_ref[s] = 0
                return c
            return lax.fori_loop(meta_ref[e], meta_ref[N_EXPERTS + e], pad_slot, carry)

        lax.fori_loop(0, N_EXPERTS, pad_expert, 0)

        def place(t, carry):
            for k in range(TOP_K):
                tok_ref[pos_ref[k * m + t]] = t
            return carry

        lax.fori_loop(0, m, place, 0, unroll=8)
        _gather_slabs(tok_ref, 0, rows, h2_hbm, xb.at[0], sem.at[0])

    @pl.when(i < n_used)
    def _():
        wait_block(slot)
        nxt = jnp.minimum(i + 1, n_used - 1)
        n_pass = 2
        per = rows // (n_pass * TOK_PACK)

        def read_block(ps):
            xs = []
            for a in range(TOK_PACK):
                pair = xb[slot, pl.ds(a, rows, stride=TOK_PITCH), :]
                xs.append(pltpu.bitcast(pair << 16, F32).astype(BF16))
                xs.append(pltpu.bitcast(pair & jnp.uint32(0xFFFF0000), F32).astype(BF16))
                g = ps * TOK_PACK + a
                _gather_slabs(tok_ref, nxt * rows + g * per, per, h2_hbm, xb.at[1 - slot],
                              sem.at[1 - slot], row0=g * per)
            return jnp.concatenate(xs, axis=1)

        prev_e = blk_e_ref[jnp.maximum(i - 1, 0)]

        @pl.when((i == 0) | (blk_e_ref[i] != prev_e))
        def _():
            wgb[...] = wg_ref[0].astype(BF16)
            wub[...] = wu_ref[0].astype(BF16)
            wdb[...] = wd_ref[0].astype(BF16)

        gate = _dot(read_block(0), wgb[...])
        act = (_silu(gate) * _dot(read_block(1), wub[...])).astype(BF16)
        o_ref[...] = _dot(act, wdb[...])

        @pl.when(i == n_used - 1)
        def _():
            wait_block(1 - slot)

    @pl.when(i >= n_used)
    def _():
        o_ref[...] = jnp.zeros_like(o_ref)


def _experts(blk_e, pos, meta, h2, wg, wu, wd, nblk):
    rows = MOE_BLOCK
    live = lambda i, mt: jnp.minimum(i, mt[2 * N_EXPERTS] - 1)
    grid_spec = pltpu.PrefetchScalarGridSpec(
        num_scalar_prefetch=3,
        grid=(nblk,),
        in_specs=[pl.BlockSpec(memory_space=pl.ANY),
                  pl.BlockSpec((1, D_MODEL, D_EXPERT), lambda i, be, ps, mt: (be[live(i, mt)], 0, 0)),
                  pl.BlockSpec((1, D_MODEL, D_EXPERT), lambda i, be, ps, mt: (be[live(i, mt)], 0, 0)),
                  pl.BlockSpec((1, D_EXPERT, D_MODEL), lambda i, be, ps, mt: (be[live(i, mt)], 0, 0))],
        out_specs=pl.BlockSpec((rows, D_MODEL), lambda i, be, ps, mt: (i, 0)),
        scratch_shapes=[pltpu.VMEM((2, rows * TOK_PITCH, LANES), jnp.uint32),
                        pltpu.SMEM((nblk * rows,), I32),
                        pltpu.VMEM((D_MODEL, D_EXPERT), BF16),
                        pltpu.VMEM((D_MODEL, D_EXPERT), BF16),
                        pltpu.VMEM((D_EXPERT, D_MODEL), BF16),
                        pltpu.SemaphoreType.DMA((2,))],
    )
    return pl.pallas_call(
        _expert_kernel,
        grid_spec=grid_spec,
        out_shape=jax.ShapeDtypeStruct((nblk * rows, D_MODEL), F32),
        compiler_params=_params(("arbitrary",)),
        name="expert_mlp",
    )(blk_e, pos, meta, h2, wg, wu, wd)


def _combine_kernel(pos_ref, eo_hbm, x1_ref, rw_ref, mod_ref, nf_ref, o_ref, gb, sem):
    i = pl.program_id(0)
    nsteps = pl.num_programs(0)
    tm = x1_ref.shape[0]
    slot = i % 2

    def issue(step, sl):
        for k in range(TOP_K):
            _gather_rows(pos_ref, (k * nsteps + step) * tm, tm, eo_hbm, gb.at[sl, k], sem.at[sl])

    @pl.when(i == 0)
    def _():
        issue(0, 0)

    @pl.when(i + 1 < nsteps)
    def _():
        issue(i + 1, 1 - slot)

    for k in range(TOP_K):
        pltpu.make_async_copy(eo_hbm.at[pl.ds(0, tm), :], gb.at[slot, k], sem.at[slot]).wait()
    w1 = rw_ref[:, 0:1]
    w2 = rw_ref[:, 1:2]
    g2 = mod_ref[0, 5:6, :]
    x2 = x1_ref[...] + g2 * (gb[slot, 0] * w1 + gb[slot, 1] * w2)
    ms = jnp.mean(x2 * x2, axis=-1, keepdims=True)
    o_ref[...] = x2 * lax.rsqrt(ms + NORM_EPS) * nf_ref[...]


def _combine(pos, eo, x1, rw, mods, nf, tm, t):
    m = x1.shape[0]
    nbm = t // tm
    grid_spec = pltpu.PrefetchScalarGridSpec(
        num_scalar_prefetch=1,
        grid=(m // tm,),
        in_specs=[pl.BlockSpec(memory_space=pl.ANY),
                  pl.BlockSpec((tm, D_MODEL), lambda i, p: (i, 0)),
                  pl.BlockSpec((tm, LANES), lambda i, p: (i, 0)),
                  pl.BlockSpec((1, N_MOD, D_MODEL), lambda i, p: (i // nbm, 0, 0)),
                  pl.BlockSpec((1, D_MODEL), lambda i, p: (0, 0))],
        out_specs=pl.BlockSpec((tm, D_MODEL), lambda i, p: (i, 0)),
        scratch_shapes=[pltpu.VMEM((2, TOP_K, tm, D_MODEL), F32),
                        pltpu.SemaphoreType.DMA((2,))],
    )
    return pl.pallas_call(
        _combine_kernel,
        grid_spec=grid_spec,
        out_shape=jax.ShapeDtypeStruct((m, D_MODEL), F32),
        compiler_params=_params(("arbitrary",)),
        name="moe_combine_norm",
    )(pos, eo, x1, rw, mods, nf)


def _rope_tables(t):
    half = ATTN_HEAD_DIM // 4
    f32 = np.float32
    pos = np.arange(t)
    row = (pos // GRID_W).astype(f32)
    col = (pos % GRID_W).astype(f32)
    freqs = (f32(ROPE_BASE) ** (-np.arange(half, dtype=f32) / f32(half))).astype(f32)
    ar = row[:, None] * freqs[None, :]
    ac = col[:, None] * freqs[None, :]
    cos_t = np.concatenate([np.cos(ar), np.cos(ar), np.cos(ac), np.cos(ac)], axis=1)
    sin_t = np.concatenate([-np.sin(ar), np.sin(ar), -np.sin(ac), np.sin(ac)], axis=1)
    return jnp.asarray(cos_t, F32), jnp.asarray(sin_t, F32)


def _tri_pair3(n):
    li = np.arange(n)[:, None]
    si = np.arange(n)[None, :]
    pair = np.stack([li >= si, li <= si]).astype(np.float32)
    return jnp.asarray(np.concatenate([pair] * 3, axis=2), BF16)


def _head_expand_matrix2():
    k = np.arange(LANES)[:, None]
    j = np.arange(D_SSD)[None, :]
    e = (k == j // SSD_HEAD_DIM).astype(np.float32)
    return jnp.asarray(np.concatenate([e, e], axis=0), BF16)


def _pad_heads(v):
    return jnp.pad(v.astype(F32), ((0, 0), (0, LANES - SSD_HEADS)))[:, None, :]


def kernel(x, c, ctx, c_ctx, w_ada, b_ada, norm_mix, norm_ffn, w_in, conv_w, conv_b, a_log, dt_bias,
           d_skip, ssd_norm, attn_sink, w_branch_ssd, w_branch_attn, w_out, w_route_group,
           w_route_expert, w_gate, w_up, w_down, norm_final):
    bsz, t, d = x.shape
    tc = ctx.shape[1]
    m = bsz * t
    l = 0

    ctx_row = bsz
    n_rows = -(-(bsz + 1) // 8) * 8
    cc = jnp.zeros((n_rows, d), F32).at[:bsz].set(c).at[ctx_row].set(c_ctx)
    mods = _ada(cc, w_ada[l], b_ada[l][None, :]).reshape(n_rows, N_MOD, d)

    wi = w_in[l]
    o_xbc = D_SSD
    o_dt = o_xbc + CONV_CH
    o_q = o_dt + 2 * SSD_HEADS
    o_k = o_q + D_ATTN
    o_v = o_k + D_KV
    o_g = o_v + D_KV
    w_xs = wi[:, o_xbc:o_xbc + D_SSD]
    w_bc = wi[:, o_xbc + D_SSD:o_dt]
    w_k = wi[:, o_k:o_v]
    w_v = wi[:, o_v:o_g]
    w_lat = jnp.concatenate([w_xs, wi[:, :D_SSD], wi[:, o_q:o_k], w_bc, w_k, w_v, wi[:, o_g:]],
                            axis=1).astype(BF16)
    w_ctx = jnp.concatenate([w_xs, w_bc, w_k, w_v], axis=1).astype(BF16)
    w_dt = jnp.zeros((d, 2 * LANES), F32)
    w_dt = w_dt.at[:, :SSD_HEADS].set(wi[:, o_dt:o_dt + SSD_HEADS])
    w_dt = w_dt.at[:, LANES:LANES + SSD_HEADS].set(wi[:, o_dt + SSD_HEADS:o_q]).astype(BF16)

    cos_t, sin_t = _rope_tables(t)
    nmix = norm_mix[l][None, :]
    tm_in = min(1024, t)
    u, dt2 = _inproj(x.reshape(m, d), nmix, mods, w_lat, w_dt, cos_t, sin_t,
                     rows_per_mod=t, mod_row0=0, tm=tm_in, rope=True)
    mc = bsz * tc
    uc, dtc2 = _inproj(ctx.reshape(mc, d), nmix, mods, w_ctx, w_dt, None, None,
                       rows_per_mod=0, mod_row0=ctx_row, tm=min(1024, mc), rope=False)

    cw = conv_w[l]
    cb = conv_b[l][None, :]
    xs_c, bc_c = _conv(u, U_XS, U_BC, cw, cb, t, min(512, t))
    xsc_c, bcc_c = _conv(uc, UC_XS, UC_BC, cw, cb, tc, min(256, tc))
    alog = _pad_heads(a_log[l])
    bias = _pad_heads(dt_bias[l])
    e_mat = _head_expand_matrix2()
    h0 = _ctx_state(xsc_c, bcc_c, dtc2, alog, bias, _tri_pair3(tc), e_mat, bsz, tc)
    dsk = jnp.repeat(d_skip[l].astype(F32), SSD_HEAD_DIM)[None, :]
    y_ssd = _ssd(xs_c, bc_c, dt2, alog, bias, _tri_pair3(SSD_CHUNK), e_mat, h0, u, dsk,
                 ssd_norm[l][None, :], bsz, t)

    y_attn = _attn(attn_sink[l].astype(F32), u, uc, bsz, t, tc)

    merged = _merge(y_ssd, y_attn, u, w_branch_ssd[l].astype(BF16), w_branch_attn[l].astype(BF16),
                    min(512, t))

    tm_o = min(512, t)
    w_r = jnp.zeros((d, 2 * LANES), F32)
    w_r = w_r.at[:, :MOE_GROUPS].set(w_route_group[l])
    w_r = w_r.at[:, LANES:LANES + N_EXPERTS].set(w_route_expert[l]).astype(BF16)
    li = np.arange(tm_o)
    tri_strict = jnp.asarray((li[:, None] > li[None, :]).astype(np.float32), BF16)
    x1, h2, ridx, rw, cnt = _outproj(merged, w_out[l].astype(BF16), x.reshape(m, d), mods,
                                        norm_ffn[l][None, :], w_r, tri_strict, tm_o, t)

    na = m * TOP_K
    nblk = -(-(na + N_EXPERTS * (MOE_BLOCK - 1)) // MOE_BLOCK)
    counts = cnt[0, :N_EXPERTS].astype(I32)
    padded = (counts + MOE_BLOCK - 1) // MOE_BLOCK * MOE_BLOCK
    pend = jnp.cumsum(padded)
    pstart = pend - padded
    blk_start = jnp.arange(nblk, dtype=I32) * MOE_BLOCK
    blk_e = jnp.clip(jnp.sum((pend[None, :] <= blk_start[:, None]).astype(I32), axis=1),
                     0, N_EXPERTS - 1)
    n_used = (pend[N_EXPERTS - 1:] // MOE_BLOCK).astype(I32)
    meta = jnp.concatenate([pstart + counts, pend, n_used]).astype(I32)
    e_sel = ridx[0:TOP_K]
    seg0 = jnp.zeros_like(e_sel)
    for e in range(N_EXPERTS):
        seg0 = jnp.where(e_sel == e, pstart[e], seg0)
    pos = (seg0 + ridx[TOP_K:2 * TOP_K]).reshape(-1)

    eo = _experts(blk_e, pos, meta, h2, w_gate[l], w_up[l], w_down[l], nblk)

    tm_c = min(512, t)
    out = _combine(pos, eo, x1, rw, mods, norm_final[None, :], tm_c, t)
    return out.reshape(bsz, t, d)
```

```python
import functools

import numpy as np
import jax
import jax.numpy as jnp
from jax import lax
from jax.experimental import pallas as pl
from jax.experimental.pallas import tpu as pltpu

F32 = jnp.float32
BF16 = jnp.bfloat16
I32 = jnp.int32

D_MODEL = 2048
GRID_W = 64
NORM_EPS = 1e-6
N_MOD = 6
SSD_HEADS = 16
SSD_HEAD_DIM = 64
D_SSD = SSD_HEADS * SSD_HEAD_DIM
SSD_GROUPS = 2
SSD_STATE = 128
SSD_CONV = 5
SSD_CHUNK = 128
CONV_CH = D_SSD + 2 * SSD_GROUPS * SSD_STATE
ATTN_HEADS = 8
ATTN_KV_HEADS = 2
ATTN_HEAD_DIM = 128
D_ATTN = ATTN_HEADS * ATTN_HEAD_DIM
D_KV = ATTN_KV_HEADS * ATTN_HEAD_DIM
WINDOW = 128
ROPE_BASE = 10000.0
N_BRANCH = 2
MOE_GROUPS = 4
EXPERTS_PER_GROUP = 8
N_EXPERTS = MOE_GROUPS * EXPERTS_PER_GROUP
TOP_K = 2
D_EXPERT = 512
MOE_BLOCK = 256

LANES = 128
TOK_PACK = D_MODEL // (2 * LANES)
TOK_PITCH = TOK_PACK + 1
VMEM_LIMIT = 56 * 1024 * 1024

U_XS, U_Z, U_Q, U_BC, U_K, U_V, U_G = 0, 1024, 2048, 3072, 3584, 3840, 4096
U_W = 8192
UC_XS, UC_BC, UC_K, UC_V = 0, 1024, 1536, 1792
UC_W = 2048
IN_TN = 1024


def _params(sem):
    return pltpu.CompilerParams(dimension_semantics=sem, vmem_limit_bytes=VMEM_LIMIT)


def _silu(v):
    return v * jax.nn.sigmoid(v)


def _split_bf16(v, n):
    parts = []
    r = v
    for _ in range(n):
        p = r.astype(BF16)
        parts.append(p)
        r = r - p.astype(F32)
    return parts


def _dot(a, b):
    return jnp.dot(a, b, preferred_element_type=F32)


def _dot_nt(a, b):
    return lax.dot_general(a, b, (((1,), (1,)), ((), ())), preferred_element_type=F32)


def _ada_kernel(c_ref, w_ref, b_ref, o_ref):
    a = _silu(c_ref[...]).astype(BF16)
    o_ref[...] = _dot(a, w_ref[...].astype(BF16)) + b_ref[...]


def _ada(cc, w, b):
    rows, d = cc.shape
    n = w.shape[1]
    tn = 1024
    return pl.pallas_call(
        _ada_kernel,
        grid=(n // tn,),
        in_specs=[pl.BlockSpec((rows, d), lambda j: (0, 0)),
                  pl.BlockSpec((d, tn), lambda j: (0, j)),
                  pl.BlockSpec((1, tn), lambda j: (0, j))],
        out_specs=pl.BlockSpec((rows, tn), lambda j: (0, j)),
        out_shape=jax.ShapeDtypeStruct((rows, n), F32),
        compiler_params=_params(("arbitrary",)),
        name="ada",
    )(cc, w, b)


def _rope(a, cos, sin_signed, first):
    partner = jnp.where(first, pltpu.roll(a, 96, 1), pltpu.roll(a, 32, 1))
    return a * cos + partner * sin_signed


def _inproj_kernel(*refs, rope_heads, sub):
    if rope_heads:
        x_ref, nw_ref, mod_ref, w_ref, wdt_ref, cos_ref, sin_ref, o_ref, dt_ref, h_scr = refs
    else:
        x_ref, nw_ref, mod_ref, w_ref, wdt_ref, o_ref, dt_ref, h_scr = refs
    j = pl.program_id(1)
    tm = x_ref.shape[0]
    tn = w_ref.shape[1]

    n_split = 2 if tm >= 512 else 1
    part = tm // n_split

    def normalize(sp):
        nw = nw_ref[...]
        sh = mod_ref[0, 0:1, :]
        sc = mod_ref[0, 1:2, :]
        for r in range(sp * part, (sp + 1) * part, sub):
            rows = slice(r, r + sub)
            xf = x_ref[rows, :]
            ms = jnp.mean(xf * xf, axis=-1, keepdims=True)
            hb = ((xf * lax.rsqrt(ms + NORM_EPS) * nw) * (1.0 + sc) + sh).astype(BF16)
            h_scr[rows, :] = hb
            d = _dot(hb, wdt_ref[...])
            dt_ref[0, rows, :] = d[:, :LANES]
            dt_ref[1, rows, :] = d[:, LANES:]

    def store(heads, first_block=False):
        if first_block:
            normalize(0)
        for sp in range(n_split):
            rows = slice(sp * part, (sp + 1) * part)
            acc = _dot(h_scr[rows, :], w_ref[...])
            if first_block and sp + 1 < n_split:
                normalize(sp + 1)
            if heads:
                cos = cos_ref[rows, :]
                sin = sin_ref[rows, :]
                lane = lax.broadcasted_iota(I32, (part, LANES), 1)
                first = (lane % 64) < 32
            for hh in range(tn // LANES):
                a = acc[:, hh * LANES:(hh + 1) * LANES]
                if hh in heads:
                    a = _rope(a, cos, sin, first)
                o_ref[rows, hh * LANES:(hh + 1) * LANES] = a.astype(BF16)

    assert all(blk != 0 for blk, _ in rope_heads)
    pl.when(j == 0)(functools.partial(store, (), True))
    plain = j != 0
    for blk, heads in rope_heads:
        pl.when(j == blk)(functools.partial(store, heads))
        plain = plain & (j != blk)
    pl.when(plain)(functools.partial(store, ()))


def _inproj(x2, nw, mods, w, wdt, cos_t, sin_t, *, rows_per_mod, mod_row0, tm, rope):
    m, d = x2.shape
    n = w.shape[1]
    tn = IN_TN
    nbm = rows_per_mod // tm if rows_per_mod else 0
    if rows_per_mod:
        mod_map = lambda i, j: (mod_row0 + i // nbm, 0, 0)
    else:
        mod_map = lambda i, j: (mod_row0, 0, 0)
    in_specs = [pl.BlockSpec((tm, d), lambda i, j: (i, 0)),
                pl.BlockSpec((1, d), lambda i, j: (0, 0)),
                pl.BlockSpec((1, N_MOD, d), mod_map),
                pl.BlockSpec((d, tn), lambda i, j: (0, j)),
                pl.BlockSpec((d, 2 * LANES), lambda i, j: (0, 0))]
    args = [x2, nw, mods, w, wdt]
    if rope:
        in_specs += [pl.BlockSpec((tm, LANES), lambda i, j: (i % nbm, 0)),
                     pl.BlockSpec((tm, LANES), lambda i, j: (i % nbm, 0))]
        args += [cos_t, sin_t]
        rotary = lambda col: U_Q <= col < U_Q + D_ATTN or U_K <= col < U_K + D_KV
        rope_heads = tuple(
            (blk, tuple(hh for hh in range(tn // LANES) if rotary(blk * tn + hh * LANES)))
            for blk in range(n // tn)
            if any(rotary(blk * tn + hh * LANES) for hh in range(tn // LANES)))
    else:
        rope_heads = ()
    kern = functools.partial(_inproj_kernel, rope_heads=rope_heads, sub=128)
    return pl.pallas_call(
        kern,
        grid=(m // tm, n // tn),
        in_specs=in_specs,
        out_specs=[pl.BlockSpec((tm, tn), lambda i, j: (i, j)),
                   pl.BlockSpec((2, tm, LANES), lambda i, j: (0, i, 0))],
        out_shape=[jax.ShapeDtypeStruct((m, n), BF16),
                   jax.ShapeDtypeStruct((2, m, LANES), F32)],
        scratch_shapes=[pltpu.VMEM((tm, d), BF16)],
        compiler_params=_params(("arbitrary", "arbitrary")),
        name="inproj_rope" if rope else "inproj_ctx",
    )(*args)


CONV_HALO = 16


def _conv_kernel(xs_ref, bc_ref, pxs_ref, pbc_ref, nxs_ref, nbc_ref, w_ref, b_ref,
                 oxs_ref, obc_ref, ext, *, blocks_per_seq):
    i = pl.program_id(0)
    r = xs_ref.shape[0]
    pos = i % blocks_per_seq
    not_first = pos != 0
    not_last = pos != blocks_per_seq - 1
    h = CONV_HALO
    ext[0:h, 0:D_SSD] = jnp.where(not_first, pxs_ref[...].astype(F32), 0.0)
    ext[0:h, D_SSD:] = jnp.where(not_first, pbc_ref[...].astype(F32), 0.0)
    ext[h:h + r, 0:D_SSD] = xs_ref[...].astype(F32)
    ext[h:h + r, D_SSD:] = bc_ref[...].astype(F32)
    ext[h + r:, 0:D_SSD] = jnp.where(not_last, nxs_ref[...].astype(F32), 0.0)
    ext[h + r:, D_SSD:] = jnp.where(not_last, nbc_ref[...].astype(F32), 0.0)
    cw = 256
    pad = SSD_CONV // 2
    for cb in range(0, CONV_CH, cw):
        acc = jnp.broadcast_to(b_ref[:, cb:cb + cw], (r, cw))
        for k in range(SSD_CONV):
            acc = acc + ext[h - pad + k:h - pad + k + r, cb:cb + cw] * w_ref[k:k + 1, cb:cb + cw]
        y = _silu(acc)
        if cb < D_SSD:
            oxs_ref[:, cb:cb + cw] = y
        else:
            obc_ref[:, cb - D_SSD:cb - D_SSD + cw] = y.astype(BF16)


def _conv(u, xs_col, bc_col, conv_w, conv_b, seq_len, r):
    m = u.shape[0]
    h = CONV_HALO
    bps = seq_len // r
    wbc = CONV_CH - D_SSD
    xs_c, bc_c = xs_col // D_SSD, bc_col // wbc
    nh = m // h
    prev = lambda i: jnp.maximum(i * (r // h) - 1, 0)
    nxt = lambda i: jnp.minimum((i + 1) * (r // h), nh - 1)
    return pl.pallas_call(
        functools.partial(_conv_kernel, blocks_per_seq=bps),
        grid=(m // r,),
        in_specs=[pl.BlockSpec((r, D_SSD), lambda i: (i, xs_c)),
                  pl.BlockSpec((r, wbc), lambda i: (i, bc_c)),
                  pl.BlockSpec((h, D_SSD), lambda i: (prev(i), xs_c)),
                  pl.BlockSpec((h, wbc), lambda i: (prev(i), bc_c)),
                  pl.BlockSpec((h, D_SSD), lambda i: (nxt(i), xs_c)),
                  pl.BlockSpec((h, wbc), lambda i: (nxt(i), bc_c)),
                  pl.BlockSpec((SSD_CONV, CONV_CH), lambda i: (0, 0)),
                  pl.BlockSpec((1, CONV_CH), lambda i: (0, 0))],
        out_specs=[pl.BlockSpec((r, D_SSD), lambda i: (i, 0)),
                   pl.BlockSpec((r, wbc), lambda i: (i, 0))],
        out_shape=[jax.ShapeDtypeStruct((m, D_SSD), F32),
                   jax.ShapeDtypeStruct((m, wbc), BF16)],
        scratch_shapes=[pltpu.VMEM((r + 2 * h, CONV_CH), F32)],
        compiler_params=_params(("arbitrary",)),
        name="conv_silu",
    )(u, u, u, u, u, u, conv_w, conv_b)


def _expand_heads(v, e2):
    return _dot(jnp.concatenate(_split_bf16(v, 2), axis=1), e2)


def _cumsum_mm(tri3, da):
    return _dot(tri3, jnp.concatenate(_split_bf16(da, 3), axis=0))


def _ctx_state_kernel(xs_ref, b_ref, dt_ref, alog_ref, bias_ref, tri_ref, e_ref, h_ref):
    d = pl.program_id(1)
    tc = xs_ref.shape[0]
    dt = jax.nn.softplus(dt_ref[0] + bias_ref[0])
    a = -jnp.exp(alog_ref[0])
    cum = _cumsum_mm(tri_ref[0], dt * a)
    cum_end = jnp.where(d == 0, cum[tc - 1:tc, :], cum[0:1, :])
    wx = _expand_heads(dt * jnp.exp(cum_end - cum), e_ref[...])
    xw = (xs_ref[...] * wx).astype(BF16)
    gw = D_SSD // SSD_GROUPS
    for g in range(SSD_GROUPS):
        bgt = b_ref[:, g * SSD_STATE:(g + 1) * SSD_STATE].astype(F32).T.astype(BF16)
        h_ref[0, 0, :, g * gw:(g + 1) * gw] = _dot(bgt, xw[:, g * gw:(g + 1) * gw])


def _ctx_state(xs_c, bc_c, dt2, alog, bias, tri3, e2, bsz, tc):
    return pl.pallas_call(
        _ctx_state_kernel,
        grid=(bsz, 2),
        in_specs=[pl.BlockSpec((tc, D_SSD), lambda b, d: (b, 0)),
                  pl.BlockSpec((tc, SSD_GROUPS * SSD_STATE), lambda b, d: (b, 0)),
                  pl.BlockSpec((1, tc, LANES), lambda b, d: (d, b, 0)),
                  pl.BlockSpec((1, 1, LANES), lambda b, d: (d, 0, 0)),
                  pl.BlockSpec((1, 1, LANES), lambda b, d: (d, 0, 0)),
                  pl.BlockSpec((1, tc, 3 * tc), lambda b, d: (d, 0, 0)),
                  pl.BlockSpec((2 * LANES, D_SSD), lambda b, d: (0, 0))],
        out_specs=pl.BlockSpec((1, 1, SSD_STATE, D_SSD), lambda b, d: (b, d, 0, 0)),
        out_shape=jax.ShapeDtypeStruct((bsz, 2, SSD_STATE, D_SSD), F32),
        compiler_params=_params(("arbitrary", "arbitrary")),
        name="ssd_ctx_state",
    )(xs_c, bc_c, dt2, alog, bias, tri3, e2)


class _SsdChunk:
    GW = D_SSD // SSD_GROUPS
    HPG = SSD_HEADS // SSD_GROUPS

    def __init__(self, fwd, xs, bc_ref, dt_raw, alog, bias, tri3, e2, st, y_out):
        lc = SSD_CHUNK
        self.bc_ref, self.st, self.y_out = bc_ref, st, y_out
        dt = jax.nn.softplus(dt_raw + bias)
        self.cum = _cumsum_mm(tri3, dt * (-jnp.exp(alog)))
        self.cum_t = self.cum.T
        self.dt_t = dt.T
        end = lc - 1 if fwd else 0
        self.ecx = _expand_heads(jnp.exp(self.cum), e2)
        wx = _expand_heads(dt * jnp.exp(self.cum[end:end + 1, :] - self.cum), e2)
        self.xs_b = xs.astype(BF16)
        self.xw_b = (xs * wx).astype(BF16)
        self.cd_row = self.ecx[end:end + 1, :]
        li = lax.broadcasted_iota(I32, (lc, lc), 0)
        si = lax.broadcasted_iota(I32, (lc, lc), 1)
        self.causal = (li >= si) if fwd else (li <= si)
        self.low = lax.broadcasted_iota(I32, (lc, LANES), 1) < SSD_HEAD_DIM

    def group_head(self, g):
        cols = slice(g * self.GW, (g + 1) * self.GW)
        self.bg = self.bc_ref[:, g * SSD_STATE:(g + 1) * SSD_STATE]
        cg = self.bc_ref[:, (SSD_GROUPS + g) * SSD_STATE:(SSD_GROUPS + g + 1) * SSD_STATE]
        self.cb = _dot_nt(cg, self.bg)
        self.yoff = _dot(cg, self.st[:, cols].astype(BF16)) * self.ecx[:, cols]

    def head_pair(self, g, pr):
        h0 = g * self.HPG + 2 * pr
        ms = []
        for h in (h0, h0 + 1):
            seg = self.cum[:, h:h + 1] - self.cum_t[h:h + 1, :]
            ms.append((self.cb * jnp.exp(jnp.where(self.causal, seg, -jnp.inf))
                       * self.dt_t[h:h + 1, :]).astype(BF16))
        xp = self.xs_b[:, h0 * SSD_HEAD_DIM:(h0 + 2) * SSD_HEAD_DIM]
        zero = jnp.zeros_like(xp)
        rhs = jnp.concatenate([jnp.where(self.low, xp, zero), jnp.where(self.low, zero, xp)], axis=0)
        self.y_out[:, h0 * SSD_HEAD_DIM:(h0 + 2) * SSD_HEAD_DIM] = (
            _dot(jnp.concatenate(ms, axis=1), rhs) + self.yoff[:, pr * LANES:(pr + 1) * LANES])

    def group_tail(self, g):
        cols = slice(g * self.GW, (g + 1) * self.GW)
        bgt = self.bg.astype(F32).T.astype(BF16)
        self.st[:, cols] = self.st[:, cols] * self.cd_row[:, cols] + _dot(bgt, self.xw_b[:, cols])


def _ssd_kernel(xsf_ref, xsb_ref, bcf_ref, bcb_ref, dtf_ref, dtb_ref, alog_ref, bias_ref, tri_ref,
                e_ref, h0_ref, z_ref, dsk_ref, nw_ref, o_ref, st, ybuf, ycur, *, nc):
    s = pl.program_id(1)
    lc = SSD_CHUNK

    @pl.when(s == 0)
    def _():
        st[...] = h0_ref[0]

    e2 = e_ref[...]
    scans = (_SsdChunk(True, xsf_ref[...], bcf_ref, dtf_ref[0], alog_ref[0], bias_ref[0], tri_ref[0],
                       e2, st.at[0], ycur.at[0]),
             _SsdChunk(False, xsb_ref[...], bcb_ref, dtb_ref[0], alog_ref[1], bias_ref[1], tri_ref[1],
                       e2, st.at[1], ycur.at[1]))
    for g in range(SSD_GROUPS):
        for sc in scans:
            sc.group_head(g)
        for pr in range(_SsdChunk.HPG // 2):
            for sc in scans:
                sc.head_pair(g, pr)
        for sc in scans:
            sc.group_tail(g)
    rows = (pl.ds(pl.multiple_of(s * lc, lc), lc), pl.ds(pl.multiple_of((nc - 1 - s) * lc, lc), lc))

    @pl.when(s < nc // 2)
    def _():
        for d in range(2):
            ybuf[rows[d], :] = ycur[d]

    @pl.when(s >= nc // 2)
    def _():
        gw = D_SSD // SSD_GROUPS
        for d, xs_ref in enumerate((xsf_ref, xsb_ref)):
            y = ycur[d] + ybuf[rows[d], :] + dsk_ref[...] * xs_ref[...]
            gy = y * _silu(z_ref[rows[d], :].astype(F32))
            for g in range(SSD_GROUPS):
                blk = gy[:, g * gw:(g + 1) * gw]
                ms = jnp.mean(blk * blk, axis=-1, keepdims=True)
                o_ref[rows[d], g * gw:(g + 1) * gw] = (blk * lax.rsqrt(ms + NORM_EPS)
                                                       * nw_ref[:, g * gw:(g + 1) * gw]).astype(BF16)


def _ssd(xs_c, bc_c, dt2, alog, bias, tri3, e2, h0, u, dsk, nw, bsz, t):
    nc = t // SSD_CHUNK
    lc = SSD_CHUNK
    m = bsz * t
    fw = lambda b, s: b * nc + s
    bw = lambda b, s: b * nc + nc - 1 - s
    zc = U_Z // D_SSD
    bcw = 2 * SSD_GROUPS * SSD_STATE
    return pl.pallas_call(
        functools.partial(_ssd_kernel, nc=nc),
        grid=(bsz, nc),
        in_specs=[pl.BlockSpec((lc, D_SSD), lambda b, s: (fw(b, s), 0)),
                  pl.BlockSpec((lc, D_SSD), lambda b, s: (bw(b, s), 0)),
                  pl.BlockSpec((lc, bcw), lambda b, s: (fw(b, s), 0)),
                  pl.BlockSpec((lc, bcw), lambda b, s: (bw(b, s), 0)),
                  pl.BlockSpec((1, lc, LANES), lambda b, s: (0, fw(b, s), 0)),
                  pl.BlockSpec((1, lc, LANES), lambda b, s: (1, bw(b, s), 0)),
                  pl.BlockSpec((2, 1, LANES), lambda b, s: (0, 0, 0)),
                  pl.BlockSpec((2, 1, LANES), lambda b, s: (0, 0, 0)),
                  pl.BlockSpec((2, lc, 3 * lc), lambda b, s: (0, 0, 0)),
                  pl.BlockSpec((2 * LANES, D_SSD), lambda b, s: (0, 0)),
                  pl.BlockSpec((1, 2, SSD_STATE, D_SSD), lambda b, s: (b, 0, 0, 0)),
                  pl.BlockSpec((t, D_SSD), lambda b, s: (b, zc)),
                  pl.BlockSpec((1, D_SSD), lambda b, s: (0, 0)),
                  pl.BlockSpec((1, D_SSD), lambda b, s: (0, 0))],
        out_specs=pl.BlockSpec((t, D_SSD), lambda b, s: (b, 0)),
        out_shape=jax.ShapeDtypeStruct((m, D_SSD), BF16),
        scratch_shapes=[pltpu.VMEM((2, SSD_STATE, D_SSD), F32),
                        pltpu.VMEM((t, D_SSD), F32),
                        pltpu.VMEM((2, lc, D_SSD), F32)],
        compiler_params=_params(("arbitrary", "arbitrary")),
        name="ssd_scan",
    )(xs_c, xs_c, bc_c, bc_c, dt2, dt2, alog, bias, tri3, e2, h0, u, dsk, nw)


def _attn_kernel(sink_ref, q_ref, k_ref, v_ref, kc_ref, vc_ref, o_ref, kp, vp, bias, sbuf, *, t):
    g = pl.program_id(1)
    w = WINDOW
    nb = t // w
    rep = ATTN_HEADS // ATTN_KV_HEADS
    tc = kc_ref.shape[0]
    nk = 3 * w + tc
    zeros = jnp.zeros((w, ATTN_HEAD_DIM), BF16)
    kp[0:w, :] = zeros
    kp[w:w + t, :] = k_ref[...]
    kp[w + t:, :] = zeros
    vp[0:w, :] = zeros
    vp[w:w + t, :] = v_ref[...]
    vp[w + t:, :] = zeros
    qi = lax.broadcasted_iota(I32, (rep * w, w), 0) % w
    kj = lax.broadcasted_iota(I32, (rep * w, w), 1)
    bias[0] = jnp.where(kj >= qi, 0.0, -jnp.inf).astype(F32)
    bias[1] = jnp.where(kj <= qi, 0.0, -jnp.inf).astype(F32)
    sink_col = jnp.concatenate(
        [jnp.full((w, 1), sink_ref[g * rep + h], F32) for h in range(rep)], axis=0)
    scale = ATTN_HEAD_DIM ** -0.5
    log2e = float(np.log2(np.e))

    def scores(n):
        rows = pl.ds(pl.multiple_of(n * w, w), w)
        q = jnp.concatenate([q_ref[rows, h * ATTN_HEAD_DIM:(h + 1) * ATTN_HEAD_DIM]
                             for h in range(rep)], axis=0)
        kall = jnp.concatenate([kp[pl.ds(pl.multiple_of(n * w, w), 3 * w), :], kc_ref[...]], axis=0)
        raw = _dot_nt(q, kall)
        prev = raw[:, 0:w] + (bias[0] + jnp.where(n == 0, -jnp.inf, 0.0))
        nxt = raw[:, 2 * w:3 * w] + (bias[1] + jnp.where(n == nb - 1, -jnp.inf, 0.0))
        return jnp.concatenate([prev, raw[:, w:2 * w], nxt, raw[:, 3 * w:]], axis=1)

    def finish(n, sc):
        rows = pl.ds(pl.multiple_of(n * w, w), w)
        vall = jnp.concatenate([vp[pl.ds(pl.multiple_of(n * w, w), 3 * w), :], vc_ref[...]], axis=0)
        mx = jnp.maximum(jnp.max(sc, axis=-1, keepdims=True) * scale, sink_col)
        p = jnp.exp2(sc * (scale * log2e) - mx * log2e)
        den = jnp.sum(p, axis=-1, keepdims=True) + jnp.exp2((sink_col - mx) * log2e)
        o = _dot(p.astype(BF16), vall) / den
        for h in range(rep):
            o_ref[rows, h * ATTN_HEAD_DIM:(h + 1) * ATTN_HEAD_DIM] = o[h * w:(h + 1) * w, :].astype(BF16)

    sbuf[0] = scores(0)

    def body(j, carry):
        n0 = 2 * j
        sbuf[1] = scores(n0 + 1)
        finish(n0, sbuf[0])
        sbuf[0] = scores(jnp.minimum(n0 + 2, nb - 1))
        finish(n0 + 1, sbuf[1])
        return carry

    lax.fori_loop(0, nb // 2, body, 0)


def _attn(sink, u, uc, bsz, t, tc):
    rep = ATTN_HEADS // ATTN_KV_HEADS
    qw = rep * ATTN_HEAD_DIM
    hd = ATTN_HEAD_DIM
    return pl.pallas_call(
        functools.partial(_attn_kernel, t=t),
        grid=(bsz, ATTN_KV_HEADS),
        in_specs=[pl.BlockSpec(memory_space=pltpu.SMEM),
                  pl.BlockSpec((t, qw), lambda b, g: (b, U_Q // qw + g)),
                  pl.BlockSpec((t, hd), lambda b, g: (b, U_K // hd + g)),
                  pl.BlockSpec((t, hd), lambda b, g: (b, U_V // hd + g)),
                  pl.BlockSpec((tc, hd), lambda b, g: (b, UC_K // hd + g)),
                  pl.BlockSpec((tc, hd), lambda b, g: (b, UC_V // hd + g))],
        out_specs=pl.BlockSpec((t, qw), lambda b, g: (b, g)),
        out_shape=jax.ShapeDtypeStruct((bsz * t, D_ATTN), BF16),
        scratch_shapes=[pltpu.VMEM((t + 2 * WINDOW, hd), BF16),
                        pltpu.VMEM((t + 2 * WINDOW, hd), BF16),
                        pltpu.VMEM((2, rep * WINDOW, WINDOW), F32),
                        pltpu.VMEM((2, rep * WINDOW, 3 * WINDOW + tc), F32)],
        compiler_params=_params(("arbitrary", "arbitrary")),
        name="window_attn",
    )(sink, u, u, u, uc, uc)


def _merge_kernel(ys_ref, ya_ref, g0_ref, g1_ref, ws_ref, wa_ref, o_ref):
    ys = ys_ref[...]
    ya = ya_ref[...]
    cw = 512
    for cb in range(0, D_MODEL, cw):
        t0 = _dot(ys, ws_ref[:, cb:cb + cw])
        t1 = _dot(ya, wa_ref[:, cb:cb + cw])
        g0 = jax.nn.sigmoid(g0_ref[:, cb:cb + cw].astype(F32))
        g1 = jax.nn.sigmoid(g1_ref[:, cb:cb + cw].astype(F32))
        o_ref[:, cb:cb + cw] = (g0 * t0 + g1 * t1).astype(BF16)


def _merge(ys, ya, u, ws, wa, tm):
    m = ys.shape[0]
    gc = U_G // D_MODEL
    return pl.pallas_call(
        _merge_kernel,
        grid=(m // tm,),
        in_specs=[pl.BlockSpec((tm, D_SSD), lambda i: (i, 0)),
                  pl.BlockSpec((tm, D_ATTN), lambda i: (i, 0)),
                  pl.BlockSpec((tm, D_MODEL), lambda i: (i, gc)),
                  pl.BlockSpec((tm, D_MODEL), lambda i: (i, gc + 1)),
                  pl.BlockSpec((D_SSD, D_MODEL), lambda i: (0, 0)),
                  pl.BlockSpec((D_ATTN, D_MODEL), lambda i: (0, 0))],
        out_specs=pl.BlockSpec((tm, D_MODEL), lambda i: (i, 0)),
        out_shape=jax.ShapeDtypeStruct((m, D_MODEL), BF16),
        compiler_params=_params(("arbitrary",)),
        name="branch_merge",
    )(ys, ya, u, u, ws, wa)


def _lane_min_index(mask, lane):
    return jnp.min(jnp.where(mask, lane, LANES), axis=-1, keepdims=True)


def _outproj_kernel(mg_ref, wo_ref, x_ref, mod_ref, nw_ref, wr_ref, tri_ref,
                    x1_ref, h2_ref, ridx_ref, rw_ref, cnt_ref, carry, mixbuf):
    i = pl.program_id(0)
    tm = x_ref.shape[0]

    @pl.when(i == 0)
    def _():
        mixbuf[1] = jnp.zeros(mixbuf.shape[1:], F32)

    @pl.when(i <= 1)
    def _():
        carry[...] = jnp.zeros_like(carry)

    g1 = mod_ref[0, 2:3, :]
    sh2 = mod_ref[0, 3:4, :]
    sc2 = mod_ref[0, 4:5, :]
    cw = 2 * LANES
    dw = D_MODEL // 4

    def step(cur):
        def project(j):
            mixbuf[cur, :, j * dw:(j + 1) * dw] = _dot(mg_ref[...], wo_ref[:, j * dw:(j + 1) * dw])

        ssq = jnp.zeros((tm, 1), F32)
        for c in range(TOK_PACK):
            cols = slice(c * cw, (c + 1) * cw)
            x1c = x_ref[:, cols] + g1[:, cols] * mixbuf[1 - cur, :, cols]
            x1_ref[:, cols] = x1c
            ssq = ssq + jnp.sum(x1c * x1c, axis=-1, keepdims=True)
            if c == 1:
                project(0)
            if c == 5:
                project(1)
        rinv = lax.rsqrt(ssq / D_MODEL + NORM_EPS)
        hbs = []
        for c in range(TOK_PACK):
            cols = slice(c * cw, (c + 1) * cw)
            h2c = (x1_ref[:, cols] * rinv * nw_ref[:, cols]) * (1.0 + sc2[:, cols]) + sh2[:, cols]
            hb = h2c.astype(BF16)
            hbs.append(hb)
            bits = pltpu.bitcast(hb.astype(F32), jnp.uint32)
            h2_ref[pl.ds(c, tm, stride=TOK_PACK), :] = (
                (bits[:, LANES:] & jnp.uint32(0xFFFF0000)) | (bits[:, :LANES] >> 16))
            if c == 3:
                project(2)
        lg = _dot(jnp.concatenate(hbs, axis=1), wr_ref[...])
        project(3)
        _route(lg, tri_ref, ridx_ref, rw_ref, cnt_ref, carry)

    for cur in range(2):
        pl.when(i % 2 == cur)(functools.partial(step, cur))


def _route(lg, tri_ref, ridx_ref, rw_ref, cnt_ref, carry):
    tm = tri_ref.shape[0]
    lane = lax.broadcasted_iota(I32, (tm, LANES), 1)
    gl = jnp.where(lane < MOE_GROUPS, lg[:, :LANES], -jnp.inf)
    gmax = jnp.max(gl, axis=-1, keepdims=True)
    g_w = 1.0 / jnp.sum(jnp.exp(gl - gmax), axis=-1, keepdims=True)
    g_idx = _lane_min_index(gl == gmax, lane)
    el = jnp.where((lane // EXPERTS_PER_GROUP) == g_idx, lg[:, LANES:], -jnp.inf)
    v1 = jnp.max(el, axis=-1, keepdims=True)
    i1 = _lane_min_index(el == v1, lane)
    el2 = jnp.where(lane == i1, -jnp.inf, el)
    v2 = jnp.max(el2, axis=-1, keepdims=True)
    i2 = _lane_min_index(el2 == v2, lane)
    e2 = jnp.exp(v2 - v1)
    w1 = g_w * (1.0 / (1.0 + e2))
    w2 = g_w * (e2 / (1.0 + e2))

    hit1 = lane == i1
    hit2 = lane == i2
    onehot = jnp.where(hit1 | hit2, 1.0, 0.0).astype(F32)
    before = _dot(tri_ref[...], onehot.astype(BF16)) + carry[0:1, :]
    r1 = jnp.sum(jnp.where(hit1, before, 0.0), axis=-1, keepdims=True)
    r2 = jnp.sum(jnp.where(hit2, before, 0.0), axis=-1, keepdims=True)
    total = carry[0:1, :] + jnp.sum(onehot, axis=0, keepdims=True)
    carry[...] = jnp.broadcast_to(total, carry.shape)
    cnt_ref[...] = jnp.broadcast_to(total, cnt_ref.shape)
    packed = jnp.where(lane == 0, i1.astype(F32), jnp.where(lane == 1, i2.astype(F32), jnp.where(
        lane == 2, r1, jnp.where(lane == 3, r2, 0.0))))
    ridx_ref[...] = packed.T[0:8, :].astype(I32)
    rw_ref[...] = jnp.where(lane == 0, w1, jnp.where(lane == 1, w2, 0.0))


def _outproj(mg, wo, x2, mods, nw, wr, tri, tm, t):
    m = x2.shape[0]
    nbm = t // tm
    nsteps = m // tm
    mm = lambda i: jnp.minimum(i, nsteps - 1)
    ep = lambda i: jnp.maximum(i - 1, 0)
    return pl.pallas_call(
        _outproj_kernel,
        grid=(nsteps + 1,),
        in_specs=[pl.BlockSpec((tm, D_MODEL), lambda i: (mm(i), 0)),
                  pl.BlockSpec((D_MODEL, D_MODEL), lambda i: (0, 0), pipeline_mode=pl.Buffered(1)),
                  pl.BlockSpec((tm, D_MODEL), lambda i: (ep(i), 0)),
                  pl.BlockSpec((1, N_MOD, D_MODEL), lambda i: (ep(i) // nbm, 0, 0)),
                  pl.BlockSpec((1, D_MODEL), lambda i: (0, 0)),
                  pl.BlockSpec((D_MODEL, 2 * LANES), lambda i: (0, 0)),
                  pl.BlockSpec((tm, tm), lambda i: (0, 0))],
        out_specs=[pl.BlockSpec((tm, D_MODEL), lambda i: (ep(i), 0)),
                   pl.BlockSpec((tm * TOK_PACK, LANES), lambda i: (ep(i), 0)),
                   pl.BlockSpec((8, tm), lambda i: (0, ep(i))),
                   pl.BlockSpec((tm, LANES), lambda i: (ep(i), 0)),
                   pl.BlockSpec((8, LANES), lambda i: (0, 0))],
        out_shape=[jax.ShapeDtypeStruct((m, D_MODEL), F32),
                   jax.ShapeDtypeStruct((m * TOK_PACK, LANES), jnp.uint32),
                   jax.ShapeDtypeStruct((8, m), I32),
                   jax.ShapeDtypeStruct((m, LANES), F32),
                   jax.ShapeDtypeStruct((8, LANES), F32)],
        scratch_shapes=[pltpu.VMEM((8, LANES), F32),
                        pltpu.VMEM((2, tm, D_MODEL), F32)],
        compiler_params=_params(("arbitrary",)),
        name="outproj_router",
    )(mg, wo, x2, mods, nw, wr, tri)


def _gather_rows(idx_ref, base, n, src, dst, sem, row0=0):
    for r in range(n):
        tok = idx_ref[base + r]
        pltpu.make_async_copy(src.at[pl.ds(tok, 1), :], dst.at[pl.ds(row0 + r, 1), :],
                              sem).start(priority=r % 2)


def _gather_slabs(idx_ref, base, n, src, dst, sem, row0=0):
    for r in range(n):
        tok = idx_ref[base + r]
        pltpu.make_async_copy(src.at[pl.ds(pl.multiple_of(tok * TOK_PACK, TOK_PACK), TOK_PACK), :],
                              dst.at[pl.ds((row0 + r) * TOK_PITCH, TOK_PACK), :],
                              sem).start(priority=r % 2)


def _expert_kernel(blk_e_ref, pos_ref, meta_ref, h2_hbm, wg_ref, wu_ref, wd_ref, o_ref,
                   xb, tok_ref, wgb, wub, wdb, sem):
    i = pl.program_id(0)
    n_used = meta_ref[2 * N_EXPERTS]
    slot = i % 2
    rows = MOE_BLOCK
    m = pos_ref.shape[0] // TOP_K

    def wait_block(sl):
        pltpu.make_async_copy(h2_hbm.at[pl.ds(0, rows * TOK_PACK), :],
                              xb.at[sl, pl.ds(0, rows * TOK_PACK), :], sem.at[sl]).wait()

    @pl.when(i == 0)
    def _():
        def pad_expert(e, carry):
            def pad_slot(s, c):
                tok_ref[s] = 0
                return c
            return lax.fori_loop(meta_ref[e], meta_ref[N_EXPERTS + e], pad_slot, carry)

        lax.fori_loop(0, N_EXPERTS, pad_expert, 0)

        def place(t, carry):
            for k in range(TOP_K):
                tok_ref[pos_ref[k * m + t]] = t
            return carry

        lax.fori_loop(0, m, place, 0, unroll=8)
        _gather_slabs(tok_ref, 0, rows, h2_hbm, xb.at[0], sem.at[0])

    @pl.when(i < n_used)
    def _():
        wait_block(slot)
        nxt = jnp.minimum(i + 1, n_used - 1)
        n_pass = 2
        per = rows // (n_pass * TOK_PACK)

        def read_block(ps):
            xs = []
            for a in range(TOK_PACK):
                pair = xb[slot, pl.ds(a, rows, stride=TOK_PITCH), :]
                xs.append(pltpu.bitcast(pair << 16, F32).astype(BF16))
                xs.append(pltpu.bitcast(pair & jnp.uint32(0xFFFF0000), F32).astype(BF16))
                g = ps * TOK_PACK + a
                _gather_slabs(tok_ref, nxt * rows + g * per, per, h2_hbm, xb.at[1 - slot],
                              sem.at[1 - slot], row0=g * per)
            return jnp.concatenate(xs, axis=1)

        prev_e = blk_e_ref[jnp.maximum(i - 1, 0)]

        @pl.when((i == 0) | (blk_e_ref[i] != prev_e))
        def _():
            wgb[...] = wg_ref[0].astype(BF16)
            wub[...] = wu_ref[0].astype(BF16)
            wdb[...] = wd_ref[0].astype(BF16)

        gate = _dot(read_block(0), wgb[...])
        act = (_silu(gate) * _dot(read_block(1), wub[...])).astype(BF16)
        o_ref[...] = _dot(act, wdb[...])

        @pl.when(i == n_used - 1)
        def _():
            wait_block(1 - slot)

    @pl.when(i >= n_used)
    def _():
        o_ref[...] = jnp.zeros_like(o_ref)


def _experts(blk_e, pos, meta, h2, wg, wu, wd, nblk):
    rows = MOE_BLOCK
    live = lambda i, mt: jnp.minimum(i, mt[2 * N_EXPERTS] - 1)
    grid_spec = pltpu.PrefetchScalarGridSpec(
        num_scalar_prefetch=3,
        grid=(nblk,),
        in_specs=[pl.BlockSpec(memory_space=pl.ANY),
                  pl.BlockSpec((1, D_MODEL, D_EXPERT), lambda i, be, ps, mt: (be[live(i, mt)], 0, 0)),
                  pl.BlockSpec((1, D_MODEL, D_EXPERT), lambda i, be, ps, mt: (be[live(i, mt)], 0, 0)),
                  pl.BlockSpec((1, D_EXPERT, D_MODEL), lambda i, be, ps, mt: (be[live(i, mt)], 0, 0))],
        out_specs=pl.BlockSpec((rows, D_MODEL), lambda i, be, ps, mt: (i, 0)),
        scratch_shapes=[pltpu.VMEM((2, rows * TOK_PITCH, LANES), jnp.uint32),
                        pltpu.SMEM((nblk * rows,), I32),
                        pltpu.VMEM((D_MODEL, D_EXPERT), BF16),
                        pltpu.VMEM((D_MODEL, D_EXPERT), BF16),
                        pltpu.VMEM((D_EXPERT, D_MODEL), BF16),
                        pltpu.SemaphoreType.DMA((2,))],
    )
    return pl.pallas_call(
        _expert_kernel,
        grid_spec=grid_spec,
        out_shape=jax.ShapeDtypeStruct((nblk * rows, D_MODEL), F32),
        compiler_params=_params(("arbitrary",)),
        name="expert_mlp",
    )(blk_e, pos, meta, h2, wg, wu, wd)


def _combine_kernel(pos_ref, eo_hbm, x1_ref, rw_ref, mod_ref, nf_ref, o_ref, gb, sem):
    i = pl.program_id(0)
    nsteps = pl.num_programs(0)
    tm = x1_ref.shape[0]
    slot = i % 2

    def issue(step, sl):
        for k in range(TOP_K):
            _gather_rows(pos_ref, (k * nsteps + step) * tm, tm, eo_hbm, gb.at[sl, k], sem.at[sl])

    @pl.when(i == 0)
    def _():
        issue(0, 0)

    @pl.when(i + 1 < nsteps)
    def _():
        issue(i + 1, 1 - slot)

    for k in range(TOP_K):
        pltpu.make_async_copy(eo_hbm.at[pl.ds(0, tm), :], gb.at[slot, k], sem.at[slot]).wait()
    w1 = rw_ref[:, 0:1]
    w2 = rw_ref[:, 1:2]
    g2 = mod_ref[0, 5:6, :]
    x2 = x1_ref[...] + g2 * (gb[slot, 0] * w1 + gb[slot, 1] * w2)
    ms = jnp.mean(x2 * x2, axis=-1, keepdims=True)
    o_ref[...] = x2 * lax.rsqrt(ms + NORM_EPS) * nf_ref[...]


def _combine(pos, eo, x1, rw, mods, nf, tm, t):
    m = x1.shape[0]
    nbm = t // tm
    grid_spec = pltpu.PrefetchScalarGridSpec(
        num_scalar_prefetch=1,
        grid=(m // tm,),
        in_specs=[pl.BlockSpec(memory_space=pl.ANY),
                  pl.BlockSpec((tm, D_MODEL), lambda i, p: (i, 0)),
                  pl.BlockSpec((tm, LANES), lambda i, p: (i, 0)),
                  pl.BlockSpec((1, N_MOD, D_MODEL), lambda i, p: (i // nbm, 0, 0)),
                  pl.BlockSpec((1, D_MODEL), lambda i, p: (0, 0))],
        out_specs=pl.BlockSpec((tm, D_MODEL), lambda i, p: (i, 0)),
        scratch_shapes=[pltpu.VMEM((2, TOP_K, tm, D_MODEL), F32),
                        pltpu.SemaphoreType.DMA((2,))],
    )
    return pl.pallas_call(
        _combine_kernel,
        grid_spec=grid_spec,
        out_shape=jax.ShapeDtypeStruct((m, D_MODEL), F32),
        compiler_params=_params(("arbitrary",)),
        name="moe_combine_norm",
    )(pos, eo, x1, rw, mods, nf)


def _rope_tables(t):
    half = ATTN_HEAD_DIM // 4
    f32 = np.float32
    pos = np.arange(t)
    row = (pos // GRID_W).astype(f32)
    col = (pos % GRID_W).astype(f32)
    freqs = (f32(ROPE_BASE) ** (-np.arange(half, dtype=f32) / f32(half))).astype(f32)
    ar = row[:, None] * freqs[None, :]
    ac = col[:, None] * freqs[None, :]
    cos_t = np.concatenate([np.cos(ar), np.cos(ar), np.cos(ac), np.cos(ac)], axis=1)
    sin_t = np.concatenate([-np.sin(ar), np.sin(ar), -np.sin(ac), np.sin(ac)], axis=1)
    return jnp.asarray(cos_t, F32), jnp.asarray(sin_t, F32)


def _tri_pair3(n):
    li = np.arange(n)[:, None]
    si = np.arange(n)[None, :]
    pair = np.stack([li >= si, li <= si]).astype(np.float32)
    return jnp.asarray(np.concatenate([pair] * 3, axis=2), BF16)


def _head_expand_matrix2():
    k = np.arange(LANES)[:, None]
    j = np.arange(D_SSD)[None, :]
    e = (k == j // SSD_HEAD_DIM).astype(np.float32)
    return jnp.asarray(np.concatenate([e, e], axis=0), BF16)


def _pad_heads(v):
    return jnp.pad(v.astype(F32), ((0, 0), (0, LANES - SSD_HEADS)))[:, None, :]


def _row_blocks(t, mc, tc):
    return dict(inproj=min(1024, t), inproj_ctx=min(1024, mc), conv=min(512, t), conv_ctx=min(256, tc),
                merge=min(512, t), outproj=min(512, t), combine=min(256, t))


def kernel(x, c, ctx, c_ctx, w_ada, b_ada, norm_mix, norm_ffn, w_in, conv_w, conv_b, a_log, dt_bias,
           d_skip, ssd_norm, attn_sink, w_branch_ssd, w_branch_attn, w_out, w_route_group,
           w_route_expert, w_gate, w_up, w_down, norm_final):
    bsz, t, d = x.shape
    tc = ctx.shape[1]
    m = bsz * t
    l = 0

    ctx_row = bsz
    n_rows = -(-(bsz + 1) // 8) * 8
    cc = jnp.zeros((n_rows, d), F32).at[:bsz].set(c).at[ctx_row].set(c_ctx)
    mods = _ada(cc, w_ada[l], b_ada[l][None, :]).reshape(n_rows, N_MOD, d)

    wi = w_in[l]
    o_xbc = D_SSD
    o_dt = o_xbc + CONV_CH
    o_q = o_dt + 2 * SSD_HEADS
    o_k = o_q + D_ATTN
    o_v = o_k + D_KV
    o_g = o_v + D_KV
    w_xs = wi[:, o_xbc:o_xbc + D_SSD]
    w_bc = wi[:, o_xbc + D_SSD:o_dt]
    w_k = wi[:, o_k:o_v]
    w_v = wi[:, o_v:o_g]
    w_lat = jnp.concatenate([w_xs, wi[:, :D_SSD], wi[:, o_q:o_k], w_bc, w_k, w_v, wi[:, o_g:]],
                            axis=1).astype(BF16)
    w_ctx = jnp.concatenate([w_xs, w_bc, w_k, w_v], axis=1).astype(BF16)
    w_dt = jnp.zeros((d, 2 * LANES), F32)
    w_dt = w_dt.at[:, :SSD_HEADS].set(wi[:, o_dt:o_dt + SSD_HEADS])
    w_dt = w_dt.at[:, LANES:LANES + SSD_HEADS].set(wi[:, o_dt + SSD_HEADS:o_q]).astype(BF16)

    cos_t, sin_t = _rope_tables(t)
    nmix = norm_mix[l][None, :]
    mc = bsz * tc
    blk = _row_blocks(t, mc, tc)
    u, dt2 = _inproj(x.reshape(m, d), nmix, mods, w_lat, w_dt, cos_t, sin_t,
                     rows_per_mod=t, mod_row0=0, tm=blk["inproj"], rope=True)
    uc, dtc2 = _inproj(ctx.reshape(mc, d), nmix, mods, w_ctx, w_dt, None, None,
                       rows_per_mod=0, mod_row0=ctx_row, tm=blk["inproj_ctx"], rope=False)

    cw = conv_w[l]
    cb = conv_b[l][None, :]
    xs_c, bc_c = _conv(u, U_XS, U_BC, cw, cb, t, blk["conv"])
    xsc_c, bcc_c = _conv(uc, UC_XS, UC_BC, cw, cb, tc, blk["conv_ctx"])
    alog = _pad_heads(a_log[l])
    bias = _pad_heads(dt_bias[l])
    e_mat = _head_expand_matrix2()
    h0 = _ctx_state(xsc_c, bcc_c, dtc2, alog, bias, _tri_pair3(tc), e_mat, bsz, tc)
    dsk = jnp.repeat(d_skip[l].astype(F32), SSD_HEAD_DIM)[None, :]
    y_ssd = _ssd(xs_c, bc_c, dt2, alog, bias, _tri_pair3(SSD_CHUNK), e_mat, h0, u, dsk,
                 ssd_norm[l][None, :], bsz, t)

    y_attn = _attn(attn_sink[l].astype(F32), u, uc, bsz, t, tc)

    merged = _merge(y_ssd, y_attn, u, w_branch_ssd[l].astype(BF16), w_branch_attn[l].astype(BF16),
                    blk["merge"])

    tm_o = blk["outproj"]
    w_r = jnp.zeros((d, 2 * LANES), F32)
    w_r = w_r.at[:, :MOE_GROUPS].set(w_route_group[l])
    w_r = w_r.at[:, LANES:LANES + N_EXPERTS].set(w_route_expert[l]).astype(BF16)
    li = np.arange(tm_o)
    tri_strict = jnp.asarray((li[:, None] > li[None, :]).astype(np.float32), BF16)
    x1, h2, ridx, rw, cnt = _outproj(merged, w_out[l].astype(BF16), x.reshape(m, d), mods,
                                        norm_ffn[l][None, :], w_r, tri_strict, tm_o, t)

    na = m * TOP_K
    nblk = -(-(na + N_EXPERTS * (MOE_BLOCK - 1)) // MOE_BLOCK)
    counts = cnt[0, :N_EXPERTS].astype(I32)
    padded = (counts + MOE_BLOCK - 1) // MOE_BLOCK * MOE_BLOCK
    pend = jnp.cumsum(padded)
    pstart = pend - padded
    blk_start = jnp.arange(nblk, dtype=I32) * MOE_BLOCK
    blk_e = jnp.clip(jnp.sum((pend[None, :] <= blk_start[:, None]).astype(I32), axis=1),
                     0, N_EXPERTS - 1)
    n_used = (pend[N_EXPERTS - 1:] // MOE_BLOCK).astype(I32)
    meta = jnp.concatenate([pstart + counts, pend, n_used]).astype(I32)
    e_sel = ridx[0:TOP_K]
    seg0 = jnp.zeros_like(e_sel)
    for e in range(N_EXPERTS):
        seg0 = jnp.where(e_sel == e, pstart[e], seg0)
    pos = (seg0 + ridx[TOP_K:2 * TOP_K]).reshape(-1)

    eo = _experts(blk_e, pos, meta, h2, w_gate[l], w_up[l], w_down[l], nblk)

    out = _combine(pos, eo, x1, rw, mods, norm_final[None, :], blk["combine"], t)
    return out.reshape(bsz, t, d)
```

```python
import functools

import numpy as np
import jax
import jax.numpy as jnp
from jax import lax
from jax.experimental import pallas as pl
from jax.experimental.pallas import tpu as pltpu

F32 = jnp.float32
BF16 = jnp.bfloat16
I32 = jnp.int32

D_MODEL = 2048
GRID_W = 64
NORM_EPS = 1e-6
N_MOD = 6
SSD_HEADS = 16
SSD_HEAD_DIM = 64
D_SSD = SSD_HEADS * SSD_HEAD_DIM
SSD_GROUPS = 2
SSD_STATE = 128
SSD_CONV = 5
SSD_CHUNK = 128
CONV_CH = D_SSD + 2 * SSD_GROUPS * SSD_STATE
ATTN_HEADS = 8
ATTN_KV_HEADS = 2
ATTN_HEAD_DIM = 128
D_ATTN = ATTN_HEADS * ATTN_HEAD_DIM
D_KV = ATTN_KV_HEADS * ATTN_HEAD_DIM
WINDOW = 128
ROPE_BASE = 10000.0
N_BRANCH = 2
MOE_GROUPS = 4
EXPERTS_PER_GROUP = 8
N_EXPERTS = MOE_GROUPS * EXPERTS_PER_GROUP
TOP_K = 2
D_EXPERT = 512
MOE_BLOCK = 256

LANES = 128
TOK_PACK = D_MODEL // (2 * LANES)
TOK_PITCH = TOK_PACK + 1
VMEM_LIMIT = 56 * 1024 * 1024

U_XS, U_Z, U_Q, U_BC, U_K, U_V, U_G = 0, 1024, 2048, 3072, 3584, 3840, 4096
U_W = 8192
UC_XS, UC_BC, UC_K, UC_V = 0, 1024, 1536, 1792
UC_W = 2048
IN_TN = 1024


def _params(sem):
    return pltpu.CompilerParams(dimension_semantics=sem, vmem_limit_bytes=VMEM_LIMIT)


def _silu(v):
    return v * jax.nn.sigmoid(v)


def _split_bf16(v, n):
    parts = []
    r = v
    for _ in range(n):
        p = r.astype(BF16)
        parts.append(p)
        r = r - p.astype(F32)
    return parts


def _dot(a, b):
    return jnp.dot(a, b, preferred_element_type=F32)


def _dot_nt(a, b):
    return lax.dot_general(a, b, (((1,), (1,)), ((), ())), preferred_element_type=F32)


def _ada_kernel(c_ref, w_ref, b_ref, o_ref):
    a = _silu(c_ref[...]).astype(BF16)
    o_ref[...] = _dot(a, w_ref[...].astype(BF16)) + b_ref[...]


def _ada(cc, w, b):
    rows, d = cc.shape
    n = w.shape[1]
    tn = 1024
    return pl.pallas_call(
        _ada_kernel,
        grid=(n // tn,),
        in_specs=[pl.BlockSpec((rows, d), lambda j: (0, 0)),
                  pl.BlockSpec((d, tn), lambda j: (0, j)),
                  pl.BlockSpec((1, tn), lambda j: (0, j))],
        out_specs=pl.BlockSpec((rows, tn), lambda j: (0, j)),
        out_shape=jax.ShapeDtypeStruct((rows, n), F32),
        compiler_params=_params(("arbitrary",)),
        name="ada",
    )(cc, w, b)


def _rope(a, cos, sin_signed, first):
    partner = jnp.where(first, pltpu.roll(a, 96, 1), pltpu.roll(a, 32, 1))
    return a * cos + partner * sin_signed


def _inproj_kernel(*refs, rope_heads, sub):
    if rope_heads:
        x_ref, nw_ref, mod_ref, w_ref, wdt_ref, cos_ref, sin_ref, o_ref, dt_ref, h_scr = refs
    else:
        x_ref, nw_ref, mod_ref, w_ref, wdt_ref, o_ref, dt_ref, h_scr = refs
    j = pl.program_id(1)
    tm = x_ref.shape[0]
    tn = w_ref.shape[1]

    n_split = 2 if tm >= 512 else 1
    part = tm // n_split

    def normalize(sp):
        nw = nw_ref[...]
        sh = mod_ref[0, 0:1, :]
        sc = mod_ref[0, 1:2, :]
        for r in range(sp * part, (sp + 1) * part, sub):
            rows = slice(r, r + sub)
            xf = x_ref[rows, :]
            ms = jnp.mean(xf * xf, axis=-1, keepdims=True)
            hb = ((xf * lax.rsqrt(ms + NORM_EPS) * nw) * (1.0 + sc) + sh).astype(BF16)
            h_scr[rows, :] = hb
            d = _dot(hb, wdt_ref[...])
            dt_ref[0, rows, :] = d[:, :LANES]
            dt_ref[1, rows, :] = d[:, LANES:]

    def store(heads, first_block=False):
        if first_block:
            normalize(0)
        for sp in range(n_split):
            rows = slice(sp * part, (sp + 1) * part)
            acc = _dot(h_scr[rows, :], w_ref[...])
            if first_block and sp + 1 < n_split:
                normalize(sp + 1)
            if heads:
                cos = cos_ref[rows, :]
                sin = sin_ref[rows, :]
                lane = lax.broadcasted_iota(I32, (part, LANES), 1)
                first = (lane % 64) < 32
            for hh in range(tn // LANES):
                a = acc[:, hh * LANES:(hh + 1) * LANES]
                if hh in heads:
                    a = _rope(a, cos, sin, first)
                o_ref[rows, hh * LANES:(hh + 1) * LANES] = a.astype(BF16)

    assert all(blk != 0 for blk, _ in rope_heads)
    pl.when(j == 0)(functools.partial(store, (), True))
    plain = j != 0
    for blk, heads in rope_heads:
        pl.when(j == blk)(functools.partial(store, heads))
        plain = plain & (j != blk)
    pl.when(plain)(functools.partial(store, ()))


def _inproj(x2, nw, mods, w, wdt, cos_t, sin_t, *, rows_per_mod, mod_row0, tm, rope):
    m, d = x2.shape
    n = w.shape[1]
    tn = IN_TN
    nbm = rows_per_mod // tm if rows_per_mod else 0
    if rows_per_mod:
        mod_map = lambda i, j: (mod_row0 + i // nbm, 0, 0)
    else:
        mod_map = lambda i, j: (mod_row0, 0, 0)
    in_specs = [pl.BlockSpec((tm, d), lambda i, j: (i, 0)),
                pl.BlockSpec((1, d), lambda i, j: (0, 0)),
                pl.BlockSpec((1, N_MOD, d), mod_map),
                pl.BlockSpec((d, tn), lambda i, j: (0, j)),
                pl.BlockSpec((d, 2 * LANES), lambda i, j: (0, 0))]
    args = [x2, nw, mods, w, wdt]
    if rope:
        in_specs += [pl.BlockSpec((tm, LANES), lambda i, j: (i % nbm, 0)),
                     pl.BlockSpec((tm, LANES), lambda i, j: (i % nbm, 0))]
        args += [cos_t, sin_t]
        rotary = lambda col: U_Q <= col < U_Q + D_ATTN or U_K <= col < U_K + D_KV
        rope_heads = tuple(
            (blk, tuple(hh for hh in range(tn // LANES) if rotary(blk * tn + hh * LANES)))
            for blk in range(n // tn)
            if any(rotary(blk * tn + hh * LANES) for hh in range(tn // LANES)))
    else:
        rope_heads = ()
    kern = functools.partial(_inproj_kernel, rope_heads=rope_heads, sub=128)
    return pl.pallas_call(
        kern,
        grid=(m // tm, n // tn),
        in_specs=in_specs,
        out_specs=[pl.BlockSpec((tm, tn), lambda i, j: (i, j)),
                   pl.BlockSpec((2, tm, LANES), lambda i, j: (0, i, 0))],
        out_shape=[jax.ShapeDtypeStruct((m, n), BF16),
                   jax.ShapeDtypeStruct((2, m, LANES), F32)],
        scratch_shapes=[pltpu.VMEM((tm, d), BF16)],
        compiler_params=_params(("arbitrary", "arbitrary")),
        name="inproj_rope" if rope else "inproj_ctx",
    )(*args)


CONV_HALO = 16


def _conv_kernel(xs_ref, bc_ref, pxs_ref, pbc_ref, nxs_ref, nbc_ref, w_ref, b_ref,
                 oxs_ref, obc_ref, ext, *, blocks_per_seq):
    i = pl.program_id(0)
    r = xs_ref.shape[0]
    pos = i % blocks_per_seq
    not_first = pos != 0
    not_last = pos != blocks_per_seq - 1
    h = CONV_HALO
    ext[0:h, 0:D_SSD] = jnp.where(not_first, pxs_ref[...].astype(F32), 0.0)
    ext[0:h, D_SSD:] = jnp.where(not_first, pbc_ref[...].astype(F32), 0.0)
    ext[h:h + r, 0:D_SSD] = xs_ref[...].astype(F32)
    ext[h:h + r, D_SSD:] = bc_ref[...].astype(F32)
    ext[h + r:, 0:D_SSD] = jnp.where(not_last, nxs_ref[...].astype(F32), 0.0)
    ext[h + r:, D_SSD:] = jnp.where(not_last, nbc_ref[...].astype(F32), 0.0)
    cw = 256
    pad = SSD_CONV // 2
    for cb in range(0, CONV_CH, cw):
        acc = jnp.broadcast_to(b_ref[:, cb:cb + cw], (r, cw))
        for k in range(SSD_CONV):
            acc = acc + ext[h - pad + k:h - pad + k + r, cb:cb + cw] * w_ref[k:k + 1, cb:cb + cw]
        y = _silu(acc)
        if cb < D_SSD:
            oxs_ref[:, cb:cb + cw] = y
        else:
            obc_ref[:, cb - D_SSD:cb - D_SSD + cw] = y.astype(BF16)


def _conv(u, xs_col, bc_col, conv_w, conv_b, seq_len, r):
    m = u.shape[0]
    h = CONV_HALO
    bps = seq_len // r
    wbc = CONV_CH - D_SSD
    xs_c, bc_c = xs_col // D_SSD, bc_col // wbc
    nh = m // h
    prev = lambda i: jnp.maximum(i * (r // h) - 1, 0)
    nxt = lambda i: jnp.minimum((i + 1) * (r // h), nh - 1)
    return pl.pallas_call(
        functools.partial(_conv_kernel, blocks_per_seq=bps),
        grid=(m // r,),
        in_specs=[pl.BlockSpec((r, D_SSD), lambda i: (i, xs_c)),
                  pl.BlockSpec((r, wbc), lambda i: (i, bc_c)),
                  pl.BlockSpec((h, D_SSD), lambda i: (prev(i), xs_c)),
                  pl.BlockSpec((h, wbc), lambda i: (prev(i), bc_c)),
                  pl.BlockSpec((h, D_SSD), lambda i: (nxt(i), xs_c)),
                  pl.BlockSpec((h, wbc), lambda i: (nxt(i), bc_c)),
                  pl.BlockSpec((SSD_CONV, CONV_CH), lambda i: (0, 0)),
                  pl.BlockSpec((1, CONV_CH), lambda i: (0, 0))],
        out_specs=[pl.BlockSpec((r, D_SSD), lambda i: (i, 0)),
                   pl.BlockSpec((r, wbc), lambda i: (i, 0))],
        out_shape=[jax.ShapeDtypeStruct((m, D_SSD), F32),
                   jax.ShapeDtypeStruct((m, wbc), BF16)],
        scratch_shapes=[pltpu.VMEM((r + 2 * h, CONV_CH), F32)],
        compiler_params=_params(("arbitrary",)),
        name="conv_silu",
    )(u, u, u, u, u, u, conv_w, conv_b)


def _expand_heads(v, e2):
    return _dot(jnp.concatenate(_split_bf16(v, 2), axis=1), e2)


def _cumsum_mm(tri3, da):
    return _dot(tri3, jnp.concatenate(_split_bf16(da, 3), axis=0))


def _ctx_state_kernel(xs_ref, b_ref, dt_ref, alog_ref, bias_ref, tri_ref, e_ref, h_ref):
    d = pl.program_id(1)
    tc = xs_ref.shape[0]
    dt = jax.nn.softplus(dt_ref[0] + bias_ref[0])
    a = -jnp.exp(alog_ref[0])
    cum = _cumsum_mm(tri_ref[0], dt * a)
    cum_end = jnp.where(d == 0, cum[tc - 1:tc, :], cum[0:1, :])
    wx = _expand_heads(dt * jnp.exp(cum_end - cum), e_ref[...])
    xw = (xs_ref[...] * wx).astype(BF16)
    gw = D_SSD // SSD_GROUPS
    for g in range(SSD_GROUPS):
        bgt = b_ref[:, g * SSD_STATE:(g + 1) * SSD_STATE].astype(F32).T.astype(BF16)
        h_ref[0, 0, :, g * gw:(g + 1) * gw] = _dot(bgt, xw[:, g * gw:(g + 1) * gw])


def _ctx_state(xs_c, bc_c, dt2, alog, bias, tri3, e2, bsz, tc):
    return pl.pallas_call(
        _ctx_state_kernel,
        grid=(bsz, 2),
        in_specs=[pl.BlockSpec((tc, D_SSD), lambda b, d: (b, 0)),
                  pl.BlockSpec((tc, SSD_GROUPS * SSD_STATE), lambda b, d: (b, 0)),
                  pl.BlockSpec((1, tc, LANES), lambda b, d: (d, b, 0)),
                  pl.BlockSpec((1, 1, LANES), lambda b, d: (d, 0, 0)),
                  pl.BlockSpec((1, 1, LANES), lambda b, d: (d, 0, 0)),
                  pl.BlockSpec((1, tc, 3 * tc), lambda b, d: (d, 0, 0)),
                  pl.BlockSpec((2 * LANES, D_SSD), lambda b, d: (0, 0))],
        out_specs=pl.BlockSpec((1, 1, SSD_STATE, D_SSD), lambda b, d: (b, d, 0, 0)),
        out_shape=jax.ShapeDtypeStruct((bsz, 2, SSD_STATE, D_SSD), F32),
        compiler_params=_params(("arbitrary", "arbitrary")),
        name="ssd_ctx_state",
    )(xs_c, bc_c, dt2, alog, bias, tri3, e2)


class _SsdChunk:
    GW = D_SSD // SSD_GROUPS
    HPG = SSD_HEADS // SSD_GROUPS

    def __init__(self, fwd, xs, bc_ref, dt_raw, alog, bias, tri3, e2, st, y_out):
        lc = SSD_CHUNK
        self.bc_ref, self.st, self.y_out = bc_ref, st, y_out
        dt = jax.nn.softplus(dt_raw + bias)
        self.cum = _cumsum_mm(tri3, dt * (-jnp.exp(alog)))
        self.cum_t = (self.cum - jnp.log(dt)).T
        end = lc - 1 if fwd else 0
        self.ecx = _expand_heads(jnp.exp(self.cum), e2)
        wx = _expand_heads(dt * jnp.exp(self.cum[end:end + 1, :] - self.cum), e2)
        self.xs_b = xs.astype(BF16)
        self.xw_b = (xs * wx).astype(BF16)
        self.cd_row = self.ecx[end:end + 1, :]
        li = lax.broadcasted_iota(I32, (lc, lc), 0)
        si = lax.broadcasted_iota(I32, (lc, lc), 1)
        self.causal = (li >= si) if fwd else (li <= si)
        self.low = lax.broadcasted_iota(I32, (lc, LANES), 1) < SSD_HEAD_DIM

    def group_head(self, g):
        cols = slice(g * self.GW, (g + 1) * self.GW)
        self.bg = self.bc_ref[:, g * SSD_STATE:(g + 1) * SSD_STATE]
        cg = self.bc_ref[:, (SSD_GROUPS + g) * SSD_STATE:(SSD_GROUPS + g + 1) * SSD_STATE]
        self.cb = _dot_nt(cg, self.bg)
        self.yoff = _dot(cg, self.st[:, cols].astype(BF16)) * self.ecx[:, cols]

    def head_pair(self, g, pr):
        h0 = g * self.HPG + 2 * pr
        ms = []
        for h in (h0, h0 + 1):
            seg = self.cum[:, h:h + 1] - self.cum_t[h:h + 1, :]
            ms.append((self.cb * jnp.exp(jnp.where(self.causal, seg, -jnp.inf))).astype(BF16))
        xp = self.xs_b[:, h0 * SSD_HEAD_DIM:(h0 + 2) * SSD_HEAD_DIM]
        zero = jnp.zeros_like(xp)
        rhs = jnp.concatenate([jnp.where(self.low, xp, zero), jnp.where(self.low, zero, xp)], axis=0)
        self.y_out[:, h0 * SSD_HEAD_DIM:(h0 + 2) * SSD_HEAD_DIM] = (
            _dot(jnp.concatenate(ms, axis=1), rhs) + self.yoff[:, pr * LANES:(pr + 1) * LANES])

    def group_tail(self, g):
        cols = slice(g * self.GW, (g + 1) * self.GW)
        bgt = self.bg.astype(F32).T.astype(BF16)
        self.st[:, cols] = self.st[:, cols] * self.cd_row[:, cols] + _dot(bgt, self.xw_b[:, cols])


def _ssd_kernel(xsf_ref, xsb_ref, bcf_ref, bcb_ref, dtf_ref, dtb_ref, alog_ref, bias_ref, tri_ref,
                e_ref, h0_ref, z_ref, dsk_ref, nw_ref, o_ref, st, ybuf, ycur, *, nc):
    s = pl.program_id(1)
    lc = SSD_CHUNK

    @pl.when(s == 0)
    def _():
        st[...] = h0_ref[0]

    e2 = e_ref[...]
    scans = (_SsdChunk(True, xsf_ref[...], bcf_ref, dtf_ref[0], alog_ref[0], bias_ref[0], tri_ref[0],
                       e2, st.at[0], ycur.at[0]),
             _SsdChunk(False, xsb_ref[...], bcb_ref, dtb_ref[0], alog_ref[1], bias_ref[1], tri_ref[1],
                       e2, st.at[1], ycur.at[1]))
    for g in range(SSD_GROUPS):
        for sc in scans:
            sc.group_head(g)
        for pr in range(_SsdChunk.HPG // 2):
            for sc in scans:
                sc.head_pair(g, pr)
        for sc in scans:
            sc.group_tail(g)
    rows = (pl.ds(pl.multiple_of(s * lc, lc), lc), pl.ds(pl.multiple_of((nc - 1 - s) * lc, lc), lc))

    @pl.when(s < nc // 2)
    def _():
        for d in range(2):
            ybuf[rows[d], :] = ycur[d]

    @pl.when(s >= nc // 2)
    def _():
        gw = D_SSD // SSD_GROUPS
        for d, xs_ref in enumerate((xsf_ref, xsb_ref)):
            y = ycur[d] + ybuf[rows[d], :] + dsk_ref[...] * xs_ref[...]
            gy = y * _silu(z_ref[rows[d], :].astype(F32))
            for g in range(SSD_GROUPS):
                blk = gy[:, g * gw:(g + 1) * gw]
                ms = jnp.mean(blk * blk, axis=-1, keepdims=True)
                o_ref[rows[d], g * gw:(g + 1) * gw] = (blk * lax.rsqrt(ms + NORM_EPS)
                                                       * nw_ref[:, g * gw:(g + 1) * gw]).astype(BF16)


def _ssd(xs_c, bc_c, dt2, alog, bias, tri3, e2, h0, u, dsk, nw, bsz, t):
    nc = t // SSD_CHUNK
    lc = SSD_CHUNK
    m = bsz * t
    fw = lambda b, s: b * nc + s
    bw = lambda b, s: b * nc + nc - 1 - s
    zc = U_Z // D_SSD
    bcw = 2 * SSD_GROUPS * SSD_STATE
    return pl.pallas_call(
        functools.partial(_ssd_kernel, nc=nc),
        grid=(bsz, nc),
        in_specs=[pl.BlockSpec((lc, D_SSD), lambda b, s: (fw(b, s), 0)),
                  pl.BlockSpec((lc, D_SSD), lambda b, s: (bw(b, s), 0)),
                  pl.BlockSpec((lc, bcw), lambda b, s: (fw(b, s), 0)),
                  pl.BlockSpec((lc, bcw), lambda b, s: (bw(b, s), 0)),
                  pl.BlockSpec((1, lc, LANES), lambda b, s: (0, fw(b, s), 0)),
                  pl.BlockSpec((1, lc, LANES), lambda b, s: (1, bw(b, s), 0)),
                  pl.BlockSpec((2, 1, LANES), lambda b, s: (0, 0, 0)),
                  pl.BlockSpec((2, 1, LANES), lambda b, s: (0, 0, 0)),
                  pl.BlockSpec((2, lc, 3 * lc), lambda b, s: (0, 0, 0)),
                  pl.BlockSpec((2 * LANES, D_SSD), lambda b, s: (0, 0)),
                  pl.BlockSpec((1, 2, SSD_STATE, D_SSD), lambda b, s: (b, 0, 0, 0)),
                  pl.BlockSpec((t, D_SSD), lambda b, s: (b, zc)),
                  pl.BlockSpec((1, D_SSD), lambda b, s: (0, 0)),
                  pl.BlockSpec((1, D_SSD), lambda b, s: (0, 0))],
        out_specs=pl.BlockSpec((t, D_SSD), lambda b, s: (b, 0)),
        out_shape=jax.ShapeDtypeStruct((m, D_SSD), BF16),
        scratch_shapes=[pltpu.VMEM((2, SSD_STATE, D_SSD), F32),
                        pltpu.VMEM((t, D_SSD), F32),
                        pltpu.VMEM((2, lc, D_SSD), F32)],
        compiler_params=_params(("arbitrary", "arbitrary")),
        name="ssd_scan",
    )(xs_c, xs_c, bc_c, bc_c, dt2, dt2, alog, bias, tri3, e2, h0, u, dsk, nw)


def _attn_kernel(sink_ref, q_ref, k_ref, v_ref, kc_ref, vc_ref, o_ref, kp, vp, bias, sbuf, *, t):
    g = pl.program_id(1)
    w = WINDOW
    nb = t // w
    rep = ATTN_HEADS // ATTN_KV_HEADS
    tc = kc_ref.shape[0]
    nk = 3 * w + tc
    zeros = jnp.zeros((w, ATTN_HEAD_DIM), BF16)
    kp[0:w, :] = zeros
    kp[w:w + t, :] = k_ref[...]
    kp[w + t:, :] = zeros
    vp[0:w, :] = zeros
    vp[w:w + t, :] = v_ref[...]
    vp[w + t:, :] = zeros
    qi = lax.broadcasted_iota(I32, (rep * w, w), 0) % w
    kj = lax.broadcasted_iota(I32, (rep * w, w), 1)
    bias[0] = jnp.where(kj >= qi, 0.0, -jnp.inf).astype(F32)
    bias[1] = jnp.where(kj <= qi, 0.0, -jnp.inf).astype(F32)
    sink_col = jnp.concatenate(
        [jnp.full((w, 1), sink_ref[g * rep + h], F32) for h in range(rep)], axis=0)
    scale = ATTN_HEAD_DIM ** -0.5
    log2e = float(np.log2(np.e))

    def scores(n):
        rows = pl.ds(pl.multiple_of(n * w, w), w)
        q = jnp.concatenate([q_ref[rows, h * ATTN_HEAD_DIM:(h + 1) * ATTN_HEAD_DIM]
                             for h in range(rep)], axis=0)
        kall = jnp.concatenate([kp[pl.ds(pl.multiple_of(n * w, w), 3 * w), :], kc_ref[...]], axis=0)
        raw = _dot_nt(q, kall)
        prev = raw[:, 0:w] + (bias[0] + jnp.where(n == 0, -jnp.inf, 0.0))
        nxt = raw[:, 2 * w:3 * w] + (bias[1] + jnp.where(n == nb - 1, -jnp.inf, 0.0))
        return jnp.concatenate([prev, raw[:, w:2 * w], nxt, raw[:, 3 * w:]], axis=1)

    def finish(n, sc):
        rows = pl.ds(pl.multiple_of(n * w, w), w)
        vall = jnp.concatenate([vp[pl.ds(pl.multiple_of(n * w, w), 3 * w), :], vc_ref[...]], axis=0)
        mx = jnp.maximum(jnp.max(sc, axis=-1, keepdims=True) * scale, sink_col)
        p = jnp.exp2(sc * (scale * log2e) - mx * log2e)
        den = jnp.sum(p, axis=-1, keepdims=True) + jnp.exp2((sink_col - mx) * log2e)
        o = _dot(p.astype(BF16), vall) / den
        for h in range(rep):
            o_ref[rows, h * ATTN_HEAD_DIM:(h + 1) * ATTN_HEAD_DIM] = o[h * w:(h + 1) * w, :].astype(BF16)

    sbuf[0] = scores(0)

    def body(j, carry):
        n0 = 2 * j
        sbuf[1] = scores(n0 + 1)
        finish(n0, sbuf[0])
        sbuf[0] = scores(jnp.minimum(n0 + 2, nb - 1))
        finish(n0 + 1, sbuf[1])
        return carry

    lax.fori_loop(0, nb // 2, body, 0)


def _attn(sink, u, uc, bsz, t, tc):
    rep = ATTN_HEADS // ATTN_KV_HEADS
    qw = rep * ATTN_HEAD_DIM
    hd = ATTN_HEAD_DIM
    return pl.pallas_call(
        functools.partial(_attn_kernel, t=t),
        grid=(bsz, ATTN_KV_HEADS),
        in_specs=[pl.BlockSpec(memory_space=pltpu.SMEM),
                  pl.BlockSpec((t, qw), lambda b, g: (b, U_Q // qw + g)),
                  pl.BlockSpec((t, hd), lambda b, g: (b, U_K // hd + g)),
                  pl.BlockSpec((t, hd), lambda b, g: (b, U_V // hd + g)),
                  pl.BlockSpec((tc, hd), lambda b, g: (b, UC_K // hd + g)),
                  pl.BlockSpec((tc, hd), lambda b, g: (b, UC_V // hd + g))],
        out_specs=pl.BlockSpec((t, qw), lambda b, g: (b, g)),
        out_shape=jax.ShapeDtypeStruct((bsz * t, D_ATTN), BF16),
        scratch_shapes=[pltpu.VMEM((t + 2 * WINDOW, hd), BF16),
                        pltpu.VMEM((t + 2 * WINDOW, hd), BF16),
                        pltpu.VMEM((2, rep * WINDOW, WINDOW), F32),
                        pltpu.VMEM((2, rep * WINDOW, 3 * WINDOW + tc), F32)],
        compiler_params=_params(("arbitrary", "arbitrary")),
        name="window_attn",
    )(sink, u, u, u, uc, uc)


def _merge_kernel(ys_ref, ya_ref, g0_ref, g1_ref, ws_ref, wa_ref, o_ref):
    ys = ys_ref[...]
    ya = ya_ref[...]
    cw = 512
    for cb in range(0, D_MODEL, cw):
        t0 = _dot(ys, ws_ref[:, cb:cb + cw])
        t1 = _dot(ya, wa_ref[:, cb:cb + cw])
        g0 = jax.nn.sigmoid(g0_ref[:, cb:cb + cw].astype(F32))
        g1 = jax.nn.sigmoid(g1_ref[:, cb:cb + cw].astype(F32))
        o_ref[:, cb:cb + cw] = (g0 * t0 + g1 * t1).astype(BF16)


def _merge(ys, ya, u, ws, wa, tm):
    m = ys.shape[0]
    gc = U_G // D_MODEL
    return pl.pallas_call(
        _merge_kernel,
        grid=(m // tm,),
        in_specs=[pl.BlockSpec((tm, D_SSD), lambda i: (i, 0)),
                  pl.BlockSpec((tm, D_ATTN), lambda i: (i, 0)),
                  pl.BlockSpec((tm, D_MODEL), lambda i: (i, gc)),
                  pl.BlockSpec((tm, D_MODEL), lambda i: (i, gc + 1)),
                  pl.BlockSpec((D_SSD, D_MODEL), lambda i: (0, 0), pipeline_mode=pl.Buffered(1)),
                  pl.BlockSpec((D_ATTN, D_MODEL), lambda i: (0, 0), pipeline_mode=pl.Buffered(1))],
        out_specs=pl.BlockSpec((tm, D_MODEL), lambda i: (i, 0)),
        out_shape=jax.ShapeDtypeStruct((m, D_MODEL), BF16),
        compiler_params=_params(("arbitrary",)),
        name="branch_merge",
    )(ys, ya, u, u, ws, wa)


def _lane_min_index(mask, lane):
    return jnp.min(jnp.where(mask, lane, LANES), axis=-1, keepdims=True)


def _outproj_kernel(mg_ref, wo_ref, x_ref, mod_ref, nw_ref, wr_ref, tri_ref,
                    x1_ref, h2_ref, ridx_ref, rw_ref, cnt_ref, carry, mixbuf):
    i = pl.program_id(0)
    tm = x_ref.shape[0]

    @pl.when(i == 0)
    def _():
        mixbuf[1] = jnp.zeros(mixbuf.shape[1:], F32)

    @pl.when(i <= 1)
    def _():
        carry[...] = jnp.zeros_like(carry)

    g1 = mod_ref[0, 2:3, :]
    sh2 = mod_ref[0, 3:4, :]
    sc2 = mod_ref[0, 4:5, :]
    cw = 2 * LANES
    dw = D_MODEL // 4

    def step(cur):
        def project(j):
            mixbuf[cur, :, j * dw:(j + 1) * dw] = _dot(mg_ref[...], wo_ref[:, j * dw:(j + 1) * dw])

        ssq = jnp.zeros((tm, 1), F32)
        for c in range(TOK_PACK):
            cols = slice(c * cw, (c + 1) * cw)
            x1c = x_ref[:, cols] + g1[:, cols] * mixbuf[1 - cur, :, cols]
            x1_ref[:, cols] = x1c
            ssq = ssq + jnp.sum(x1c * x1c, axis=-1, keepdims=True)
            if c == 1:
                project(0)
            if c == 5:
                project(1)
        rinv = lax.rsqrt(ssq / D_MODEL + NORM_EPS)
        hbs = []
        for c in range(TOK_PACK):
            cols = slice(c * cw, (c + 1) * cw)
            h2c = (x1_ref[:, cols] * rinv * nw_ref[:, cols]) * (1.0 + sc2[:, cols]) + sh2[:, cols]
            hb = h2c.astype(BF16)
            hbs.append(hb)
            bits = pltpu.bitcast(hb.astype(F32), jnp.uint32)
            h2_ref[pl.ds(c, tm, stride=TOK_PACK), :] = (
                (bits[:, LANES:] & jnp.uint32(0xFFFF0000)) | (bits[:, :LANES] >> 16))
            if c == 3:
                project(2)
        lg = _dot(jnp.concatenate(hbs, axis=1), wr_ref[...])
        project(3)
        _route(lg, tri_ref, ridx_ref, rw_ref, cnt_ref, carry)

    for cur in range(2):
        pl.when(i % 2 == cur)(functools.partial(step, cur))


def _route(lg, tri_ref, ridx_ref, rw_ref, cnt_ref, carry):
    tm = tri_ref.shape[0]
    lane = lax.broadcasted_iota(I32, (tm, LANES), 1)
    gl = jnp.where(lane < MOE_GROUPS, lg[:, :LANES], -jnp.inf)
    gmax = jnp.max(gl, axis=-1, keepdims=True)
    g_w = 1.0 / jnp.sum(jnp.exp(gl - gmax), axis=-1, keepdims=True)
    g_idx = _lane_min_index(gl == gmax, lane)
    el = jnp.where((lane // EXPERTS_PER_GROUP) == g_idx, lg[:, LANES:], -jnp.inf)
    v1 = jnp.max(el, axis=-1, keepdims=True)
    i1 = _lane_min_index(el == v1, lane)
    el2 = jnp.where(lane == i1, -jnp.inf, el)
    v2 = jnp.max(el2, axis=-1, keepdims=True)
    i2 = _lane_min_index(el2 == v2, lane)
    e2 = jnp.exp(v2 - v1)
    w1 = g_w * (1.0 / (1.0 + e2))
    w2 = g_w * (e2 / (1.0 + e2))

    hit1 = lane == i1
    hit2 = lane == i2
    onehot = jnp.where(hit1 | hit2, 1.0, 0.0).astype(F32)
    before = _dot(tri_ref[...], onehot.astype(BF16)) + carry[0:1, :]
    r1 = jnp.sum(jnp.where(hit1, before, 0.0), axis=-1, keepdims=True)
    r2 = jnp.sum(jnp.where(hit2, before, 0.0), axis=-1, keepdims=True)
    total = carry[0:1, :] + jnp.sum(onehot, axis=0, keepdims=True)
    carry[...] = jnp.broadcast_to(total, carry.shape)
    cnt_ref[...] = jnp.broadcast_to(total, cnt_ref.shape)
    packed = jnp.where(lane == 0, i1.astype(F32), jnp.where(lane == 1, i2.astype(F32), jnp.where(
        lane == 2, r1, jnp.where(lane == 3, r2, 0.0))))
    ridx_ref[...] = packed.T[0:8, :].astype(I32)
    rw_ref[...] = jnp.where(lane == 0, w1, jnp.where(lane == 1, w2, 0.0))


def _outproj(mg, wo, x2, mods, nw, wr, tri, tm, t):
    m = x2.shape[0]
    nbm = t // tm
    nsteps = m // tm
    mm = lambda i: jnp.minimum(i, nsteps - 1)
    ep = lambda i: jnp.maximum(i - 1, 0)
    return pl.pallas_call(
        _outproj_kernel,
        grid=(nsteps + 1,),
        in_specs=[pl.BlockSpec((tm, D_MODEL), lambda i: (mm(i), 0)),
                  pl.BlockSpec((D_MODEL, D_MODEL), lambda i: (0, 0), pipeline_mode=pl.Buffered(1)),
                  pl.BlockSpec((tm, D_MODEL), lambda i: (ep(i), 0)),
                  pl.BlockSpec((1, N_MOD, D_MODEL), lambda i: (ep(i) // nbm, 0, 0)),
                  pl.BlockSpec((1, D_MODEL), lambda i: (0, 0)),
                  pl.BlockSpec((D_MODEL, 2 * LANES), lambda i: (0, 0)),
                  pl.BlockSpec((tm, tm), lambda i: (0, 0))],
        out_specs=[pl.BlockSpec((tm, D_MODEL), lambda i: (ep(i), 0)),
                   pl.BlockSpec((tm * TOK_PACK, LANES), lambda i: (ep(i), 0)),
                   pl.BlockSpec((8, tm), lambda i: (0, ep(i))),
                   pl.BlockSpec((tm, LANES), lambda i: (ep(i), 0)),
                   pl.BlockSpec((8, LANES), lambda i: (0, 0))],
        out_shape=[jax.ShapeDtypeStruct((m, D_MODEL), F32),
                   jax.ShapeDtypeStruct((m * TOK_PACK, LANES), jnp.uint32),
                   jax.ShapeDtypeStruct((8, m), I32),
                   jax.ShapeDtypeStruct((m, LANES), F32),
                   jax.ShapeDtypeStruct((8, LANES), F32)],
        scratch_shapes=[pltpu.VMEM((8, LANES), F32),
                        pltpu.VMEM((2, tm, D_MODEL), F32)],
        compiler_params=_params(("arbitrary",)),
        name="outproj_router",
    )(mg, wo, x2, mods, nw, wr, tri)


def _gather_rows(idx_ref, base, n, src, dst, sem, row0=0):
    for r in range(n):
        tok = idx_ref[base + r]
        pltpu.make_async_copy(src.at[pl.ds(tok, 1), :], dst.at[pl.ds(row0 + r, 1), :],
                              sem).start(priority=r % 2)


def _gather_slabs(idx_ref, base, n, src, dst, sem, row0=0):
    for r in range(n):
        tok = idx_ref[base + r]
        pltpu.make_async_copy(src.at[pl.ds(pl.multiple_of(tok * TOK_PACK, TOK_PACK), TOK_PACK), :],
                              dst.at[pl.ds((row0 + r) * TOK_PITCH, TOK_PACK), :],
                              sem).start(priority=r % 2)


def _expert_kernel(blk_e_ref, pos_ref, meta_ref, h2_hbm, wg_ref, wu_ref, wd_ref, o_ref,
                   xb, tok_ref, wgb, wub, wdb, sem):
    i = pl.program_id(0)
    n_used = meta_ref[2 * N_EXPERTS]
    slot = i % 2
    rows = MOE_BLOCK
    m = pos_ref.shape[0] // TOP_K

    def wait_block(sl):
        pltpu.make_async_copy(h2_hbm.at[pl.ds(0, rows * TOK_PACK), :],
                              xb.at[sl, pl.ds(0, rows * TOK_PACK), :], sem.at[sl]).wait()

    @pl.when(i == 0)
    def _():
        def pad_expert(e, carry):
            def pad_slot(s, c):
                tok_ref[s] = 0
                return c
            return lax.fori_loop(meta_ref[e], meta_ref[N_EXPERTS + e], pad_slot, carry)

        lax.fori_loop(0, N_EXPERTS, pad_expert, 0)

        def place(t, carry):
            for k in range(TOP_K):
                tok_ref[pos_ref[k * m + t]] = t
            return carry

        lax.fori_loop(0, m, place, 0, unroll=8)
        _gather_slabs(tok_ref, 0, rows, h2_hbm, xb.at[0], sem.at[0])

    @pl.when(i < n_used)
    def _():
        wait_block(slot)
        nxt = jnp.minimum(i + 1, n_used - 1)
        n_pass = 2
        per = rows // (n_pass * TOK_PACK)

        def read_block(ps):
            xs = []
            for a in range(TOK_PACK):
                pair = xb[slot, pl.ds(a, rows, stride=TOK_PITCH), :]
                xs.append(pltpu.bitcast(pair << 16, F32).astype(BF16))
                xs.append(pltpu.bitcast(pair & jnp.uint32(0xFFFF0000), F32).astype(BF16))
                g = ps * TOK_PACK + a
                _gather_slabs(tok_ref, nxt * rows + g * per, per, h2_hbm, xb.at[1 - slot],
                              sem.at[1 - slot], row0=g * per)
            return jnp.concatenate(xs, axis=1)

        prev_e = blk_e_ref[jnp.maximum(i - 1, 0)]

        @pl.when((i == 0) | (blk_e_ref[i] != prev_e))
        def _():
            wgb[...] = wg_ref[0].astype(BF16)
            wub[...] = wu_ref[0].astype(BF16)
            wdb[...] = wd_ref[0].astype(BF16)

        gate = _dot(read_block(0), wgb[...])
        act = (_silu(gate) * _dot(read_block(1), wub[...])).astype(BF16)
        o_ref[...] = _dot(act, wdb[...])

        @pl.when(i == n_used - 1)
        def _():
            wait_block(1 - slot)

    @pl.when(i >= n_used)
    def _():
        o_ref[...] = jnp.zeros_like(o_ref)


def _experts(blk_e, pos, meta, h2, wg, wu, wd, nblk):
    rows = MOE_BLOCK
    live = lambda i, mt: jnp.minimum(i, mt[2 * N_EXPERTS] - 1)
    grid_spec = pltpu.PrefetchScalarGridSpec(
        num_scalar_prefetch=3,
        grid=(nblk,),
        in_specs=[pl.BlockSpec(memory_space=pl.ANY),
                  pl.BlockSpec((1, D_MODEL, D_EXPERT), lambda i, be, ps, mt: (be[live(i, mt)], 0, 0)),
                  pl.BlockSpec((1, D_MODEL, D_EXPERT), lambda i, be, ps, mt: (be[live(i, mt)], 0, 0)),
                  pl.BlockSpec((1, D_EXPERT, D_MODEL), lambda i, be, ps, mt: (be[live(i, mt)], 0, 0))],
        out_specs=pl.BlockSpec((rows, D_MODEL), lambda i, be, ps, mt: (i, 0)),
        scratch_shapes=[pltpu.VMEM((2, rows * TOK_PITCH, LANES), jnp.uint32),
                        pltpu.SMEM((nblk * rows,), I32),
                        pltpu.VMEM((D_MODEL, D_EXPERT), BF16),
                        pltpu.VMEM((D_MODEL, D_EXPERT), BF16),
                        pltpu.VMEM((D_EXPERT, D_MODEL), BF16),
                        pltpu.SemaphoreType.DMA((2,))],
    )
    return pl.pallas_call(
        _expert_kernel,
        grid_spec=grid_spec,
        out_shape=jax.ShapeDtypeStruct((nblk * rows, D_MODEL), F32),
        compiler_params=_params(("arbitrary",)),
        name="expert_mlp",
    )(blk_e, pos, meta, h2, wg, wu, wd)


def _combine_kernel(pos_ref, eo_hbm, x1_ref, rw_ref, mod_ref, nf_ref, o_ref, gb, sem):
    i = pl.program_id(0)
    nsteps = pl.num_programs(0)
    tm = x1_ref.shape[0]
    slot = i % 2

    def issue(step, sl):
        for k in range(TOP_K):
            _gather_rows(pos_ref, (k * nsteps + step) * tm, tm, eo_hbm, gb.at[sl, k], sem.at[sl])

    @pl.when(i == 0)
    def _():
        issue(0, 0)

    @pl.when(i + 1 < nsteps)
    def _():
        issue(i + 1, 1 - slot)

    for k in range(TOP_K):
        pltpu.make_async_copy(eo_hbm.at[pl.ds(0, tm), :], gb.at[slot, k], sem.at[slot]).wait()
    w1 = rw_ref[:, 0:1]
    w2 = rw_ref[:, 1:2]
    g2 = mod_ref[0, 5:6, :]
    x2 = x1_ref[...] + g2 * (gb[slot, 0] * w1 + gb[slot, 1] * w2)
    ms = jnp.mean(x2 * x2, axis=-1, keepdims=True)
    o_ref[...] = x2 * lax.rsqrt(ms + NORM_EPS) * nf_ref[...]


def _combine(pos, eo, x1, rw, mods, nf, tm, t):
    m = x1.shape[0]
    nbm = t // tm
    grid_spec = pltpu.PrefetchScalarGridSpec(
        num_scalar_prefetch=1,
        grid=(m // tm,),
        in_specs=[pl.BlockSpec(memory_space=pl.ANY),
                  pl.BlockSpec((tm, D_MODEL), lambda i, p: (i, 0)),
                  pl.BlockSpec((tm, LANES), lambda i, p: (i, 0)),
                  pl.BlockSpec((1, N_MOD, D_MODEL), lambda i, p: (i // nbm, 0, 0)),
                  pl.BlockSpec((1, D_MODEL), lambda i, p: (0, 0))],
        out_specs=pl.BlockSpec((tm, D_MODEL), lambda i, p: (i, 0)),
        scratch_shapes=[pltpu.VMEM((2, TOP_K, tm, D_MODEL), F32),
                        pltpu.SemaphoreType.DMA((2,))],
    )
    return pl.pallas_call(
        _combine_kernel,
        grid_spec=grid_spec,
        out_shape=jax.ShapeDtypeStruct((m, D_MODEL), F32),
        compiler_params=_params(("arbitrary",)),
        name="moe_combine_norm",
    )(pos, eo, x1, rw, mods, nf)


def _rope_tables(t):
    half = ATTN_HEAD_DIM // 4
    f32 = np.float32
    pos = np.arange(t)
    row = (pos // GRID_W).astype(f32)
    col = (pos % GRID_W).astype(f32)
    freqs = (f32(ROPE_BASE) ** (-np.arange(half, dtype=f32) / f32(half))).astype(f32)
    ar = row[:, None] * freqs[None, :]
    ac = col[:, None] * freqs[None, :]
    cos_t = np.concatenate([np.cos(ar), np.cos(ar), np.cos(ac), np.cos(ac)], axis=1)
    sin_t = np.concatenate([-np.sin(ar), np.sin(ar), -np.sin(ac), np.sin(ac)], axis=1)
    return jnp.asarray(cos_t, F32), jnp.asarray(sin_t, F32)


def _tri_pair3(n):
    li = np.arange(n)[:, None]
    si = np.arange(n)[None, :]
    pair = np.stack([li >= si, li <= si]).astype(np.float32)
    return jnp.asarray(np.concatenate([pair] * 3, axis=2), BF16)


def _head_expand_matrix2():
    k = np.arange(LANES)[:, None]
    j = np.arange(D_SSD)[None, :]
    e = (k == j // SSD_HEAD_DIM).astype(np.float32)
    return jnp.asarray(np.concatenate([e, e], axis=0), BF16)


def _pad_heads(v):
    return jnp.pad(v.astype(F32), ((0, 0), (0, LANES - SSD_HEADS)))[:, None, :]


def _row_blocks(t, mc, tc):
    return dict(inproj=min(1024, t), inproj_ctx=min(1024, mc), conv=min(512, t), conv_ctx=min(256, tc),
                merge=min(1024, t), outproj=min(512, t), combine=min(256, t))


def kernel(x, c, ctx, c_ctx, w_ada, b_ada, norm_mix, norm_ffn, w_in, conv_w, conv_b, a_log, dt_bias,
           d_skip, ssd_norm, attn_sink, w_branch_ssd, w_branch_attn, w_out, w_route_group,
           w_route_expert, w_gate, w_up, w_down, norm_final):
    bsz, t, d = x.shape
    tc = ctx.shape[1]
    m = bsz * t
    l = 0

    ctx_row = bsz
    n_rows = -(-(bsz + 1) // 8) * 8
    cc = jnp.zeros((n_rows, d), F32).at[:bsz].set(c).at[ctx_row].set(c_ctx)
    mods = _ada(cc, w_ada[l], b_ada[l][None, :]).reshape(n_rows, N_MOD, d)

    wi = w_in[l]
    o_xbc = D_SSD
    o_dt = o_xbc + CONV_CH
    o_q = o_dt + 2 * SSD_HEADS
    o_k = o_q + D_ATTN
    o_v = o_k + D_KV
    o_g = o_v + D_KV
    w_xs = wi[:, o_xbc:o_xbc + D_SSD]
    w_bc = wi[:, o_xbc + D_SSD:o_dt]
    w_k = wi[:, o_k:o_v]
    w_v = wi[:, o_v:o_g]
    w_lat = jnp.concatenate([w_xs, wi[:, :D_SSD], wi[:, o_q:o_k], w_bc, w_k, w_v, wi[:, o_g:]],
                            axis=1).astype(BF16)
    w_ctx = jnp.concatenate([w_xs, w_bc, w_k, w_v], axis=1).astype(BF16)
    w_dt = jnp.zeros((d, 2 * LANES), F32)
    w_dt = w_dt.at[:, :SSD_HEADS].set(wi[:, o_dt:o_dt + SSD_HEADS])
    w_dt = w_dt.at[:, LANES:LANES + SSD_HEADS].set(wi[:, o_dt + SSD_HEADS:o_q]).astype(BF16)

    cos_t, sin_t = _rope_tables(t)
    nmix = norm_mix[l][None, :]
    mc = bsz * tc
    blk = _row_blocks(t, mc, tc)
    u, dt2 = _inproj(x.reshape(m, d), nmix, mods, w_lat, w_dt, cos_t, sin_t,
                     rows_per_mod=t, mod_row0=0, tm=blk["inproj"], rope=True)
    uc, dtc2 = _inproj(ctx.reshape(mc, d), nmix, mods, w_ctx, w_dt, None, None,
                       rows_per_mod=0, mod_row0=ctx_row, tm=blk["inproj_ctx"], rope=False)

    cw = conv_w[l]
    cb = conv_b[l][None, :]
    xs_c, bc_c = _conv(u, U_XS, U_BC, cw, cb, t, blk["conv"])
    xsc_c, bcc_c = _conv(uc, UC_XS, UC_BC, cw, cb, tc, blk["conv_ctx"])
    alog = _pad_heads(a_log[l])
    bias = _pad_heads(dt_bias[l])
    e_mat = _head_expand_matrix2()
    h0 = _ctx_state(xsc_c, bcc_c, dtc2, alog, bias, _tri_pair3(tc), e_mat, bsz, tc)
    dsk = jnp.repeat(d_skip[l].astype(F32), SSD_HEAD_DIM)[None, :]
    y_ssd = _ssd(xs_c, bc_c, dt2, alog, bias, _tri_pair3(SSD_CHUNK), e_mat, h0, u, dsk,
                 ssd_norm[l][None, :], bsz, t)

    y_attn = _attn(attn_sink[l].astype(F32), u, uc, bsz, t, tc)

    merged = _merge(y_ssd, y_attn, u, w_branch_ssd[l].astype(BF16), w_branch_attn[l].astype(BF16),
                    blk["merge"])

    tm_o = blk["outproj"]
    w_r = jnp.zeros((d, 2 * LANES), F32)
    w_r = w_r.at[:, :MOE_GROUPS].set(w_route_group[l])
    w_r = w_r.at[:, LANES:LANES + N_EXPERTS].set(w_route_expert[l]).astype(BF16)
    li = np.arange(tm_o)
    tri_strict = jnp.asarray((li[:, None] > li[None, :]).astype(np.float32), BF16)
    x1, h2, ridx, rw, cnt = _outproj(merged, w_out[l].astype(BF16), x.reshape(m, d), mods,
                                        norm_ffn[l][None, :], w_r, tri_strict, tm_o, t)

    na = m * TOP_K
    nblk = -(-(na + N_EXPERTS * (MOE_BLOCK - 1)) // MOE_BLOCK)
    counts = cnt[0, :N_EXPERTS].astype(I32)
    padded = (counts + MOE_BLOCK - 1) // MOE_BLOCK * MOE_BLOCK
    pend = jnp.cumsum(padded)
    pstart = pend - padded
    blk_start = jnp.arange(nblk, dtype=I32) * MOE_BLOCK
    blk_e = jnp.clip(jnp.sum((pend[None, :] <= blk_start[:, None]).astype(I32), axis=1),
                     0, N_EXPERTS - 1)
    n_used = (pend[N_EXPERTS - 1:] // MOE_BLOCK).astype(I32)
    meta = jnp.concatenate([pstart + counts, pend, n_used]).astype(I32)
    e_sel = ridx[0:TOP_K]
    seg0 = jnp.zeros_like(e_sel)
    for e in range(N_EXPERTS):
        seg0 = jnp.where(e_sel == e, pstart[e], seg0)
    pos = (seg0 + ridx[TOP_K:2 * TOP_K]).reshape(-1)

    eo = _experts(blk_e, pos, meta, h2, w_gate[l], w_up[l], w_down[l], nblk)

    out = _combine(pos, eo, x1, rw, mods, norm_final[None, :], blk["combine"], t)
    return out.reshape(bsz, t, d)
```

```python
import functools

import numpy as np
import jax
import jax.numpy as jnp
from jax import lax
from jax.experimental import pallas as pl
from jax.experimental.pallas import tpu as pltpu

F32 = jnp.float32
BF16 = jnp.bfloat16
I32 = jnp.int32

D_MODEL = 2048
GRID_W = 64
NORM_EPS = 1e-6
N_MOD = 6
SSD_HEADS = 16
SSD_HEAD_DIM = 64
D_SSD = SSD_HEADS * SSD_HEAD_DIM
SSD_GROUPS = 2
SSD_STATE = 128
SSD_CONV = 5
SSD_CHUNK = 128
CONV_CH = D_SSD + 2 * SSD_GROUPS * SSD_STATE
ATTN_HEADS = 8
ATTN_KV_HEADS = 2
ATTN_HEAD_DIM = 128
D_ATTN = ATTN_HEADS * ATTN_HEAD_DIM
D_KV = ATTN_KV_HEADS * ATTN_HEAD_DIM
WINDOW = 128
ROPE_BASE = 10000.0
N_BRANCH = 2
MOE_GROUPS = 4
EXPERTS_PER_GROUP = 8
N_EXPERTS = MOE_GROUPS * EXPERTS_PER_GROUP
TOP_K = 2
D_EXPERT = 512
MOE_BLOCK = 256

LANES = 128
TOK_PACK = D_MODEL // (2 * LANES)
TOK_PITCH = TOK_PACK + 1
VMEM_LIMIT = 56 * 1024 * 1024

U_XS, U_Z, U_Q, U_BC, U_K, U_V, U_G = 0, 1024, 2048, 3072, 3584, 3840, 4096
U_W = 8192
UC_XS, UC_BC, UC_K, UC_V = 0, 1024, 1536, 1792
UC_W = 2048
IN_TN = 1024


def _params(sem):
    return pltpu.CompilerParams(dimension_semantics=sem, vmem_limit_bytes=VMEM_LIMIT)


def _silu(v):
    return v * jax.nn.sigmoid(v)


def _split_bf16(v, n):
    parts = []
    r = v
    for _ in range(n):
        p = r.astype(BF16)
        parts.append(p)
        r = r - p.astype(F32)
    return parts


def _dot(a, b):
    return jnp.dot(a, b, preferred_element_type=F32)


def _dot_nt(a, b):
    return lax.dot_general(a, b, (((1,), (1,)), ((), ())), preferred_element_type=F32)


def _ada_kernel(c_ref, w_ref, b_ref, o_ref):
    a = _silu(c_ref[...]).astype(BF16)
    o_ref[...] = _dot(a, w_ref[...].astype(BF16)) + b_ref[...]


def _ada(cc, w, b):
    rows, d = cc.shape
    n = w.shape[1]
    tn = 1024
    return pl.pallas_call(
        _ada_kernel,
        grid=(n // tn,),
        in_specs=[pl.BlockSpec((rows, d), lambda j: (0, 0)),
                  pl.BlockSpec((d, tn), lambda j: (0, j)),
                  pl.BlockSpec((1, tn), lambda j: (0, j))],
        out_specs=pl.BlockSpec((rows, tn), lambda j: (0, j)),
        out_shape=jax.ShapeDtypeStruct((rows, n), F32),
        compiler_params=_params(("arbitrary",)),
        name="ada",
    )(cc, w, b)


def _rope(a, cos, sin_signed, first):
    partner = jnp.where(first, pltpu.roll(a, 96, 1), pltpu.roll(a, 32, 1))
    return a * cos + partner * sin_signed


def _inproj_kernel(*refs, rope_heads, sub):
    if rope_heads:
        x_ref, nw_ref, mod_ref, w_ref, wdt_ref, cos_ref, sin_ref, o_ref, dt_ref, h_scr = refs
    else:
        x_ref, nw_ref, mod_ref, w_ref, wdt_ref, o_ref, dt_ref, h_scr = refs
    j = pl.program_id(1)
    tm = x_ref.shape[0]
    tn = w_ref.shape[1]

    n_split = 2 if tm >= 512 else 1
    part = tm // n_split

    def normalize(sp):
        nw = nw_ref[...]
        sh = mod_ref[0, 0:1, :]
        sc = mod_ref[0, 1:2, :]
        for r in range(sp * part, (sp + 1) * part, sub):
            rows = slice(r, r + sub)
            xf = x_ref[rows, :]
            ms = jnp.mean(xf * xf, axis=-1, keepdims=True)
            hb = ((xf * lax.rsqrt(ms + NORM_EPS) * nw) * (1.0 + sc) + sh).astype(BF16)
            h_scr[rows, :] = hb
            d = _dot(hb, wdt_ref[...])
            dt_ref[0, rows, :] = d[:, :LANES]
            dt_ref[1, rows, :] = d[:, LANES:]

    def store(heads, first_block=False):
        if first_block:
            normalize(0)
        for sp in range(n_split):
            rows = slice(sp * part, (sp + 1) * part)
            acc = _dot(h_scr[rows, :], w_ref[...])
            if first_block and sp + 1 < n_split:
                normalize(sp + 1)
            if heads:
                cos = cos_ref[rows, :]
                sin = sin_ref[rows, :]
                lane = lax.broadcasted_iota(I32, (part, LANES), 1)
                first = (lane % 64) < 32
            for hh in range(tn // LANES):
                a = acc[:, hh * LANES:(hh + 1) * LANES]
                if hh in heads:
                    a = _rope(a, cos, sin, first)
                o_ref[rows, hh * LANES:(hh + 1) * LANES] = a.astype(BF16)

    assert all(blk != 0 for blk, _ in rope_heads)
    pl.when(j == 0)(functools.partial(store, (), True))
    plain = j != 0
    for blk, heads in rope_heads:
        pl.when(j == blk)(functools.partial(store, heads))
        plain = plain & (j != blk)
    pl.when(plain)(functools.partial(store, ()))


def _inproj(x2, nw, mods, w, wdt, cos_t, sin_t, *, rows_per_mod, mod_row0, tm, rope):
    m, d = x2.shape
    n = w.shape[1]
    tn = IN_TN
    nbm = rows_per_mod // tm if rows_per_mod else 0
    if rows_per_mod:
        mod_map = lambda i, j: (mod_row0 + i // nbm, 0, 0)
    else:
        mod_map = lambda i, j: (mod_row0, 0, 0)
    in_specs = [pl.BlockSpec((tm, d), lambda i, j: (i, 0)),
                pl.BlockSpec((1, d), lambda i, j: (0, 0)),
                pl.BlockSpec((1, N_MOD, d), mod_map),
                pl.BlockSpec((d, tn), lambda i, j: (0, j)),
                pl.BlockSpec((d, 2 * LANES), lambda i, j: (0, 0))]
    args = [x2, nw, mods, w, wdt]
    if rope:
        in_specs += [pl.BlockSpec((tm, LANES), lambda i, j: (i % nbm, 0)),
                     pl.BlockSpec((tm, LANES), lambda i, j: (i % nbm, 0))]
        args += [cos_t, sin_t]
        rotary = lambda col: U_Q <= col < U_Q + D_ATTN or U_K <= col < U_K + D_KV
        rope_heads = tuple(
            (blk, tuple(hh for hh in range(tn // LANES) if rotary(blk * tn + hh * LANES)))
            for blk in range(n // tn)
            if any(rotary(blk * tn + hh * LANES) for hh in range(tn // LANES)))
    else:
        rope_heads = ()
    kern = functools.partial(_inproj_kernel, rope_heads=rope_heads, sub=128)
    return pl.pallas_call(
        kern,
        grid=(m // tm, n // tn),
        in_specs=in_specs,
        out_specs=[pl.BlockSpec((tm, tn), lambda i, j: (i, j)),
                   pl.BlockSpec((2, tm, LANES), lambda i, j: (0, i, 0))],
        out_shape=[jax.ShapeDtypeStruct((m, n), BF16),
                   jax.ShapeDtypeStruct((2, m, LANES), F32)],
        scratch_shapes=[pltpu.VMEM((tm, d), BF16)],
        compiler_params=_params(("arbitrary", "arbitrary")),
        name="inproj_rope" if rope else "inproj_ctx",
    )(*args)


CONV_HALO = 16


def _conv_kernel(xs_ref, bc_ref, pxs_ref, pbc_ref, nxs_ref, nbc_ref, w_ref, b_ref,
                 oxs_ref, obc_ref, ext, *, blocks_per_seq):
    i = pl.program_id(0)
    r = xs_ref.shape[0]
    pos = i % blocks_per_seq
    not_first = pos != 0
    not_last = pos != blocks_per_seq - 1
    h = CONV_HALO
    ext[0:h, 0:D_SSD] = jnp.where(not_first, pxs_ref[...].astype(F32), 0.0)
    ext[0:h, D_SSD:] = jnp.where(not_first, pbc_ref[...].astype(F32), 0.0)
    ext[h:h + r, 0:D_SSD] = xs_ref[...].astype(F32)
    ext[h:h + r, D_SSD:] = bc_ref[...].astype(F32)
    ext[h + r:, 0:D_SSD] = jnp.where(not_last, nxs_ref[...].astype(F32), 0.0)
    ext[h + r:, D_SSD:] = jnp.where(not_last, nbc_ref[...].astype(F32), 0.0)
    cw = 256
    pad = SSD_CONV // 2
    for cb in range(0, CONV_CH, cw):
        acc = jnp.broadcast_to(b_ref[:, cb:cb + cw], (r, cw))
        for k in range(SSD_CONV):
            acc = acc + ext[h - pad + k:h - pad + k + r, cb:cb + cw] * w_ref[k:k + 1, cb:cb + cw]
        y = _silu(acc)
        if cb < D_SSD:
            oxs_ref[:, cb:cb + cw] = y
        else:
            obc_ref[:, cb - D_SSD:cb - D_SSD + cw] = y.astype(BF16)


def _conv(u, xs_col, bc_col, conv_w, conv_b, seq_len, r):
    m = u.shape[0]
    h = CONV_HALO
    bps = seq_len // r
    wbc = CONV_CH - D_SSD
    xs_c, bc_c = xs_col // D_SSD, bc_col // wbc
    nh = m // h
    prev = lambda i: jnp.maximum(i * (r // h) - 1, 0)
    nxt = lambda i: jnp.minimum((i + 1) * (r // h), nh - 1)
    return pl.pallas_call(
        functools.partial(_conv_kernel, blocks_per_seq=bps),
        grid=(m // r,),
        in_specs=[pl.BlockSpec((r, D_SSD), lambda i: (i, xs_c)),
                  pl.BlockSpec((r, wbc), lambda i: (i, bc_c)),
                  pl.BlockSpec((h, D_SSD), lambda i: (prev(i), xs_c)),
                  pl.BlockSpec((h, wbc), lambda i: (prev(i), bc_c)),
                  pl.BlockSpec((h, D_SSD), lambda i: (nxt(i), xs_c)),
                  pl.BlockSpec((h, wbc), lambda i: (nxt(i), bc_c)),
                  pl.BlockSpec((SSD_CONV, CONV_CH), lambda i: (0, 0)),
                  pl.BlockSpec((1, CONV_CH), lambda i: (0, 0))],
        out_specs=[pl.BlockSpec((r, D_SSD), lambda i: (i, 0)),
                   pl.BlockSpec((r, wbc), lambda i: (i, 0))],
        out_shape=[jax.ShapeDtypeStruct((m, D_SSD), F32),
                   jax.ShapeDtypeStruct((m, wbc), BF16)],
        scratch_shapes=[pltpu.VMEM((r + 2 * h, CONV_CH), F32)],
        compiler_params=_params(("arbitrary",)),
        name="conv_silu",
    )(u, u, u, u, u, u, conv_w, conv_b)


def _expand_heads(v, e2):
    return _dot(jnp.concatenate(_split_bf16(v, 2), axis=1), e2)


def _cumsum_mm(tri3, da):
    return _dot(tri3, jnp.concatenate(_split_bf16(da, 3), axis=0))


def _ctx_state_kernel(xs_ref, b_ref, dt_ref, alog_ref, bias_ref, tri_ref, e_ref, h_ref):
    d = pl.program_id(1)
    tc = xs_ref.shape[0]
    dt = jax.nn.softplus(dt_ref[0] + bias_ref[0])
    a = -jnp.exp(alog_ref[0])
    cum = _cumsum_mm(tri_ref[0], dt * a)
    cum_end = jnp.where(d == 0, cum[tc - 1:tc, :], cum[0:1, :])
    wx = _expand_heads(dt * jnp.exp(cum_end - cum), e_ref[...])
    xw = (xs_ref[...] * wx).astype(BF16)
    gw = D_SSD // SSD_GROUPS
    for g in range(SSD_GROUPS):
        bgt = b_ref[:, g * SSD_STATE:(g + 1) * SSD_STATE].astype(F32).T.astype(BF16)
        h_ref[0, 0, :, g * gw:(g + 1) * gw] = _dot(bgt, xw[:, g * gw:(g + 1) * gw])


def _ctx_state(xs_c, bc_c, dt2, alog, bias, tri3, e2, bsz, tc):
    return pl.pallas_call(
        _ctx_state_kernel,
        grid=(bsz, 2),
        in_specs=[pl.BlockSpec((tc, D_SSD), lambda b, d: (b, 0)),
                  pl.BlockSpec((tc, SSD_GROUPS * SSD_STATE), lambda b, d: (b, 0)),
                  pl.BlockSpec((1, tc, LANES), lambda b, d: (d, b, 0)),
                  pl.BlockSpec((1, 1, LANES), lambda b, d: (d, 0, 0)),
                  pl.BlockSpec((1, 1, LANES), lambda b, d: (d, 0, 0)),
                  pl.BlockSpec((1, tc, 3 * tc), lambda b, d: (d, 0, 0)),
                  pl.BlockSpec((2 * LANES, D_SSD), lambda b, d: (0, 0))],
        out_specs=pl.BlockSpec((1, 1, SSD_STATE, D_SSD), lambda b, d: (b, d, 0, 0)),
        out_shape=jax.ShapeDtypeStruct((bsz, 2, SSD_STATE, D_SSD), F32),
        compiler_params=_params(("arbitrary", "arbitrary")),
        name="ssd_ctx_state",
    )(xs_c, bc_c, dt2, alog, bias, tri3, e2)


class _SsdChunk:
    GW = D_SSD // SSD_GROUPS
    HPG = SSD_HEADS // SSD_GROUPS

    def __init__(self, fwd, xs, bc_ref, dt_raw, alog, bias, tri3, e2, st, y_out):
        lc = SSD_CHUNK
        self.bc_ref, self.st, self.y_out = bc_ref, st, y_out
        dt = jax.nn.softplus(dt_raw + bias)
        self.cum = _cumsum_mm(tri3, dt * (-jnp.exp(alog)))
        self.cum_t = (self.cum - jnp.log(dt)).T
        end = lc - 1 if fwd else 0
        self.ecx = _expand_heads(jnp.exp(self.cum), e2)
        wx = _expand_heads(dt * jnp.exp(self.cum[end:end + 1, :] - self.cum), e2)
        self.xs_b = xs.astype(BF16)
        self.xw_b = (xs * wx).astype(BF16)
        self.cd_row = self.ecx[end:end + 1, :]
        li = lax.broadcasted_iota(I32, (lc, lc), 0)
        si = lax.broadcasted_iota(I32, (lc, lc), 1)
        self.causal = (li >= si) if fwd else (li <= si)
        self.low = lax.broadcasted_iota(I32, (lc, LANES), 1) < SSD_HEAD_DIM

    def group_head(self, g):
        cols = slice(g * self.GW, (g + 1) * self.GW)
        self.bg = self.bc_ref[:, g * SSD_STATE:(g + 1) * SSD_STATE]
        cg = self.bc_ref[:, (SSD_GROUPS + g) * SSD_STATE:(SSD_GROUPS + g + 1) * SSD_STATE]
        self.cb = _dot_nt(cg, self.bg)
        self.yoff = _dot(cg, self.st[:, cols].astype(BF16)) * self.ecx[:, cols]

    def head_pair(self, g, pr):
        h0 = g * self.HPG + 2 * pr
        ms = []
        for h in (h0, h0 + 1):
            seg = self.cum[:, h:h + 1] - self.cum_t[h:h + 1, :]
            ms.append((self.cb * jnp.exp(jnp.where(self.causal, seg, -jnp.inf))).astype(BF16))
        xp = self.xs_b[:, h0 * SSD_HEAD_DIM:(h0 + 2) * SSD_HEAD_DIM]
        zero = jnp.zeros_like(xp)
        rhs = jnp.concatenate([jnp.where(self.low, xp, zero), jnp.where(self.low, zero, xp)], axis=0)
        self.y_out[:, h0 * SSD_HEAD_DIM:(h0 + 2) * SSD_HEAD_DIM] = (
            _dot(jnp.concatenate(ms, axis=1), rhs) + self.yoff[:, pr * LANES:(pr + 1) * LANES])

    def group_tail(self, g):
        cols = slice(g * self.GW, (g + 1) * self.GW)
        bgt = self.bg.astype(F32).T.astype(BF16)
        self.st[:, cols] = self.st[:, cols] * self.cd_row[:, cols] + _dot(bgt, self.xw_b[:, cols])


def _ssd_kernel(xsf_ref, xsb_ref, bcf_ref, bcb_ref, dtf_ref, dtb_ref, alog_ref, bias_ref, tri_ref,
                e_ref, h0_ref, z_ref, dsk_ref, nw_ref, o_ref, st, ybuf, ycur, *, nc):
    s = pl.program_id(1)
    lc = SSD_CHUNK

    @pl.when(s == 0)
    def _():
        st[...] = h0_ref[0]

    e2 = e_ref[...]
    scans = (_SsdChunk(True, xsf_ref[...], bcf_ref, dtf_ref[0], alog_ref[0], bias_ref[0], tri_ref[0],
                       e2, st.at[0], ycur.at[0]),
             _SsdChunk(False, xsb_ref[...], bcb_ref, dtb_ref[0], alog_ref[1], bias_ref[1], tri_ref[1],
                       e2, st.at[1], ycur.at[1]))
    for g in range(SSD_GROUPS):
        for sc in scans:
            sc.group_head(g)
        for pr in range(_SsdChunk.HPG // 2):
            for sc in scans:
                sc.head_pair(g, pr)
        for sc in scans:
            sc.group_tail(g)
    rows = (pl.ds(pl.multiple_of(s * lc, lc), lc), pl.ds(pl.multiple_of((nc - 1 - s) * lc, lc), lc))

    @pl.when(s < nc // 2)
    def _():
        for d in range(2):
            ybuf[rows[d], :] = ycur[d]

    @pl.when(s >= nc // 2)
    def _():
        gw = D_SSD // SSD_GROUPS
        for d, xs_ref in enumerate((xsf_ref, xsb_ref)):
            y = ycur[d] + ybuf[rows[d], :] + dsk_ref[...] * xs_ref[...]
            gy = y * _silu(z_ref[rows[d], :].astype(F32))
            for g in range(SSD_GROUPS):
                blk = gy[:, g * gw:(g + 1) * gw]
                ms = jnp.mean(blk * blk, axis=-1, keepdims=True)
                o_ref[rows[d], g * gw:(g + 1) * gw] = (blk * lax.rsqrt(ms + NORM_EPS)
                                                       * nw_ref[:, g * gw:(g + 1) * gw]).astype(BF16)


def _ssd(xs_c, bc_c, dt2, alog, bias, tri3, e2, h0, u, dsk, nw, bsz, t):
    nc = t // SSD_CHUNK
    lc = SSD_CHUNK
    m = bsz * t
    fw = lambda b, s: b * nc + s
    bw = lambda b, s: b * nc + nc - 1 - s
    zc = U_Z // D_SSD
    bcw = 2 * SSD_GROUPS * SSD_STATE
    return pl.pallas_call(
        functools.partial(_ssd_kernel, nc=nc),
        grid=(bsz, nc),
        in_specs=[pl.BlockSpec((lc, D_SSD), lambda b, s: (fw(b, s), 0)),
                  pl.BlockSpec((lc, D_SSD), lambda b, s: (bw(b, s), 0)),
                  pl.BlockSpec((lc, bcw), lambda b, s: (fw(b, s), 0)),
                  pl.BlockSpec((lc, bcw), lambda b, s: (bw(b, s), 0)),
                  pl.BlockSpec((1, lc, LANES), lambda b, s: (0, fw(b, s), 0)),
                  pl.BlockSpec((1, lc, LANES), lambda b, s: (1, bw(b, s), 0)),
                  pl.BlockSpec((2, 1, LANES), lambda b, s: (0, 0, 0)),
                  pl.BlockSpec((2, 1, LANES), lambda b, s: (0, 0, 0)),
                  pl.BlockSpec((2, lc, 3 * lc), lambda b, s: (0, 0, 0)),
                  pl.BlockSpec((2 * LANES, D_SSD), lambda b, s: (0, 0)),
                  pl.BlockSpec((1, 2, SSD_STATE, D_SSD), lambda b, s: (b, 0, 0, 0)),
                  pl.BlockSpec((t, D_SSD), lambda b, s: (b, zc)),
                  pl.BlockSpec((1, D_SSD), lambda b, s: (0, 0)),
                  pl.BlockSpec((1, D_SSD), lambda b, s: (0, 0))],
        out_specs=pl.BlockSpec((t, D_SSD), lambda b, s: (b, 0)),
        out_shape=jax.ShapeDtypeStruct((m, D_SSD), BF16),
        scratch_shapes=[pltpu.VMEM((2, SSD_STATE, D_SSD), F32),
                        pltpu.VMEM((t, D_SSD), F32),
                        pltpu.VMEM((2, lc, D_SSD), F32)],
        compiler_params=_params(("arbitrary", "arbitrary")),
        name="ssd_scan",
    )(xs_c, xs_c, bc_c, bc_c, dt2, dt2, alog, bias, tri3, e2, h0, u, dsk, nw)


def _attn_kernel(sink_ref, q_ref, k_ref, v_ref, kc_ref, vc_ref, o_ref, kp, vp, bias, sbuf, *, t):
    g = pl.program_id(1)
    w = WINDOW
    nb = t // w
    rep = ATTN_HEADS // ATTN_KV_HEADS
    tc = kc_ref.shape[0]
    nk = 3 * w + tc
    zeros = jnp.zeros((w, ATTN_HEAD_DIM), BF16)
    kp[0:w, :] = zeros
    kp[w:w + t, :] = k_ref[...]
    kp[w + t:, :] = zeros
    vp[0:w, :] = zeros
    vp[w:w + t, :] = v_ref[...]
    vp[w + t:, :] = zeros
    qi = lax.broadcasted_iota(I32, (rep * w, w), 0) % w
    kj = lax.broadcasted_iota(I32, (rep * w, w), 1)
    bias[0] = jnp.where(kj >= qi, 0.0, -jnp.inf).astype(F32)
    bias[1] = jnp.where(kj <= qi, 0.0, -jnp.inf).astype(F32)
    sink_col = jnp.concatenate(
        [jnp.full((w, 1), sink_ref[g * rep + h], F32) for h in range(rep)], axis=0)
    scale = ATTN_HEAD_DIM ** -0.5
    log2e = float(np.log2(np.e))

    def scores(n):
        rows = pl.ds(pl.multiple_of(n * w, w), w)
        q = jnp.concatenate([q_ref[rows, h * ATTN_HEAD_DIM:(h + 1) * ATTN_HEAD_DIM]
                             for h in range(rep)], axis=0)
        kall = jnp.concatenate([kp[pl.ds(pl.multiple_of(n * w, w), 3 * w), :], kc_ref[...]], axis=0)
        raw = _dot_nt(q, kall)
        prev = raw[:, 0:w] + (bias[0] + jnp.where(n == 0, -jnp.inf, 0.0))
        nxt = raw[:, 2 * w:3 * w] + (bias[1] + jnp.where(n == nb - 1, -jnp.inf, 0.0))
        return jnp.concatenate([prev, raw[:, w:2 * w], nxt, raw[:, 3 * w:]], axis=1)

    def finish(n, sc):
        rows = pl.ds(pl.multiple_of(n * w, w), w)
        vall = jnp.concatenate([vp[pl.ds(pl.multiple_of(n * w, w), 3 * w), :], vc_ref[...]], axis=0)
        mx = jnp.maximum(jnp.max(sc, axis=-1, keepdims=True) * scale, sink_col)
        p = jnp.exp2(sc * (scale * log2e) - mx * log2e)
        den = jnp.sum(p, axis=-1, keepdims=True) + jnp.exp2((sink_col - mx) * log2e)
        o = _dot(p.astype(BF16), vall) / den
        for h in range(rep):
            o_ref[rows, h * ATTN_HEAD_DIM:(h + 1) * ATTN_HEAD_DIM] = o[h * w:(h + 1) * w, :].astype(BF16)

    sbuf[0] = scores(0)

    def body(j, carry):
        n0 = 2 * j
        sbuf[1] = scores(n0 + 1)
        finish(n0, sbuf[0])
        sbuf[0] = scores(jnp.minimum(n0 + 2, nb - 1))
        finish(n0 + 1, sbuf[1])
        return carry

    lax.fori_loop(0, nb // 2, body, 0)


def _attn(sink, u, uc, bsz, t, tc):
    rep = ATTN_HEADS // ATTN_KV_HEADS
    qw = rep * ATTN_HEAD_DIM
    hd = ATTN_HEAD_DIM
    return pl.pallas_call(
        functools.partial(_attn_kernel, t=t),
        grid=(bsz, ATTN_KV_HEADS),
        in_specs=[pl.BlockSpec(memory_space=pltpu.SMEM),
                  pl.BlockSpec((t, qw), lambda b, g: (b, U_Q // qw + g)),
                  pl.BlockSpec((t, hd), lambda b, g: (b, U_K // hd + g)),
                  pl.BlockSpec((t, hd), lambda b, g: (b, U_V // hd + g)),
                  pl.BlockSpec((tc, hd), lambda b, g: (b, UC_K // hd + g)),
                  pl.BlockSpec((tc, hd), lambda b, g: (b, UC_V // hd + g))],
        out_specs=pl.BlockSpec((t, qw), lambda b, g: (b, g)),
        out_shape=jax.ShapeDtypeStruct((bsz * t, D_ATTN), BF16),
        scratch_shapes=[pltpu.VMEM((t + 2 * WINDOW, hd), BF16),
                        pltpu.VMEM((t + 2 * WINDOW, hd), BF16),
                        pltpu.VMEM((2, rep * WINDOW, WINDOW), F32),
                        pltpu.VMEM((2, rep * WINDOW, 3 * WINDOW + tc), F32)],
        compiler_params=_params(("arbitrary", "arbitrary")),
        name="window_attn",
    )(sink, u, u, u, uc, uc)


def _merge_kernel(ys_ref, ya_ref, g0_ref, g1_ref, ws_ref, wa_ref, o_ref):
    ys = ys_ref[...]
    ya = ya_ref[...]
    cw = 512
    for cb in range(0, D_MODEL, cw):
        t0 = _dot(ys, ws_ref[:, cb:cb + cw])
        t1 = _dot(ya, wa_ref[:, cb:cb + cw])
        g0 = jax.nn.sigmoid(g0_ref[:, cb:cb + cw].astype(F32))
        g1 = jax.nn.sigmoid(g1_ref[:, cb:cb + cw].astype(F32))
        o_ref[:, cb:cb + cw] = (g0 * t0 + g1 * t1).astype(BF16)


def _merge(ys, ya, u, ws, wa, tm):
    m = ys.shape[0]
    gc = U_G // D_MODEL
    return pl.pallas_call(
        _merge_kernel,
        grid=(m // tm,),
        in_specs=[pl.BlockSpec((tm, D_SSD), lambda i: (i, 0)),
                  pl.BlockSpec((tm, D_ATTN), lambda i: (i, 0)),
                  pl.BlockSpec((tm, D_MODEL), lambda i: (i, gc)),
                  pl.BlockSpec((tm, D_MODEL), lambda i: (i, gc + 1)),
                  pl.BlockSpec((D_SSD, D_MODEL), lambda i: (0, 0), pipeline_mode=pl.Buffered(1)),
                  pl.BlockSpec((D_ATTN, D_MODEL), lambda i: (0, 0), pipeline_mode=pl.Buffered(1))],
        out_specs=pl.BlockSpec((tm, D_MODEL), lambda i: (i, 0)),
        out_shape=jax.ShapeDtypeStruct((m, D_MODEL), BF16),
        compiler_params=_params(("arbitrary",)),
        name="branch_merge",
    )(ys, ya, u, u, ws, wa)


def _lane_min_index(mask, lane):
    return jnp.min(jnp.where(mask, lane, LANES), axis=-1, keepdims=True)


def _outproj_kernel(mg_ref, wo_ref, x_ref, mod_ref, nw_ref, wr_ref, tri_ref,
                    x1_ref, h2_ref, ridx_ref, rw_ref, cnt_ref, carry, mixbuf):
    i = pl.program_id(0)
    tm = x_ref.shape[0]

    @pl.when(i == 0)
    def _():
        mixbuf[1] = jnp.zeros(mixbuf.shape[1:], F32)

    @pl.when(i <= 1)
    def _():
        carry[...] = jnp.zeros_like(carry)

    g1 = mod_ref[0, 2:3, :]
    sh2 = mod_ref[0, 3:4, :]
    sc2 = mod_ref[0, 4:5, :]
    cw = 2 * LANES
    dw = D_MODEL // 4

    def step(cur):
        def project(j):
            mixbuf[cur, :, j * dw:(j + 1) * dw] = _dot(mg_ref[...], wo_ref[:, j * dw:(j + 1) * dw])

        ssq = jnp.zeros((tm, 1), F32)
        for c in range(TOK_PACK):
            cols = slice(c * cw, (c + 1) * cw)
            x1c = x_ref[:, cols] + g1[:, cols] * mixbuf[1 - cur, :, cols]
            x1_ref[:, cols] = x1c
            ssq = ssq + jnp.sum(x1c * x1c, axis=-1, keepdims=True)
            if c == 1:
                project(0)
            if c == 5:
                project(1)
        rinv = lax.rsqrt(ssq / D_MODEL + NORM_EPS)
        hbs = []
        for c in range(TOK_PACK):
            cols = slice(c * cw, (c + 1) * cw)
            h2c = (x1_ref[:, cols] * rinv * nw_ref[:, cols]) * (1.0 + sc2[:, cols]) + sh2[:, cols]
            hb = h2c.astype(BF16)
            hbs.append(hb)
            bits = pltpu.bitcast(hb.astype(F32), jnp.uint32)
            h2_ref[pl.ds(c, tm, stride=TOK_PACK), :] = (
                (bits[:, LANES:] & jnp.uint32(0xFFFF0000)) | (bits[:, :LANES] >> 16))
            if c == 3:
                project(2)
        lg = _dot(jnp.concatenate(hbs, axis=1), wr_ref[...])
        project(3)
        _route(lg, tri_ref, ridx_ref, rw_ref, cnt_ref, carry)

    for cur in range(2):
        pl.when(i % 2 == cur)(functools.partial(step, cur))


def _route(lg, tri_ref, ridx_ref, rw_ref, cnt_ref, carry):
    tm = tri_ref.shape[0]
    lane = lax.broadcasted_iota(I32, (tm, LANES), 1)
    gl = jnp.where(lane < MOE_GROUPS, lg[:, :LANES], -jnp.inf)
    gmax = jnp.max(gl, axis=-1, keepdims=True)
    g_w = 1.0 / jnp.sum(jnp.exp(gl - gmax), axis=-1, keepdims=True)
    g_idx = _lane_min_index(gl == gmax, lane)
    el = jnp.where((lane // EXPERTS_PER_GROUP) == g_idx, lg[:, LANES:], -jnp.inf)
    v1 = jnp.max(el, axis=-1, keepdims=True)
    i1 = _lane_min_index(el == v1, lane)
    el2 = jnp.where(lane == i1, -jnp.inf, el)
    v2 = jnp.max(el2, axis=-1, keepdims=True)
    i2 = _lane_min_index(el2 == v2, lane)
    e2 = jnp.exp(v2 - v1)
    w1 = g_w * (1.0 / (1.0 + e2))
    w2 = g_w * (e2 / (1.0 + e2))

    hit1 = lane == i1
    hit2 = lane == i2
    onehot = jnp.where(hit1 | hit2, 1.0, 0.0).astype(F32)
    before = _dot(tri_ref[...], onehot.astype(BF16)) + carry[0:1, :]
    r1 = jnp.sum(jnp.where(hit1, before, 0.0), axis=-1, keepdims=True)
    r2 = jnp.sum(jnp.where(hit2, before, 0.0), axis=-1, keepdims=True)
    total = carry[0:1, :] + jnp.sum(onehot, axis=0, keepdims=True)
    carry[...] = jnp.broadcast_to(total, carry.shape)
    cnt_ref[...] = jnp.broadcast_to(total, cnt_ref.shape)
    packed = jnp.where(lane == 0, i1.astype(F32), jnp.where(lane == 1, i2.astype(F32), jnp.where(
        lane == 2, r1, jnp.where(lane == 3, r2, 0.0))))
    ridx_ref[...] = packed.T[0:8, :].astype(I32)
    rw_ref[...] = jnp.where(lane == 0, w1, jnp.where(lane == 1, w2, 0.0))


def _outproj(mg, wo, x2, mods, nw, wr, tri, tm, t):
    m = x2.shape[0]
    nbm = t // tm
    nsteps = m // tm
    mm = lambda i: jnp.minimum(i, nsteps - 1)
    ep = lambda i: jnp.maximum(i - 1, 0)
    return pl.pallas_call(
        _outproj_kernel,
        grid=(nsteps + 1,),
        in_specs=[pl.BlockSpec((tm, D_MODEL), lambda i: (mm(i), 0)),
                  pl.BlockSpec((D_MODEL, D_MODEL), lambda i: (0, 0), pipeline_mode=pl.Buffered(1)),
                  pl.BlockSpec((tm, D_MODEL), lambda i: (ep(i), 0)),
                  pl.BlockSpec((1, N_MOD, D_MODEL), lambda i: (ep(i) // nbm, 0, 0)),
                  pl.BlockSpec((1, D_MODEL), lambda i: (0, 0)),
                  pl.BlockSpec((D_MODEL, 2 * LANES), lambda i: (0, 0)),
                  pl.BlockSpec((tm, tm), lambda i: (0, 0))],
        out_specs=[pl.BlockSpec((tm, D_MODEL), lambda i: (ep(i), 0)),
                   pl.BlockSpec((tm * TOK_PACK, LANES), lambda i: (ep(i), 0)),
                   pl.BlockSpec((8, tm), lambda i: (0, ep(i))),
                   pl.BlockSpec((tm, LANES), lambda i: (ep(i), 0)),
                   pl.BlockSpec((8, LANES), lambda i: (0, 0))],
        out_shape=[jax.ShapeDtypeStruct((m, D_MODEL), F32),
                   jax.ShapeDtypeStruct((m * TOK_PACK, LANES), jnp.uint32),
                   jax.ShapeDtypeStruct((8, m), I32),
                   jax.ShapeDtypeStruct((m, LANES), F32),
                   jax.ShapeDtypeStruct((8, LANES), F32)],
        scratch_shapes=[pltpu.VMEM((8, LANES), F32),
                        pltpu.VMEM((2, tm, D_MODEL), F32)],
        compiler_params=_params(("arbitrary",)),
        name="outproj_router",
    )(mg, wo, x2, mods, nw, wr, tri)


def _gather_rows(idx_ref, base, n, src, dst, sem, row0=0):
    for r in range(n):
        tok = idx_ref[base + r]
        pltpu.make_async_copy(src.at[pl.ds(tok, 1), :], dst.at[pl.ds(row0 + r, 1), :],
                              sem).start(priority=r % 2)


def _gather_slabs(idx_ref, base, n, src, dst, sem, row0=0):
    for r in range(n):
        tok = idx_ref[base + r]
        pltpu.make_async_copy(src.at[pl.ds(pl.multiple_of(tok * TOK_PACK, TOK_PACK), TOK_PACK), :],
                              dst.at[pl.ds((row0 + r) * TOK_PITCH, TOK_PACK), :],
                              sem).start(priority=r % 2)


def _expert_kernel(blk_e_ref, pos_ref, meta_ref, h2_hbm, wg_ref, wu_ref, wd_ref, o_ref,
                   xb, tok_ref, wgb, wub, wdb, sem, wgf, wuf, wdf, wsem):
    i = pl.program_id(0)
    n_used = meta_ref[2 * N_EXPERTS]
    slot = i % 2
    rows = MOE_BLOCK
    m = pos_ref.shape[0] // TOP_K

    def wait_block(sl):
        pltpu.make_async_copy(h2_hbm.at[pl.ds(0, rows * TOK_PACK), :],
                              xb.at[sl, pl.ds(0, rows * TOK_PACK), :], sem.at[sl]).wait()

    @pl.when(i == 0)
    def _():
        def pad_expert(e, carry):
            def pad_slot(s, c):
                tok_ref[s] = 0
                return c
            return lax.fori_loop(meta_ref[e], meta_ref[N_EXPERTS + e], pad_slot, carry)

        lax.fori_loop(0, N_EXPERTS, pad_expert, 0)

        def place(t, carry):
            for k in range(TOP_K):
                tok_ref[pos_ref[k * m + t]] = t
            return carry

        lax.fori_loop(0, m, place, 0, unroll=8)
        _gather_slabs(tok_ref, 0, rows, h2_hbm, xb.at[0], sem.at[0])

    @pl.when(i < n_used)
    def _():
        wait_block(slot)
        nxt = jnp.minimum(i + 1, n_used - 1)
        n_pass = 2
        per = rows // (n_pass * TOK_PACK)

        def read_block(ps):
            xs = []
            for a in range(TOK_PACK):
                pair = xb[slot, pl.ds(a, rows, stride=TOK_PITCH), :]
                xs.append(pltpu.bitcast(pair << 16, F32).astype(BF16))
                xs.append(pltpu.bitcast(pair & jnp.uint32(0xFFFF0000), F32).astype(BF16))
                g = ps * TOK_PACK + a
                _gather_slabs(tok_ref, nxt * rows + g * per, per, h2_hbm, xb.at[1 - slot],
                              sem.at[1 - slot], row0=g * per)
            return jnp.concatenate(xs, axis=1)

        prev_e = blk_e_ref[jnp.maximum(i - 1, 0)]
        nblk = pl.num_programs(0)
        par = meta_ref[2 * N_EXPERTS + 1 + i]
        nxt_e = meta_ref[2 * N_EXPERTS + 1 + nblk + i]

        def weight_copies(e, sl):
            return [pltpu.make_async_copy(w_hbm.at[e], buf.at[sl], wsem.at[sl])
                    for w_hbm, buf in ((wg_ref, wgf), (wu_ref, wuf), (wd_ref, wdf))]

        @pl.when(i == 0)
        def _():
            for cp in weight_copies(blk_e_ref[0], 0):
                cp.start()

        @pl.when((i == 0) | (blk_e_ref[i] != prev_e))
        def _():
            for cp in weight_copies(blk_e_ref[i], par):
                cp.wait()
            wgb[...] = wgf[par].astype(BF16)
            wub[...] = wuf[par].astype(BF16)
            wdb[...] = wdf[par].astype(BF16)

            @pl.when(nxt_e >= 0)
            def _():
                for cp in weight_copies(nxt_e, 1 - par):
                    cp.start()

        gate = _dot(read_block(0), wgb[...])
        act = (_silu(gate) * _dot(read_block(1), wub[...])).astype(BF16)
        o_ref[...] = _dot(act, wdb[...])

        @pl.when(i == n_used - 1)
        def _():
            wait_block(1 - slot)

    @pl.when(i >= n_used)
    def _():
        o_ref[...] = jnp.zeros_like(o_ref)


def _experts(blk_e, pos, meta, h2, wg, wu, wd, nblk):
    rows = MOE_BLOCK
    live = lambda i, mt: jnp.minimum(i, mt[2 * N_EXPERTS] - 1)
    grid_spec = pltpu.PrefetchScalarGridSpec(
        num_scalar_prefetch=3,
        grid=(nblk,),
        in_specs=[pl.BlockSpec(memory_space=pl.ANY),
                  pl.BlockSpec(memory_space=pl.ANY),
                  pl.BlockSpec(memory_space=pl.ANY),
                  pl.BlockSpec(memory_space=pl.ANY)],
        out_specs=pl.BlockSpec((rows, D_MODEL), lambda i, be, ps, mt: (i, 0)),
        scratch_shapes=[pltpu.VMEM((2, rows * TOK_PITCH, LANES), jnp.uint32),
                        pltpu.SMEM((nblk * rows,), I32),
                        pltpu.VMEM((D_MODEL, D_EXPERT), BF16),
                        pltpu.VMEM((D_MODEL, D_EXPERT), BF16),
                        pltpu.VMEM((D_EXPERT, D_MODEL), BF16),
                        pltpu.SemaphoreType.DMA((2,)),
                        pltpu.VMEM((2, D_MODEL, D_EXPERT), F32),
                        pltpu.VMEM((2, D_MODEL, D_EXPERT), F32),
                        pltpu.VMEM((2, D_EXPERT, D_MODEL), F32),
                        pltpu.SemaphoreType.DMA((2,))],
    )
    return pl.pallas_call(
        _expert_kernel,
        grid_spec=grid_spec,
        out_shape=jax.ShapeDtypeStruct((nblk * rows, D_MODEL), F32),
        compiler_params=_params(("arbitrary",)),
        name="expert_mlp",
    )(blk_e, pos, meta, h2, wg, wu, wd)


def _combine_kernel(pos_ref, eo_hbm, x1_ref, rw_ref, mod_ref, nf_ref, o_ref, gb, sem):
    i = pl.program_id(0)
    nsteps = pl.num_programs(0)
    tm = x1_ref.shape[0]
    slot = i % 2

    def issue(step, sl):
        for k in range(TOP_K):
            _gather_rows(pos_ref, (k * nsteps + step) * tm, tm, eo_hbm, gb.at[sl, k], sem.at[sl])

    @pl.when(i == 0)
    def _():
        issue(0, 0)

    @pl.when(i + 1 < nsteps)
    def _():
        issue(i + 1, 1 - slot)

    for k in range(TOP_K):
        pltpu.make_async_copy(eo_hbm.at[pl.ds(0, tm), :], gb.at[slot, k], sem.at[slot]).wait()
    w1 = rw_ref[:, 0:1]
    w2 = rw_ref[:, 1:2]
    g2 = mod_ref[0, 5:6, :]
    x2 = x1_ref[...] + g2 * (gb[slot, 0] * w1 + gb[slot, 1] * w2)
    ms = jnp.mean(x2 * x2, axis=-1, keepdims=True)
    o_ref[...] = x2 * lax.rsqrt(ms + NORM_EPS) * nf_ref[...]


def _combine(pos, eo, x1, rw, mods, nf, tm, t):
    m = x1.shape[0]
    nbm = t // tm
    grid_spec = pltpu.PrefetchScalarGridSpec(
        num_scalar_prefetch=1,
        grid=(m // tm,),
        in_specs=[pl.BlockSpec(memory_space=pl.ANY),
                  pl.BlockSpec((tm, D_MODEL), lambda i, p: (i, 0)),
                  pl.BlockSpec((tm, LANES), lambda i, p: (i, 0)),
                  pl.BlockSpec((1, N_MOD, D_MODEL), lambda i, p: (i // nbm, 0, 0)),
                  pl.BlockSpec((1, D_MODEL), lambda i, p: (0, 0))],
        out_specs=pl.BlockSpec((tm, D_MODEL), lambda i, p: (i, 0)),
        scratch_shapes=[pltpu.VMEM((2, TOP_K, tm, D_MODEL), F32),
                        pltpu.SemaphoreType.DMA((2,))],
    )
    return pl.pallas_call(
        _combine_kernel,
        grid_spec=grid_spec,
        out_shape=jax.ShapeDtypeStruct((m, D_MODEL), F32),
        compiler_params=_params(("arbitrary",)),
        name="moe_combine_norm",
    )(pos, eo, x1, rw, mods, nf)


def _rope_tables(t):
    half = ATTN_HEAD_DIM // 4
    f32 = np.float32
    pos = np.arange(t)
    row = (pos // GRID_W).astype(f32)
    col = (pos % GRID_W).astype(f32)
    freqs = (f32(ROPE_BASE) ** (-np.arange(half, dtype=f32) / f32(half))).astype(f32)
    ar = row[:, None] * freqs[None, :]
    ac = col[:, None] * freqs[None, :]
    cos_t = np.concatenate([np.cos(ar), np.cos(ar), np.cos(ac), np.cos(ac)], axis=1)
    sin_t = np.concatenate([-np.sin(ar), np.sin(ar), -np.sin(ac), np.sin(ac)], axis=1)
    return jnp.asarray(cos_t, F32), jnp.asarray(sin_t, F32)


def _tri_pair3(n):
    li = np.arange(n)[:, None]
    si = np.arange(n)[None, :]
    pair = np.stack([li >= si, li <= si]).astype(np.float32)
    return jnp.asarray(np.concatenate([pair] * 3, axis=2), BF16)


def _head_expand_matrix2():
    k = np.arange(LANES)[:, None]
    j = np.arange(D_SSD)[None, :]
    e = (k == j // SSD_HEAD_DIM).astype(np.float32)
    return jnp.asarray(np.concatenate([e, e], axis=0), BF16)


def _pad_heads(v):
    return jnp.pad(v.astype(F32), ((0, 0), (0, LANES - SSD_HEADS)))[:, None, :]


def _row_blocks(t, mc, tc):
    return dict(inproj=min(1024, t), inproj_ctx=min(1024, mc), conv=min(512, t), conv_ctx=min(256, tc),
                merge=min(1024, t), outproj=min(512, t), combine=min(256, t))


def kernel(x, c, ctx, c_ctx, w_ada, b_ada, norm_mix, norm_ffn, w_in, conv_w, conv_b, a_log, dt_bias,
           d_skip, ssd_norm, attn_sink, w_branch_ssd, w_branch_attn, w_out, w_route_group,
           w_route_expert, w_gate, w_up, w_down, norm_final):
    bsz, t, d = x.shape
    tc = ctx.shape[1]
    m = bsz * t
    l = 0

    ctx_row = bsz
    n_rows = -(-(bsz + 1) // 8) * 8
    cc = jnp.zeros((n_rows, d), F32).at[:bsz].set(c).at[ctx_row].set(c_ctx)
    mods = _ada(cc, w_ada[l], b_ada[l][None, :]).reshape(n_rows, N_MOD, d)

    wi = w_in[l]
    o_xbc = D_SSD
    o_dt = o_xbc + CONV_CH
    o_q = o_dt + 2 * SSD_HEADS
    o_k = o_q + D_ATTN
    o_v = o_k + D_KV
    o_g = o_v + D_KV
    w_xs = wi[:, o_xbc:o_xbc + D_SSD]
    w_bc = wi[:, o_xbc + D_SSD:o_dt]
    w_k = wi[:, o_k:o_v]
    w_v = wi[:, o_v:o_g]
    w_lat = jnp.concatenate([w_xs, wi[:, :D_SSD], wi[:, o_q:o_k], w_bc, w_k, w_v, wi[:, o_g:]],
                            axis=1).astype(BF16)
    w_ctx = jnp.concatenate([w_xs, w_bc, w_k, w_v], axis=1).astype(BF16)
    w_dt = jnp.zeros((d, 2 * LANES), F32)
    w_dt = w_dt.at[:, :SSD_HEADS].set(wi[:, o_dt:o_dt + SSD_HEADS])
    w_dt = w_dt.at[:, LANES:LANES + SSD_HEADS].set(wi[:, o_dt + SSD_HEADS:o_q]).astype(BF16)

    cos_t, sin_t = _rope_tables(t)
    nmix = norm_mix[l][None, :]
    mc = bsz * tc
    blk = _row_blocks(t, mc, tc)
    u, dt2 = _inproj(x.reshape(m, d), nmix, mods, w_lat, w_dt, cos_t, sin_t,
                     rows_per_mod=t, mod_row0=0, tm=blk["inproj"], rope=True)
    uc, dtc2 = _inproj(ctx.reshape(mc, d), nmix, mods, w_ctx, w_dt, None, None,
                       rows_per_mod=0, mod_row0=ctx_row, tm=blk["inproj_ctx"], rope=False)

    cw = conv_w[l]
    cb = conv_b[l][None, :]
    xs_c, bc_c = _conv(u, U_XS, U_BC, cw, cb, t, blk["conv"])
    xsc_c, bcc_c = _conv(uc, UC_XS, UC_BC, cw, cb, tc, blk["conv_ctx"])
    alog = _pad_heads(a_log[l])
    bias = _pad_heads(dt_bias[l])
    e_mat = _head_expand_matrix2()
    h0 = _ctx_state(xsc_c, bcc_c, dtc2, alog, bias, _tri_pair3(tc), e_mat, bsz, tc)
    dsk = jnp.repeat(d_skip[l].astype(F32), SSD_HEAD_DIM)[None, :]
    y_ssd = _ssd(xs_c, bc_c, dt2, alog, bias, _tri_pair3(SSD_CHUNK), e_mat, h0, u, dsk,
                 ssd_norm[l][None, :], bsz, t)

    y_attn = _attn(attn_sink[l].astype(F32), u, uc, bsz, t, tc)

    merged = _merge(y_ssd, y_attn, u, w_branch_ssd[l].astype(BF16), w_branch_attn[l].astype(BF16),
                    blk["merge"])

    tm_o = blk["outproj"]
    w_r = jnp.zeros((d, 2 * LANES), F32)
    w_r = w_r.at[:, :MOE_GROUPS].set(w_route_group[l])
    w_r = w_r.at[:, LANES:LANES + N_EXPERTS].set(w_route_expert[l]).astype(BF16)
    li = np.arange(tm_o)
    tri_strict = jnp.asarray((li[:, None] > li[None, :]).astype(np.float32), BF16)
    x1, h2, ridx, rw, cnt = _outproj(merged, w_out[l].astype(BF16), x.reshape(m, d), mods,
                                        norm_ffn[l][None, :], w_r, tri_strict, tm_o, t)

    na = m * TOP_K
    nblk = -(-(na + N_EXPERTS * (MOE_BLOCK - 1)) // MOE_BLOCK)
    counts = cnt[0, :N_EXPERTS].astype(I32)
    padded = (counts + MOE_BLOCK - 1) // MOE_BLOCK * MOE_BLOCK
    pend = jnp.cumsum(padded)
    pstart = pend - padded
    blk_start = jnp.arange(nblk, dtype=I32) * MOE_BLOCK
    blk_e = jnp.clip(jnp.sum((pend[None, :] <= blk_start[:, None]).astype(I32), axis=1),
                     0, N_EXPERTS - 1)
    n_used = (pend[N_EXPERTS - 1:] // MOE_BLOCK).astype(I32)
    bidx = jnp.arange(nblk, dtype=I32)
    used = bidx < n_used[0]
    chg = used & ((bidx == 0) | (blk_e != jnp.roll(blk_e, 1)))
    run_par = (jnp.cumsum(chg.astype(I32)) - 1) % 2
    later = chg[None, :] & (bidx[None, :] > bidx[:, None])
    j_next = jnp.min(jnp.where(later, bidx[None, :], nblk), axis=1)
    nxt_e = jnp.where(j_next < nblk, blk_e[jnp.minimum(j_next, nblk - 1)], -1)
    meta = jnp.concatenate([pstart + counts, pend, n_used, run_par, nxt_e]).astype(I32)
    e_sel = ridx[0:TOP_K]
    seg0 = jnp.zeros_like(e_sel)
    for e in range(N_EXPERTS):
        seg0 = jnp.where(e_sel == e, pstart[e], seg0)
    pos = (seg0 + ridx[TOP_K:2 * TOP_K]).reshape(-1)

    eo = _experts(blk_e, pos, meta, h2, w_gate[l], w_up[l], w_down[l], nblk)

    out = _combine(pos, eo, x1, rw, mods, norm_final[None, :], blk["combine"], t)
    return out.reshape(bsz, t, d)
```
